```python
import jax, jax.numpy as jnp
from jax import lax
import numpy as np

D_MODEL = 1024
BATCH = 2
SEQ = 8192
DEPTH = 1

GRID_W = 64
CTX_LEN = 256

RET_HEADS = 4
RET_QK_DIM = 128
RET_V_DIM = 256
RET_CHUNK = 128
ROPE_THETA = 10000.0
RET_QK_W = RET_HEADS * RET_QK_DIM
RET_V_W = RET_HEADS * RET_V_DIM

SGU_GROUPS = 4
SGU_CHUNK = 128
SGU_WIDTH = 1024

N_EXPERTS = 16
EC_CAPACITY_FACTOR = 2
D_EXPERT = 1024

LN_EPS = 1e-6
DEEPNORM_ALPHA = (2.0 * DEPTH) ** 0.25
DEEPNORM_BETA = (8.0 * DEPTH) ** -0.25

IN_WIDTHS = (RET_QK_W, RET_QK_W, RET_V_W, RET_V_W, SGU_WIDTH, SGU_WIDTH, D_MODEL, D_MODEL)
IN_OFFSETS = tuple(sum(IN_WIDTHS[:i + 1]) for i in range(len(IN_WIDTHS) - 1))
D_IN = sum(IN_WIDTHS)

kernel_name = "hybrid_retention_sgu_ecmoe_dit"


def layer_norm(x, w=None, b=None):
    xf = x.astype(jnp.float32)
    mu = jnp.mean(xf, axis=-1, keepdims=True)
    var = jnp.mean(jnp.square(xf - mu), axis=-1, keepdims=True)
    y = (xf - mu) * lax.rsqrt(var + LN_EPS)
    if w is not None:
        y = y * w.astype(jnp.float32) + b.astype(jnp.float32)
    return y.astype(x.dtype)


def modulate(x, shift, scale):
    return layer_norm(x) * (1.0 + scale) + shift


def axial_rope_tables(n_tokens, dtype):
    rows = n_tokens // GRID_W
    r, col = jnp.meshgrid(jnp.arange(rows), jnp.arange(GRID_W), indexing="ij")
    n_freq = RET_QK_DIM // 4
    inv = ROPE_THETA ** (-jnp.arange(n_freq, dtype=jnp.float32) / n_freq)
    pos = jnp.stack([r.reshape(-1), col.reshape(-1)], axis=-1).astype(jnp.float32)
    ang = pos[..., None] * inv
    return jnp.cos(ang).astype(dtype), jnp.sin(ang).astype(dtype)


def apply_axial_rope(x, cos, sin):
    bsz, t, nh, dk = x.shape
    xs = x.reshape(bsz, t, nh, 2, 2, dk // 4)
    x1, x2 = xs[..., 0, :], xs[..., 1, :]
    c, s = cos[:, None], sin[:, None]
    return jnp.stack([x1 * c - x2 * s, x2 * c + x1 * s], axis=-2).reshape(bsz, t, nh, dk)


def split_heads(z, nh):
    bsz, t, _ = z.shape
    return z.reshape(bsz, t, nh, -1).transpose(0, 2, 1, 3)


def retention_chunked(q, k, v, log_gamma, state0):
    bsz, nh, t, dk = q.shape
    dv = v.shape[-1]
    n = t // RET_CHUNK
    q = q.reshape(bsz, nh, n, RET_CHUNK, dk)
    k = k.reshape(bsz, nh, n, RET_CHUNK, dk)
    v = v.reshape(bsz, nh, n, RET_CHUNK, dv)
    idx = jnp.arange(RET_CHUNK, dtype=jnp.float32)
    diff = idx[:, None] - idx[None, :]
    decay_mask = jnp.where(diff[None] >= 0, jnp.exp(diff[None] * log_gamma[:, None, None]), 0.0)
    scores = jnp.einsum("bhnld,bhnmd->bhnlm", q, k) * decay_mask[None, :, None].astype(q.dtype)
    intra = jnp.einsum("bhnlm,bhnmv->bhnlv", scores, v)
    zeta = jnp.exp((RET_CHUNK - 1 - idx)[None] * log_gamma[:, None])
    xi = jnp.exp((idx + 1.0)[None] * log_gamma[:, None])
    chunk_kv = jnp.einsum("bhnld,bhnlv->nbhdv", k * zeta[None, :, None, :, None], v).astype(jnp.float32)
    chunk_decay = jnp.exp(RET_CHUNK * log_gamma)[None, :, None, None]

    def step(state, kv):
        return chunk_decay * state + kv, state

    state_final, state_prev = lax.scan(step, state0, chunk_kv)
    cross = jnp.einsum("bhnld,nbhdv->bhnlv", q * xi[None, :, None, :, None], state_prev)
    out = (intra + cross).reshape(bsz, nh, t, dv)
    return out, state_final


def retention_final_state(k, v, log_gamma):
    t = k.shape[2]
    w = jnp.exp((t - 1 - jnp.arange(t, dtype=jnp.float32))[None] * log_gamma[:, None])
    return jnp.einsum("bhtd,bhtv->bhdv", k * w[None, :, :, None], v).astype(jnp.float32)


def context_retention_states(hc, w_in, b_in, decay_f, decay_b):
    lo, hi = RET_QK_W, 2 * RET_QK_W + RET_V_W
    kv = hc @ w_in[:, lo:hi] + b_in[lo:hi]
    k, v = jnp.split(kv, [RET_QK_W], axis=-1)
    k = split_heads(k, RET_HEADS) * (RET_QK_DIM ** -0.5)
    v = split_heads(v, RET_HEADS)
    sf = retention_final_state(k, v, jax.nn.log_sigmoid(decay_f.astype(jnp.float32)))
    sb = retention_final_state(jnp.flip(k, axis=2), jnp.flip(v, axis=2),
                               jax.nn.log_sigmoid(decay_b.astype(jnp.float32)))
    return sf, sb


def token_mixer(h, rope, s0f, s0b, w_in, b_in, decay_f, decay_b, gn_w, gn_b,
                sgu_ln_w, sgu_ln_b, sgu_w, sgu_b, w_pa, w_pb, w_o, b_o):
    bsz, t, _ = h.shape
    z = h @ w_in + b_in
    q, k, v, g, u, vs, ga, gb = jnp.split(z, IN_OFFSETS, axis=-1)

    q = q.reshape(bsz, t, RET_HEADS, RET_QK_DIM)
    k = k.reshape(bsz, t, RET_HEADS, RET_QK_DIM) * (RET_QK_DIM ** -0.5)
    if rope is not None:
        q = apply_axial_rope(q, *rope)
        k = apply_axial_rope(k, *rope)
    q = q.transpose(0, 2, 1, 3)
    k = k.transpose(0, 2, 1, 3)
    v = split_heads(v, RET_HEADS)
    lgf = jax.nn.log_sigmoid(decay_f.astype(jnp.float32))
    lgb = jax.nn.log_sigmoid(decay_b.astype(jnp.float32))
    o_f, sf = retention_chunked(q, k, v, lgf, s0f)
    o_b, sb = retention_chunked(jnp.flip(q, axis=2), jnp.flip(k, axis=2), jnp.flip(v, axis=2), lgb, s0b)
    o = (o_f + jnp.flip(o_b, axis=2)).transpose(0, 2, 1, 3)
    o = layer_norm(o).reshape(bsz, t, RET_V_W) * gn_w + gn_b
    ret = o.astype(h.dtype) * jax.nn.silu(g)

    u = jax.nn.gelu(u)
    vs = layer_norm(jax.nn.gelu(vs), sgu_ln_w, sgu_ln_b)
    vs = vs.reshape(bsz, t // SGU_CHUNK, SGU_CHUNK, SGU_GROUPS, SGU_WIDTH // SGU_GROUPS)
    spatial = jnp.einsum("gpq,bnqgc->bnpgc", sgu_w, vs) + sgu_b.T[None, None, :, :, None]
    sgu = u * spatial.reshape(bsz, t, SGU_WIDTH)

    y = jax.nn.sigmoid(ga) * (ret @ w_pa) + jax.nn.sigmoid(gb) * (sgu @ w_pb)
    return y @ w_o + b_o, sf, sb


def expert_choice_ffn(h, w_router, b_router, w_gate, w_up, w_down):
    bsz, t, d = h.shape
    cap = EC_CAPACITY_FACTOR * t // N_EXPERTS
    logits = (h @ w_router + b_router).astype(jnp.float32)
    affinity = jax.nn.softmax(logits, axis=-1)
    gates, idx = lax.top_k(affinity.transpose(0, 2, 1), cap)
    xs = jax.vmap(lambda hb, ib: hb[ib])(h, idx)
    hidden = jax.nn.silu(jnp.einsum("becd,edf->becf", xs, w_gate)) * jnp.einsum("becd,edf->becf", xs, w_up)
    ye = jnp.einsum("becf,efd->becd", hidden, w_down) * gates[..., None].astype(h.dtype)
    return jax.vmap(lambda yb, ib: jnp.zeros((t, d), yb.dtype).at[ib.reshape(-1)].add(yb.reshape(-1, d)))(ye, idx)


def setup_inputs(seed: int = 0) -> dict:
    key = jax.random.key(seed)
    ks = jax.random.split(key, 32)
    f32 = jnp.float32
    L, D = DEPTH, D_MODEL

    def nrm(k, shape, scale):
        return jax.random.normal(k, shape, f32) * scale

    base_decay = jnp.log(2.0 ** (5.0 + jnp.arange(RET_HEADS, dtype=f32)) - 1.0)
    return {
        "x": nrm(ks[0], (BATCH, SEQ, D), 1.0),
        "c": nrm(ks[1], (BATCH, D), 1.0),
        "ctx": nrm(ks[2], (BATCH, CTX_LEN, D), 1.0),
        "c_ctx": nrm(ks[3], (D,), 1.0),
        "w_ada": nrm(ks[4], (L, D, 6 * D), 0.5 * D ** -0.5),
        "b_ada": nrm(ks[5], (L, 6 * D), 0.02),
        "w_in": nrm(ks[6], (L, D, D_IN), D ** -0.5),
        "b_in": nrm(ks[7], (L, D_IN), 0.02),
        "ret_decay_f": base_decay[None] + nrm(ks[8], (L, RET_HEADS), 0.1),
        "ret_decay_b": base_decay[None] + nrm(ks[9], (L, RET_HEADS), 0.1),
        "ret_gn_w": 1.0 + nrm(ks[10], (L, RET_V_W), 0.02),
        "ret_gn_b": nrm(ks[11], (L, RET_V_W), 0.02),
        "sgu_ln_w": 1.0 + nrm(ks[12], (L, SGU_WIDTH), 0.02),
        "sgu_ln_b": nrm(ks[13], (L, SGU_WIDTH), 0.02),
        "sgu_w": nrm(ks[14], (L, SGU_GROUPS, SGU_CHUNK, SGU_CHUNK), SGU_CHUNK ** -0.5),
        "sgu_b": 1.0 + nrm(ks[15], (L, SGU_GROUPS, SGU_CHUNK), 0.02),
        "w_pa": nrm(ks[16], (L, RET_V_W, D), DEEPNORM_BETA * RET_V_W ** -0.5),
        "w_pb": nrm(ks[17], (L, SGU_WIDTH, D), DEEPNORM_BETA * SGU_WIDTH ** -0.5),
        "w_o": nrm(ks[18], (L, D, D), DEEPNORM_BETA * D ** -0.5),
        "b_o": nrm(ks[19], (L, D), 0.02),
        "ln1_w": 1.0 + nrm(ks[20], (L, D), 0.02),
        "ln1_b": nrm(ks[21], (L, D), 0.02),
        "w_router": nrm(ks[22], (L, D, N_EXPERTS), D ** -0.5),
        "b_router": nrm(ks[23], (L, N_EXPERTS), 0.01),
        "w_gate": nrm(ks[24], (L, N_EXPERTS, D, D_EXPERT), D ** -0.5),
        "w_up": nrm(ks[25], (L, N_EXPERTS, D, D_EXPERT), D ** -0.5),
        "w_down": nrm(ks[26], (L, N_EXPERTS, D_EXPERT, D), DEEPNORM_BETA * D_EXPERT ** -0.5),
        "ln2_w": 1.0 + nrm(ks[27], (L, D), 0.02),
        "ln2_b": nrm(ks[28], (L, D), 0.02),
    }


def reference(x, c, ctx, c_ctx, w_ada, b_ada, w_in, b_in, ret_decay_f, ret_decay_b, ret_gn_w, ret_gn_b,
              sgu_ln_w, sgu_ln_b, sgu_w, sgu_b, w_pa, w_pb, w_o, b_o, ln1_w, ln1_b,
              w_router, b_router, w_gate, w_up, w_down, ln2_w, ln2_b):
    bsz, n_tok, _ = x.shape
    rope = axial_rope_tables(n_tok, x.dtype)
    zero_state = jnp.zeros((bsz, RET_HEADS, RET_QK_DIM, RET_V_DIM), jnp.float32)

    for l in range(DEPTH):
        mod = jax.nn.silu(c) @ w_ada[l] + b_ada[l]
        mod_ctx = jax.nn.silu(c_ctx) @ w_ada[l] + b_ada[l]
        sh1, sc1, g1, sh2, sc2, g2 = jnp.split(mod[:, None, :], 6, axis=-1)
        csh1, csc1, cg1, csh2, csc2, cg2 = jnp.split(mod_ctx[None, None, :], 6, axis=-1)
        mixer_params = (w_in[l], b_in[l], ret_decay_f[l], ret_decay_b[l], ret_gn_w[l], ret_gn_b[l],
                        sgu_ln_w[l], sgu_ln_b[l], sgu_w[l], sgu_b[l], w_pa[l], w_pb[l], w_o[l], b_o[l])
        moe_params = (w_router[l], b_router[l], w_gate[l], w_up[l], w_down[l])
        last = (l + 1 == DEPTH)

        hc = modulate(ctx, csh1, csc1)
        if last:
            sf, sb = context_retention_states(hc, w_in[l], b_in[l], ret_decay_f[l], ret_decay_b[l])
        else:
            mix_c, sf, sb = token_mixer(hc, None, zero_state, zero_state, *mixer_params)

        h = modulate(x, sh1, sc1)
        mix, _, _ = token_mixer(h, rope, sf, sb, *mixer_params)
        x = layer_norm(DEEPNORM_ALPHA * x + g1 * mix, ln1_w[l], ln1_b[l])
        x = layer_norm(DEEPNORM_ALPHA * x + g2 * expert_choice_ffn(modulate(x, sh2, sc2), *moe_params),
                       ln2_w[l], ln2_b[l])

        if not last:
            ctx = layer_norm(DEEPNORM_ALPHA * ctx + cg1 * mix_c, ln1_w[l], ln1_b[l])
            ctx = layer_norm(DEEPNORM_ALPHA * ctx + cg2 * expert_choice_ffn(modulate(ctx, csh2, csc2), *moe_params),
                             ln2_w[l], ln2_b[l])
    return x
```

```python
import functools
import math

import jax
import jax.numpy as jnp
from jax import lax
from jax.experimental import pallas as pl
from jax.experimental.pallas import tpu as pltpu

F32 = jnp.float32
BF16 = jnp.bfloat16
I32 = jnp.int32
HIGHEST = lax.Precision.HIGHEST

D_MODEL = 1024
DEPTH = 1
GRID_W = 64
RET_HEADS = 4
RET_QK_DIM = 128
RET_V_DIM = 256
RET_CHUNK = 128
ROPE_THETA = 10000.0
RET_QK_W = RET_HEADS * RET_QK_DIM
RET_V_W = RET_HEADS * RET_V_DIM
SGU_GROUPS = 4
SGU_CHUNK = 128
SGU_WIDTH = 1024
SGU_GW = SGU_WIDTH // SGU_GROUPS
N_EXPERTS = 16
EC_CAPACITY_FACTOR = 2
D_EXPERT = 1024
LN_EPS = 1e-6
DEEPNORM_ALPHA = (2.0 * DEPTH) ** 0.25
D_IN = 2 * RET_QK_W + 2 * RET_V_W + 2 * SGU_WIDTH + 2 * D_MODEL

V7X_VMEM_BYTES = 64 * 1024 * 1024
VMEM_LIMIT = 56 * 1024 * 1024
LANES = 128

MOD_ROWS = 8
TM_INPROJ = 256
TT_MIXER = 256
TB_ROUTE = 256
RT_GATHER = 128
RT_SCATTER = 256
TM_FINAL = 512


def _cparams(sem):
    return pltpu.CompilerParams(dimension_semantics=sem, vmem_limit_bytes=VMEM_LIMIT)


def _ln(x):
    mu = jnp.mean(x, axis=-1, keepdims=True)
    xc = x - mu
    var = jnp.mean(xc * xc, axis=-1, keepdims=True)
    return xc * lax.rsqrt(var + LN_EPS)


def _iota_f(shape, dim):
    return lax.broadcasted_iota(I32, shape, dim).astype(F32)


def _mod_kernel(c_ref, w_ref, b_ref, o_ref):
    c = c_ref[...]
    a = c * jax.nn.sigmoid(c)
    o_ref[...] = jnp.dot(a, w_ref[...], preferred_element_type=F32, precision=HIGHEST) + b_ref[...]


def _mod_call(cvec, w_ada, b_ada):
    d, n = w_ada.shape
    tn = 1024
    return pl.pallas_call(
        _mod_kernel,
        out_shape=jax.ShapeDtypeStruct((MOD_ROWS, n), F32),
        grid=(n // tn,),
        in_specs=[
            pl.BlockSpec((MOD_ROWS, d), lambda j: (0, 0)),
            pl.BlockSpec((d, tn), lambda j: (0, j)),
            pl.BlockSpec((1, tn), lambda j: (0, j)),
        ],
        out_specs=pl.BlockSpec((MOD_ROWS, tn), lambda j: (0, j)),
        compiler_params=_cparams(("parallel",)),
        name="mod",
    )(cvec, w_ada, b_ada)


def _ctx_kv_kernel(x_ref, mod_ref, w_ref, b_ref, k_ref, v_ref):
    d = D_MODEL
    sh = mod_ref[0, :, 0:d]
    sc = mod_ref[0, :, d:2 * d]
    h = (_ln(x_ref[0]) * (1.0 + sc) + sh).astype(BF16)
    z = jnp.dot(h, w_ref[...], preferred_element_type=F32) + b_ref[...]
    k_ref[0] = (z[:, :RET_QK_W] * (RET_QK_DIM ** -0.5)).astype(BF16)
    v_ref[0] = z[:, RET_QK_W:].astype(BF16)


def _ctx_kv_call(ctx, mod3, w_kv, b_kv, ctx_row):
    bsz, n_ctx, d = ctx.shape
    wkv = w_kv.shape[1]
    return pl.pallas_call(
        _ctx_kv_kernel,
        out_shape=(jax.ShapeDtypeStruct((bsz, n_ctx, RET_QK_W), BF16),
                   jax.ShapeDtypeStruct((bsz, n_ctx, RET_V_W), BF16)),
        grid=(bsz,),
        in_specs=[
            pl.BlockSpec((1, n_ctx, d), lambda b: (b, 0, 0)),
            pl.BlockSpec((1, 1, 6 * d), lambda b: (ctx_row, 0, 0)),
            pl.BlockSpec((d, wkv), lambda b: (0, 0)),
            pl.BlockSpec((1, wkv), lambda b: (0, 0)),
        ],
        out_specs=(pl.BlockSpec((1, n_ctx, RET_QK_W), lambda b: (b, 0, 0)),
                   pl.BlockSpec((1, n_ctx, RET_V_W), lambda b: (b, 0, 0))),
        compiler_params=_cparams(("parallel",)),
        name="ctx_kv",
    )(ctx, mod3, w_kv, b_kv)


def _inproj_kernel(x_ref, mod_ref, w_ref, b_ref, cos_ref, sin_ref, lnw_ref, lnb_ref,
                   qk_ref, v_ref, sg_ref, u_ref, vs_ref, ga_ref, gb_ref):
    d = D_MODEL
    sh = mod_ref[0, :, 0:d]
    sc = mod_ref[0, :, d:2 * d]
    h = (_ln(x_ref[0]) * (1.0 + sc) + sh).astype(BF16)

    def proj(c0, width):
        return jnp.dot(h, w_ref[:, c0:c0 + width], preferred_element_type=F32) + b_ref[:, c0:c0 + width]

    zq = proj(0, 2 * RET_QK_W)
    cos = cos_ref[...]
    sin = sin_ref[...]
    tm = zq.shape[0]
    upper = (lax.broadcasted_iota(I32, (tm, LANES), 1) & 32) != 0
    for hb in range(2 * RET_HEADS):
        zs = zq[:, hb * LANES:(hb + 1) * LANES]
        if hb >= RET_HEADS:
            zs = zs * (RET_QK_DIM ** -0.5)
        sw = jnp.where(upper, pltpu.roll(zs, 32, 1), pltpu.roll(zs, LANES - 32, 1))
        qk_ref[0, :, hb * LANES:(hb + 1) * LANES] = (zs * cos + sw * sin).astype(BF16)

    c0 = 2 * RET_QK_W
    v_ref[0] = proj(c0, RET_V_W).astype(BF16)
    c0 += RET_V_W
    sg_ref[0] = jax.nn.silu(proj(c0, RET_V_W)).astype(BF16)
    c0 += RET_V_W
    u_ref[0] = jax.nn.gelu(proj(c0, SGU_WIDTH)).astype(BF16)
    c0 += SGU_WIDTH
    vs = _ln(jax.nn.gelu(proj(c0, SGU_WIDTH))) * lnw_ref[...] + lnb_ref[...]
    vs_ref[0] = vs.astype(BF16)
    c0 += SGU_WIDTH
    ga_ref[0] = jax.nn.sigmoid(proj(c0, d)).astype(BF16)
    c0 += d
    gb_ref[0] = jax.nn.sigmoid(proj(c0, d)).astype(BF16)


def _inproj_call(x, mod3, w_in, b_in, cos_t, sin_t, sgu_ln_w, sgu_ln_b):
    bsz, t, d = x.shape
    tm = TM_INPROJ
    row = lambda b, i: (b, i, 0)
    const2 = lambda b, i: (0, 0)
    outs = tuple(jax.ShapeDtypeStruct((bsz, t, 1024), BF16) for _ in range(7))
    return pl.pallas_call(
        _inproj_kernel,
        out_shape=outs,
        grid=(bsz, t // tm),
        in_specs=[
            pl.BlockSpec((1, tm, d), row),
            pl.BlockSpec((1, 1, 6 * d), lambda b, i: (b, 0, 0)),
            pl.BlockSpec((d, D_IN), const2),
            pl.BlockSpec((1, D_IN), const2),
            pl.BlockSpec((tm, LANES), lambda b, i: (i, 0)),
            pl.BlockSpec((tm, LANES), lambda b, i: (i, 0)),
            pl.BlockSpec((1, SGU_WIDTH), const2),
            pl.BlockSpec((1, SGU_WIDTH), const2),
        ],
        out_specs=tuple(pl.BlockSpec((1, tm, 1024), row) for _ in range(7)),
        compiler_params=_cparams(("parallel", "parallel")),
        name="inproj",
    )(x, mod3, w_in, b_in, cos_t, sin_t, sgu_ln_w, sgu_ln_b)


def _dot_t0(a, b):
    return lax.dot_general(a, b, (((0,), (0,)), ((), ())), preferred_element_type=F32)


def _state_kernel(lg_ref, k_ref, v_ref, kc_ref, vc_ref, s_ref):
    hd = pl.program_id(1)
    lgf = lg_ref[0, hd]
    lgb = lg_ref[1, hd]
    n_chunks = s_ref.shape[2]
    n_ctx = kc_ref.shape[1]
    c = RET_CHUNK
    li = _iota_f((c, RET_QK_DIM), 0)
    zeta_f = jnp.exp((c - 1.0 - li) * lgf)
    zeta_b = jnp.exp(li * lgb)
    one = jnp.ones((1, 1), F32)
    cd_f = jnp.exp(one * (c * lgf))
    cd_b = jnp.exp(one * (c * lgb))

    tc = _iota_f((n_ctx, RET_QK_DIM), 0)
    kc = kc_ref[0].astype(F32)
    vc = vc_ref[0]
    s0f = _dot_t0((kc * jnp.exp((n_ctx - 1.0 - tc) * lgf)).astype(BF16), vc)
    s0b = _dot_t0((kc * jnp.exp(tc * lgb)).astype(BF16), vc)

    def chunk_kv(n, zeta):
        off = pl.multiple_of(n * c, c)
        kk = k_ref[0, pl.ds(off, c), :].astype(F32)
        return _dot_t0((kk * zeta).astype(BF16), v_ref[0, pl.ds(off, c), :])

    def fwd(n, s):
        s_ref[0, 0, n, 0:RET_QK_DIM, :] = s.astype(BF16)
        return cd_f * s + chunk_kv(n, zeta_f)

    lax.fori_loop(0, n_chunks, fwd, s0f)

    def bwd(i, s):
        n = n_chunks - 1 - i
        s_ref[0, 0, n, RET_QK_DIM:2 * RET_QK_DIM, :] = s.astype(BF16)
        return cd_b * s + chunk_kv(n, zeta_b)

    lax.fori_loop(0, n_chunks, bwd, s0b)


def _state_call(lg, qk, v, kctx, vctx):
    bsz, t, _ = qk.shape
    n_ctx = kctx.shape[1]
    n_chunks = t // RET_CHUNK
    return pl.pallas_call(
        _state_kernel,
        out_shape=jax.ShapeDtypeStruct((bsz, RET_HEADS, n_chunks, 2 * RET_QK_DIM, RET_V_DIM), BF16),
        grid=(bsz, RET_HEADS),
        in_specs=[
            pl.BlockSpec(memory_space=pltpu.SMEM),
            pl.BlockSpec((1, t, RET_QK_DIM), lambda b, h: (b, 0, RET_HEADS + h)),
            pl.BlockSpec((1, t, RET_V_DIM), lambda b, h: (b, 0, h)),
            pl.BlockSpec((1, n_ctx, RET_QK_DIM), lambda b, h: (b, 0, h)),
            pl.BlockSpec((1, n_ctx, RET_V_DIM), lambda b, h: (b, 0, h)),
        ],
        out_specs=pl.BlockSpec((1, 1, n_chunks, 2 * RET_QK_DIM, RET_V_DIM), lambda b, h: (b, h, 0, 0, 0)),
        compiler_params=_cparams(("parallel", "parallel")),
        name="states",
    )(lg, qk, v, kctx, vctx)


def _mixer_kernel(lg_ref, qk_ref, v_ref, sg_ref, u_ref, vs_ref, ga_ref, gb_ref, s_ref, x_ref, mod_ref,
                  wpa_ref, wpb_ref, wo_ref, sguw_ref, sgub_ref, gnw_ref, gnb_ref, bo_ref,
                  ln1w_ref, ln1b_ref, wr_ref, br_ref,
                  x1_ref, hm_ref, lgt_ref, ret_scr, sgu_scr):
    d = D_MODEL
    c = RET_CHUNK
    n_sub = qk_ref.shape[1] // c
    row = lax.broadcasted_iota(I32, (c, c), 0)
    col = lax.broadcasted_iota(I32, (c, c), 1)
    diff = (row - col).astype(F32)
    rowq = _iota_f((c, RET_QK_DIM), 0)

    for hd in range(RET_HEADS):
        lgf = lg_ref[0, hd]
        lgb = lg_ref[1, hd]
        mask = (jnp.where(diff >= 0, jnp.exp(jnp.maximum(diff, 0.0) * lgf), 0.0)
                + jnp.where(diff <= 0, jnp.exp(jnp.maximum(-diff, 0.0) * lgb), 0.0))
        xi_f = jnp.exp((rowq + 1.0) * lgf)
        xi_b = jnp.exp((c - rowq) * lgb)
        for ci in range(n_sub):
            r0 = ci * c
            q = qk_ref[0, r0:r0 + c, hd * RET_QK_DIM:(hd + 1) * RET_QK_DIM]
            k = qk_ref[0, r0:r0 + c, RET_QK_W + hd * RET_QK_DIM:RET_QK_W + (hd + 1) * RET_QK_DIM]
            vv = v_ref[0, r0:r0 + c, hd * RET_V_DIM:(hd + 1) * RET_V_DIM]
            s = lax.dot_general(q, k, (((1,), (1,)), ((), ())), preferred_element_type=F32)
            intra = jnp.dot((s * mask).astype(BF16), vv, preferred_element_type=F32)
            qf = q.astype(F32)
            qx = jnp.concatenate([(qf * xi_f).astype(BF16), (qf * xi_b).astype(BF16)], axis=1)
            cross = jnp.dot(qx, s_ref[0, hd, ci], preferred_element_type=F32)
            o = _ln(intra + cross)
            o = o * gnw_ref[:, hd * RET_V_DIM:(hd + 1) * RET_V_DIM] + gnb_ref[:, hd * RET_V_DIM:(hd + 1) * RET_V_DIM]
            gate = sg_ref[0, r0:r0 + c, hd * RET_V_DIM:(hd + 1) * RET_V_DIM].astype(F32)
            ret_scr[r0:r0 + c, hd * RET_V_DIM:(hd + 1) * RET_V_DIM] = (o * gate).astype(BF16)

    for g in range(SGU_GROUPS):
        wg = sguw_ref[g]
        bg = sgub_ref[:, g:g + 1]
        for ci in range(n_sub):
            r0 = ci * c
            vsb = vs_ref[0, r0:r0 + c, g * SGU_GW:(g + 1) * SGU_GW]
            sp = jnp.dot(wg, vsb, preferred_element_type=F32) + bg
            ub = u_ref[0, r0:r0 + c, g * SGU_GW:(g + 1) * SGU_GW].astype(F32)
            sgu_scr[r0:r0 + c, g * SGU_GW:(g + 1) * SGU_GW] = (ub * sp).astype(BF16)

    pa = jnp.dot(ret_scr[...], wpa_ref[...], preferred_element_type=F32)
    pb = jnp.dot(sgu_scr[...], wpb_ref[...], preferred_element_type=F32)
    y = (ga_ref[0].astype(F32) * pa + gb_ref[0].astype(F32) * pb).astype(BF16)
    mix = jnp.dot(y, wo_ref[...], preferred_element_type=F32) + bo_ref[...]
    g1 = mod_ref[0, :, 2 * d:3 * d]
    sh2 = mod_ref[0, :, 3 * d:4 * d]
    sc2 = mod_ref[0, :, 4 * d:5 * d]
    x1 = _ln(DEEPNORM_ALPHA * x_ref[0] + g1 * mix) * ln1w_ref[...] + ln1b_ref[...]
    x1_ref[0] = x1
    hm = _ln(x1) * (1.0 + sc2) + sh2
    hm_ref[0] = hm.astype(BF16)
    lgt_ref[0] = lax.dot_general(wr_ref[...], hm, (((1,), (1,)), ((), ())),
                                 preferred_element_type=F32, precision=HIGHEST) + br_ref[...]


def _mixer_call(lg, qk, v, sg, u, vs, ga, gb, states, x, mod3, wpa, wpb, wo, sguw, sgub_t, gnw, gnb, bo,
                ln1w, ln1b, wr_t, br):
    bsz, t, d = x.shape
    tt = TT_MIXER
    n_sub = tt // RET_CHUNK
    row = lambda b, i: (b, i, 0)
    c2 = lambda b, i: (0, 0)
    c3 = lambda b, i: (0, 0, 0)
    act = pl.BlockSpec((1, tt, 1024), row)
    return pl.pallas_call(
        _mixer_kernel,
        out_shape=(jax.ShapeDtypeStruct((bsz, t, d), F32),
                   jax.ShapeDtypeStruct((bsz, t, d), BF16),
                   jax.ShapeDtypeStruct((bsz, N_EXPERTS, t), F32)),
        grid=(bsz, t // tt),
        in_specs=[
            pl.BlockSpec(memory_space=pltpu.SMEM),
            act, act, act, act, act, act, act,
            pl.BlockSpec((1, RET_HEADS, n_sub, 2 * RET_QK_DIM, RET_V_DIM), lambda b, i: (b, 0, i, 0, 0)),
            pl.BlockSpec((1, tt, d), row),
            pl.BlockSpec((1, 1, 6 * d), lambda b, i: (b, 0, 0)),
            pl.BlockSpec((RET_V_W, d), c2),
            pl.BlockSpec((SGU_WIDTH, d), c2),
            pl.BlockSpec((d, d), c2),
            pl.BlockSpec((SGU_GROUPS, SGU_CHUNK, SGU_CHUNK), c3),
            pl.BlockSpec((SGU_CHUNK, SGU_GROUPS), c2),
            pl.BlockSpec((1, RET_V_W), c2),
            pl.BlockSpec((1, RET_V_W), c2),
            pl.BlockSpec((1, d), c2),
            pl.BlockSpec((1, d), c2),
            pl.BlockSpec((1, d), c2),
            pl.BlockSpec((N_EXPERTS, d), c2),
            pl.BlockSpec((N_EXPERTS, 1), c2),
        ],
        out_specs=(pl.BlockSpec((1, tt, d), row),
                   pl.BlockSpec((1, tt, d), row),
                   pl.BlockSpec((1, N_EXPERTS, tt), lambda b, i: (b, 0, i))),
        scratch_shapes=[pltpu.VMEM((tt, RET_V_W), BF16), pltpu.VMEM((tt, SGU_WIDTH), BF16)],
        compiler_params=_cparams(("parallel", "parallel")),
        name="mixer",
    )(lg, qk, v, sg, u, vs, ga, gb, states, x, mod3, wpa, wpb, wo, sguw, sgub_t, gnw, gnb, bo,
      ln1w, ln1b, wr_t, br)


def _route_kernel(lgt_ref, pos_ref, gate_ref, *, cap):
    lg = lgt_ref[0]
    n_e, t = lg.shape
    tb = TB_ROUTE
    m = jnp.max(lg, axis=0, keepdims=True)
    ex = jnp.exp(lg - m)
    aff = ex / jnp.sum(ex, axis=0, keepdims=True)
    gate_ref[0] = aff

    def search(i, thr_bits):
        cand = thr_bits | lax.shift_left(jnp.int32(1), 30 - i)
        cnt = jnp.sum((aff >= lax.bitcast_convert_type(cand, F32)).astype(I32), axis=1, keepdims=True)
        return jnp.where(cnt >= cap, cand, thr_bits)

    thr_bits = lax.fori_loop(0, 31, search, jnp.zeros((n_e, 1), I32))
    floor_f = lax.bitcast_convert_type(thr_bits, F32)
    thr = jnp.min(jnp.where(aff >= floor_f, aff, jnp.inf), axis=1, keepdims=True)
    need = (cap - jnp.sum((aff > thr).astype(I32), axis=1, keepdims=True)).astype(F32)

    r = lax.broadcasted_iota(I32, (tb, tb), 0)
    cc = lax.broadcasted_iota(I32, (tb, tb), 1)
    tri = (r <= cc).astype(BF16)
    carry_eq = jnp.zeros((n_e, 1), F32)
    carry_sel = jnp.zeros((n_e, 1), F32)
    for blk in range(t // tb):
        sl = slice(blk * tb, (blk + 1) * tb)
        aff_b = aff[:, sl]
        eq = aff_b == thr
        eq_b = eq.astype(BF16)
        inc_eq = jnp.dot(eq_b, tri, preferred_element_type=F32)
        before = carry_eq + inc_eq - eq_b.astype(F32)
        sel = (aff_b > thr) | (eq & (before < need))
        sel_b = sel.astype(BF16)
        inc_sel = jnp.dot(sel_b, tri, preferred_element_type=F32)
        pos = carry_sel + inc_sel - 1.0
        pos_ref[0, :, sl] = jnp.where(sel, pos.astype(I32), -1)
        carry_eq = carry_eq + inc_eq[:, tb - 1:tb]
        carry_sel = carry_sel + inc_sel[:, tb - 1:tb]


def _route_call(logits_t, cap):
    bsz, n_e, t = logits_t.shape
    blk = pl.BlockSpec((1, n_e, t), lambda b: (b, 0, 0))
    return pl.pallas_call(
        functools.partial(_route_kernel, cap=cap),
        out_shape=(jax.ShapeDtypeStruct((bsz, n_e, t), I32),
                   jax.ShapeDtypeStruct((bsz, n_e, t), F32)),
        grid=(bsz,),
        in_specs=[blk],
        out_specs=(blk, blk),
        compiler_params=_cparams(("parallel",)),
        name="route",
    )(logits_t)


def _gather_kernel(lo_ref, hi_ref, hm_ref, pos_ref, gate_ref, xs_ref, gc_ref, acc_scr, g_scr):
    b = pl.program_id(0)
    e = pl.program_id(1)
    n_e = pl.num_programs(1)
    rt = RT_GATHER
    tb = TB_ROUTE
    n_tiles = xs_ref.shape[2] // rt
    rows0 = lax.broadcasted_iota(I32, (rt, tb), 0)
    for j in range(n_tiles):
        base = (b * n_e + e) * n_tiles + j
        rows = rows0 + j * rt
        acc_scr[...] = jnp.zeros_like(acc_scr)
        g_scr[...] = jnp.zeros_like(g_scr)

        def body(kb, carry):
            match = rows == pos_ref[0, 0, pl.ds(kb, 1), :]
            hb = hm_ref[0, pl.ds(pl.multiple_of(kb * tb, tb), tb), :]
            acc_scr[...] += jnp.dot(match.astype(BF16), hb, preferred_element_type=F32)
            g_scr[...] += jnp.sum(jnp.where(match, gate_ref[0, 0, pl.ds(kb, 1), :], 0.0), axis=1, keepdims=True)
            return carry

        lax.fori_loop(lo_ref[base], hi_ref[base] + 1, body, 0)
        xs_ref[0, 0, j * rt:(j + 1) * rt, :] = acc_scr[...].astype(BF16)
        gc_ref[0, 0, j * rt:(j + 1) * rt, :] = g_scr[...]


def _gather_call(lo, hi, hm, pos4, gate4, cap):
    bsz, t, d = hm.shape
    n_e = pos4.shape[1]
    nblk, tb = pos4.shape[2], pos4.shape[3]
    grid_spec = pltpu.PrefetchScalarGridSpec(
        num_scalar_prefetch=2,
        grid=(bsz, n_e),
        in_specs=[
            pl.BlockSpec((1, t, d), lambda b, e, lo, hi: (b, 0, 0)),
            pl.BlockSpec((1, 1, nblk, tb), lambda b, e, lo, hi: (b, e, 0, 0)),
            pl.BlockSpec((1, 1, nblk, tb), lambda b, e, lo, hi: (b, e, 0, 0)),
        ],
        out_specs=(pl.BlockSpec((1, 1, cap, d), lambda b, e, lo, hi: (b, e, 0, 0)),
                   pl.BlockSpec((1, 1, cap, 1), lambda b, e, lo, hi: (b, e, 0, 0))),
        scratch_shapes=[pltpu.VMEM((RT_GATHER, d), F32), pltpu.VMEM((RT_GATHER, 1), F32)],
    )
    return pl.pallas_call(
        _gather_kernel,
        out_shape=(jax.ShapeDtypeStruct((bsz, n_e, cap, d), BF16),
                   jax.ShapeDtypeStruct((bsz, n_e, cap, 1), F32)),
        grid_spec=grid_spec,
        compiler_params=_cparams(("parallel", "parallel")),
        name="gather",
    )(lo, hi, hm, pos4, gate4)


def _ffn_kernel(xs_ref, gc_ref, wg_ref, wu_ref, wd_ref, ye_ref, wg_s, wu_s, wd_s):
    @pl.when(pl.program_id(1) == 0)
    def _():
        wg_s[...] = wg_ref[0].astype(BF16)
        wu_s[...] = wu_ref[0].astype(BF16)
        wd_s[...] = wd_ref[0].astype(BF16)

    rt = 256
    for j in range(xs_ref.shape[2] // rt):
        sl = slice(j * rt, (j + 1) * rt)
        xs = xs_ref[0, 0, sl, :]
        xg = jnp.dot(xs, wg_s[...], preferred_element_type=F32)
        xu = jnp.dot(xs, wu_s[...], preferred_element_type=F32)
        hid = (jax.nn.silu(xg) * xu).astype(BF16)
        ye = jnp.dot(hid, wd_s[...], preferred_element_type=F32) * gc_ref[0, 0, sl, :]
        ye_ref[0, 0, sl, :] = ye.astype(BF16)


def _ffn_call(xs, gc, w_gate, w_up, w_down):
    bsz, n_e, cap, d = xs.shape
    f = w_gate.shape[2]
    return pl.pallas_call(
        _ffn_kernel,
        out_shape=jax.ShapeDtypeStruct((bsz, n_e, cap, d), BF16),
        grid=(n_e, bsz),
        in_specs=[
            pl.BlockSpec((1, 1, cap, d), lambda e, b: (b, e, 0, 0)),
            pl.BlockSpec((1, 1, cap, 1), lambda e, b: (b, e, 0, 0)),
            pl.BlockSpec((1, d, f), lambda e, b: (e, 0, 0)),
            pl.BlockSpec((1, d, f), lambda e, b: (e, 0, 0)),
            pl.BlockSpec((1, f, d), lambda e, b: (e, 0, 0)),
        ],
        out_specs=pl.BlockSpec((1, 1, cap, d), lambda e, b: (b, e, 0, 0)),
        scratch_shapes=[pltpu.VMEM((d, f), BF16), pltpu.VMEM((d, f), BF16), pltpu.VMEM((f, d), BF16)],
        compiler_params=_cparams(("arbitrary", "arbitrary")),
        name="ffn",
    )(xs, gc, w_gate, w_up, w_down)


def _combine_kernel(lo_ref, hi_ref, ye_ref, pos_ref, out_ref):
    b = pl.program_id(0)
    half = pl.program_id(1)
    e = pl.program_id(2)
    n_e = pl.num_programs(2)
    rt = RT_SCATTER
    tb = TB_ROUTE
    n_tiles = ye_ref.shape[2] // rt
    hblk = out_ref.shape[1] // tb

    @pl.when(e == 0)
    def _():
        out_ref[...] = jnp.zeros_like(out_ref)

    rows0 = lax.broadcasted_iota(I32, (rt, tb), 0)
    for j in range(n_tiles):
        base = (b * n_e + e) * n_tiles + j
        rows = rows0 + j * rt
        lo = jnp.maximum(lo_ref[base], half * hblk)
        hi = jnp.minimum(hi_ref[base], half * hblk + hblk - 1)

        def body(kb, carry):
            match = rows == pos_ref[0, 0, pl.ds(kb, 1), :]
            contrib = _dot_t0(match.astype(BF16), ye_ref[0, 0, j * rt:(j + 1) * rt, :])
            off = pl.multiple_of((kb - half * hblk) * tb, tb)
            out_ref[0, pl.ds(off, tb), :] += contrib
            return carry

        lax.fori_loop(lo, hi + 1, body, 0)


def _combine_call(lo, hi, ye, pos4, t):
    bsz, n_e, cap, d = ye.shape
    nblk, tb = pos4.shape[2], pos4.shape[3]
    n_half = 2
    grid_spec = pltpu.PrefetchScalarGridSpec(
        num_scalar_prefetch=2,
        grid=(bsz, n_half, n_e),
        in_specs=[
            pl.BlockSpec((1, 1, cap, d), lambda b, h, e, lo, hi: (b, e, 0, 0)),
            pl.BlockSpec((1, 1, nblk, tb), lambda b, h, e, lo, hi: (b, e, 0, 0)),
        ],
        out_specs=pl.BlockSpec((1, t // n_half, d), lambda b, h, e, lo, hi: (b, h, 0)),
    )
    return pl.pallas_call(
        _combine_kernel,
        out_shape=jax.ShapeDtypeStruct((bsz, t, d), F32),
        grid_spec=grid_spec,
        compiler_params=_cparams(("parallel", "parallel", "arbitrary")),
        name="combine",
    )(lo, hi, ye, pos4)


def _final_kernel(x1_ref, moe_ref, mod_ref, w_ref, b_ref, o_ref):
    d = D_MODEL
    g2 = mod_ref[0, :, 5 * d:6 * d]
    o_ref[0] = _ln(DEEPNORM_ALPHA * x1_ref[0] + g2 * moe_ref[0]) * w_ref[...] + b_ref[...]


def _final_call(x1, moe, mod3, ln2w, ln2b):
    bsz, t, d = x1.shape
    tm = TM_FINAL
    row = lambda b, i: (b, i, 0)
    return pl.pallas_call(
        _final_kernel,
        out_shape=jax.ShapeDtypeStruct((bsz, t, d), F32),
        grid=(bsz, t // tm),
        in_specs=[
            pl.BlockSpec((1, tm, d), row),
            pl.BlockSpec((1, tm, d), row),
            pl.BlockSpec((1, 1, 6 * d), lambda b, i: (b, 0, 0)),
            pl.BlockSpec((1, d), lambda b, i: (0, 0)),
            pl.BlockSpec((1, d), lambda b, i: (0, 0)),
        ],
        out_specs=pl.BlockSpec((1, tm, d), row),
        compiler_params=_cparams(("parallel", "parallel")),
        name="final",
    )(x1, moe, mod3, ln2w, ln2b)


def _rope_tables(n_tokens):
    rows = n_tokens // GRID_W
    r, col = jnp.meshgrid(jnp.arange(rows), jnp.arange(GRID_W), indexing="ij")
    n_freq = RET_QK_DIM // 4
    inv = ROPE_THETA ** (-jnp.arange(n_freq, dtype=F32) / n_freq)
    pos = jnp.stack([r.reshape(-1), col.reshape(-1)], axis=-1).astype(F32)
    ang = pos[..., None] * inv
    cos, sin = jnp.cos(ang), jnp.sin(ang)
    cos_t = jnp.concatenate([cos[:, 0], cos[:, 0], cos[:, 1], cos[:, 1]], axis=-1)
    sin_t = jnp.concatenate([-sin[:, 0], sin[:, 0], -sin[:, 1], sin[:, 1]], axis=-1)
    return cos_t, sin_t


def _tile_block_ranges(pos, rows_per_tile, cap, tb):
    starts = jnp.arange(0, cap, rows_per_tile, dtype=I32)
    first_tok = jnp.argmax(pos[:, :, None, :] == starts[None, None, :, None], axis=-1)
    last_tok = jnp.argmax(pos[:, :, None, :] == (starts + rows_per_tile - 1)[None, None, :, None], axis=-1)
    return (first_tok // tb).astype(I32).reshape(-1), (last_tok // tb).astype(I32).reshape(-1)


def kernel(x, c, ctx, c_ctx, w_ada, b_ada, w_in, b_in, ret_decay_f, ret_decay_b, ret_gn_w, ret_gn_b,
           sgu_ln_w, sgu_ln_b, sgu_w, sgu_b, w_pa, w_pb, w_o, b_o, ln1_w, ln1_b,
           w_router, b_router, w_gate, w_up, w_down, ln2_w, ln2_b):
    bsz, t, d = x.shape
    assert d == D_MODEL and w_ada.shape[0] == DEPTH == 1 and bsz < MOD_ROWS
    cap = EC_CAPACITY_FACTOR * t // N_EXPERTS
    assert t % (2 * TB_ROUTE) == 0 and cap % RT_SCATTER == 0 and t % TM_FINAL == 0
    l = 0

    cvec = jnp.zeros((MOD_ROWS, d), F32).at[:bsz].set(c).at[bsz].set(c_ctx)
    mod = _mod_call(cvec, w_ada[l], b_ada[l][None])
    mod3 = mod[:, None, :]

    w_in_b = w_in[l].astype(BF16)
    b_in2 = b_in[l][None]
    lo_kv, hi_kv = RET_QK_W, 2 * RET_QK_W + RET_V_W
    kctx, vctx = _ctx_kv_call(ctx, mod3, w_in_b[:, lo_kv:hi_kv], b_in2[:, lo_kv:hi_kv], bsz)

    cos_t, sin_t = _rope_tables(t)
    qk, v, sg, u, vs, ga, gb = _inproj_call(x, mod3, w_in_b, b_in2, cos_t, sin_t,
                                            sgu_ln_w[l][None], sgu_ln_b[l][None])

    lg = jnp.stack([jax.nn.log_sigmoid(ret_decay_f[l].astype(F32)),
                    jax.nn.log_sigmoid(ret_decay_b[l].astype(F32))])
    states = _state_call(lg, qk, v, kctx, vctx)

    x1, hm, logits_t = _mixer_call(
        lg, qk, v, sg, u, vs, ga, gb, states, x, mod3,
        w_pa[l].astype(BF16), w_pb[l].astype(BF16), w_o[l].astype(BF16),
        sgu_w[l].astype(BF16), sgu_b[l].T, ret_gn_w[l][None], ret_gn_b[l][None], b_o[l][None],
        ln1_w[l][None], ln1_b[l][None], w_router[l].T, b_router[l][:, None])

    pos, gate = _route_call(logits_t, cap)
    nblk = t // TB_ROUTE
    pos4 = pos.reshape(bsz, N_EXPERTS, nblk, TB_ROUTE)
    gate4 = gate.reshape(bsz, N_EXPERTS, nblk, TB_ROUTE)
    lo_g, hi_g = _tile_block_ranges(pos, RT_GATHER, cap, TB_ROUTE)
    lo_s, hi_s = _tile_block_ranges(pos, RT_SCATTER, cap, TB_ROUTE)

    xs, gc = _gather_call(lo_g, hi_g, hm, pos4, gate4, cap)
    ye = _ffn_call(xs, gc, w_gate[l], w_up[l], w_down[l])
    moe = _combine_call(lo_s, hi_s, ye, pos4, t)
    return _final_call(x1, moe, mod3, ln2_w[l][None], ln2_b[l][None])
```

```python
import functools
import math

import jax
import jax.numpy as jnp
from jax import lax
from jax.experimental import pallas as pl
from jax.experimental.pallas import tpu as pltpu

F32 = jnp.float32
BF16 = jnp.bfloat16
I32 = jnp.int32
HIGHEST = lax.Precision.HIGHEST

D_MODEL = 1024
DEPTH = 1
GRID_W = 64
RET_HEADS = 4
RET_QK_DIM = 128
RET_V_DIM = 256
RET_CHUNK = 128
ROPE_THETA = 10000.0
RET_QK_W = RET_HEADS * RET_QK_DIM
RET_V_W = RET_HEADS * RET_V_DIM
SGU_GROUPS = 4
SGU_CHUNK = 128
SGU_WIDTH = 1024
SGU_GW = SGU_WIDTH // SGU_GROUPS
N_EXPERTS = 16
EC_CAPACITY_FACTOR = 2
D_EXPERT = 1024
LN_EPS = 1e-6
DEEPNORM_ALPHA = (2.0 * DEPTH) ** 0.25
D_IN = 2 * RET_QK_W + 2 * RET_V_W + 2 * SGU_WIDTH + 2 * D_MODEL

V7X_VMEM_BYTES = 64 * 1024 * 1024
VMEM_LIMIT = 56 * 1024 * 1024
LANES = 128

MOD_ROWS = 8
TM_INPROJ = 256
TT_MIXER = 256
TB_ROUTE = 256
RT_GATHER = 128
RT_SCATTER = 256
TM_FINAL = 512
STATE_UNROLL = 4


def _cparams(sem):
    return pltpu.CompilerParams(dimension_semantics=sem, vmem_limit_bytes=VMEM_LIMIT)


def _ln(x):
    mu = jnp.mean(x, axis=-1, keepdims=True)
    xc = x - mu
    var = jnp.mean(xc * xc, axis=-1, keepdims=True)
    return xc * lax.rsqrt(var + LN_EPS)


def _iota_f(shape, dim):
    return lax.broadcasted_iota(I32, shape, dim).astype(F32)


def _mod_kernel(c_ref, w_ref, b_ref, o_ref):
    c = c_ref[...]
    a = c * jax.nn.sigmoid(c)
    o_ref[...] = jnp.dot(a, w_ref[...], preferred_element_type=F32, precision=HIGHEST) + b_ref[...]


def _mod_call(cvec, w_ada, b_ada):
    d, n = w_ada.shape
    tn = 1024
    return pl.pallas_call(
        _mod_kernel,
        out_shape=jax.ShapeDtypeStruct((MOD_ROWS, n), F32),
        grid=(n // tn,),
        in_specs=[
            pl.BlockSpec((MOD_ROWS, d), lambda j: (0, 0)),
            pl.BlockSpec((d, tn), lambda j: (0, j)),
            pl.BlockSpec((1, tn), lambda j: (0, j)),
        ],
        out_specs=pl.BlockSpec((MOD_ROWS, tn), lambda j: (0, j)),
        compiler_params=_cparams(("parallel",)),
        name="mod",
    )(cvec, w_ada, b_ada)


def _ctx_kv_kernel(x_ref, mod_ref, w_ref, b_ref, k_ref, v_ref):
    d = D_MODEL
    sh = mod_ref[0, :, 0:d]
    sc = mod_ref[0, :, d:2 * d]
    h = (_ln(x_ref[0]) * (1.0 + sc) + sh).astype(BF16)
    z = jnp.dot(h, w_ref[...], preferred_element_type=F32) + b_ref[...]
    k_ref[0] = (z[:, :RET_QK_W] * (RET_QK_DIM ** -0.5)).astype(BF16)
    v_ref[0] = z[:, RET_QK_W:].astype(BF16)


def _ctx_kv_call(ctx, mod3, w_kv, b_kv, ctx_row):
    bsz, n_ctx, d = ctx.shape
    wkv = w_kv.shape[1]
    return pl.pallas_call(
        _ctx_kv_kernel,
        out_shape=(jax.ShapeDtypeStruct((bsz, n_ctx, RET_QK_W), BF16),
                   jax.ShapeDtypeStruct((bsz, n_ctx, RET_V_W), BF16)),
        grid=(bsz,),
        in_specs=[
            pl.BlockSpec((1, n_ctx, d), lambda b: (b, 0, 0)),
            pl.BlockSpec((1, 1, 6 * d), lambda b: (ctx_row, 0, 0)),
            pl.BlockSpec((d, wkv), lambda b: (0, 0)),
            pl.BlockSpec((1, wkv), lambda b: (0, 0)),
        ],
        out_specs=(pl.BlockSpec((1, n_ctx, RET_QK_W), lambda b: (b, 0, 0)),
                   pl.BlockSpec((1, n_ctx, RET_V_W), lambda b: (b, 0, 0))),
        compiler_params=_cparams(("parallel",)),
        name="ctx_kv",
    )(ctx, mod3, w_kv, b_kv)


def _inproj_kernel(x_ref, mod_ref, w_ref, b_ref, cos_ref, sin_ref, lnw_ref, lnb_ref,
                   qk_ref, v_ref, sg_ref, u_ref, vs_ref, ga_ref, gb_ref):
    d = D_MODEL
    sh = mod_ref[0, :, 0:d]
    sc = mod_ref[0, :, d:2 * d]
    h = (_ln(x_ref[0]) * (1.0 + sc) + sh).astype(BF16)

    def proj(c0, width):
        return jnp.dot(h, w_ref[:, c0:c0 + width], preferred_element_type=F32) + b_ref[:, c0:c0 + width]

    zq = proj(0, 2 * RET_QK_W)
    cos = cos_ref[...]
    sin = sin_ref[...]
    tm = zq.shape[0]
    upper = (lax.broadcasted_iota(I32, (tm, LANES), 1) & 32) != 0
    for hb in range(2 * RET_HEADS):
        zs = zq[:, hb * LANES:(hb + 1) * LANES]
        if hb >= RET_HEADS:
            zs = zs * (RET_QK_DIM ** -0.5)
        sw = jnp.where(upper, pltpu.roll(zs, 32, 1), pltpu.roll(zs, LANES - 32, 1))
        qk_ref[0, :, hb * LANES:(hb + 1) * LANES] = (zs * cos + sw * sin).astype(BF16)

    c0 = 2 * RET_QK_W
    v_ref[0] = proj(c0, RET_V_W).astype(BF16)
    c0 += RET_V_W
    sg_ref[0] = jax.nn.silu(proj(c0, RET_V_W)).astype(BF16)
    c0 += RET_V_W
    u_ref[0] = jax.nn.gelu(proj(c0, SGU_WIDTH)).astype(BF16)
    c0 += SGU_WIDTH
    vs = _ln(jax.nn.gelu(proj(c0, SGU_WIDTH))) * lnw_ref[...] + lnb_ref[...]
    vs_ref[0] = vs.astype(BF16)
    c0 += SGU_WIDTH
    ga_ref[0] = jax.nn.sigmoid(proj(c0, d)).astype(BF16)
    c0 += d
    gb_ref[0] = jax.nn.sigmoid(proj(c0, d)).astype(BF16)


def _inproj_call(x, mod3, w_in, b_in, cos_t, sin_t, sgu_ln_w, sgu_ln_b):
    bsz, t, d = x.shape
    tm = TM_INPROJ
    row = lambda b, i: (b, i, 0)
    const2 = lambda b, i: (0, 0)
    outs = tuple(jax.ShapeDtypeStruct((bsz, t, 1024), BF16) for _ in range(7))
    return pl.pallas_call(
        _inproj_kernel,
        out_shape=outs,
        grid=(bsz, t // tm),
        in_specs=[
            pl.BlockSpec((1, tm, d), row),
            pl.BlockSpec((1, 1, 6 * d), lambda b, i: (b, 0, 0)),
            pl.BlockSpec((d, D_IN), const2),
            pl.BlockSpec((1, D_IN), const2),
            pl.BlockSpec((tm, LANES), lambda b, i: (i, 0)),
            pl.BlockSpec((tm, LANES), lambda b, i: (i, 0)),
            pl.BlockSpec((1, SGU_WIDTH), const2),
            pl.BlockSpec((1, SGU_WIDTH), const2),
        ],
        out_specs=tuple(pl.BlockSpec((1, tm, 1024), row) for _ in range(7)),
        compiler_params=_cparams(("parallel", "parallel")),
        name="inproj",
    )(x, mod3, w_in, b_in, cos_t, sin_t, sgu_ln_w, sgu_ln_b)


def _dot_t0(a, b):
    return lax.dot_general(a, b, (((0,), (0,)), ((), ())), preferred_element_type=F32)


def _state_kernel(lg_ref, k_ref, v_ref, kc_ref, vc_ref, s_ref):
    hd = pl.program_id(1)
    lgf = lg_ref[0, hd]
    lgb = lg_ref[1, hd]
    n_chunks = s_ref.shape[2]
    n_ctx = kc_ref.shape[1]
    c = RET_CHUNK
    li = _iota_f((c, RET_QK_DIM), 0)
    zeta_f = jnp.exp((c - 1.0 - li) * lgf)
    zeta_b = jnp.exp(li * lgb)
    one = jnp.ones((1, 1), F32)
    cd_f = jnp.exp(one * (c * lgf))
    cd_b = jnp.exp(one * (c * lgb))

    tc = _iota_f((n_ctx, RET_QK_DIM), 0)
    kc = kc_ref[0].astype(F32)
    vc = vc_ref[0]
    s0f = _dot_t0((kc * jnp.exp((n_ctx - 1.0 - tc) * lgf)).astype(BF16), vc)
    s0b = _dot_t0((kc * jnp.exp(tc * lgb)).astype(BF16), vc)

    def chunk_kv(n, zeta):
        off = pl.multiple_of(n * c, c)
        kk = k_ref[0, pl.ds(off, c), :].astype(F32)
        return _dot_t0((kk * zeta).astype(BF16), v_ref[0, pl.ds(off, c), :])

    def step(i, carry):
        sf, sb = carry
        nb = n_chunks - 1 - i
        s_ref[0, 0, i, 0:RET_QK_DIM, :] = sf.astype(BF16)
        s_ref[0, 0, nb, RET_QK_DIM:2 * RET_QK_DIM, :] = sb.astype(BF16)
        return cd_f * sf + chunk_kv(i, zeta_f), cd_b * sb + chunk_kv(nb, zeta_b)

    lax.fori_loop(0, n_chunks, step, (s0f, s0b), unroll=STATE_UNROLL)


def _state_call(lg, qk, v, kctx, vctx):
    bsz, t, _ = qk.shape
    n_ctx = kctx.shape[1]
    n_chunks = t // RET_CHUNK
    return pl.pallas_call(
        _state_kernel,
        out_shape=jax.ShapeDtypeStruct((bsz, RET_HEADS, n_chunks, 2 * RET_QK_DIM, RET_V_DIM), BF16),
        grid=(bsz, RET_HEADS),
        in_specs=[
            pl.BlockSpec(memory_space=pltpu.SMEM),
            pl.BlockSpec((1, t, RET_QK_DIM), lambda b, h: (b, 0, RET_HEADS + h)),
            pl.BlockSpec((1, t, RET_V_DIM), lambda b, h: (b, 0, h)),
            pl.BlockSpec((1, n_ctx, RET_QK_DIM), lambda b, h: (b, 0, h)),
            pl.BlockSpec((1, n_ctx, RET_V_DIM), lambda b, h: (b, 0, h)),
        ],
        out_specs=pl.BlockSpec((1, 1, n_chunks, 2 * RET_QK_DIM, RET_V_DIM), lambda b, h: (b, h, 0, 0, 0)),
        compiler_params=_cparams(("parallel", "parallel")),
        name="states",
    )(lg, qk, v, kctx, vctx)


def _mixer_kernel(lg_ref, qk_ref, v_ref, sg_ref, u_ref, vs_ref, ga_ref, gb_ref, s_ref, x_ref, mod_ref,
                  wpa_ref, wpb_ref, wo_ref, sguw_ref, sgub_ref, gnw_ref, gnb_ref, bo_ref,
                  ln1w_ref, ln1b_ref, wr_ref, br_ref,
                  x1_ref, hm_ref, lgt_ref, ret_scr, sgu_scr):
    d = D_MODEL
    c = RET_CHUNK
    n_sub = qk_ref.shape[1] // c
    row = lax.broadcasted_iota(I32, (c, c), 0)
    col = lax.broadcasted_iota(I32, (c, c), 1)
    diff = (row - col).astype(F32)
    rowq = _iota_f((c, RET_QK_DIM), 0)

    for hd in range(RET_HEADS):
        lgf = lg_ref[0, hd]
        lgb = lg_ref[1, hd]
        mask = (jnp.where(diff >= 0, jnp.exp(jnp.maximum(diff, 0.0) * lgf), 0.0)
                + jnp.where(diff <= 0, jnp.exp(jnp.maximum(-diff, 0.0) * lgb), 0.0))
        xi_f = jnp.exp((rowq + 1.0) * lgf)
        xi_b = jnp.exp((c - rowq) * lgb)
        for ci in range(n_sub):
            r0 = ci * c
            q = qk_ref[0, r0:r0 + c, hd * RET_QK_DIM:(hd + 1) * RET_QK_DIM]
            k = qk_ref[0, r0:r0 + c, RET_QK_W + hd * RET_QK_DIM:RET_QK_W + (hd + 1) * RET_QK_DIM]
            vv = v_ref[0, r0:r0 + c, hd * RET_V_DIM:(hd + 1) * RET_V_DIM]
            s = lax.dot_general(q, k, (((1,), (1,)), ((), ())), preferred_element_type=F32)
            intra = jnp.dot((s * mask).astype(BF16), vv, preferred_element_type=F32)
            qf = q.astype(F32)
            qx = jnp.concatenate([(qf * xi_f).astype(BF16), (qf * xi_b).astype(BF16)], axis=1)
            cross = jnp.dot(qx, s_ref[0, hd, ci], preferred_element_type=F32)
            o = _ln(intra + cross)
            o = o * gnw_ref[:, hd * RET_V_DIM:(hd + 1) * RET_V_DIM] + gnb_ref[:, hd * RET_V_DIM:(hd + 1) * RET_V_DIM]
            gate = sg_ref[0, r0:r0 + c, hd * RET_V_DIM:(hd + 1) * RET_V_DIM].astype(F32)
            ret_scr[r0:r0 + c, hd * RET_V_DIM:(hd + 1) * RET_V_DIM] = (o * gate).astype(BF16)

    for g in range(SGU_GROUPS):
        wg = sguw_ref[g]
        bg = sgub_ref[:, g:g + 1]
        for ci in range(n_sub):
            r0 = ci * c
            vsb = vs_ref[0, r0:r0 + c, g * SGU_GW:(g + 1) * SGU_GW]
            sp = jnp.dot(wg, vsb, preferred_element_type=F32) + bg
            ub = u_ref[0, r0:r0 + c, g * SGU_GW:(g + 1) * SGU_GW].astype(F32)
            sgu_scr[r0:r0 + c, g * SGU_GW:(g + 1) * SGU_GW] = (ub * sp).astype(BF16)

    pa = jnp.dot(ret_scr[...], wpa_ref[...], preferred_element_type=F32)
    pb = jnp.dot(sgu_scr[...], wpb_ref[...], preferred_element_type=F32)
    y = (ga_ref[0].astype(F32) * pa + gb_ref[0].astype(F32) * pb).astype(BF16)
    mix = jnp.dot(y, wo_ref[...], preferred_element_type=F32) + bo_ref[...]
    g1 = mod_ref[0, :, 2 * d:3 * d]
    sh2 = mod_ref[0, :, 3 * d:4 * d]
    sc2 = mod_ref[0, :, 4 * d:5 * d]
    x1 = _ln(DEEPNORM_ALPHA * x_ref[0] + g1 * mix) * ln1w_ref[...] + ln1b_ref[...]
    x1_ref[0] = x1
    hm = _ln(x1) * (1.0 + sc2) + sh2
    hm_ref[0] = hm.astype(BF16)
    lgt_ref[0] = lax.dot_general(wr_ref[...], hm, (((1,), (1,)), ((), ())),
                                 preferred_element_type=F32, precision=HIGHEST) + br_ref[...]


def _mixer_call(lg, qk, v, sg, u, vs, ga, gb, states, x, mod3, wpa, wpb, wo, sguw, sgub_t, gnw, gnb, bo,
                ln1w, ln1b, wr_t, br):
    bsz, t, d = x.shape
    tt = TT_MIXER
    n_sub = tt // RET_CHUNK
    row = lambda b, i: (b, i, 0)
    c2 = lambda b, i: (0, 0)
    c3 = lambda b, i: (0, 0, 0)
    act = pl.BlockSpec((1, tt, 1024), row)
    return pl.pallas_call(
        _mixer_kernel,
        out_shape=(jax.ShapeDtypeStruct((bsz, t, d), F32),
                   jax.ShapeDtypeStruct((bsz, t, d), BF16),
                   jax.ShapeDtypeStruct((bsz, N_EXPERTS, t), F32)),
        grid=(bsz, t // tt),
        in_specs=[
            pl.BlockSpec(memory_space=pltpu.SMEM),
            act, act, act, act, act, act, act,
            pl.BlockSpec((1, RET_HEADS, n_sub, 2 * RET_QK_DIM, RET_V_DIM), lambda b, i: (b, 0, i, 0, 0)),
            pl.BlockSpec((1, tt, d), row),
            pl.BlockSpec((1, 1, 6 * d), lambda b, i: (b, 0, 0)),
            pl.BlockSpec((RET_V_W, d), c2),
            pl.BlockSpec((SGU_WIDTH, d), c2),
            pl.BlockSpec((d, d), c2),
            pl.BlockSpec((SGU_GROUPS, SGU_CHUNK, SGU_CHUNK), c3),
            pl.BlockSpec((SGU_CHUNK, SGU_GROUPS), c2),
            pl.BlockSpec((1, RET_V_W), c2),
            pl.BlockSpec((1, RET_V_W), c2),
            pl.BlockSpec((1, d), c2),
            pl.BlockSpec((1, d), c2),
            pl.BlockSpec((1, d), c2),
            pl.BlockSpec((N_EXPERTS, d), c2),
            pl.BlockSpec((N_EXPERTS, 1), c2),
        ],
        out_specs=(pl.BlockSpec((1, tt, d), row),
                   pl.BlockSpec((1, tt, d), row),
                   pl.BlockSpec((1, N_EXPERTS, tt), lambda b, i: (b, 0, i))),
        scratch_shapes=[pltpu.VMEM((tt, RET_V_W), BF16), pltpu.VMEM((tt, SGU_WIDTH), BF16)],
        compiler_params=_cparams(("parallel", "parallel")),
        name="mixer",
    )(lg, qk, v, sg, u, vs, ga, gb, states, x, mod3, wpa, wpb, wo, sguw, sgub_t, gnw, gnb, bo,
      ln1w, ln1b, wr_t, br)


def _route_kernel(lgt_ref, pos_ref, gate_ref, *, cap):
    lg = lgt_ref[0]
    n_e, t = lg.shape
    tb = TB_ROUTE
    m = jnp.max(lg, axis=0, keepdims=True)
    ex = jnp.exp(lg - m)
    aff = ex / jnp.sum(ex, axis=0, keepdims=True)
    gate_ref[0] = aff

    def search(i, thr_bits):
        cand = thr_bits | lax.shift_left(jnp.int32(1), 30 - i)
        cnt = jnp.sum((aff >= lax.bitcast_convert_type(cand, F32)).astype(I32), axis=1, keepdims=True)
        return jnp.where(cnt >= cap, cand, thr_bits)

    thr_bits = lax.fori_loop(0, 31, search, jnp.zeros((n_e, 1), I32))
    floor_f = lax.bitcast_convert_type(thr_bits, F32)
    thr = jnp.min(jnp.where(aff >= floor_f, aff, jnp.inf), axis=1, keepdims=True)
    need = (cap - jnp.sum((aff > thr).astype(I32), axis=1, keepdims=True)).astype(F32)

    r = lax.broadcasted_iota(I32, (tb, tb), 0)
    cc = lax.broadcasted_iota(I32, (tb, tb), 1)
    tri = (r <= cc).astype(BF16)
    carry_eq = jnp.zeros((n_e, 1), F32)
    carry_sel = jnp.zeros((n_e, 1), F32)
    for blk in range(t // tb):
        sl = slice(blk * tb, (blk + 1) * tb)
        aff_b = aff[:, sl]
        eq = aff_b == thr
        eq_b = eq.astype(BF16)
        inc_eq = jnp.dot(eq_b, tri, preferred_element_type=F32)
        before = carry_eq + inc_eq - eq_b.astype(F32)
        sel = (aff_b > thr) | (eq & (before < need))
        sel_b = sel.astype(BF16)
        inc_sel = jnp.dot(sel_b, tri, preferred_element_type=F32)
        pos = carry_sel + inc_sel - 1.0
        pos_ref[0, :, sl] = jnp.where(sel, pos.astype(I32), -1)
        carry_eq = carry_eq + inc_eq[:, tb - 1:tb]
        carry_sel = carry_sel + inc_sel[:, tb - 1:tb]


def _route_call(logits_t, cap):
    bsz, n_e, t = logits_t.shape
    blk = pl.BlockSpec((1, n_e, t), lambda b: (b, 0, 0))
    return pl.pallas_call(
        functools.partial(_route_kernel, cap=cap),
        out_shape=(jax.ShapeDtypeStruct((bsz, n_e, t), I32),
                   jax.ShapeDtypeStruct((bsz, n_e, t), F32)),
        grid=(bsz,),
        in_specs=[blk],
        out_specs=(blk, blk),
        compiler_params=_cparams(("parallel",)),
        name="route",
    )(logits_t)


def _gather_kernel(lo_ref, hi_ref, hm_ref, pos_ref, gate_ref, xs_ref, gc_ref, acc_scr, g_scr):
    b = pl.program_id(0)
    e = pl.program_id(1)
    n_e = pl.num_programs(1)
    rt = RT_GATHER
    tb = TB_ROUTE
    n_tiles = xs_ref.shape[2] // rt
    rows0 = lax.broadcasted_iota(I32, (rt, tb), 0)
    for j in range(n_tiles):
        base = (b * n_e + e) * n_tiles + j
        rows = rows0 + j * rt
        acc_scr[...] = jnp.zeros_like(acc_scr)
        g_scr[...] = jnp.zeros_like(g_scr)

        def body(kb, carry):
            match = rows == pos_ref[0, 0, pl.ds(kb, 1), :]
            hb = hm_ref[0, pl.ds(pl.multiple_of(kb * tb, tb), tb), :]
            acc_scr[...] += jnp.dot(match.astype(BF16), hb, preferred_element_type=F32)
            g_scr[...] += jnp.sum(jnp.where(match, gate_ref[0, 0, pl.ds(kb, 1), :], 0.0), axis=1, keepdims=True)
            return carry

        lax.fori_loop(lo_ref[base], hi_ref[base] + 1, body, 0)
        xs_ref[0, 0, j * rt:(j + 1) * rt, :] = acc_scr[...].astype(BF16)
        gc_ref[0, 0, j * rt:(j + 1) * rt, :] = g_scr[...]


def _gather_call(lo, hi, hm, pos4, gate4, cap):
    bsz, t, d = hm.shape
    n_e = pos4.shape[1]
    nblk, tb = pos4.shape[2], pos4.shape[3]
    grid_spec = pltpu.PrefetchScalarGridSpec(
        num_scalar_prefetch=2,
        grid=(bsz, n_e),
        in_specs=[
            pl.BlockSpec((1, t, d), lambda b, e, lo, hi: (b, 0, 0)),
            pl.BlockSpec((1, 1, nblk, tb), lambda b, e, lo, hi: (b, e, 0, 0)),
            pl.BlockSpec((1, 1, nblk, tb), lambda b, e, lo, hi: (b, e, 0, 0)),
        ],
        out_specs=(pl.BlockSpec((1, 1, cap, d), lambda b, e, lo, hi: (b, e, 0, 0)),
                   pl.BlockSpec((1, 1, cap, 1), lambda b, e, lo, hi: (b, e, 0, 0))),
        scratch_shapes=[pltpu.VMEM((RT_GATHER, d), F32), pltpu.VMEM((RT_GATHER, 1), F32)],
    )
    return pl.pallas_call(
        _gather_kernel,
        out_shape=(jax.ShapeDtypeStruct((bsz, n_e, cap, d), BF16),
                   jax.ShapeDtypeStruct((bsz, n_e, cap, 1), F32)),
        grid_spec=grid_spec,
        compiler_params=_cparams(("parallel", "parallel")),
        name="gather",
    )(lo, hi, hm, pos4, gate4)


def _ffn_kernel(xs_ref, gc_ref, wg_ref, wu_ref, wd_ref, ye_ref, wg_s, wu_s, wd_s):
    @pl.when(pl.program_id(1) == 0)
    def _():
        wg_s[...] = wg_ref[0].astype(BF16)
        wu_s[...] = wu_ref[0].astype(BF16)
        wd_s[...] = wd_ref[0].astype(BF16)

    rt = 256
    for j in range(xs_ref.shape[2] // rt):
        sl = slice(j * rt, (j + 1) * rt)
        xs = xs_ref[0, 0, sl, :]
        xg = jnp.dot(xs, wg_s[...], preferred_element_type=F32)
        xu = jnp.dot(xs, wu_s[...], preferred_element_type=F32)
        hid = (jax.nn.silu(xg) * xu).astype(BF16)
        ye = jnp.dot(hid, wd_s[...], preferred_element_type=F32) * gc_ref[0, 0, sl, :]
        ye_ref[0, 0, sl, :] = ye.astype(BF16)


def _ffn_call(xs, gc, w_gate, w_up, w_down):
    bsz, n_e, cap, d = xs.shape
    f = w_gate.shape[2]
    return pl.pallas_call(
        _ffn_kernel,
        out_shape=jax.ShapeDtypeStruct((bsz, n_e, cap, d), BF16),
        grid=(n_e, bsz),
        in_specs=[
            pl.BlockSpec((1, 1, cap, d), lambda e, b: (b, e, 0, 0)),
            pl.BlockSpec((1, 1, cap, 1), lambda e, b: (b, e, 0, 0)),
            pl.BlockSpec((1, d, f), lambda e, b: (e, 0, 0)),
            pl.BlockSpec((1, d, f), lambda e, b: (e, 0, 0)),
            pl.BlockSpec((1, f, d), lambda e, b: (e, 0, 0)),
        ],
        out_specs=pl.BlockSpec((1, 1, cap, d), lambda e, b: (b, e, 0, 0)),
        scratch_shapes=[pltpu.VMEM((d, f), BF16), pltpu.VMEM((d, f), BF16), pltpu.VMEM((f, d), BF16)],
        compiler_params=_cparams(("arbitrary", "arbitrary")),
        name="ffn",
    )(xs, gc, w_gate, w_up, w_down)


def _combine_kernel(lo_ref, hi_ref, ye_ref, pos_ref, out_ref):
    b = pl.program_id(0)
    half = pl.program_id(1)
    e = pl.program_id(2)
    n_e = pl.num_programs(2)
    rt = RT_SCATTER
    tb = TB_ROUTE
    n_tiles = ye_ref.shape[2] // rt
    hblk = out_ref.shape[1] // tb

    @pl.when(e == 0)
    def _():
        out_ref[...] = jnp.zeros_like(out_ref)

    rows0 = lax.broadcasted_iota(I32, (rt, tb), 0)
    for j in range(n_tiles):
        base = (b * n_e + e) * n_tiles + j
        rows = rows0 + j * rt
        lo = jnp.maximum(lo_ref[base], half * hblk)
        hi = jnp.minimum(hi_ref[base], half * hblk + hblk - 1)

        def body(kb, carry):
            match = rows == pos_ref[0, 0, pl.ds(kb, 1), :]
            contrib = _dot_t0(match.astype(BF16), ye_ref[0, 0, j * rt:(j + 1) * rt, :])
            off = pl.multiple_of((kb - half * hblk) * tb, tb)
            out_ref[0, pl.ds(off, tb), :] += contrib
            return carry

        lax.fori_loop(lo, hi + 1, body, 0)


def _combine_call(lo, hi, ye, pos4, t):
    bsz, n_e, cap, d = ye.shape
    nblk, tb = pos4.shape[2], pos4.shape[3]
    n_half = 2
    grid_spec = pltpu.PrefetchScalarGridSpec(
        num_scalar_prefetch=2,
        grid=(bsz, n_half, n_e),
        in_specs=[
            pl.BlockSpec((1, 1, cap, d), lambda b, h, e, lo, hi: (b, e, 0, 0)),
            pl.BlockSpec((1, 1, nblk, tb), lambda b, h, e, lo, hi: (b, e, 0, 0)),
        ],
        out_specs=pl.BlockSpec((1, t // n_half, d), lambda b, h, e, lo, hi: (b, h, 0)),
    )
    return pl.pallas_call(
        _combine_kernel,
        out_shape=jax.ShapeDtypeStruct((bsz, t, d), F32),
        grid_spec=grid_spec,
        compiler_params=_cparams(("parallel", "parallel", "arbitrary")),
        name="combine",
    )(lo, hi, ye, pos4)


def _final_kernel(x1_ref, moe_ref, mod_ref, w_ref, b_ref, o_ref):
    d = D_MODEL
    g2 = mod_ref[0, :, 5 * d:6 * d]
    o_ref[0] = _ln(DEEPNORM_ALPHA * x1_ref[0] + g2 * moe_ref[0]) * w_ref[...] + b_ref[...]


def _final_call(x1, moe, mod3, ln2w, ln2b):
    bsz, t, d = x1.shape
    tm = TM_FINAL
    row = lambda b, i: (b, i, 0)
    return pl.pallas_call(
        _final_kernel,
        out_shape=jax.ShapeDtypeStruct((bsz, t, d), F32),
        grid=(bsz, t // tm),
        in_specs=[
            pl.BlockSpec((1, tm, d), row),
            pl.BlockSpec((1, tm, d), row),
            pl.BlockSpec((1, 1, 6 * d), lambda b, i: (b, 0, 0)),
            pl.BlockSpec((1, d), lambda b, i: (0, 0)),
            pl.BlockSpec((1, d), lambda b, i: (0, 0)),
        ],
        out_specs=pl.BlockSpec((1, tm, d), row),
        compiler_params=_cparams(("parallel", "parallel")),
        name="final",
    )(x1, moe, mod3, ln2w, ln2b)


def _rope_tables(n_tokens):
    rows = n_tokens // GRID_W
    n_freq = RET_QK_DIM // 4
    inv = ROPE_THETA ** (-jnp.arange(n_freq, dtype=F32) / n_freq)
    ang_r = jnp.arange(rows, dtype=F32)[:, None] * inv
    ang_c = jnp.arange(GRID_W, dtype=F32)[:, None] * inv
    shape = (rows, GRID_W, n_freq)
    cr = jnp.broadcast_to(jnp.cos(ang_r)[:, None, :], shape)
    sr = jnp.broadcast_to(jnp.sin(ang_r)[:, None, :], shape)
    cc = jnp.broadcast_to(jnp.cos(ang_c)[None, :, :], shape)
    sc = jnp.broadcast_to(jnp.sin(ang_c)[None, :, :], shape)
    cos_t = jnp.concatenate([cr, cr, cc, cc], axis=-1).reshape(n_tokens, RET_QK_DIM)
    sin_t = jnp.concatenate([-sr, sr, -sc, sc], axis=-1).reshape(n_tokens, RET_QK_DIM)
    return cos_t, sin_t


def _tile_block_ranges(pos, rows_per_tile, cap, tb):
    starts = jnp.arange(0, cap, rows_per_tile, dtype=I32)
    first_tok = jnp.argmax(pos[:, :, None, :] == starts[None, None, :, None], axis=-1)
    last_tok = jnp.argmax(pos[:, :, None, :] == (starts + rows_per_tile - 1)[None, None, :, None], axis=-1)
    return (first_tok // tb).astype(I32).reshape(-1), (last_tok // tb).astype(I32).reshape(-1)


def kernel(x, c, ctx, c_ctx, w_ada, b_ada, w_in, b_in, ret_decay_f, ret_decay_b, ret_gn_w, ret_gn_b,
           sgu_ln_w, sgu_ln_b, sgu_w, sgu_b, w_pa, w_pb, w_o, b_o, ln1_w, ln1_b,
           w_router, b_router, w_gate, w_up, w_down, ln2_w, ln2_b):
    bsz, t, d = x.shape
    assert d == D_MODEL and w_ada.shape[0] == DEPTH == 1 and bsz < MOD_ROWS
    cap = EC_CAPACITY_FACTOR * t // N_EXPERTS
    assert t % (2 * TB_ROUTE) == 0 and cap % RT_SCATTER == 0 and t % TM_FINAL == 0
    l = 0

    cvec = jnp.zeros((MOD_ROWS, d), F32).at[:bsz].set(c).at[bsz].set(c_ctx)
    mod = _mod_call(cvec, w_ada[l], b_ada[l][None])
    mod3 = mod[:, None, :]

    w_in_b = w_in[l].astype(BF16)
    b_in2 = b_in[l][None]
    lo_kv, hi_kv = RET_QK_W, 2 * RET_QK_W + RET_V_W
    kctx, vctx = _ctx_kv_call(ctx, mod3, w_in_b[:, lo_kv:hi_kv], b_in2[:, lo_kv:hi_kv], bsz)

    cos_t, sin_t = _rope_tables(t)
    qk, v, sg, u, vs, ga, gb = _inproj_call(x, mod3, w_in_b, b_in2, cos_t, sin_t,
                                            sgu_ln_w[l][None], sgu_ln_b[l][None])

    lg = jnp.stack([jax.nn.log_sigmoid(ret_decay_f[l].astype(F32)),
                    jax.nn.log_sigmoid(ret_decay_b[l].astype(F32))])
    states = _state_call(lg, qk, v, kctx, vctx)

    x1, hm, logits_t = _mixer_call(
        lg, qk, v, sg, u, vs, ga, gb, states, x, mod3,
        w_pa[l].astype(BF16), w_pb[l].astype(BF16), w_o[l].astype(BF16),
        sgu_w[l].astype(BF16), sgu_b[l].T, ret_gn_w[l][None], ret_gn_b[l][None], b_o[l][None],
        ln1_w[l][None], ln1_b[l][None], w_router[l].T, b_router[l][:, None])

    pos, gate = _route_call(logits_t, cap)
    nblk = t // TB_ROUTE
    pos4 = pos.reshape(bsz, N_EXPERTS, nblk, TB_ROUTE)
    gate4 = gate.reshape(bsz, N_EXPERTS, nblk, TB_ROUTE)
    lo_g, hi_g = _tile_block_ranges(pos, RT_GATHER, cap, TB_ROUTE)
    lo_s, hi_s = _tile_block_ranges(pos, RT_SCATTER, cap, TB_ROUTE)

    xs, gc = _gather_call(lo_g, hi_g, hm, pos4, gate4, cap)
    ye = _ffn_call(xs, gc, w_gate[l], w_up[l], w_down[l])
    moe = _combine_call(lo_s, hi_s, ye, pos4, t)
    return _final_call(x1, moe, mod3, ln2_w[l][None], ln2_b[l][None])
```

```python
import functools

import jax
import jax.numpy as jnp
from jax import lax
from jax.experimental import pallas as pl
from jax.experimental.pallas import tpu as pltpu

F32 = jnp.float32
BF16 = jnp.bfloat16
I32 = jnp.int32
HIGHEST = lax.Precision.HIGHEST

D_MODEL = 1024
DEPTH = 1
GRID_W = 64
RET_HEADS = 4
RET_QK_DIM = 128
RET_V_DIM = 256
RET_CHUNK = 128
ROPE_THETA = 10000.0
RET_QK_W = RET_HEADS * RET_QK_DIM
RET_V_W = RET_HEADS * RET_V_DIM
SGU_GROUPS = 4
SGU_CHUNK = 128
SGU_WIDTH = 1024
SGU_GW = SGU_WIDTH // SGU_GROUPS
N_EXPERTS = 16
EC_CAPACITY_FACTOR = 2
D_EXPERT = 1024
LN_EPS = 1e-6
DEEPNORM_ALPHA = (2.0 * DEPTH) ** 0.25
D_IN = 2 * RET_QK_W + 2 * RET_V_W + 2 * SGU_WIDTH + 2 * D_MODEL

V7X_VMEM_BYTES = 64 * 1024 * 1024
VMEM_LIMIT = 56 * 1024 * 1024
LANES = 128

MOD_ROWS = 8
TM_INPROJ = 256
TT_MIXER = 256
TB_ROUTE = 256
WIN = 128
WIN_ALIGN = 16
DISPATCH_GROUPS = 2
STATE_UNROLL = 4


def _cparams(sem):
    return pltpu.CompilerParams(dimension_semantics=sem, vmem_limit_bytes=VMEM_LIMIT)


def _ln(x):
    mu = jnp.mean(x, axis=-1, keepdims=True)
    xc = x - mu
    var = jnp.mean(xc * xc, axis=-1, keepdims=True)
    return xc * lax.rsqrt(var + LN_EPS)


def _iota_f(shape, dim):
    return lax.broadcasted_iota(I32, shape, dim).astype(F32)


def _mod_kernel(c_ref, w_ref, b_ref, o_ref):
    c = c_ref[...]
    a = c * jax.nn.sigmoid(c)
    o_ref[...] = jnp.dot(a, w_ref[...], preferred_element_type=F32, precision=HIGHEST) + b_ref[...]


def _mod_call(cvec, w_ada, b_ada):
    d, n = w_ada.shape
    tn = 1024
    return pl.pallas_call(
        _mod_kernel,
        out_shape=jax.ShapeDtypeStruct((MOD_ROWS, n), F32),
        grid=(n // tn,),
        in_specs=[
            pl.BlockSpec((MOD_ROWS, d), lambda j: (0, 0)),
            pl.BlockSpec((d, tn), lambda j: (0, j)),
            pl.BlockSpec((1, tn), lambda j: (0, j)),
        ],
        out_specs=pl.BlockSpec((MOD_ROWS, tn), lambda j: (0, j)),
        compiler_params=_cparams(("parallel",)),
        name="mod",
    )(cvec, w_ada, b_ada)


def _ctx_kv_kernel(x_ref, mod_ref, w_ref, b_ref, k_ref, v_ref):
    d = D_MODEL
    sh = mod_ref[0, :, 0:d]
    sc = mod_ref[0, :, d:2 * d]
    h = (_ln(x_ref[0]) * (1.0 + sc) + sh).astype(BF16)
    z = jnp.dot(h, w_ref[...], preferred_element_type=F32) + b_ref[...]
    k_ref[0] = (z[:, :RET_QK_W] * (RET_QK_DIM ** -0.5)).astype(BF16)
    v_ref[0] = z[:, RET_QK_W:].astype(BF16)


def _ctx_kv_call(ctx, mod3, w_kv, b_kv, ctx_row):
    bsz, n_ctx, d = ctx.shape
    wkv = w_kv.shape[1]
    return pl.pallas_call(
        _ctx_kv_kernel,
        out_shape=(jax.ShapeDtypeStruct((bsz, n_ctx, RET_QK_W), BF16),
                   jax.ShapeDtypeStruct((bsz, n_ctx, RET_V_W), BF16)),
        grid=(bsz,),
        in_specs=[
            pl.BlockSpec((1, n_ctx, d), lambda b: (b, 0, 0)),
            pl.BlockSpec((1, 1, 6 * d), lambda b: (ctx_row, 0, 0)),
            pl.BlockSpec((d, wkv), lambda b: (0, 0)),
            pl.BlockSpec((1, wkv), lambda b: (0, 0)),
        ],
        out_specs=(pl.BlockSpec((1, n_ctx, RET_QK_W), lambda b: (b, 0, 0)),
                   pl.BlockSpec((1, n_ctx, RET_V_W), lambda b: (b, 0, 0))),
        compiler_params=_cparams(("parallel",)),
        name="ctx_kv",
    )(ctx, mod3, w_kv, b_kv)


def _inproj_kernel(x_ref, mod_ref, w_ref, b_ref, cos_ref, sin_ref, lnw_ref, lnb_ref,
                   qk_ref, v_ref, sg_ref, u_ref, vs_ref, ga_ref, gb_ref):
    d = D_MODEL
    sh = mod_ref[0, :, 0:d]
    sc = mod_ref[0, :, d:2 * d]
    h = (_ln(x_ref[0]) * (1.0 + sc) + sh).astype(BF16)

    def proj(c0, width):
        return jnp.dot(h, w_ref[:, c0:c0 + width], preferred_element_type=F32) + b_ref[:, c0:c0 + width]

    zq = proj(0, 2 * RET_QK_W)
    cos = cos_ref[...]
    sin = sin_ref[...]
    tm = zq.shape[0]
    upper = (lax.broadcasted_iota(I32, (tm, LANES), 1) & 32) != 0
    for hb in range(2 * RET_HEADS):
        zs = zq[:, hb * LANES:(hb + 1) * LANES]
        if hb >= RET_HEADS:
            zs = zs * (RET_QK_DIM ** -0.5)
        sw = jnp.where(upper, pltpu.roll(zs, 32, 1), pltpu.roll(zs, LANES - 32, 1))
        qk_ref[0, :, hb * LANES:(hb + 1) * LANES] = (zs * cos + sw * sin).astype(BF16)

    c0 = 2 * RET_QK_W
    v_ref[0] = proj(c0, RET_V_W).astype(BF16)
    c0 += RET_V_W
    sg_ref[0] = jax.nn.silu(proj(c0, RET_V_W)).astype(BF16)
    c0 += RET_V_W
    u_ref[0] = jax.nn.gelu(proj(c0, SGU_WIDTH)).astype(BF16)
    c0 += SGU_WIDTH
    vs = _ln(jax.nn.gelu(proj(c0, SGU_WIDTH))) * lnw_ref[...] + lnb_ref[...]
    vs_ref[0] = vs.astype(BF16)
    c0 += SGU_WIDTH
    ga_ref[0] = jax.nn.sigmoid(proj(c0, d)).astype(BF16)
    c0 += d
    gb_ref[0] = jax.nn.sigmoid(proj(c0, d)).astype(BF16)


def _inproj_call(x, mod3, w_in, b_in, cos_t, sin_t, sgu_ln_w, sgu_ln_b):
    bsz, t, d = x.shape
    tm = TM_INPROJ
    row = lambda b, i: (b, i, 0)
    const2 = lambda b, i: (0, 0)
    outs = tuple(jax.ShapeDtypeStruct((bsz, t, 1024), BF16) for _ in range(7))
    return pl.pallas_call(
        _inproj_kernel,
        out_shape=outs,
        grid=(bsz, t // tm),
        in_specs=[
            pl.BlockSpec((1, tm, d), row),
            pl.BlockSpec((1, 1, 6 * d), lambda b, i: (b, 0, 0)),
            pl.BlockSpec((d, D_IN), const2),
            pl.BlockSpec((1, D_IN), const2),
            pl.BlockSpec((tm, LANES), lambda b, i: (i, 0)),
            pl.BlockSpec((tm, LANES), lambda b, i: (i, 0)),
            pl.BlockSpec((1, SGU_WIDTH), const2),
            pl.BlockSpec((1, SGU_WIDTH), const2),
        ],
        out_specs=tuple(pl.BlockSpec((1, tm, 1024), row) for _ in range(7)),
        compiler_params=_cparams(("parallel", "parallel")),
        name="inproj",
    )(x, mod3, w_in, b_in, cos_t, sin_t, sgu_ln_w, sgu_ln_b)


def _dot_t0(a, b):
    return lax.dot_general(a, b, (((0,), (0,)), ((), ())), preferred_element_type=F32)


def _state_kernel(lg_ref, k_ref, v_ref, kc_ref, vc_ref, s_ref):
    hd = pl.program_id(1)
    lgf = lg_ref[0, hd]
    lgb = lg_ref[1, hd]
    n_chunks = s_ref.shape[2]
    n_ctx = kc_ref.shape[1]
    c = RET_CHUNK
    li = _iota_f((c, RET_QK_DIM), 0)
    zeta_f = jnp.exp((c - 1.0 - li) * lgf)
    zeta_b = jnp.exp(li * lgb)
    one = jnp.ones((1, 1), F32)
    cd_f = jnp.exp(one * (c * lgf))
    cd_b = jnp.exp(one * (c * lgb))

    tc = _iota_f((n_ctx, RET_QK_DIM), 0)
    kc = kc_ref[0].astype(F32)
    vc = vc_ref[0]
    s0f = _dot_t0((kc * jnp.exp((n_ctx - 1.0 - tc) * lgf)).astype(BF16), vc)
    s0b = _dot_t0((kc * jnp.exp(tc * lgb)).astype(BF16), vc)

    def chunk_kv(n, zeta):
        off = pl.multiple_of(n * c, c)
        kk = k_ref[0, pl.ds(off, c), :].astype(F32)
        return _dot_t0((kk * zeta).astype(BF16), v_ref[0, pl.ds(off, c), :])

    def step(i, carry):
        sf, sb = carry
        nb = n_chunks - 1 - i
        s_ref[0, 0, i, 0:RET_QK_DIM, :] = sf.astype(BF16)
        s_ref[0, 0, nb, RET_QK_DIM:2 * RET_QK_DIM, :] = sb.astype(BF16)
        return cd_f * sf + chunk_kv(i, zeta_f), cd_b * sb + chunk_kv(nb, zeta_b)

    lax.fori_loop(0, n_chunks, step, (s0f, s0b), unroll=STATE_UNROLL)


def _state_call(lg, qk, v, kctx, vctx):
    bsz, t, _ = qk.shape
    n_ctx = kctx.shape[1]
    n_chunks = t // RET_CHUNK
    return pl.pallas_call(
        _state_kernel,
        out_shape=jax.ShapeDtypeStruct((bsz, RET_HEADS, n_chunks, 2 * RET_QK_DIM, RET_V_DIM), BF16),
        grid=(bsz, RET_HEADS),
        in_specs=[
            pl.BlockSpec(memory_space=pltpu.SMEM),
            pl.BlockSpec((1, t, RET_QK_DIM), lambda b, h: (b, 0, RET_HEADS + h)),
            pl.BlockSpec((1, t, RET_V_DIM), lambda b, h: (b, 0, h)),
            pl.BlockSpec((1, n_ctx, RET_QK_DIM), lambda b, h: (b, 0, h)),
            pl.BlockSpec((1, n_ctx, RET_V_DIM), lambda b, h: (b, 0, h)),
        ],
        out_specs=pl.BlockSpec((1, 1, n_chunks, 2 * RET_QK_DIM, RET_V_DIM), lambda b, h: (b, h, 0, 0, 0)),
        compiler_params=_cparams(("parallel", "parallel")),
        name="states",
    )(lg, qk, v, kctx, vctx)


def _mixer_kernel(lg_ref, qk_ref, v_ref, sg_ref, u_ref, vs_ref, ga_ref, gb_ref, s_ref, x_ref, mod_ref,
                  wpa_ref, wpb_ref, wo_ref, sguw_ref, sgub_ref, gnw_ref, gnb_ref, bo_ref,
                  ln1w_ref, ln1b_ref, wr_ref, br_ref,
                  x1_ref, hm_ref, lgt_ref, ret_scr, sgu_scr):
    d = D_MODEL
    c = RET_CHUNK
    n_sub = qk_ref.shape[1] // c
    row = lax.broadcasted_iota(I32, (c, c), 0)
    col = lax.broadcasted_iota(I32, (c, c), 1)
    diff = (row - col).astype(F32)
    rowq = _iota_f((c, RET_QK_DIM), 0)

    for hd in range(RET_HEADS):
        lgf = lg_ref[0, hd]
        lgb = lg_ref[1, hd]
        mask = (jnp.where(diff >= 0, jnp.exp(jnp.maximum(diff, 0.0) * lgf), 0.0)
                + jnp.where(diff <= 0, jnp.exp(jnp.maximum(-diff, 0.0) * lgb), 0.0))
        xi_f = jnp.exp((rowq + 1.0) * lgf)
        xi_b = jnp.exp((c - rowq) * lgb)
        for ci in range(n_sub):
            r0 = ci * c
            q = qk_ref[0, r0:r0 + c, hd * RET_QK_DIM:(hd + 1) * RET_QK_DIM]
            k = qk_ref[0, r0:r0 + c, RET_QK_W + hd * RET_QK_DIM:RET_QK_W + (hd + 1) * RET_QK_DIM]
            vv = v_ref[0, r0:r0 + c, hd * RET_V_DIM:(hd + 1) * RET_V_DIM]
            s = lax.dot_general(q, k, (((1,), (1,)), ((), ())), preferred_element_type=F32)
            intra = jnp.dot((s * mask).astype(BF16), vv, preferred_element_type=F32)
            qf = q.astype(F32)
            qx = jnp.concatenate([(qf * xi_f).astype(BF16), (qf * xi_b).astype(BF16)], axis=1)
            cross = jnp.dot(qx, s_ref[0, hd, ci], preferred_element_type=F32)
            o = _ln(intra + cross)
            o = o * gnw_ref[:, hd * RET_V_DIM:(hd + 1) * RET_V_DIM] + gnb_ref[:, hd * RET_V_DIM:(hd + 1) * RET_V_DIM]
            gate = sg_ref[0, r0:r0 + c, hd * RET_V_DIM:(hd + 1) * RET_V_DIM].astype(F32)
            ret_scr[r0:r0 + c, hd * RET_V_DIM:(hd + 1) * RET_V_DIM] = (o * gate).astype(BF16)

    for g in range(SGU_GROUPS):
        wg = sguw_ref[g]
        bg = sgub_ref[:, g:g + 1]
        for ci in range(n_sub):
            r0 = ci * c
            vsb = vs_ref[0, r0:r0 + c, g * SGU_GW:(g + 1) * SGU_GW]
            sp = jnp.dot(wg, vsb, preferred_element_type=F32) + bg
            ub = u_ref[0, r0:r0 + c, g * SGU_GW:(g + 1) * SGU_GW].astype(F32)
            sgu_scr[r0:r0 + c, g * SGU_GW:(g + 1) * SGU_GW] = (ub * sp).astype(BF16)

    pa = jnp.dot(ret_scr[...], wpa_ref[...], preferred_element_type=F32)
    pb = jnp.dot(sgu_scr[...], wpb_ref[...], preferred_element_type=F32)
    y = (ga_ref[0].astype(F32) * pa + gb_ref[0].astype(F32) * pb).astype(BF16)
    mix = jnp.dot(y, wo_ref[...], preferred_element_type=F32) + bo_ref[...]
    g1 = mod_ref[0, :, 2 * d:3 * d]
    sh2 = mod_ref[0, :, 3 * d:4 * d]
    sc2 = mod_ref[0, :, 4 * d:5 * d]
    x1 = _ln(DEEPNORM_ALPHA * x_ref[0] + g1 * mix) * ln1w_ref[...] + ln1b_ref[...]
    x1_ref[0] = x1
    hm = _ln(x1) * (1.0 + sc2) + sh2
    hm_ref[0] = hm.astype(BF16)
    lgt_ref[0] = lax.dot_general(wr_ref[...], hm, (((1,), (1,)), ((), ())),
                                 preferred_element_type=F32, precision=HIGHEST) + br_ref[...]


def _mixer_call(lg, qk, v, sg, u, vs, ga, gb, states, x, mod3, wpa, wpb, wo, sguw, sgub_t, gnw, gnb, bo,
                ln1w, ln1b, wr_t, br):
    bsz, t, d = x.shape
    tt = TT_MIXER
    n_sub = tt // RET_CHUNK
    row = lambda b, i: (b, i, 0)
    c2 = lambda b, i: (0, 0)
    c3 = lambda b, i: (0, 0, 0)
    act = pl.BlockSpec((1, tt, 1024), row)
    return pl.pallas_call(
        _mixer_kernel,
        out_shape=(jax.ShapeDtypeStruct((bsz, t, d), F32),
                   jax.ShapeDtypeStruct((bsz, t, d), BF16),
                   jax.ShapeDtypeStruct((bsz, N_EXPERTS, t), F32)),
        grid=(bsz, t // tt),
        in_specs=[
            pl.BlockSpec(memory_space=pltpu.SMEM),
            act, act, act, act, act, act, act,
            pl.BlockSpec((1, RET_HEADS, n_sub, 2 * RET_QK_DIM, RET_V_DIM), lambda b, i: (b, 0, i, 0, 0)),
            pl.BlockSpec((1, tt, d), row),
            pl.BlockSpec((1, 1, 6 * d), lambda b, i: (b, 0, 0)),
            pl.BlockSpec((RET_V_W, d), c2),
            pl.BlockSpec((SGU_WIDTH, d), c2),
            pl.BlockSpec((d, d), c2),
            pl.BlockSpec((SGU_GROUPS, SGU_CHUNK, SGU_CHUNK), c3),
            pl.BlockSpec((SGU_CHUNK, SGU_GROUPS), c2),
            pl.BlockSpec((1, RET_V_W), c2),
            pl.BlockSpec((1, RET_V_W), c2),
            pl.BlockSpec((1, d), c2),
            pl.BlockSpec((1, d), c2),
            pl.BlockSpec((1, d), c2),
            pl.BlockSpec((N_EXPERTS, d), c2),
            pl.BlockSpec((N_EXPERTS, 1), c2),
        ],
        out_specs=(pl.BlockSpec((1, tt, d), row),
                   pl.BlockSpec((1, tt, d), row),
                   pl.BlockSpec((1, N_EXPERTS, tt), lambda b, i: (b, 0, i))),
        scratch_shapes=[pltpu.VMEM((tt, RET_V_W), BF16), pltpu.VMEM((tt, SGU_WIDTH), BF16)],
        compiler_params=_cparams(("parallel", "parallel")),
        name="mixer",
    )(lg, qk, v, sg, u, vs, ga, gb, states, x, mod3, wpa, wpb, wo, sguw, sgub_t, gnw, gnb, bo,
      ln1w, ln1b, wr_t, br)


def _route_kernel(lgt_ref, pos_ref, gate_ref, cnt_ref, *, cap):
    lg = lgt_ref[0]
    n_e, t = lg.shape
    tb = TB_ROUTE
    m = jnp.max(lg, axis=0, keepdims=True)
    ex = jnp.exp(lg - m)
    aff = ex / jnp.sum(ex, axis=0, keepdims=True)

    def search(i, thr_bits):
        cand = thr_bits | lax.shift_left(jnp.int32(1), 30 - i)
        cnt = jnp.sum((aff >= lax.bitcast_convert_type(cand, F32)).astype(I32), axis=1, keepdims=True)
        return jnp.where(cnt >= cap, cand, thr_bits)

    thr_bits = lax.fori_loop(0, 31, search, jnp.zeros((n_e, 1), I32))
    floor_f = lax.bitcast_convert_type(thr_bits, F32)
    thr = jnp.min(jnp.where(aff >= floor_f, aff, jnp.inf), axis=1, keepdims=True)
    need = (cap - jnp.sum((aff > thr).astype(I32), axis=1, keepdims=True)).astype(F32)

    r = lax.broadcasted_iota(I32, (tb, tb), 0)
    cc = lax.broadcasted_iota(I32, (tb, tb), 1)
    tri = (r <= cc).astype(BF16)
    carry_eq = jnp.zeros((n_e, 1), F32)
    carry_sel = jnp.zeros((n_e, 1), F32)
    for blk in range(t // tb):
        sl = slice(blk * tb, (blk + 1) * tb)
        aff_b = aff[:, sl]
        eq = aff_b == thr
        eq_b = eq.astype(BF16)
        inc_eq = jnp.dot(eq_b, tri, preferred_element_type=F32)
        before = carry_eq + inc_eq - eq_b.astype(F32)
        sel = (aff_b > thr) | (eq & (before < need))
        sel_b = sel.astype(BF16)
        inc_sel = jnp.dot(sel_b, tri, preferred_element_type=F32)
        pos = carry_sel + inc_sel - 1.0
        pos_ref[0, blk] = jnp.where(sel, pos.astype(I32), -1)
        gate_ref[0, blk] = aff_b
        cnt_ref[0, blk] = jnp.broadcast_to(carry_sel, (n_e, LANES)).astype(I32)
        carry_eq = carry_eq + inc_eq[:, tb - 1:tb]
        carry_sel = carry_sel + inc_sel[:, tb - 1:tb]


def _route_call(logits_t, cap):
    bsz, n_e, t = logits_t.shape
    nblk = t // TB_ROUTE
    return pl.pallas_call(
        functools.partial(_route_kernel, cap=cap),
        out_shape=(jax.ShapeDtypeStruct((bsz, nblk, n_e, TB_ROUTE), I32),
                   jax.ShapeDtypeStruct((bsz, nblk, n_e, TB_ROUTE), F32),
                   jax.ShapeDtypeStruct((bsz, nblk, n_e, LANES), I32)),
        grid=(bsz,),
        in_specs=[pl.BlockSpec((1, n_e, t), lambda b: (b, 0, 0))],
        out_specs=(pl.BlockSpec((1, nblk, n_e, TB_ROUTE), lambda b: (b, 0, 0, 0)),
                   pl.BlockSpec((1, nblk, n_e, TB_ROUTE), lambda b: (b, 0, 0, 0)),
                   pl.BlockSpec((1, nblk, n_e, LANES), lambda b: (b, 0, 0, 0))),
        compiler_params=_cparams(("parallel",)),
        name="route",
    )(logits_t)


def _window_rows(ws):
    return lax.broadcasted_iota(I32, (WIN, TB_ROUTE), 0) + ws


def _gather_kernel(ws_ref, nx_ref, hm_ref, pos_ref, gate_ref, xs_ref, gc_ref):
    b = pl.program_id(0)
    g = pl.program_id(1)
    kb = pl.program_id(2)
    eg = pos_ref.shape[2]
    n_e = eg * pl.num_programs(1)
    cap = xs_ref.shape[2]

    @pl.when(kb == 0)
    def _():
        xs_ref[...] = jnp.zeros_like(xs_ref)
        gc_ref[...] = jnp.zeros_like(gc_ref)

    hb = hm_ref[0]
    base = (b * pl.num_programs(2) + kb) * n_e + g * eg
    starts = [pl.multiple_of(ws_ref[base + el], WIN_ALIGN) for el in range(eg)]
    matches = [_window_rows(starts[el]) == pos_ref[0, 0, el:el + 1, :] for el in range(eg)]
    onehot = jnp.concatenate([m.astype(BF16) for m in matches], axis=0)
    picked = jnp.dot(onehot, hb, preferred_element_type=F32)
    for el in range(eg):
        win = pl.ds(starts[el], WIN)
        xs_ref[0, el, win, :] += picked[el * WIN:(el + 1) * WIN].astype(BF16)
        gc_ref[0, el, win, :] += jnp.sum(jnp.where(matches[el], gate_ref[0, 0, el:el + 1, :], 0.0),
                                         axis=1, keepdims=True)

        @pl.when(nx_ref[base + el] > 0)
        def _(el=el):
            prow = pos_ref[0, 0, el:el + 1, :]
            grow = gate_ref[0, 0, el:el + 1, :]
            rest = prow >= starts[el] + WIN

            def body(w, carry):
                off = pl.multiple_of(w * WIN, WIN)
                m = (_window_rows(off) == prow) & rest
                xs_ref[0, el, pl.ds(off, WIN), :] += jnp.dot(m.astype(BF16), hb,
                                                              preferred_element_type=F32).astype(BF16)
                gc_ref[0, el, pl.ds(off, WIN), :] += jnp.sum(jnp.where(m, grow, 0.0), axis=1, keepdims=True)
                return carry

            lax.fori_loop(0, cap // WIN, body, 0)


def _gather_call(ws, nx, hm, pos_b, gate_b, cap):
    bsz, t, d = hm.shape
    nblk, n_e, tb = pos_b.shape[1:]
    eg = n_e // DISPATCH_GROUPS
    grid_spec = pltpu.PrefetchScalarGridSpec(
        num_scalar_prefetch=2,
        grid=(bsz, DISPATCH_GROUPS, nblk),
        in_specs=[
            pl.BlockSpec((1, tb, d), lambda b, g, k, ws, nx: (b, k, 0)),
            pl.BlockSpec((1, 1, eg, tb), lambda b, g, k, ws, nx: (b, k, g, 0)),
            pl.BlockSpec((1, 1, eg, tb), lambda b, g, k, ws, nx: (b, k, g, 0)),
        ],
        out_specs=(pl.BlockSpec((1, eg, cap, d), lambda b, g, k, ws, nx: (b, g, 0, 0)),
                   pl.BlockSpec((1, eg, cap, 1), lambda b, g, k, ws, nx: (b, g, 0, 0))),
    )
    return pl.pallas_call(
        _gather_kernel,
        out_shape=(jax.ShapeDtypeStruct((bsz, n_e, cap, d), BF16),
                   jax.ShapeDtypeStruct((bsz, n_e, cap, 1), F32)),
        grid_spec=grid_spec,
        compiler_params=_cparams(("parallel", "parallel", "arbitrary")),
        name="gather",
    )(ws, nx, hm, pos_b, gate_b)


def _ffn_kernel(xs_ref, gc_ref, wg_ref, wu_ref, wd_ref, ye_ref, wg_s, wu_s, wd_s):
    @pl.when(pl.program_id(1) == 0)
    def _():
        wg_s[...] = wg_ref[0].astype(BF16)
        wu_s[...] = wu_ref[0].astype(BF16)
        wd_s[...] = wd_ref[0].astype(BF16)

    rt = 256
    for j in range(xs_ref.shape[2] // rt):
        sl = slice(j * rt, (j + 1) * rt)
        xs = xs_ref[0, 0, sl, :]
        xg = jnp.dot(xs, wg_s[...], preferred_element_type=F32)
        xu = jnp.dot(xs, wu_s[...], preferred_element_type=F32)
        hid = (jax.nn.silu(xg) * xu).astype(BF16)
        ye = jnp.dot(hid, wd_s[...], preferred_element_type=F32) * gc_ref[0, 0, sl, :]
        ye_ref[0, 0, sl, :] = ye.astype(BF16)


def _ffn_call(xs, gc, w_gate, w_up, w_down):
    bsz, n_e, cap, d = xs.shape
    f = w_gate.shape[2]
    return pl.pallas_call(
        _ffn_kernel,
        out_shape=jax.ShapeDtypeStruct((bsz, n_e, cap, d), BF16),
        grid=(n_e, bsz),
        in_specs=[
            pl.BlockSpec((1, 1, cap, d), lambda e, b: (b, e, 0, 0)),
            pl.BlockSpec((1, 1, cap, 1), lambda e, b: (b, e, 0, 0)),
            pl.BlockSpec((1, d, f), lambda e, b: (e, 0, 0)),
            pl.BlockSpec((1, d, f), lambda e, b: (e, 0, 0)),
            pl.BlockSpec((1, f, d), lambda e, b: (e, 0, 0)),
        ],
        out_specs=pl.BlockSpec((1, 1, cap, d), lambda e, b: (b, e, 0, 0)),
        scratch_shapes=[pltpu.VMEM((d, f), BF16), pltpu.VMEM((d, f), BF16), pltpu.VMEM((f, d), BF16)],
        compiler_params=_cparams(("arbitrary", "arbitrary")),
        name="ffn",
    )(xs, gc, w_gate, w_up, w_down)


def _combine_kernel(ws_ref, nx_ref, ye_ref, pos_ref, x1_ref, mod_ref, lnw_ref, lnb_ref, out_ref, acc_scr):
    b = pl.program_id(0)
    kb = pl.program_id(1)
    n_e = pos_ref.shape[2]
    cap = ye_ref.shape[2]
    d = D_MODEL
    base = (b * pl.num_programs(1) + kb) * n_e
    starts = [pl.multiple_of(ws_ref[base + e], WIN_ALIGN) for e in range(n_e)]
    onehot = jnp.concatenate(
        [(_window_rows(starts[e]) == pos_ref[0, 0, e:e + 1, :]).astype(BF16) for e in range(n_e)], axis=0)
    rows = jnp.concatenate([ye_ref[0, e, pl.ds(starts[e], WIN), :] for e in range(n_e)], axis=0)
    acc_scr[...] = _dot_t0(onehot, rows)

    for e in range(n_e):
        @pl.when(nx_ref[base + e] > 0)
        def _(e=e):
            prow = pos_ref[0, 0, e:e + 1, :]
            rest = prow >= starts[e] + WIN

            def body(w, carry):
                off = pl.multiple_of(w * WIN, WIN)
                m = (_window_rows(off) == prow) & rest
                acc_scr[...] += _dot_t0(m.astype(BF16), ye_ref[0, e, pl.ds(off, WIN), :])
                return carry

            lax.fori_loop(0, cap // WIN, body, 0)

    g2 = mod_ref[0, :, 5 * d:6 * d]
    out_ref[0] = _ln(DEEPNORM_ALPHA * x1_ref[0] + g2 * acc_scr[...]) * lnw_ref[...] + lnb_ref[...]


def _combine_call(ws, nx, ye, pos_b, x1, mod3, ln2w, ln2b):
    bsz, n_e, cap, d = ye.shape
    nblk, _, tb = pos_b.shape[1:]
    grid_spec = pltpu.PrefetchScalarGridSpec(
        num_scalar_prefetch=2,
        grid=(bsz, nblk),
        in_specs=[
            pl.BlockSpec((1, n_e, cap, d), lambda b, k, ws, nx: (b, 0, 0, 0), pipeline_mode=pl.Buffered(1)),
            pl.BlockSpec((1, 1, n_e, tb), lambda b, k, ws, nx: (b, k, 0, 0)),
            pl.BlockSpec((1, tb, d), lambda b, k, ws, nx: (b, k, 0)),
            pl.BlockSpec((1, 1, 6 * d), lambda b, k, ws, nx: (b, 0, 0)),
            pl.BlockSpec((1, d), lambda b, k, ws, nx: (0, 0)),
            pl.BlockSpec((1, d), lambda b, k, ws, nx: (0, 0)),
        ],
        out_specs=pl.BlockSpec((1, tb, d), lambda b, k, ws, nx: (b, k, 0)),
        scratch_shapes=[pltpu.VMEM((tb, d), F32)],
    )
    return pl.pallas_call(
        _combine_kernel,
        out_shape=jax.ShapeDtypeStruct((bsz, nblk * tb, d), F32),
        grid_spec=grid_spec,
        compiler_params=_cparams(("parallel", "arbitrary")),
        name="combine",
    )(ws, nx, ye, pos_b, x1, mod3, ln2w, ln2b)


def _rope_tables(n_tokens):
    rows = n_tokens // GRID_W
    n_freq = RET_QK_DIM // 4
    inv = ROPE_THETA ** (-jnp.arange(n_freq, dtype=F32) / n_freq)
    ang_r = jnp.arange(rows, dtype=F32)[:, None] * inv
    ang_c = jnp.arange(GRID_W, dtype=F32)[:, None] * inv
    shape = (rows, GRID_W, n_freq)
    cr = jnp.broadcast_to(jnp.cos(ang_r)[:, None, :], shape)
    sr = jnp.broadcast_to(jnp.sin(ang_r)[:, None, :], shape)
    cc = jnp.broadcast_to(jnp.cos(ang_c)[None, :, :], shape)
    sc = jnp.broadcast_to(jnp.sin(ang_c)[None, :, :], shape)
    cos_t = jnp.concatenate([cr, cr, cc, cc], axis=-1).reshape(n_tokens, RET_QK_DIM)
    sin_t = jnp.concatenate([-sr, sr, -sc, sc], axis=-1).reshape(n_tokens, RET_QK_DIM)
    return cos_t, sin_t


def kernel(x, c, ctx, c_ctx, w_ada, b_ada, w_in, b_in, ret_decay_f, ret_decay_b, ret_gn_w, ret_gn_b,
           sgu_ln_w, sgu_ln_b, sgu_w, sgu_b, w_pa, w_pb, w_o, b_o, ln1_w, ln1_b,
           w_router, b_router, w_gate, w_up, w_down, ln2_w, ln2_b):
    bsz, t, d = x.shape
    assert d == D_MODEL and w_ada.shape[0] == DEPTH == 1 and bsz < MOD_ROWS
    cap = EC_CAPACITY_FACTOR * t // N_EXPERTS
    assert t % TB_ROUTE == 0 and cap % WIN == 0 and N_EXPERTS % DISPATCH_GROUPS == 0
    l = 0

    cvec = jnp.zeros((MOD_ROWS, d), F32).at[:bsz].set(c).at[bsz].set(c_ctx)
    mod = _mod_call(cvec, w_ada[l], b_ada[l][None])
    mod3 = mod[:, None, :]

    w_in_b = w_in[l].astype(BF16)
    b_in2 = b_in[l][None]
    lo_kv, hi_kv = RET_QK_W, 2 * RET_QK_W + RET_V_W
    kctx, vctx = _ctx_kv_call(ctx, mod3, w_in_b[:, lo_kv:hi_kv], b_in2[:, lo_kv:hi_kv], bsz)

    cos_t, sin_t = _rope_tables(t)
    qk, v, sg, u, vs, ga, gb = _inproj_call(x, mod3, w_in_b, b_in2, cos_t, sin_t,
                                            sgu_ln_w[l][None], sgu_ln_b[l][None])

    lg = jnp.stack([jax.nn.log_sigmoid(ret_decay_f[l].astype(F32)),
                    jax.nn.log_sigmoid(ret_decay_b[l].astype(F32))])
    states = _state_call(lg, qk, v, kctx, vctx)

    x1, hm, logits_t = _mixer_call(
        lg, qk, v, sg, u, vs, ga, gb, states, x, mod3,
        w_pa[l].astype(BF16), w_pb[l].astype(BF16), w_o[l].astype(BF16),
        sgu_w[l].astype(BF16), sgu_b[l].T, ret_gn_w[l][None], ret_gn_b[l][None], b_o[l][None],
        ln1_w[l][None], ln1_b[l][None], w_router[l].T, b_router[l][:, None])

    pos_b, gate_b, cnt_b = _route_call(logits_t, cap)

    c0 = cnt_b[..., 0]
    c1 = jnp.concatenate([c0[:, 1:], jnp.full_like(c0[:, :1], cap)], axis=1)
    ws = jnp.minimum((c0 // WIN_ALIGN) * WIN_ALIGN, cap - WIN)
    nx = (c1 > ws + WIN).astype(I32)
    ws, nx = ws.reshape(-1), nx.reshape(-1)

    xs, gc = _gather_call(ws, nx, hm, pos_b, gate_b, cap)
    ye = _ffn_call(xs, gc, w_gate[l], w_up[l], w_down[l])
    return _combine_call(ws, nx, ye, pos_b, x1, mod3, ln2_w[l][None], ln2_b[l][None])
```

```python
import functools

import jax
import jax.numpy as jnp
from jax import lax
from jax.experimental import pallas as pl
from jax.experimental.pallas import tpu as pltpu

F32 = jnp.float32
BF16 = jnp.bfloat16
I32 = jnp.int32
HIGHEST = lax.Precision.HIGHEST

D_MODEL = 1024
DEPTH = 1
GRID_W = 64
RET_HEADS = 4
RET_QK_DIM = 128
RET_V_DIM = 256
RET_CHUNK = 128
ROPE_THETA = 10000.0
RET_QK_W = RET_HEADS * RET_QK_DIM
RET_V_W = RET_HEADS * RET_V_DIM
SGU_GROUPS = 4
SGU_CHUNK = 128
SGU_WIDTH = 1024
SGU_GW = SGU_WIDTH // SGU_GROUPS
N_EXPERTS = 16
EC_CAPACITY_FACTOR = 2
D_EXPERT = 1024
LN_EPS = 1e-6
DEEPNORM_ALPHA = (2.0 * DEPTH) ** 0.25
D_IN = 2 * RET_QK_W + 2 * RET_V_W + 2 * SGU_WIDTH + 2 * D_MODEL

V7X_VMEM_BYTES = 64 * 1024 * 1024
VMEM_LIMIT = 56 * 1024 * 1024
LANES = 128

MOD_ROWS = 8
TM_INPROJ = 512
TT_MIXER = 512
TB_ROUTE = 512
WIN = 128
WIN_ALIGN = 16
DISPATCH_GROUPS = 2
STATE_UNROLL = 4


def _cparams(sem):
    return pltpu.CompilerParams(dimension_semantics=sem, vmem_limit_bytes=VMEM_LIMIT)


def _ln(x):
    mu = jnp.mean(x, axis=-1, keepdims=True)
    xc = x - mu
    var = jnp.mean(xc * xc, axis=-1, keepdims=True)
    return xc * lax.rsqrt(var + LN_EPS)


def _iota_f(shape, dim):
    return lax.broadcasted_iota(I32, shape, dim).astype(F32)


def _mod_kernel(c_ref, w_ref, b_ref, o_ref):
    c = c_ref[...]
    a = c * jax.nn.sigmoid(c)
    o_ref[...] = jnp.dot(a, w_ref[...], preferred_element_type=F32, precision=HIGHEST) + b_ref[...]


def _mod_call(cvec, w_ada, b_ada):
    d, n = w_ada.shape
    tn = 1024
    return pl.pallas_call(
        _mod_kernel,
        out_shape=jax.ShapeDtypeStruct((MOD_ROWS, n), F32),
        grid=(n // tn,),
        in_specs=[
            pl.BlockSpec((MOD_ROWS, d), lambda j: (0, 0)),
            pl.BlockSpec((d, tn), lambda j: (0, j)),
            pl.BlockSpec((1, tn), lambda j: (0, j)),
        ],
        out_specs=pl.BlockSpec((MOD_ROWS, tn), lambda j: (0, j)),
        compiler_params=_cparams(("parallel",)),
        name="mod",
    )(cvec, w_ada, b_ada)


def _ctx_kv_kernel(x_ref, mod_ref, w_ref, b_ref, k_ref, v_ref):
    d = D_MODEL
    sh = mod_ref[0, :, 0:d]
    sc = mod_ref[0, :, d:2 * d]
    h = (_ln(x_ref[0]) * (1.0 + sc) + sh).astype(BF16)
    z = jnp.dot(h, w_ref[...], preferred_element_type=F32) + b_ref[...]
    k_ref[0] = (z[:, :RET_QK_W] * (RET_QK_DIM ** -0.5)).astype(BF16)
    v_ref[0] = z[:, RET_QK_W:].astype(BF16)


def _ctx_kv_call(ctx, mod3, w_kv, b_kv, ctx_row):
    bsz, n_ctx, d = ctx.shape
    wkv = w_kv.shape[1]
    return pl.pallas_call(
        _ctx_kv_kernel,
        out_shape=(jax.ShapeDtypeStruct((bsz, n_ctx, RET_QK_W), BF16),
                   jax.ShapeDtypeStruct((bsz, n_ctx, RET_V_W), BF16)),
        grid=(bsz,),
        in_specs=[
            pl.BlockSpec((1, n_ctx, d), lambda b: (b, 0, 0)),
            pl.BlockSpec((1, 1, 6 * d), lambda b: (ctx_row, 0, 0)),
            pl.BlockSpec((d, wkv), lambda b: (0, 0)),
            pl.BlockSpec((1, wkv), lambda b: (0, 0)),
        ],
        out_specs=(pl.BlockSpec((1, n_ctx, RET_QK_W), lambda b: (b, 0, 0)),
                   pl.BlockSpec((1, n_ctx, RET_V_W), lambda b: (b, 0, 0))),
        compiler_params=_cparams(("parallel",)),
        name="ctx_kv",
    )(ctx, mod3, w_kv, b_kv)


def _inproj_kernel(x_ref, mod_ref, w_ref, b_ref, cos_ref, sin_ref, lnw_ref, lnb_ref,
                   qk_ref, v_ref, sg_ref, u_ref, vs_ref, ga_ref, gb_ref):
    d = D_MODEL
    sh = mod_ref[0, :, 0:d]
    sc = mod_ref[0, :, d:2 * d]
    h = (_ln(x_ref[0]) * (1.0 + sc) + sh).astype(BF16)

    def proj(c0, width):
        return jnp.dot(h, w_ref[:, c0:c0 + width], preferred_element_type=F32) + b_ref[:, c0:c0 + width]

    zq = proj(0, 2 * RET_QK_W)
    cos = cos_ref[...]
    sin = sin_ref[...]
    tm = zq.shape[0]
    upper = (lax.broadcasted_iota(I32, (tm, LANES), 1) & 32) != 0
    for hb in range(2 * RET_HEADS):
        zs = zq[:, hb * LANES:(hb + 1) * LANES]
        if hb >= RET_HEADS:
            zs = zs * (RET_QK_DIM ** -0.5)
        sw = jnp.where(upper, pltpu.roll(zs, 32, 1), pltpu.roll(zs, LANES - 32, 1))
        qk_ref[0, :, hb * LANES:(hb + 1) * LANES] = (zs * cos + sw * sin).astype(BF16)

    c0 = 2 * RET_QK_W
    v_ref[0] = proj(c0, RET_V_W).astype(BF16)
    c0 += RET_V_W
    sg_ref[0] = jax.nn.silu(proj(c0, RET_V_W)).astype(BF16)
    c0 += RET_V_W
    u_ref[0] = jax.nn.gelu(proj(c0, SGU_WIDTH)).astype(BF16)
    c0 += SGU_WIDTH
    vs = _ln(jax.nn.gelu(proj(c0, SGU_WIDTH))) * lnw_ref[...] + lnb_ref[...]
    vs_ref[0] = vs.astype(BF16)
    c0 += SGU_WIDTH
    ga_ref[0] = jax.nn.sigmoid(proj(c0, d)).astype(BF16)
    c0 += d
    gb_ref[0] = jax.nn.sigmoid(proj(c0, d)).astype(BF16)


def _inproj_call(x, mod3, w_in, b_in, cos_t, sin_t, sgu_ln_w, sgu_ln_b):
    bsz, t, d = x.shape
    tm = TM_INPROJ
    row = lambda b, i: (b, i, 0)
    const2 = lambda b, i: (0, 0)
    outs = tuple(jax.ShapeDtypeStruct((bsz, t, 1024), BF16) for _ in range(7))
    return pl.pallas_call(
        _inproj_kernel,
        out_shape=outs,
        grid=(bsz, t // tm),
        in_specs=[
            pl.BlockSpec((1, tm, d), row),
            pl.BlockSpec((1, 1, 6 * d), lambda b, i: (b, 0, 0)),
            pl.BlockSpec((d, D_IN), const2, pipeline_mode=pl.Buffered(1)),
            pl.BlockSpec((1, D_IN), const2),
            pl.BlockSpec((tm, LANES), lambda b, i: (i, 0)),
            pl.BlockSpec((tm, LANES), lambda b, i: (i, 0)),
            pl.BlockSpec((1, SGU_WIDTH), const2),
            pl.BlockSpec((1, SGU_WIDTH), const2),
        ],
        out_specs=tuple(pl.BlockSpec((1, tm, 1024), row) for _ in range(7)),
        compiler_params=_cparams(("parallel", "parallel")),
        name="inproj",
    )(x, mod3, w_in, b_in, cos_t, sin_t, sgu_ln_w, sgu_ln_b)


def _dot_t0(a, b):
    return lax.dot_general(a, b, (((0,), (0,)), ((), ())), preferred_element_type=F32)


def _state_kernel(lg_ref, k_ref, v_ref, kc_ref, vc_ref, s_ref):
    hd = pl.program_id(1)
    lgf = lg_ref[0, hd]
    lgb = lg_ref[1, hd]
    n_chunks = s_ref.shape[2]
    n_ctx = kc_ref.shape[1]
    c = RET_CHUNK
    li = _iota_f((c, RET_QK_DIM), 0)
    zeta_f = jnp.exp((c - 1.0 - li) * lgf)
    zeta_b = jnp.exp(li * lgb)
    one = jnp.ones((1, 1), F32)
    cd_f = jnp.exp(one * (c * lgf))
    cd_b = jnp.exp(one * (c * lgb))

    tc = _iota_f((n_ctx, RET_QK_DIM), 0)
    kc = kc_ref[0].astype(F32)
    vc = vc_ref[0]
    s0f = _dot_t0((kc * jnp.exp((n_ctx - 1.0 - tc) * lgf)).astype(BF16), vc)
    s0b = _dot_t0((kc * jnp.exp(tc * lgb)).astype(BF16), vc)

    def chunk_kv(n, zeta):
        off = pl.multiple_of(n * c, c)
        kk = k_ref[0, pl.ds(off, c), :].astype(F32)
        return _dot_t0((kk * zeta).astype(BF16), v_ref[0, pl.ds(off, c), :])

    def step(i, carry):
        sf, sb = carry
        nb = n_chunks - 1 - i
        s_ref[0, 0, i, 0:RET_QK_DIM, :] = sf.astype(BF16)
        s_ref[0, 0, nb, RET_QK_DIM:2 * RET_QK_DIM, :] = sb.astype(BF16)
        return cd_f * sf + chunk_kv(i, zeta_f), cd_b * sb + chunk_kv(nb, zeta_b)

    lax.fori_loop(0, n_chunks, step, (s0f, s0b), unroll=STATE_UNROLL)


def _state_call(lg, qk, v, kctx, vctx):
    bsz, t, _ = qk.shape
    n_ctx = kctx.shape[1]
    n_chunks = t // RET_CHUNK
    return pl.pallas_call(
        _state_kernel,
        out_shape=jax.ShapeDtypeStruct((bsz, RET_HEADS, n_chunks, 2 * RET_QK_DIM, RET_V_DIM), BF16),
        grid=(bsz, RET_HEADS),
        in_specs=[
            pl.BlockSpec(memory_space=pltpu.SMEM),
            pl.BlockSpec((1, t, RET_QK_DIM), lambda b, h: (b, 0, RET_HEADS + h)),
            pl.BlockSpec((1, t, RET_V_DIM), lambda b, h: (b, 0, h)),
            pl.BlockSpec((1, n_ctx, RET_QK_DIM), lambda b, h: (b, 0, h)),
            pl.BlockSpec((1, n_ctx, RET_V_DIM), lambda b, h: (b, 0, h)),
        ],
        out_specs=pl.BlockSpec((1, 1, n_chunks, 2 * RET_QK_DIM, RET_V_DIM), lambda b, h: (b, h, 0, 0, 0)),
        compiler_params=_cparams(("parallel", "parallel")),
        name="states",
    )(lg, qk, v, kctx, vctx)


def _mixer_kernel(lg_ref, qk_ref, v_ref, sg_ref, u_ref, vs_ref, ga_ref, gb_ref, s_ref, x_ref, mod_ref,
                  wpa_ref, wpb_ref, wo_ref, sguw_ref, sgub_ref, gnw_ref, gnb_ref, bo_ref,
                  ln1w_ref, ln1b_ref, wr_ref, br_ref,
                  x1_ref, hm_ref, lgt_ref, ret_scr, sgu_scr):
    d = D_MODEL
    c = RET_CHUNK
    n_sub = qk_ref.shape[1] // c
    row = lax.broadcasted_iota(I32, (c, c), 0)
    col = lax.broadcasted_iota(I32, (c, c), 1)
    diff = (row - col).astype(F32)
    rowq = _iota_f((c, RET_QK_DIM), 0)

    for hd in range(RET_HEADS):
        lgf = lg_ref[0, hd]
        lgb = lg_ref[1, hd]
        mask = (jnp.where(diff >= 0, jnp.exp(jnp.maximum(diff, 0.0) * lgf), 0.0)
                + jnp.where(diff <= 0, jnp.exp(jnp.maximum(-diff, 0.0) * lgb), 0.0))
        xi_f = jnp.exp((rowq + 1.0) * lgf)
        xi_b = jnp.exp((c - rowq) * lgb)
        for ci in range(n_sub):
            r0 = ci * c
            q = qk_ref[0, r0:r0 + c, hd * RET_QK_DIM:(hd + 1) * RET_QK_DIM]
            k = qk_ref[0, r0:r0 + c, RET_QK_W + hd * RET_QK_DIM:RET_QK_W + (hd + 1) * RET_QK_DIM]
            vv = v_ref[0, r0:r0 + c, hd * RET_V_DIM:(hd + 1) * RET_V_DIM]
            s = lax.dot_general(q, k, (((1,), (1,)), ((), ())), preferred_element_type=F32)
            intra = jnp.dot((s * mask).astype(BF16), vv, preferred_element_type=F32)
            qf = q.astype(F32)
            qx = jnp.concatenate([(qf * xi_f).astype(BF16), (qf * xi_b).astype(BF16)], axis=1)
            cross = jnp.dot(qx, s_ref[0, hd, ci], preferred_element_type=F32)
            o = _ln(intra + cross)
            o = o * gnw_ref[:, hd * RET_V_DIM:(hd + 1) * RET_V_DIM] + gnb_ref[:, hd * RET_V_DIM:(hd + 1) * RET_V_DIM]
            gate = sg_ref[0, r0:r0 + c, hd * RET_V_DIM:(hd + 1) * RET_V_DIM].astype(F32)
            ret_scr[r0:r0 + c, hd * RET_V_DIM:(hd + 1) * RET_V_DIM] = (o * gate).astype(BF16)

    for g in range(SGU_GROUPS):
        wg = sguw_ref[g]
        bg = sgub_ref[:, g:g + 1]
        for ci in range(n_sub):
            r0 = ci * c
            vsb = vs_ref[0, r0:r0 + c, g * SGU_GW:(g + 1) * SGU_GW]
            sp = jnp.dot(wg, vsb, preferred_element_type=F32) + bg
            ub = u_ref[0, r0:r0 + c, g * SGU_GW:(g + 1) * SGU_GW].astype(F32)
            sgu_scr[r0:r0 + c, g * SGU_GW:(g + 1) * SGU_GW] = (ub * sp).astype(BF16)

    pa = jnp.dot(ret_scr[...], wpa_ref[...], preferred_element_type=F32)
    pb = jnp.dot(sgu_scr[...], wpb_ref[...], preferred_element_type=F32)
    y = (ga_ref[0].astype(F32) * pa + gb_ref[0].astype(F32) * pb).astype(BF16)
    mix = jnp.dot(y, wo_ref[...], preferred_element_type=F32) + bo_ref[...]
    g1 = mod_ref[0, :, 2 * d:3 * d]
    sh2 = mod_ref[0, :, 3 * d:4 * d]
    sc2 = mod_ref[0, :, 4 * d:5 * d]
    x1 = _ln(DEEPNORM_ALPHA * x_ref[0] + g1 * mix) * ln1w_ref[...] + ln1b_ref[...]
    x1_ref[0] = x1
    hm = _ln(x1) * (1.0 + sc2) + sh2
    hm_ref[0] = hm.astype(BF16)
    lgt_ref[0] = lax.dot_general(wr_ref[...], hm, (((1,), (1,)), ((), ())),
                                 preferred_element_type=F32, precision=HIGHEST) + br_ref[...]


def _mixer_call(lg, qk, v, sg, u, vs, ga, gb, states, x, mod3, wpa, wpb, wo, sguw, sgub_t, gnw, gnb, bo,
                ln1w, ln1b, wr_t, br):
    bsz, t, d = x.shape
    tt = TT_MIXER
    n_sub = tt // RET_CHUNK
    row = lambda b, i: (b, i, 0)
    c2 = lambda b, i: (0, 0)
    c3 = lambda b, i: (0, 0, 0)
    act = pl.BlockSpec((1, tt, 1024), row)
    return pl.pallas_call(
        _mixer_kernel,
        out_shape=(jax.ShapeDtypeStruct((bsz, t, d), F32),
                   jax.ShapeDtypeStruct((bsz, t, d), BF16),
                   jax.ShapeDtypeStruct((bsz, N_EXPERTS, t), F32)),
        grid=(bsz, t // tt),
        in_specs=[
            pl.BlockSpec(memory_space=pltpu.SMEM),
            act, act, act, act, act, act, act,
            pl.BlockSpec((1, RET_HEADS, n_sub, 2 * RET_QK_DIM, RET_V_DIM), lambda b, i: (b, 0, i, 0, 0)),
            pl.BlockSpec((1, tt, d), row),
            pl.BlockSpec((1, 1, 6 * d), lambda b, i: (b, 0, 0)),
            pl.BlockSpec((RET_V_W, d), c2, pipeline_mode=pl.Buffered(1)),
            pl.BlockSpec((SGU_WIDTH, d), c2, pipeline_mode=pl.Buffered(1)),
            pl.BlockSpec((d, d), c2, pipeline_mode=pl.Buffered(1)),
            pl.BlockSpec((SGU_GROUPS, SGU_CHUNK, SGU_CHUNK), c3),
            pl.BlockSpec((SGU_CHUNK, SGU_GROUPS), c2),
            pl.BlockSpec((1, RET_V_W), c2),
            pl.BlockSpec((1, RET_V_W), c2),
            pl.BlockSpec((1, d), c2),
            pl.BlockSpec((1, d), c2),
            pl.BlockSpec((1, d), c2),
            pl.BlockSpec((N_EXPERTS, d), c2),
            pl.BlockSpec((N_EXPERTS, 1), c2),
        ],
        out_specs=(pl.BlockSpec((1, tt, d), row),
                   pl.BlockSpec((1, tt, d), row),
                   pl.BlockSpec((1, N_EXPERTS, tt), lambda b, i: (b, 0, i))),
        scratch_shapes=[pltpu.VMEM((tt, RET_V_W), BF16), pltpu.VMEM((tt, SGU_WIDTH), BF16)],
        compiler_params=_cparams(("parallel", "parallel")),
        name="mixer",
    )(lg, qk, v, sg, u, vs, ga, gb, states, x, mod3, wpa, wpb, wo, sguw, sgub_t, gnw, gnb, bo,
      ln1w, ln1b, wr_t, br)


def _route_kernel(lgt_ref, pos_ref, gate_ref, cnt_ref, *, cap):
    lg = lgt_ref[0]
    n_e, t = lg.shape
    tb = TB_ROUTE
    m = jnp.max(lg, axis=0, keepdims=True)
    ex = jnp.exp(lg - m)
    aff = ex / jnp.sum(ex, axis=0, keepdims=True)

    def search(i, thr_bits):
        cand = thr_bits | lax.shift_left(jnp.int32(1), 30 - i)
        cnt = jnp.sum((aff >= lax.bitcast_convert_type(cand, F32)).astype(I32), axis=1, keepdims=True)
        return jnp.where(cnt >= cap, cand, thr_bits)

    thr_bits = lax.fori_loop(0, 31, search, jnp.zeros((n_e, 1), I32))
    floor_f = lax.bitcast_convert_type(thr_bits, F32)
    thr = jnp.min(jnp.where(aff >= floor_f, aff, jnp.inf), axis=1, keepdims=True)
    need = (cap - jnp.sum((aff > thr).astype(I32), axis=1, keepdims=True)).astype(F32)

    r = lax.broadcasted_iota(I32, (tb, tb), 0)
    cc = lax.broadcasted_iota(I32, (tb, tb), 1)
    tri = (r <= cc).astype(BF16)
    carry_eq = jnp.zeros((n_e, 1), F32)
    carry_sel = jnp.zeros((n_e, 1), F32)
    for blk in range(t // tb):
        sl = slice(blk * tb, (blk + 1) * tb)
        aff_b = aff[:, sl]
        eq = aff_b == thr
        eq_b = eq.astype(BF16)
        inc_eq = jnp.dot(eq_b, tri, preferred_element_type=F32)
        before = carry_eq + inc_eq - eq_b.astype(F32)
        sel = (aff_b > thr) | (eq & (before < need))
        sel_b = sel.astype(BF16)
        inc_sel = jnp.dot(sel_b, tri, preferred_element_type=F32)
        pos = carry_sel + inc_sel - 1.0
        pos_ref[0, blk] = jnp.where(sel, pos.astype(I32), -1)
        gate_ref[0, blk] = aff_b
        cnt_ref[0, blk] = jnp.broadcast_to(carry_sel, (n_e, LANES)).astype(I32)
        carry_eq = carry_eq + inc_eq[:, tb - 1:tb]
        carry_sel = carry_sel + inc_sel[:, tb - 1:tb]


def _route_call(logits_t, cap):
    bsz, n_e, t = logits_t.shape
    nblk = t // TB_ROUTE
    return pl.pallas_call(
        functools.partial(_route_kernel, cap=cap),
        out_shape=(jax.ShapeDtypeStruct((bsz, nblk, n_e, TB_ROUTE), I32),
                   jax.ShapeDtypeStruct((bsz, nblk, n_e, TB_ROUTE), F32),
                   jax.ShapeDtypeStruct((bsz, nblk, n_e, LANES), I32)),
        grid=(bsz,),
        in_specs=[pl.BlockSpec((1, n_e, t), lambda b: (b, 0, 0))],
        out_specs=(pl.BlockSpec((1, nblk, n_e, TB_ROUTE), lambda b: (b, 0, 0, 0)),
                   pl.BlockSpec((1, nblk, n_e, TB_ROUTE), lambda b: (b, 0, 0, 0)),
                   pl.BlockSpec((1, nblk, n_e, LANES), lambda b: (b, 0, 0, 0))),
        compiler_params=_cparams(("parallel",)),
        name="route",
    )(logits_t)


def _window_rows(ws):
    return lax.broadcasted_iota(I32, (WIN, TB_ROUTE), 0) + ws


def _gather_kernel(ws_ref, nx_ref, hm_ref, pos_ref, gate_ref, xs_ref, gc_ref):
    b = pl.program_id(0)
    g = pl.program_id(1)
    kb = pl.program_id(2)
    eg = pos_ref.shape[2]
    n_e = eg * pl.num_programs(1)
    cap = xs_ref.shape[2]

    @pl.when(kb == 0)
    def _():
        xs_ref[...] = jnp.zeros_like(xs_ref)
        gc_ref[...] = jnp.zeros_like(gc_ref)

    hb = hm_ref[0]
    base = (b * pl.num_programs(2) + kb) * n_e + g * eg
    starts = [pl.multiple_of(ws_ref[base + el], WIN_ALIGN) for el in range(eg)]
    for el in range(eg):
        match = _window_rows(starts[el]) == pos_ref[0, 0, el:el + 1, :]
        win = pl.ds(starts[el], WIN)
        xs_ref[0, el, win, :] += jnp.dot(match.astype(BF16), hb, preferred_element_type=F32).astype(BF16)
        gc_ref[0, el, win, :] += jnp.sum(jnp.where(match, gate_ref[0, 0, el:el + 1, :], 0.0),
                                         axis=1, keepdims=True)

    for el in range(eg):
        @pl.when(nx_ref[base + el] > 0)
        def _(el=el):
            prow = pos_ref[0, 0, el:el + 1, :]
            grow = gate_ref[0, 0, el:el + 1, :]
            rest = prow >= starts[el] + WIN

            def body(w, carry):
                off = pl.multiple_of(w * WIN, WIN)
                m = (_window_rows(off) == prow) & rest
                xs_ref[0, el, pl.ds(off, WIN), :] += jnp.dot(m.astype(BF16), hb,
                                                              preferred_element_type=F32).astype(BF16)
                gc_ref[0, el, pl.ds(off, WIN), :] += jnp.sum(jnp.where(m, grow, 0.0), axis=1, keepdims=True)
                return carry

            lax.fori_loop(0, cap // WIN, body, 0)


def _gather_call(ws, nx, hm, pos_b, gate_b, cap):
    bsz, t, d = hm.shape
    nblk, n_e, tb = pos_b.shape[1:]
    eg = n_e // DISPATCH_GROUPS
    grid_spec = pltpu.PrefetchScalarGridSpec(
        num_scalar_prefetch=2,
        grid=(bsz, DISPATCH_GROUPS, nblk),
        in_specs=[
            pl.BlockSpec((1, tb, d), lambda b, g, k, ws, nx: (b, k, 0)),
            pl.BlockSpec((1, 1, eg, tb), lambda b, g, k, ws, nx: (b, k, g, 0)),
            pl.BlockSpec((1, 1, eg, tb), lambda b, g, k, ws, nx: (b, k, g, 0)),
        ],
        out_specs=(pl.BlockSpec((1, eg, cap, d), lambda b, g, k, ws, nx: (b, g, 0, 0)),
                   pl.BlockSpec((1, eg, cap, 1), lambda b, g, k, ws, nx: (b, g, 0, 0))),
    )
    return pl.pallas_call(
        _gather_kernel,
        out_shape=(jax.ShapeDtypeStruct((bsz, n_e, cap, d), BF16),
                   jax.ShapeDtypeStruct((bsz, n_e, cap, 1), F32)),
        grid_spec=grid_spec,
        compiler_params=_cparams(("parallel", "parallel", "arbitrary")),
        name="gather",
    )(ws, nx, hm, pos_b, gate_b)


def _ffn_kernel(xs_ref, gc_ref, wg_ref, wu_ref, wd_ref, ye_ref, wg_s, wu_s, wd_s):
    @pl.when(pl.program_id(1) == 0)
    def _():
        wg_s[...] = wg_ref[0].astype(BF16)
        wu_s[...] = wu_ref[0].astype(BF16)
        wd_s[...] = wd_ref[0].astype(BF16)

    rt = 256
    for j in range(xs_ref.shape[2] // rt):
        sl = slice(j * rt, (j + 1) * rt)
        xs = xs_ref[0, 0, sl, :]
        xg = jnp.dot(xs, wg_s[...], preferred_element_type=F32)
        xu = jnp.dot(xs, wu_s[...], preferred_element_type=F32)
        hid = (jax.nn.silu(xg) * xu).astype(BF16)
        ye = jnp.dot(hid, wd_s[...], preferred_element_type=F32) * gc_ref[0, 0, sl, :]
        ye_ref[0, 0, sl, :] = ye.astype(BF16)


def _ffn_call(xs, gc, w_gate, w_up, w_down):
    bsz, n_e, cap, d = xs.shape
    f = w_gate.shape[2]
    return pl.pallas_call(
        _ffn_kernel,
        out_shape=jax.ShapeDtypeStruct((bsz, n_e, cap, d), BF16),
        grid=(n_e, bsz),
        in_specs=[
            pl.BlockSpec((1, 1, cap, d), lambda e, b: (b, e, 0, 0)),
            pl.BlockSpec((1, 1, cap, 1), lambda e, b: (b, e, 0, 0)),
            pl.BlockSpec((1, d, f), lambda e, b: (e, 0, 0)),
            pl.BlockSpec((1, d, f), lambda e, b: (e, 0, 0)),
            pl.BlockSpec((1, f, d), lambda e, b: (e, 0, 0)),
        ],
        out_specs=pl.BlockSpec((1, 1, cap, d), lambda e, b: (b, e, 0, 0)),
        scratch_shapes=[pltpu.VMEM((d, f), BF16), pltpu.VMEM((d, f), BF16), pltpu.VMEM((f, d), BF16)],
        compiler_params=_cparams(("arbitrary", "arbitrary")),
        name="ffn",
    )(xs, gc, w_gate, w_up, w_down)


def _combine_kernel(ws_ref, nx_ref, ye_ref, pos_ref, x1_ref, mod_ref, lnw_ref, lnb_ref, out_ref, acc_scr):
    b = pl.program_id(0)
    kb = pl.program_id(1)
    n_e = pos_ref.shape[2]
    cap = ye_ref.shape[2]
    d = D_MODEL
    base = (b * pl.num_programs(1) + kb) * n_e
    starts = [pl.multiple_of(ws_ref[base + e], WIN_ALIGN) for e in range(n_e)]
    onehot = jnp.concatenate(
        [(_window_rows(starts[e]) == pos_ref[0, 0, e:e + 1, :]).astype(BF16) for e in range(n_e)], axis=0)
    rows = jnp.concatenate([ye_ref[0, e, pl.ds(starts[e], WIN), :] for e in range(n_e)], axis=0)
    acc_scr[...] = _dot_t0(onehot, rows)

    for e in range(n_e):
        @pl.when(nx_ref[base + e] > 0)
        def _(e=e):
            prow = pos_ref[0, 0, e:e + 1, :]
            rest = prow >= starts[e] + WIN

            def body(w, carry):
                off = pl.multiple_of(w * WIN, WIN)
                m = (_window_rows(off) == prow) & rest
                acc_scr[...] += _dot_t0(m.astype(BF16), ye_ref[0, e, pl.ds(off, WIN), :])
                return carry

            lax.fori_loop(0, cap // WIN, body, 0)

    g2 = mod_ref[0, :, 5 * d:6 * d]
    out_ref[0] = _ln(DEEPNORM_ALPHA * x1_ref[0] + g2 * acc_scr[...]) * lnw_ref[...] + lnb_ref[...]


def _combine_call(ws, nx, ye, pos_b, x1, mod3, ln2w, ln2b):
    bsz, n_e, cap, d = ye.shape
    nblk, _, tb = pos_b.shape[1:]
    grid_spec = pltpu.PrefetchScalarGridSpec(
        num_scalar_prefetch=2,
        grid=(bsz, nblk),
        in_specs=[
            pl.BlockSpec((1, n_e, cap, d), lambda b, k, ws, nx: (b, 0, 0, 0), pipeline_mode=pl.Buffered(1)),
            pl.BlockSpec((1, 1, n_e, tb), lambda b, k, ws, nx: (b, k, 0, 0)),
            pl.BlockSpec((1, tb, d), lambda b, k, ws, nx: (b, k, 0)),
            pl.BlockSpec((1, 1, 6 * d), lambda b, k, ws, nx: (b, 0, 0)),
            pl.BlockSpec((1, d), lambda b, k, ws, nx: (0, 0)),
            pl.BlockSpec((1, d), lambda b, k, ws, nx: (0, 0)),
        ],
        out_specs=pl.BlockSpec((1, tb, d), lambda b, k, ws, nx: (b, k, 0)),
        scratch_shapes=[pltpu.VMEM((tb, d), F32)],
    )
    return pl.pallas_call(
        _combine_kernel,
        out_shape=jax.ShapeDtypeStruct((bsz, nblk * tb, d), F32),
        grid_spec=grid_spec,
        compiler_params=_cparams(("parallel", "arbitrary")),
        name="combine",
    )(ws, nx, ye, pos_b, x1, mod3, ln2w, ln2b)


def _rope_tables(n_tokens):
    rows = n_tokens // GRID_W
    n_freq = RET_QK_DIM // 4
    inv = ROPE_THETA ** (-jnp.arange(n_freq, dtype=F32) / n_freq)
    ang_r = jnp.arange(rows, dtype=F32)[:, None] * inv
    ang_c = jnp.arange(GRID_W, dtype=F32)[:, None] * inv
    shape = (rows, GRID_W, n_freq)
    cr = jnp.broadcast_to(jnp.cos(ang_r)[:, None, :], shape)
    sr = jnp.broadcast_to(jnp.sin(ang_r)[:, None, :], shape)
    cc = jnp.broadcast_to(jnp.cos(ang_c)[None, :, :], shape)
    sc = jnp.broadcast_to(jnp.sin(ang_c)[None, :, :], shape)
    cos_t = jnp.concatenate([cr, cr, cc, cc], axis=-1).reshape(n_tokens, RET_QK_DIM)
    sin_t = jnp.concatenate([-sr, sr, -sc, sc], axis=-1).reshape(n_tokens, RET_QK_DIM)
    return cos_t, sin_t


def kernel(x, c, ctx, c_ctx, w_ada, b_ada, w_in, b_in, ret_decay_f, ret_decay_b, ret_gn_w, ret_gn_b,
           sgu_ln_w, sgu_ln_b, sgu_w, sgu_b, w_pa, w_pb, w_o, b_o, ln1_w, ln1_b,
           w_router, b_router, w_gate, w_up, w_down, ln2_w, ln2_b):
    bsz, t, d = x.shape
    assert d == D_MODEL and w_ada.shape[0] == DEPTH == 1 and bsz < MOD_ROWS
    cap = EC_CAPACITY_FACTOR * t // N_EXPERTS
    assert t % TB_ROUTE == 0 and cap % WIN == 0 and N_EXPERTS % DISPATCH_GROUPS == 0
    l = 0

    cvec = jnp.zeros((MOD_ROWS, d), F32).at[:bsz].set(c).at[bsz].set(c_ctx)
    mod = _mod_call(cvec, w_ada[l], b_ada[l][None])
    mod3 = mod[:, None, :]

    w_in_b = w_in[l].astype(BF16)
    b_in2 = b_in[l][None]
    lo_kv, hi_kv = RET_QK_W, 2 * RET_QK_W + RET_V_W
    kctx, vctx = _ctx_kv_call(ctx, mod3, w_in_b[:, lo_kv:hi_kv], b_in2[:, lo_kv:hi_kv], bsz)

    cos_t, sin_t = _rope_tables(t)
    qk, v, sg, u, vs, ga, gb = _inproj_call(x, mod3, w_in_b, b_in2, cos_t, sin_t,
                                            sgu_ln_w[l][None], sgu_ln_b[l][None])

    lg = jnp.stack([jax.nn.log_sigmoid(ret_decay_f[l].astype(F32)),
                    jax.nn.log_sigmoid(ret_decay_b[l].astype(F32))])
    states = _state_call(lg, qk, v, kctx, vctx)

    x1, hm, logits_t = _mixer_call(
        lg, qk, v, sg, u, vs, ga, gb, states, x, mod3,
        w_pa[l].astype(BF16), w_pb[l].astype(BF16), w_o[l].astype(BF16),
        sgu_w[l].astype(BF16), sgu_b[l].T, ret_gn_w[l][None], ret_gn_b[l][None], b_o[l][None],
        ln1_w[l][None], ln1_b[l][None], w_router[l].T, b_router[l][:, None])

    pos_b, gate_b, cnt_b = _route_call(logits_t, cap)

    c0 = cnt_b[..., 0]
    c1 = jnp.concatenate([c0[:, 1:], jnp.full_like(c0[:, :1], cap)], axis=1)
    ws = jnp.minimum((c0 // WIN_ALIGN) * WIN_ALIGN, cap - WIN)
    nx = (c1 > ws + WIN).astype(I32)
    ws, nx = ws.reshape(-1), nx.reshape(-1)

    xs, gc = _gather_call(ws, nx, hm, pos_b, gate_b, cap)
    ye = _ffn_call(xs, gc, w_gate[l], w_up[l], w_down[l])
    return _combine_call(ws, nx, ye, pos_b, x1, mod3, ln2_w[l][None], ln2_b[l][None])
```

```python
import functools

import jax
import jax.numpy as jnp
from jax import lax
from jax.experimental import pallas as pl
from jax.experimental.pallas import tpu as pltpu

F32 = jnp.float32
BF16 = jnp.bfloat16
I32 = jnp.int32

D_MODEL = 1024
DEPTH = 1
GRID_W = 64
RET_HEADS = 4
RET_QK_DIM = 128
RET_V_DIM = 256
RET_CHUNK = 128
ROPE_THETA = 10000.0
RET_QK_W = RET_HEADS * RET_QK_DIM
RET_V_W = RET_HEADS * RET_V_DIM
SGU_GROUPS = 4
SGU_CHUNK = 128
SGU_WIDTH = 1024
SGU_GW = SGU_WIDTH // SGU_GROUPS
N_EXPERTS = 16
EC_CAPACITY_FACTOR = 2
D_EXPERT = 1024
LN_EPS = 1e-6
DEEPNORM_ALPHA = (2.0 * DEPTH) ** 0.25
D_IN = 2 * RET_QK_W + 2 * RET_V_W + 2 * SGU_WIDTH + 2 * D_MODEL
Z_V = 2 * RET_QK_W
Z_SG = Z_V + RET_V_W
Z_U = Z_SG + RET_V_W
Z_VS = Z_U + SGU_WIDTH
Z_GA = Z_VS + SGU_WIDTH
Z_GB = Z_GA + D_MODEL
Z_BLOCK = 3 * 1024

V7X_VMEM_BYTES = 64 * 1024 * 1024
VMEM_LIMIT = 56 * 1024 * 1024
LANES = 128

MOD_ROWS = 8
TM_INPROJ = 512
INPROJ_CHUNK = 256
TT_MIXER = 512
TB_ROUTE = 512
WIN_DISPATCH = 128
WIN_COMBINE = 96
WIN_ALIGN = 16
DISPATCH_GROUPS = 2
STATE_UNROLL = 8
FFN_ROWS = 256


def _cparams(sem):
    return pltpu.CompilerParams(dimension_semantics=sem, vmem_limit_bytes=VMEM_LIMIT)


def _ln(x):
    mu = jnp.mean(x, axis=-1, keepdims=True)
    xc = x - mu
    var = jnp.mean(xc * xc, axis=-1, keepdims=True)
    return xc * lax.rsqrt(var + LN_EPS)


def _iota_f(shape, dim):
    return lax.broadcasted_iota(I32, shape, dim).astype(F32)


def _mod_kernel(ct_ref, w_ref, b_ref, o_ref, *, n_rows):
    c = ct_ref[...]
    a = c * jax.nn.sigmoid(c)
    w = w_ref[...]
    rows = [jnp.sum(w * a[:, m:m + 1], axis=0, keepdims=True) + b_ref[...] for m in range(n_rows)]
    rows += [jnp.zeros_like(rows[0])] * (MOD_ROWS - n_rows)
    o_ref[...] = jnp.concatenate(rows, axis=0)


def _mod_call(cvec_t, w_ada, b_ada, n_rows):
    d, n = w_ada.shape
    tn = 1024
    return pl.pallas_call(
        functools.partial(_mod_kernel, n_rows=n_rows),
        out_shape=jax.ShapeDtypeStruct((MOD_ROWS, n), F32),
        grid=(n // tn,),
        in_specs=[
            pl.BlockSpec((d, MOD_ROWS), lambda j: (0, 0)),
            pl.BlockSpec((d, tn), lambda j: (0, j)),
            pl.BlockSpec((1, tn), lambda j: (0, j)),
        ],
        out_specs=pl.BlockSpec((MOD_ROWS, tn), lambda j: (0, j)),
        compiler_params=_cparams(("parallel",)),
        name="mod",
    )(cvec_t, w_ada, b_ada)


def _ctx_kv_kernel(x_ref, mod_ref, w_ref, b_ref, k_ref, v_ref):
    d = D_MODEL
    sh = mod_ref[0, :, 0:d]
    sc = mod_ref[0, :, d:2 * d]
    h = (_ln(x_ref[0]) * (1.0 + sc) + sh).astype(BF16)
    z = jnp.dot(h, w_ref[...], preferred_element_type=F32) + b_ref[...]
    k_ref[0] = (z[:, :RET_QK_W] * (RET_QK_DIM ** -0.5)).astype(BF16)
    v_ref[0] = z[:, RET_QK_W:].astype(BF16)


def _ctx_kv_call(ctx, mod3, w_kv, b_kv, ctx_row):
    bsz, n_ctx, d = ctx.shape
    wkv = w_kv.shape[1]
    return pl.pallas_call(
        _ctx_kv_kernel,
        out_shape=(jax.ShapeDtypeStruct((bsz, n_ctx, RET_QK_W), BF16),
                   jax.ShapeDtypeStruct((bsz, n_ctx, RET_V_W), BF16)),
        grid=(bsz,),
        in_specs=[
            pl.BlockSpec((1, n_ctx, d), lambda b: (b, 0, 0)),
            pl.BlockSpec((1, 1, 6 * d), lambda b: (ctx_row, 0, 0)),
            pl.BlockSpec((d, wkv), lambda b: (0, 0)),
            pl.BlockSpec((1, wkv), lambda b: (0, 0)),
        ],
        out_specs=(pl.BlockSpec((1, n_ctx, RET_QK_W), lambda b: (b, 0, 0)),
                   pl.BlockSpec((1, n_ctx, RET_V_W), lambda b: (b, 0, 0))),
        compiler_params=_cparams(("parallel",)),
        name="ctx_kv",
    )(ctx, mod3, w_kv, b_kv)


def _inproj_kernel(x_ref, mod_ref, w_ref, b_ref, rope_ref, lnw_ref, lnb_ref, z_ref):
    d = D_MODEL
    sh = mod_ref[0, :, 0:d]
    sc = mod_ref[0, :, d:2 * d]
    h = (_ln(x_ref[0]) * (1.0 + sc) + sh).astype(BF16)

    def proj(c0, width):
        return jnp.dot(h, w_ref[:, c0:c0 + width], preferred_element_type=F32) + b_ref[:, c0:c0 + width]

    zq = proj(0, 2 * RET_QK_W)
    cos = rope_ref[:, 0:LANES]
    sin = rope_ref[:, LANES:2 * LANES]
    tm = zq.shape[0]
    upper = (lax.broadcasted_iota(I32, (tm, LANES), 1) & 32) != 0
    for hb in range(2 * RET_HEADS):
        zs = zq[:, hb * LANES:(hb + 1) * LANES]
        if hb >= RET_HEADS:
            zs = zs * (RET_QK_DIM ** -0.5)
        sw = jnp.where(upper, pltpu.roll(zs, 32, 1), pltpu.roll(zs, LANES - 32, 1))
        z_ref[0, :, hb * LANES:(hb + 1) * LANES] = (zs * cos + sw * sin).astype(BF16)

    def emit(c0, width, fn):
        for cc in range(c0, c0 + width, INPROJ_CHUNK):
            z_ref[0, :, cc:cc + INPROJ_CHUNK] = fn(proj(cc, INPROJ_CHUNK)).astype(BF16)

    vs = _ln(jax.nn.gelu(proj(Z_VS, SGU_WIDTH))) * lnw_ref[...] + lnb_ref[...]
    z_ref[0, :, Z_VS:Z_VS + SGU_WIDTH] = vs.astype(BF16)
    emit(Z_U, SGU_WIDTH, jax.nn.gelu)
    emit(Z_SG, RET_V_W, jax.nn.silu)
    emit(Z_GA, d, jax.nn.sigmoid)
    emit(Z_GB, d, jax.nn.sigmoid)
    emit(Z_V, RET_V_W, lambda z: z)


def _inproj_call(x, mod3, w_in, b_in, rope_t, sgu_ln_w, sgu_ln_b):
    bsz, t, d = x.shape
    tm = TM_INPROJ
    row = lambda b, i: (b, i, 0)
    const2 = lambda b, i: (0, 0)
    return pl.pallas_call(
        _inproj_kernel,
        out_shape=jax.ShapeDtypeStruct((bsz, t, D_IN), BF16),
        grid=(bsz, t // tm),
        in_specs=[
            pl.BlockSpec((1, tm, d), row),
            pl.BlockSpec((1, 1, 6 * d), lambda b, i: (b, 0, 0)),
            pl.BlockSpec((d, D_IN), const2, pipeline_mode=pl.Buffered(1)),
            pl.BlockSpec((1, D_IN), const2),
            pl.BlockSpec((tm, 2 * LANES), lambda b, i: (i, 0)),
            pl.BlockSpec((1, SGU_WIDTH), const2),
            pl.BlockSpec((1, SGU_WIDTH), const2),
        ],
        out_specs=pl.BlockSpec((1, tm, D_IN), row),
        compiler_params=_cparams(("parallel", "parallel")),
        name="inproj",
    )(x, mod3, w_in, b_in, rope_t, sgu_ln_w, sgu_ln_b)


def _dot_t0(a, b):
    return lax.dot_general(a, b, (((0,), (0,)), ((), ())), preferred_element_type=F32)


def _state_kernel(lg_ref, k_ref, v_ref, kc_ref, vc_ref, s_ref):
    hd = pl.program_id(1)
    lgf = lg_ref[0, hd]
    lgb = lg_ref[1, hd]
    n_chunks = s_ref.shape[2]
    n_ctx = kc_ref.shape[1]
    c = RET_CHUNK
    li = _iota_f((c, RET_QK_DIM), 0)
    zeta_f = jnp.exp((c - 1.0 - li) * lgf)
    zeta_b = jnp.exp(li * lgb)
    one = jnp.ones((1, 1), F32)
    cd_f = jnp.exp(one * (c * lgf))
    cd_b = jnp.exp(one * (c * lgb))

    tc = _iota_f((n_ctx, RET_QK_DIM), 0)
    kc = kc_ref[0].astype(F32)
    vc = vc_ref[0]
    s0f = _dot_t0((kc * jnp.exp((n_ctx - 1.0 - tc) * lgf)).astype(BF16), vc)
    s0b = _dot_t0((kc * jnp.exp(tc * lgb)).astype(BF16), vc)

    def chunk_kv(n, zeta):
        off = pl.multiple_of(n * c, c)
        kk = k_ref[0, pl.ds(off, c), :].astype(F32)
        return _dot_t0((kk * zeta).astype(BF16), v_ref[0, pl.ds(off, c), :])

    def step(i, carry):
        sf, sb = carry
        nb = n_chunks - 1 - i
        s_ref[0, 0, i, 0:RET_QK_DIM, :] = sf.astype(BF16)
        s_ref[0, 0, nb, RET_QK_DIM:2 * RET_QK_DIM, :] = sb.astype(BF16)
        return cd_f * sf + chunk_kv(i, zeta_f), cd_b * sb + chunk_kv(nb, zeta_b)

    lax.fori_loop(0, n_chunks, step, (s0f, s0b), unroll=STATE_UNROLL)


def _state_call(lg, z, kctx, vctx):
    bsz, t, _ = z.shape
    n_ctx = kctx.shape[1]
    n_chunks = t // RET_CHUNK
    return pl.pallas_call(
        _state_kernel,
        out_shape=jax.ShapeDtypeStruct((bsz, RET_HEADS, n_chunks, 2 * RET_QK_DIM, RET_V_DIM), BF16),
        grid=(bsz, RET_HEADS),
        in_specs=[
            pl.BlockSpec(memory_space=pltpu.SMEM),
            pl.BlockSpec((1, t, RET_QK_DIM), lambda b, h: (b, 0, RET_QK_W // RET_QK_DIM + h)),
            pl.BlockSpec((1, t, RET_V_DIM), lambda b, h: (b, 0, Z_V // RET_V_DIM + h)),
            pl.BlockSpec((1, n_ctx, RET_QK_DIM), lambda b, h: (b, 0, h)),
            pl.BlockSpec((1, n_ctx, RET_V_DIM), lambda b, h: (b, 0, h)),
        ],
        out_specs=pl.BlockSpec((1, 1, n_chunks, 2 * RET_QK_DIM, RET_V_DIM), lambda b, h: (b, h, 0, 0, 0)),
        compiler_params=_cparams(("parallel", "parallel")),
        name="states",
    )(lg, z, z, kctx, vctx)


def _mixer_kernel(lg_ref, za_ref, zb_ref, gb_ref, s_ref, x_ref, mod_ref,
                  wpa_ref, wpb_ref, wo_ref, sguw_ref, sgub_ref, gnw_ref, gnb_ref, bo_ref,
                  ln1w_ref, ln1b_ref, wr_ref, br_ref,
                  x1_ref, hm_ref, lgt_ref, ret_scr, sgu_scr):
    d = D_MODEL
    c = RET_CHUNK

    n_sub = za_ref.shape[1] // c
    row = lax.broadcasted_iota(I32, (c, c), 0)
    col = lax.broadcasted_iota(I32, (c, c), 1)
    diff = (row - col).astype(F32)
    rowq = _iota_f((c, RET_QK_DIM), 0)
    for hd in range(RET_HEADS):
        lgf = lg_ref[0, hd]
        lgb = lg_ref[1, hd]
        vcols = slice(hd * RET_V_DIM, (hd + 1) * RET_V_DIM)
        mask = (jnp.where(diff >= 0, jnp.exp(jnp.maximum(diff, 0.0) * lgf), 0.0)
                + jnp.where(diff <= 0, jnp.exp(jnp.maximum(-diff, 0.0) * lgb), 0.0))
        xi_f = jnp.exp((rowq + 1.0) * lgf)
        xi_b = jnp.exp((c - rowq) * lgb)
        for ci in range(n_sub):
            r0 = ci * c
            q = za_ref[0, r0:r0 + c, hd * RET_QK_DIM:(hd + 1) * RET_QK_DIM]
            k = za_ref[0, r0:r0 + c, RET_QK_W + hd * RET_QK_DIM:RET_QK_W + (hd + 1) * RET_QK_DIM]
            s = lax.dot_general(q, k, (((1,), (1,)), ((), ())), preferred_element_type=F32)
            vv = za_ref[0, r0:r0 + c, Z_V + hd * RET_V_DIM:Z_V + (hd + 1) * RET_V_DIM]
            intra = jnp.dot((s * mask).astype(BF16), vv, preferred_element_type=F32)
            qf = q.astype(F32)
            qx = jnp.concatenate([(qf * xi_f).astype(BF16), (qf * xi_b).astype(BF16)], axis=1)
            cross = jnp.dot(qx, s_ref[0, hd, ci], preferred_element_type=F32)
            o = _ln(intra + cross) * gnw_ref[:, vcols] + gnb_ref[:, vcols]
            gate = za_ref[0, r0:r0 + c, Z_SG + hd * RET_V_DIM:Z_SG + (hd + 1) * RET_V_DIM]
            ret_scr[r0:r0 + c, vcols] = (o * gate.astype(F32)).astype(BF16)

    for g in range(SGU_GROUPS):
        gcols = slice(g * SGU_GW, (g + 1) * SGU_GW)
        wg = sguw_ref[g]
        bg = sgub_ref[:, g:g + 1]
        for ci in range(n_sub):
            r0 = ci * c
            vsb = zb_ref[0, r0:r0 + c, Z_VS - Z_BLOCK + g * SGU_GW:Z_VS - Z_BLOCK + (g + 1) * SGU_GW]
            ub = zb_ref[0, r0:r0 + c, Z_U - Z_BLOCK + g * SGU_GW:Z_U - Z_BLOCK + (g + 1) * SGU_GW]
            sp = jnp.dot(wg, vsb, preferred_element_type=F32) + bg
            sgu_scr[r0:r0 + c, gcols] = (ub.astype(F32) * sp).astype(BF16)

    g1 = mod_ref[0, :, 2 * d:3 * d]
    sh2 = mod_ref[0, :, 3 * d:4 * d]
    sc2 = mod_ref[0, :, 4 * d:5 * d]
    pa = jnp.dot(ret_scr[...], wpa_ref[...], preferred_element_type=F32)
    pb = jnp.dot(sgu_scr[...], wpb_ref[...], preferred_element_type=F32)
    ga = zb_ref[0, :, Z_GA - Z_BLOCK:Z_GA - Z_BLOCK + d]
    y = (ga.astype(F32) * pa + gb_ref[0].astype(F32) * pb).astype(BF16)
    mix = jnp.dot(y, wo_ref[...], preferred_element_type=F32) + bo_ref[...]
    x1 = _ln(DEEPNORM_ALPHA * x_ref[0] + g1 * mix) * ln1w_ref[...] + ln1b_ref[...]
    x1_ref[0] = x1
    hm = (_ln(x1) * (1.0 + sc2) + sh2).astype(BF16)
    hm_ref[0] = hm
    lgt_ref[0] = lax.dot_general(wr_ref[...], hm, (((1,), (1,)), ((), ())),
                                 preferred_element_type=F32) + br_ref[...]


def _mixer_call(lg, z, states, x, mod3, wpa, wpb, wo, sguw, sgub_t, gnw, gnb, bo,
                ln1w, ln1b, wr_t, br):
    bsz, t, d = x.shape
    tt = TT_MIXER
    n_sub = tt // RET_CHUNK
    row = lambda b, i: (b, i, 0)
    c2 = lambda b, i: (0, 0)
    c3 = lambda b, i: (0, 0, 0)
    return pl.pallas_call(
        _mixer_kernel,
        out_shape=(jax.ShapeDtypeStruct((bsz, t, d), F32),
                   jax.ShapeDtypeStruct((bsz, t, d), BF16),
                   jax.ShapeDtypeStruct((bsz, N_EXPERTS, t), F32)),
        grid=(bsz, t // tt),
        in_specs=[
            pl.BlockSpec(memory_space=pltpu.SMEM),
            pl.BlockSpec((1, tt, Z_BLOCK), lambda b, i: (b, i, 0)),
            pl.BlockSpec((1, tt, Z_BLOCK), lambda b, i: (b, i, 1)),
            pl.BlockSpec((1, tt, d), lambda b, i: (b, i, Z_GB // d)),
            pl.BlockSpec((1, RET_HEADS, n_sub, 2 * RET_QK_DIM, RET_V_DIM), lambda b, i: (b, 0, i, 0, 0)),
            pl.BlockSpec((1, tt, d), row),
            pl.BlockSpec((1, 1, 6 * d), lambda b, i: (b, 0, 0)),
            pl.BlockSpec((RET_V_W, d), c2, pipeline_mode=pl.Buffered(1)),
            pl.BlockSpec((SGU_WIDTH, d), c2, pipeline_mode=pl.Buffered(1)),
            pl.BlockSpec((d, d), c2, pipeline_mode=pl.Buffered(1)),
            pl.BlockSpec((SGU_GROUPS, SGU_CHUNK, SGU_CHUNK), c3),
            pl.BlockSpec((SGU_CHUNK, SGU_GROUPS), c2),
            pl.BlockSpec((1, RET_V_W), c2),
            pl.BlockSpec((1, RET_V_W), c2),
            pl.BlockSpec((1, d), c2),
            pl.BlockSpec((1, d), c2),
            pl.BlockSpec((1, d), c2),
            pl.BlockSpec((N_EXPERTS, d), c2),
            pl.BlockSpec((N_EXPERTS, 1), c2),
        ],
        out_specs=(pl.BlockSpec((1, tt, d), row),
                   pl.BlockSpec((1, tt, d), row),
                   pl.BlockSpec((1, N_EXPERTS, tt), lambda b, i: (b, 0, i))),
        scratch_shapes=[pltpu.VMEM((tt, RET_V_W), BF16), pltpu.VMEM((tt, SGU_WIDTH), BF16)],
        compiler_params=_cparams(("parallel", "parallel")),
        name="mixer",
    )(lg, z, z, z, states, x, mod3, wpa, wpb, wo, sguw, sgub_t, gnw, gnb, bo,
      ln1w, ln1b, wr_t, br)


def _route_kernel(lgt_ref, pos_ref, gate_ref, cnt_ref, *, cap):
    lg = lgt_ref[0]
    n_e, t = lg.shape
    tb = TB_ROUTE
    m = jnp.max(lg, axis=0, keepdims=True)
    ex = jnp.exp(lg - m)
    aff = ex / jnp.sum(ex, axis=0, keepdims=True)

    def search(i, thr_bits):
        cand = thr_bits | lax.shift_left(jnp.int32(1), 30 - i)
        cnt = jnp.sum((aff >= lax.bitcast_convert_type(cand, F32)).astype(I32), axis=1, keepdims=True)
        return jnp.where(cnt >= cap, cand, thr_bits)

    thr_bits = lax.fori_loop(0, 31, search, jnp.zeros((n_e, 1), I32))
    floor_f = lax.bitcast_convert_type(thr_bits, F32)
    thr = jnp.min(jnp.where(aff >= floor_f, aff, jnp.inf), axis=1, keepdims=True)
    need = (cap - jnp.sum((aff > thr).astype(I32), axis=1, keepdims=True)).astype(F32)

    r = lax.broadcasted_iota(I32, (tb, tb), 0)
    cc = lax.broadcasted_iota(I32, (tb, tb), 1)
    tri = (r <= cc).astype(BF16)
    carry_eq = jnp.zeros((n_e, 1), F32)
    carry_sel = jnp.zeros((n_e, 1), F32)
    for blk in range(t // tb):
        sl = slice(blk * tb, (blk + 1) * tb)
        aff_b = aff[:, sl]
        eq = aff_b == thr
        eq_b = eq.astype(BF16)
        inc_eq = jnp.dot(eq_b, tri, preferred_element_type=F32)
        before = carry_eq + inc_eq - eq_b.astype(F32)
        sel = (aff_b > thr) | (eq & (before < need))
        sel_b = sel.astype(BF16)
        inc_sel = jnp.dot(sel_b, tri, preferred_element_type=F32)
        pos = carry_sel + inc_sel - 1.0
        pos_ref[0, blk] = jnp.where(sel, pos.astype(I32), -1)
        gate_ref[0, blk] = aff_b
        cnt_ref[0, blk] = jnp.broadcast_to(carry_sel, (n_e, LANES)).astype(I32)
        carry_eq = carry_eq + inc_eq[:, tb - 1:tb]
        carry_sel = carry_sel + inc_sel[:, tb - 1:tb]


def _route_call(logits_t, cap):
    bsz, n_e, t = logits_t.shape
    nblk = t // TB_ROUTE
    return pl.pallas_call(
        functools.partial(_route_kernel, cap=cap),
        out_shape=(jax.ShapeDtypeStruct((bsz, nblk, n_e, TB_ROUTE), I32),
                   jax.ShapeDtypeStruct((bsz, nblk, n_e, TB_ROUTE), F32),
                   jax.ShapeDtypeStruct((bsz, nblk, n_e, LANES), I32)),
        grid=(bsz,),
        in_specs=[pl.BlockSpec((1, n_e, t), lambda b: (b, 0, 0))],
        out_specs=(pl.BlockSpec((1, nblk, n_e, TB_ROUTE), lambda b: (b, 0, 0, 0)),
                   pl.BlockSpec((1, nblk, n_e, TB_ROUTE), lambda b: (b, 0, 0, 0)),
                   pl.BlockSpec((1, nblk, n_e, LANES), lambda b: (b, 0, 0, 0))),
        compiler_params=_cparams(("parallel",)),
        name="route",
    )(logits_t)


def _window_rows(ws, win):
    return lax.broadcasted_iota(I32, (win, TB_ROUTE), 0) + ws


def _sweep_window(w, cap, win, prow, rest):
    off = pl.multiple_of(jnp.minimum(w * win, cap - win), WIN_ALIGN)
    rows = _window_rows(off, win)
    return off, (rows == prow) & rest & (rows >= w * win)


def _gather_kernel(ws_ref, nx_ref, hm_ref, pos_ref, gate_ref, xs_ref, gc_ref):
    b = pl.program_id(0)
    g = pl.program_id(1)
    kb = pl.program_id(2)
    eg = pos_ref.shape[2]
    n_e = eg * pl.num_programs(1)
    cap = xs_ref.shape[2]
    win = WIN_DISPATCH

    @pl.when(kb == 0)
    def _():
        xs_ref[...] = jnp.zeros_like(xs_ref)
        gc_ref[...] = jnp.zeros_like(gc_ref)

    hb = hm_ref[0]
    base = (b * pl.num_programs(2) + kb) * n_e + g * eg
    starts = [pl.multiple_of(ws_ref[base + el], WIN_ALIGN) for el in range(eg)]
    for el in range(eg):
        match = _window_rows(starts[el], win) == pos_ref[0, 0, el:el + 1, :]
        sl = pl.ds(starts[el], win)
        xs_ref[0, el, sl, :] += jnp.dot(match.astype(BF16), hb, preferred_element_type=F32).astype(BF16)
        gc_ref[0, el, sl, :] += jnp.sum(jnp.where(match, gate_ref[0, 0, el:el + 1, :], 0.0),
                                         axis=1, keepdims=True)

    for el in range(eg):
        @pl.when(nx_ref[base + el] > 0)
        def _(el=el):
            prow = pos_ref[0, 0, el:el + 1, :]
            grow = gate_ref[0, 0, el:el + 1, :]
            rest = prow >= starts[el] + win

            def body(w, carry):
                off, m = _sweep_window(w, cap, win, prow, rest)
                xs_ref[0, el, pl.ds(off, win), :] += jnp.dot(m.astype(BF16), hb,
                                                              preferred_element_type=F32).astype(BF16)
                gc_ref[0, el, pl.ds(off, win), :] += jnp.sum(jnp.where(m, grow, 0.0), axis=1, keepdims=True)
                return carry

            lax.fori_loop(0, pl.cdiv(cap, win), body, 0)


def _gather_call(ws, nx, hm, pos_b, gate_b, cap):
    bsz, t, d = hm.shape
    nblk, n_e, tb = pos_b.shape[1:]
    eg = n_e // DISPATCH_GROUPS
    grid_spec = pltpu.PrefetchScalarGridSpec(
        num_scalar_prefetch=2,
        grid=(bsz, DISPATCH_GROUPS, nblk),
        in_specs=[
            pl.BlockSpec((1, tb, d), lambda b, g, k, ws, nx: (b, k, 0)),
            pl.BlockSpec((1, 1, eg, tb), lambda b, g, k, ws, nx: (b, k, g, 0)),
            pl.BlockSpec((1, 1, eg, tb), lambda b, g, k, ws, nx: (b, k, g, 0)),
        ],
        out_specs=(pl.BlockSpec((1, eg, cap, d), lambda b, g, k, ws, nx: (b, g, 0, 0)),
                   pl.BlockSpec((1, eg, cap, 1), lambda b, g, k, ws, nx: (b, g, 0, 0))),
    )
    return pl.pallas_call(
        _gather_kernel,
        out_shape=(jax.ShapeDtypeStruct((bsz, n_e, cap, d), BF16),
                   jax.ShapeDtypeStruct((bsz, n_e, cap, 1), F32)),
        grid_spec=grid_spec,
        compiler_params=_cparams(("parallel", "parallel", "arbitrary")),
        name="gather",
    )(ws, nx, hm, pos_b, gate_b)


def _ffn_kernel(xs_ref, gc_ref, wg_ref, wu_ref, wd_ref, ye_ref, wgu_s, wd_s):
    f = wd_s.shape[0]

    @pl.when(pl.program_id(1) == 0)
    def _():
        wgu_s[:, 0:f] = wg_ref[0].astype(BF16)
        wgu_s[:, f:2 * f] = wu_ref[0].astype(BF16)
        wd_s[...] = wd_ref[0].astype(BF16)

    rt = FFN_ROWS
    for j in range(xs_ref.shape[2] // rt):
        sl = slice(j * rt, (j + 1) * rt)
        xgu = jnp.dot(xs_ref[0, 0, sl, :], wgu_s[...], preferred_element_type=F32)
        hid = (jax.nn.silu(xgu[:, 0:f]) * xgu[:, f:2 * f]).astype(BF16)
        ye = jnp.dot(hid, wd_s[...], preferred_element_type=F32) * gc_ref[0, 0, sl, :]
        ye_ref[0, 0, sl, :] = ye.astype(BF16)


def _ffn_call(xs, gc, w_gate, w_up, w_down):
    bsz, n_e, cap, d = xs.shape
    f = w_gate.shape[2]
    return pl.pallas_call(
        _ffn_kernel,
        out_shape=jax.ShapeDtypeStruct((bsz, n_e, cap, d), BF16),
        grid=(n_e, bsz),
        in_specs=[
            pl.BlockSpec((1, 1, cap, d), lambda e, b: (b, e, 0, 0)),
            pl.BlockSpec((1, 1, cap, 1), lambda e, b: (b, e, 0, 0)),
            pl.BlockSpec((1, d, f), lambda e, b: (e, 0, 0)),
            pl.BlockSpec((1, d, f), lambda e, b: (e, 0, 0)),
            pl.BlockSpec((1, f, d), lambda e, b: (e, 0, 0)),
        ],
        out_specs=pl.BlockSpec((1, 1, cap, d), lambda e, b: (b, e, 0, 0)),
        scratch_shapes=[pltpu.VMEM((d, 2 * f), BF16), pltpu.VMEM((f, d), BF16)],
        compiler_params=_cparams(("arbitrary", "arbitrary")),
        name="ffn",
    )(xs, gc, w_gate, w_up, w_down)


def _combine_kernel(ws_ref, nx_ref, ye_ref, pos_ref, x1_ref, mod_ref, lnw_ref, lnb_ref, out_ref, acc_scr):
    b = pl.program_id(0)
    kb = pl.program_id(1)
    n_e = pos_ref.shape[2]
    cap = ye_ref.shape[2]
    d = D_MODEL
    win = WIN_COMBINE
    base = (b * pl.num_programs(1) + kb) * n_e
    starts = [pl.multiple_of(ws_ref[base + e], WIN_ALIGN) for e in range(n_e)]
    onehot = jnp.concatenate(
        [(_window_rows(starts[e], win) == pos_ref[0, 0, e:e + 1, :]).astype(BF16) for e in range(n_e)], axis=0)
    rows = jnp.concatenate([ye_ref[0, e, pl.ds(starts[e], win), :] for e in range(n_e)], axis=0)
    acc_scr[...] = _dot_t0(onehot, rows)

    for e in range(n_e):
        @pl.when(nx_ref[base + e] > 0)
        def _(e=e):
            prow = pos_ref[0, 0, e:e + 1, :]
            rest = prow >= starts[e] + win

            def body(w, carry):
                off, m = _sweep_window(w, cap, win, prow, rest)
                acc_scr[...] += _dot_t0(m.astype(BF16), ye_ref[0, e, pl.ds(off, win), :])
                return carry

            lax.fori_loop(0, pl.cdiv(cap, win), body, 0)

    g2 = mod_ref[0, :, 5 * d:6 * d]
    out_ref[0] = _ln(DEEPNORM_ALPHA * x1_ref[0] + g2 * acc_scr[...]) * lnw_ref[...] + lnb_ref[...]


def _combine_call(ws, nx, ye, pos_b, x1, mod3, ln2w, ln2b):
    bsz, n_e, cap, d = ye.shape
    nblk, _, tb = pos_b.shape[1:]
    grid_spec = pltpu.PrefetchScalarGridSpec(
        num_scalar_prefetch=2,
        grid=(bsz, nblk),
        in_specs=[
            pl.BlockSpec((1, n_e, cap, d), lambda b, k, ws, nx: (b, 0, 0, 0), pipeline_mode=pl.Buffered(1)),
            pl.BlockSpec((1, 1, n_e, tb), lambda b, k, ws, nx: (b, k, 0, 0)),
            pl.BlockSpec((1, tb, d), lambda b, k, ws, nx: (b, k, 0)),
            pl.BlockSpec((1, 1, 6 * d), lambda b, k, ws, nx: (b, 0, 0)),
            pl.BlockSpec((1, d), lambda b, k, ws, nx: (0, 0)),
            pl.BlockSpec((1, d), lambda b, k, ws, nx: (0, 0)),
        ],
        out_specs=pl.BlockSpec((1, tb, d), lambda b, k, ws, nx: (b, k, 0)),
        scratch_shapes=[pltpu.VMEM((tb, d), F32)],
    )
    return pl.pallas_call(
        _combine_kernel,
        out_shape=jax.ShapeDtypeStruct((bsz, nblk * tb, d), F32),
        grid_spec=grid_spec,
        compiler_params=_cparams(("parallel", "arbitrary")),
        name="combine",
    )(ws, nx, ye, pos_b, x1, mod3, ln2w, ln2b)


def _rope_tables(n_tokens):
    rows = n_tokens // GRID_W
    n_freq = RET_QK_DIM // 4
    inv = ROPE_THETA ** (-jnp.arange(n_freq, dtype=F32) / n_freq)
    ang_r = jnp.arange(rows, dtype=F32)[:, None] * inv
    ang_c = jnp.arange(GRID_W, dtype=F32)[:, None] * inv
    zr = jnp.zeros((rows, 2 * n_freq), F32)
    zc = jnp.zeros((GRID_W, 2 * n_freq), F32)
    cr, sr, cc, sc = jnp.cos(ang_r), jnp.sin(ang_r), jnp.cos(ang_c), jnp.sin(ang_c)
    cos_t = (jnp.concatenate([cr, cr, zr], axis=-1)[:, None, :]
             + jnp.concatenate([zc, cc, cc], axis=-1)[None, :, :]).reshape(n_tokens, RET_QK_DIM)
    sin_t = (jnp.concatenate([-sr, sr, zr], axis=-1)[:, None, :]
             + jnp.concatenate([zc, -sc, sc], axis=-1)[None, :, :]).reshape(n_tokens, RET_QK_DIM)
    return jnp.concatenate([cos_t, sin_t], axis=-1)


def kernel(x, c, ctx, c_ctx, w_ada, b_ada, w_in, b_in, ret_decay_f, ret_decay_b, ret_gn_w, ret_gn_b,
           sgu_ln_w, sgu_ln_b, sgu_w, sgu_b, w_pa, w_pb, w_o, b_o, ln1_w, ln1_b,
           w_router, b_router, w_gate, w_up, w_down, ln2_w, ln2_b):
    bsz, t, d = x.shape
    assert d == D_MODEL and w_ada.shape[0] == DEPTH == 1 and bsz < MOD_ROWS
    cap = EC_CAPACITY_FACTOR * t // N_EXPERTS
    for win in (WIN_DISPATCH, WIN_COMBINE):
        assert cap >= win and (cap - win) % WIN_ALIGN == 0 and win % WIN_ALIGN == 0
    assert t % TB_ROUTE == 0
    assert N_EXPERTS % DISPATCH_GROUPS == 0
    l = 0

    cvec = jnp.zeros((MOD_ROWS, d), F32).at[:bsz].set(c).at[bsz].set(c_ctx)
    mod = _mod_call(cvec.T, w_ada[l], b_ada[l][None], bsz + 1)
    mod3 = mod[:, None, :]

    w_in_b = w_in[l].astype(BF16)
    b_in2 = b_in[l][None]
    lo_kv, hi_kv = RET_QK_W, 2 * RET_QK_W + RET_V_W
    kctx, vctx = _ctx_kv_call(ctx, mod3, w_in_b[:, lo_kv:hi_kv], b_in2[:, lo_kv:hi_kv], bsz)

    z = _inproj_call(x, mod3, w_in_b, b_in2, _rope_tables(t), sgu_ln_w[l][None], sgu_ln_b[l][None])

    lg = jnp.stack([jax.nn.log_sigmoid(ret_decay_f[l].astype(F32)),
                    jax.nn.log_sigmoid(ret_decay_b[l].astype(F32))])
    states = _state_call(lg, z, kctx, vctx)

    x1, hm, logits_t = _mixer_call(
        lg, z, states, x, mod3,
        w_pa[l].astype(BF16), w_pb[l].astype(BF16), w_o[l].astype(BF16),
        sgu_w[l].astype(BF16), sgu_b[l].T, ret_gn_w[l][None], ret_gn_b[l][None], b_o[l][None],
        ln1_w[l][None], ln1_b[l][None], w_router[l].T.astype(BF16), b_router[l][:, None])

    pos_b, gate_b, cnt_b = _route_call(logits_t, cap)

    c0 = cnt_b[..., 0]
    c1 = jnp.concatenate([c0[:, 1:], jnp.full_like(c0[:, :1], cap)], axis=1)

    def windows(win):
        ws = jnp.minimum((c0 // WIN_ALIGN) * WIN_ALIGN, cap - win)
        return ws.reshape(-1), (c1 > ws + win).astype(I32).reshape(-1)

    ws_d, nx_d = windows(WIN_DISPATCH)
    ws_c, nx_c = windows(WIN_COMBINE)

    xs, gc = _gather_call(ws_d, nx_d, hm, pos_b, gate_b, cap)
    ye = _ffn_call(xs, gc, w_gate[l], w_up[l], w_down[l])
    return _combine_call(ws_c, nx_c, ye, pos_b, x1, mod3, ln2_w[l][None], ln2_b[l][None])
```

```python
import functools
import math

import jax
import jax.numpy as jnp
from jax import lax
from jax.experimental import pallas as pl
from jax.experimental.pallas import tpu as pltpu

F32 = jnp.float32
BF16 = jnp.bfloat16
I32 = jnp.int32

D_MODEL = 1024
DEPTH = 1
GRID_W = 64
RET_HEADS = 4
RET_QK_DIM = 128
RET_V_DIM = 256
RET_CHUNK = 128
ROPE_THETA = 10000.0
RET_QK_W = RET_HEADS * RET_QK_DIM
RET_V_W = RET_HEADS * RET_V_DIM
SGU_GROUPS = 4
SGU_CHUNK = 128
SGU_WIDTH = 1024
SGU_GW = SGU_WIDTH // SGU_GROUPS
N_EXPERTS = 16
EC_CAPACITY_FACTOR = 2
D_EXPERT = 1024
LN_EPS = 1e-6
DEEPNORM_ALPHA = (2.0 * DEPTH) ** 0.25
D_IN = 2 * RET_QK_W + 2 * RET_V_W + 2 * SGU_WIDTH + 2 * D_MODEL
Z_V = 2 * RET_QK_W
Z_SG = Z_V + RET_V_W
Z_U = Z_SG + RET_V_W
Z_VS = Z_U + SGU_WIDTH
Z_GA = Z_VS + SGU_WIDTH
Z_GB = Z_GA + D_MODEL
Z_BLOCK = 3 * 1024

V7X_VMEM_BYTES = 64 * 1024 * 1024
VMEM_LIMIT = 56 * 1024 * 1024
LANES = 128

MOD_ROWS = 8
TM_INPROJ = 512
INPROJ_CHUNK = 256
TT_MIXER = 512
MERGE_ROWS = 512
TB_ROUTE = 512
WIN_DISPATCH = 128
WIN_COMBINE = 96
WIN_ALIGN = 16
DISPATCH_GROUPS = 2
STATE_UNROLL = 8
FFN_ROWS = 256


def _cparams(sem):
    return pltpu.CompilerParams(dimension_semantics=sem, vmem_limit_bytes=VMEM_LIMIT)


def _ln(x):
    mu = jnp.mean(x, axis=-1, keepdims=True)
    xc = x - mu
    var = jnp.mean(xc * xc, axis=-1, keepdims=True)
    return xc * lax.rsqrt(var + LN_EPS)


_GELU_A = -2.0 * math.sqrt(2.0 / math.pi) * math.log2(math.e)
_GELU_B = _GELU_A * 0.044715


def _gelu(x):
    return x / (1.0 + jnp.exp2(x * (_GELU_A + _GELU_B * (x * x))))


def _iota_f(shape, dim):
    return lax.broadcasted_iota(I32, shape, dim).astype(F32)


def _mod_kernel(ct_ref, w_ref, b_ref, o_ref, *, n_rows):
    c = ct_ref[...]
    a = c * jax.nn.sigmoid(c)
    w = w_ref[...]
    rows = [jnp.sum(w * a[:, m:m + 1], axis=0, keepdims=True) + b_ref[...] for m in range(n_rows)]
    rows += [jnp.zeros_like(rows[0])] * (MOD_ROWS - n_rows)
    o_ref[...] = jnp.concatenate(rows, axis=0)


def _mod_call(cvec_t, w_ada, b_ada, n_rows):
    d, n = w_ada.shape
    tn = 1024
    return pl.pallas_call(
        functools.partial(_mod_kernel, n_rows=n_rows),
        out_shape=jax.ShapeDtypeStruct((MOD_ROWS, n), F32),
        grid=(n // tn,),
        in_specs=[
            pl.BlockSpec((d, MOD_ROWS), lambda j: (0, 0)),
            pl.BlockSpec((d, tn), lambda j: (0, j)),
            pl.BlockSpec((1, tn), lambda j: (0, j)),
        ],
        out_specs=pl.BlockSpec((MOD_ROWS, tn), lambda j: (0, j)),
        compiler_params=_cparams(("parallel",)),
        name="mod",
    )(cvec_t, w_ada, b_ada)


def _ctx_kv_kernel(x_ref, mod_ref, w_ref, b_ref, k_ref, v_ref):
    d = D_MODEL
    sh = mod_ref[0, :, 0:d]
    sc = mod_ref[0, :, d:2 * d]
    h = (_ln(x_ref[0]) * (1.0 + sc) + sh).astype(BF16)
    z = jnp.dot(h, w_ref[...], preferred_element_type=F32) + b_ref[...]
    k_ref[0] = (z[:, :RET_QK_W] * (RET_QK_DIM ** -0.5)).astype(BF16)
    v_ref[0] = z[:, RET_QK_W:].astype(BF16)


def _ctx_kv_call(ctx, mod3, w_kv, b_kv, ctx_row):
    bsz, n_ctx, d = ctx.shape
    wkv = w_kv.shape[1]
    return pl.pallas_call(
        _ctx_kv_kernel,
        out_shape=(jax.ShapeDtypeStruct((bsz, n_ctx, RET_QK_W), BF16),
                   jax.ShapeDtypeStruct((bsz, n_ctx, RET_V_W), BF16)),
        grid=(bsz,),
        in_specs=[
            pl.BlockSpec((1, n_ctx, d), lambda b: (b, 0, 0)),
            pl.BlockSpec((1, 1, 6 * d), lambda b: (ctx_row, 0, 0)),
            pl.BlockSpec((d, wkv), lambda b: (0, 0)),
            pl.BlockSpec((1, wkv), lambda b: (0, 0)),
        ],
        out_specs=(pl.BlockSpec((1, n_ctx, RET_QK_W), lambda b: (b, 0, 0)),
                   pl.BlockSpec((1, n_ctx, RET_V_W), lambda b: (b, 0, 0))),
        compiler_params=_cparams(("parallel",)),
        name="ctx_kv",
    )(ctx, mod3, w_kv, b_kv)


def _inproj_kernel(x_ref, mod_ref, w_ref, b_ref, rope_ref, lnw_ref, lnb_ref, z_ref):
    d = D_MODEL
    sh = mod_ref[0, :, 0:d]
    sc = mod_ref[0, :, d:2 * d]
    h = (_ln(x_ref[0]) * (1.0 + sc) + sh).astype(BF16)

    def proj(c0, width):
        return jnp.dot(h, w_ref[:, c0:c0 + width], preferred_element_type=F32) + b_ref[:, c0:c0 + width]

    zq = proj(0, 2 * RET_QK_W)
    cos = rope_ref[:, 0:LANES]
    sin = rope_ref[:, LANES:2 * LANES]
    tm = zq.shape[0]
    upper = (lax.broadcasted_iota(I32, (tm, LANES), 1) & 32) != 0
    for hb in range(2 * RET_HEADS):
        zs = zq[:, hb * LANES:(hb + 1) * LANES]
        if hb >= RET_HEADS:
            zs = zs * (RET_QK_DIM ** -0.5)
        sw = jnp.where(upper, pltpu.roll(zs, 32, 1), pltpu.roll(zs, LANES - 32, 1))
        z_ref[0, :, hb * LANES:(hb + 1) * LANES] = (zs * cos + sw * sin).astype(BF16)

    def emit(c0, width, fn):
        for cc in range(c0, c0 + width, INPROJ_CHUNK):
            z_ref[0, :, cc:cc + INPROJ_CHUNK] = fn(proj(cc, INPROJ_CHUNK)).astype(BF16)

    vs = _ln(_gelu(proj(Z_VS, SGU_WIDTH))) * lnw_ref[...] + lnb_ref[...]
    z_ref[0, :, Z_VS:Z_VS + SGU_WIDTH] = vs.astype(BF16)
    emit(Z_U, SGU_WIDTH, _gelu)
    emit(Z_SG, RET_V_W, jax.nn.silu)
    emit(Z_GA, d, jax.nn.sigmoid)
    emit(Z_GB, d, jax.nn.sigmoid)
    emit(Z_V, RET_V_W, lambda z: z)


def _inproj_call(x, mod3, w_in, b_in, rope_t, sgu_ln_w, sgu_ln_b):
    bsz, t, d = x.shape
    tm = TM_INPROJ
    row = lambda b, i: (b, i, 0)
    const2 = lambda b, i: (0, 0)
    return pl.pallas_call(
        _inproj_kernel,
        out_shape=jax.ShapeDtypeStruct((bsz, t, D_IN), BF16),
        grid=(bsz, t // tm),
        in_specs=[
            pl.BlockSpec((1, tm, d), row),
            pl.BlockSpec((1, 1, 6 * d), lambda b, i: (b, 0, 0)),
            pl.BlockSpec((d, D_IN), const2, pipeline_mode=pl.Buffered(1)),
            pl.BlockSpec((1, D_IN), const2),
            pl.BlockSpec((tm, 2 * LANES), lambda b, i: (i, 0)),
            pl.BlockSpec((1, SGU_WIDTH), const2),
            pl.BlockSpec((1, SGU_WIDTH), const2),
        ],
        out_specs=pl.BlockSpec((1, tm, D_IN), row),
        compiler_params=_cparams(("parallel", "parallel")),
        name="inproj",
    )(x, mod3, w_in, b_in, rope_t, sgu_ln_w, sgu_ln_b)


def _dot_t0(a, b):
    return lax.dot_general(a, b, (((0,), (0,)), ((), ())), preferred_element_type=F32)


def _state_kernel(lg_ref, k_ref, v_ref, kc_ref, vc_ref, s_ref):
    hd = pl.program_id(1)
    lgf = lg_ref[0, hd]
    lgb = lg_ref[1, hd]
    n_chunks = s_ref.shape[2]
    n_ctx = kc_ref.shape[1]
    c = RET_CHUNK
    li = _iota_f((c, RET_QK_DIM), 0)
    zeta_f = jnp.exp((c - 1.0 - li) * lgf)
    zeta_b = jnp.exp(li * lgb)
    one = jnp.ones((1, 1), F32)
    cd_f = jnp.exp(one * (c * lgf))
    cd_b = jnp.exp(one * (c * lgb))

    tc = _iota_f((n_ctx, RET_QK_DIM), 0)
    kc = kc_ref[0].astype(F32)
    vc = vc_ref[0]
    s0f = _dot_t0((kc * jnp.exp((n_ctx - 1.0 - tc) * lgf)).astype(BF16), vc)
    s0b = _dot_t0((kc * jnp.exp(tc * lgb)).astype(BF16), vc)

    def chunk_kv(n, zeta):
        off = pl.multiple_of(n * c, c)
        kk = k_ref[0, pl.ds(off, c), :].astype(F32)
        return _dot_t0((kk * zeta).astype(BF16), v_ref[0, pl.ds(off, c), :])

    def step(i, carry):
        sf, sb = carry
        nb = n_chunks - 1 - i
        s_ref[0, 0, i, 0:RET_QK_DIM, :] = sf.astype(BF16)
        s_ref[0, 0, nb, RET_QK_DIM:2 * RET_QK_DIM, :] = sb.astype(BF16)
        return cd_f * sf + chunk_kv(i, zeta_f), cd_b * sb + chunk_kv(nb, zeta_b)

    lax.fori_loop(0, n_chunks, step, (s0f, s0b), unroll=STATE_UNROLL)


def _state_call(lg, z, kctx, vctx):
    bsz, t, _ = z.shape
    n_ctx = kctx.shape[1]
    n_chunks = t // RET_CHUNK
    return pl.pallas_call(
        _state_kernel,
        out_shape=jax.ShapeDtypeStruct((bsz, RET_HEADS, n_chunks, 2 * RET_QK_DIM, RET_V_DIM), BF16),
        grid=(bsz, RET_HEADS),
        in_specs=[
            pl.BlockSpec(memory_space=pltpu.SMEM),
            pl.BlockSpec((1, t, RET_QK_DIM), lambda b, h: (b, 0, RET_QK_W // RET_QK_DIM + h)),
            pl.BlockSpec((1, t, RET_V_DIM), lambda b, h: (b, 0, Z_V // RET_V_DIM + h)),
            pl.BlockSpec((1, n_ctx, RET_QK_DIM), lambda b, h: (b, 0, h)),
            pl.BlockSpec((1, n_ctx, RET_V_DIM), lambda b, h: (b, 0, h)),
        ],
        out_specs=pl.BlockSpec((1, 1, n_chunks, 2 * RET_QK_DIM, RET_V_DIM), lambda b, h: (b, h, 0, 0, 0)),
        compiler_params=_cparams(("parallel", "parallel")),
        name="states",
    )(lg, z, z, kctx, vctx)


def _mixer_kernel(lg_ref, za_ref, zb_ref, gb_ref, s_ref, x_ref, mod_ref,
                  wpa_ref, wpb_ref, wo_ref, sguw_ref, sgub_ref, gnw_ref, gnb_ref, bo_ref,
                  ln1w_ref, ln1b_ref, wr_ref, br_ref,
                  x1_ref, hm_ref, lgt_ref, ret_scr, sgu_scr):
    d = D_MODEL
    c = RET_CHUNK

    n_sub = za_ref.shape[1] // c
    row = lax.broadcasted_iota(I32, (c, c), 0)
    col = lax.broadcasted_iota(I32, (c, c), 1)
    diff = (row - col).astype(F32)
    rowq = _iota_f((c, RET_QK_DIM), 0)
    for hd in range(RET_HEADS):
        lgf = lg_ref[0, hd]
        lgb = lg_ref[1, hd]
        vcols = slice(hd * RET_V_DIM, (hd + 1) * RET_V_DIM)
        mask = (jnp.where(diff >= 0, jnp.exp(jnp.maximum(diff, 0.0) * lgf), 0.0)
                + jnp.where(diff <= 0, jnp.exp(jnp.maximum(-diff, 0.0) * lgb), 0.0))
        xi_f = jnp.exp((rowq + 1.0) * lgf)
        xi_b = jnp.exp((c - rowq) * lgb)
        for ci in range(n_sub):
            r0 = ci * c
            q = za_ref[0, r0:r0 + c, hd * RET_QK_DIM:(hd + 1) * RET_QK_DIM]
            k = za_ref[0, r0:r0 + c, RET_QK_W + hd * RET_QK_DIM:RET_QK_W + (hd + 1) * RET_QK_DIM]
            s = lax.dot_general(q, k, (((1,), (1,)), ((), ())), preferred_element_type=F32)
            vv = za_ref[0, r0:r0 + c, Z_V + hd * RET_V_DIM:Z_V + (hd + 1) * RET_V_DIM]
            intra = jnp.dot((s * mask).astype(BF16), vv, preferred_element_type=F32)
            qf = q.astype(F32)
            qx = jnp.concatenate([(qf * xi_f).astype(BF16), (qf * xi_b).astype(BF16)], axis=1)
            cross = jnp.dot(qx, s_ref[0, hd, ci], preferred_element_type=F32)
            o = _ln(intra + cross) * gnw_ref[:, vcols] + gnb_ref[:, vcols]
            gate = za_ref[0, r0:r0 + c, Z_SG + hd * RET_V_DIM:Z_SG + (hd + 1) * RET_V_DIM]
            ret_scr[r0:r0 + c, vcols] = (o * gate.astype(F32)).astype(BF16)

    for g in range(SGU_GROUPS):
        gcols = slice(g * SGU_GW, (g + 1) * SGU_GW)
        wg = sguw_ref[g]
        bg = sgub_ref[:, g:g + 1]
        for ci in range(n_sub):
            r0 = ci * c
            vsb = zb_ref[0, r0:r0 + c, Z_VS - Z_BLOCK + g * SGU_GW:Z_VS - Z_BLOCK + (g + 1) * SGU_GW]
            ub = zb_ref[0, r0:r0 + c, Z_U - Z_BLOCK + g * SGU_GW:Z_U - Z_BLOCK + (g + 1) * SGU_GW]
            sp = jnp.dot(wg, vsb, preferred_element_type=F32) + bg
            sgu_scr[r0:r0 + c, gcols] = (ub.astype(F32) * sp).astype(BF16)

    g1 = mod_ref[0, :, 2 * d:3 * d]
    sh2 = mod_ref[0, :, 3 * d:4 * d]
    sc2 = mod_ref[0, :, 4 * d:5 * d]
    for r0 in range(0, za_ref.shape[1], MERGE_ROWS):
        rows = slice(r0, r0 + MERGE_ROWS)
        pa = jnp.dot(ret_scr[rows, :], wpa_ref[...], preferred_element_type=F32)
        pb = jnp.dot(sgu_scr[rows, :], wpb_ref[...], preferred_element_type=F32)
        ga = zb_ref[0, rows, Z_GA - Z_BLOCK:Z_GA - Z_BLOCK + d]
        y = (ga.astype(F32) * pa + gb_ref[0, rows, :].astype(F32) * pb).astype(BF16)
        mix = jnp.dot(y, wo_ref[...], preferred_element_type=F32) + bo_ref[...]
        x1 = _ln(DEEPNORM_ALPHA * x_ref[0, rows, :] + g1 * mix) * ln1w_ref[...] + ln1b_ref[...]
        x1_ref[0, rows, :] = x1
        hm = (_ln(x1) * (1.0 + sc2) + sh2).astype(BF16)
        hm_ref[0, rows, :] = hm
        lgt_ref[0, :, rows] = lax.dot_general(wr_ref[...], hm, (((1,), (1,)), ((), ())),
                                              preferred_element_type=F32) + br_ref[...]


def _mixer_call(lg, z, states, x, mod3, wpa, wpb, wo, sguw, sgub_t, gnw, gnb, bo,
                ln1w, ln1b, wr_t, br):
    bsz, t, d = x.shape
    tt = TT_MIXER
    n_sub = tt // RET_CHUNK
    row = lambda b, i: (b, i, 0)
    c2 = lambda b, i: (0, 0)
    c3 = lambda b, i: (0, 0, 0)
    return pl.pallas_call(
        _mixer_kernel,
        out_shape=(jax.ShapeDtypeStruct((bsz, t, d), F32),
                   jax.ShapeDtypeStruct((bsz, t, d), BF16),
                   jax.ShapeDtypeStruct((bsz, N_EXPERTS, t), F32)),
        grid=(bsz, t // tt),
        in_specs=[
            pl.BlockSpec(memory_space=pltpu.SMEM),
            pl.BlockSpec((1, tt, Z_BLOCK), lambda b, i: (b, i, 0)),
            pl.BlockSpec((1, tt, Z_BLOCK), lambda b, i: (b, i, 1)),
            pl.BlockSpec((1, tt, d), lambda b, i: (b, i, Z_GB // d)),
            pl.BlockSpec((1, RET_HEADS, n_sub, 2 * RET_QK_DIM, RET_V_DIM), lambda b, i: (b, 0, i, 0, 0)),
            pl.BlockSpec((1, tt, d), row),
            pl.BlockSpec((1, 1, 6 * d), lambda b, i: (b, 0, 0)),
            pl.BlockSpec((RET_V_W, d), c2, pipeline_mode=pl.Buffered(1)),
            pl.BlockSpec((SGU_WIDTH, d), c2, pipeline_mode=pl.Buffered(1)),
            pl.BlockSpec((d, d), c2, pipeline_mode=pl.Buffered(1)),
            pl.BlockSpec((SGU_GROUPS, SGU_CHUNK, SGU_CHUNK), c3),
            pl.BlockSpec((SGU_CHUNK, SGU_GROUPS), c2),
            pl.BlockSpec((1, RET_V_W), c2),
            pl.BlockSpec((1, RET_V_W), c2),
            pl.BlockSpec((1, d), c2),
            pl.BlockSpec((1, d), c2),
            pl.BlockSpec((1, d), c2),
            pl.BlockSpec((N_EXPERTS, d), c2),
            pl.BlockSpec((N_EXPERTS, 1), c2),
        ],
        out_specs=(pl.BlockSpec((1, tt, d), row),
                   pl.BlockSpec((1, tt, d), row),
                   pl.BlockSpec((1, N_EXPERTS, tt), lambda b, i: (b, 0, i))),
        scratch_shapes=[pltpu.VMEM((tt, RET_V_W), BF16), pltpu.VMEM((tt, SGU_WIDTH), BF16)],
        compiler_params=_cparams(("parallel", "parallel")),
        name="mixer",
    )(lg, z, z, z, states, x, mod3, wpa, wpb, wo, sguw, sgub_t, gnw, gnb, bo,
      ln1w, ln1b, wr_t, br)


def _route_kernel(lgt_ref, pos_ref, gate_ref, cnt_ref, *, cap):
    lg = lgt_ref[0]
    n_e, t = lg.shape
    tb = TB_ROUTE
    m = jnp.max(lg, axis=0, keepdims=True)
    ex = jnp.exp(lg - m)
    aff = ex / jnp.sum(ex, axis=0, keepdims=True)

    def search(i, thr_bits):
        cand = thr_bits | lax.shift_left(jnp.int32(1), 30 - i)
        cnt = jnp.sum((aff >= lax.bitcast_convert_type(cand, F32)).astype(I32), axis=1, keepdims=True)
        return jnp.where(cnt >= cap, cand, thr_bits)

    thr_bits = lax.fori_loop(0, 31, search, jnp.zeros((n_e, 1), I32))
    floor_f = lax.bitcast_convert_type(thr_bits, F32)
    thr = jnp.min(jnp.where(aff >= floor_f, aff, jnp.inf), axis=1, keepdims=True)
    need = (cap - jnp.sum((aff > thr).astype(I32), axis=1, keepdims=True)).astype(F32)

    r = lax.broadcasted_iota(I32, (tb, tb), 0)
    cc = lax.broadcasted_iota(I32, (tb, tb), 1)
    tri = (r <= cc).astype(BF16)
    carry_eq = jnp.zeros((n_e, 1), F32)
    carry_sel = jnp.zeros((n_e, 1), F32)
    for blk in range(t // tb):
        sl = slice(blk * tb, (blk + 1) * tb)
        aff_b = aff[:, sl]
        eq = aff_b == thr
        eq_b = eq.astype(BF16)
        inc_eq = jnp.dot(eq_b, tri, preferred_element_type=F32)
        before = carry_eq + inc_eq - eq_b.astype(F32)
        sel = (aff_b > thr) | (eq & (before < need))
        sel_b = sel.astype(BF16)
        inc_sel = jnp.dot(sel_b, tri, preferred_element_type=F32)
        pos = carry_sel + inc_sel - 1.0
        pos_ref[0, blk] = jnp.where(sel, pos.astype(I32), -1)
        gate_ref[0, blk] = aff_b
        cnt_ref[0, blk] = jnp.broadcast_to(carry_sel, (n_e, LANES)).astype(I32)
        carry_eq = carry_eq + inc_eq[:, tb - 1:tb]
        carry_sel = carry_sel + inc_sel[:, tb - 1:tb]


def _route_call(logits_t, cap):
    bsz, n_e, t = logits_t.shape
    nblk = t // TB_ROUTE
    return pl.pallas_call(
        functools.partial(_route_kernel, cap=cap),
        out_shape=(jax.ShapeDtypeStruct((bsz, nblk, n_e, TB_ROUTE), I32),
                   jax.ShapeDtypeStruct((bsz, nblk, n_e, TB_ROUTE), F32),
                   jax.ShapeDtypeStruct((bsz, nblk, n_e, LANES), I32)),
        grid=(bsz,),
        in_specs=[pl.BlockSpec((1, n_e, t), lambda b: (b, 0, 0))],
        out_specs=(pl.BlockSpec((1, nblk, n_e, TB_ROUTE), lambda b: (b, 0, 0, 0)),
                   pl.BlockSpec((1, nblk, n_e, TB_ROUTE), lambda b: (b, 0, 0, 0)),
                   pl.BlockSpec((1, nblk, n_e, LANES), lambda b: (b, 0, 0, 0))),
        compiler_params=_cparams(("parallel",)),
        name="route",
    )(logits_t)


def _window_rows(ws, win):
    return lax.broadcasted_iota(I32, (win, TB_ROUTE), 0) + ws


def _sweep_window(w, cap, win, prow, rest):
    off = pl.multiple_of(jnp.minimum(w * win, cap - win), WIN_ALIGN)
    rows = _window_rows(off, win)
    return off, (rows == prow) & rest & (rows >= w * win)


def _gather_kernel(ws_ref, nx_ref, hm_ref, pos_ref, gate_ref, xs_ref, gc_ref):
    b = pl.program_id(0)
    g = pl.program_id(1)
    kb = pl.program_id(2)
    eg = pos_ref.shape[2]
    n_e = eg * pl.num_programs(1)
    cap = xs_ref.shape[2]
    win = WIN_DISPATCH

    @pl.when(kb == 0)
    def _():
        xs_ref[...] = jnp.zeros_like(xs_ref)
        gc_ref[...] = jnp.zeros_like(gc_ref)

    base = (b * pl.num_programs(2) + kb) * n_e + g * eg
    starts = [pl.multiple_of(ws_ref[base + el], WIN_ALIGN) for el in range(eg)]
    for el in range(eg):
        match = _window_rows(starts[el], win) == pos_ref[0, 0, el:el + 1, :]
        sl = pl.ds(starts[el], win)
        xs_ref[0, el, sl, :] += jnp.dot(match.astype(BF16), hm_ref[0], preferred_element_type=F32).astype(BF16)
        gc_ref[0, el, sl, :] += jnp.sum(jnp.where(match, gate_ref[0, 0, el:el + 1, :], 0.0),
                                         axis=1, keepdims=True)

    for el in range(eg):
        @pl.when(nx_ref[base + el] > 0)
        def _(el=el):
            prow = pos_ref[0, 0, el:el + 1, :]
            grow = gate_ref[0, 0, el:el + 1, :]
            rest = prow >= starts[el] + win

            def body(w, carry):
                off, m = _sweep_window(w, cap, win, prow, rest)
                xs_ref[0, el, pl.ds(off, win), :] += jnp.dot(m.astype(BF16), hm_ref[0],
                                                              preferred_element_type=F32).astype(BF16)
                gc_ref[0, el, pl.ds(off, win), :] += jnp.sum(jnp.where(m, grow, 0.0), axis=1, keepdims=True)
                return carry

            lax.fori_loop(0, pl.cdiv(cap, win), body, 0)


def _gather_call(ws, nx, hm, pos_b, gate_b, cap):
    bsz, t, d = hm.shape
    nblk, n_e, tb = pos_b.shape[1:]
    eg = n_e // DISPATCH_GROUPS
    grid_spec = pltpu.PrefetchScalarGridSpec(
        num_scalar_prefetch=2,
        grid=(bsz, DISPATCH_GROUPS, nblk),
        in_specs=[
            pl.BlockSpec((1, tb, d), lambda b, g, k, ws, nx: (b, k, 0)),
            pl.BlockSpec((1, 1, eg, tb), lambda b, g, k, ws, nx: (b, k, g, 0)),
            pl.BlockSpec((1, 1, eg, tb), lambda b, g, k, ws, nx: (b, k, g, 0)),
        ],
        out_specs=(pl.BlockSpec((1, eg, cap, d), lambda b, g, k, ws, nx: (b, g, 0, 0)),
                   pl.BlockSpec((1, eg, cap, 1), lambda b, g, k, ws, nx: (b, g, 0, 0))),
    )
    return pl.pallas_call(
        _gather_kernel,
        out_shape=(jax.ShapeDtypeStruct((bsz, n_e, cap, d), BF16),
                   jax.ShapeDtypeStruct((bsz, n_e, cap, 1), F32)),
        grid_spec=grid_spec,
        compiler_params=_cparams(("parallel", "parallel", "arbitrary")),
        name="gather",
    )(ws, nx, hm, pos_b, gate_b)


def _ffn_kernel(xs_ref, gc_ref, wg_ref, wu_ref, wd_ref, ye_ref, wgu_s, wd_s):
    f = wd_s.shape[0]

    @pl.when(pl.program_id(1) == 0)
    def _():
        wgu_s[:, 0:f] = wg_ref[0].astype(BF16)
        wgu_s[:, f:2 * f] = wu_ref[0].astype(BF16)
        wd_s[...] = wd_ref[0].astype(BF16)

    rt = FFN_ROWS
    for j in range(xs_ref.shape[2] // rt):
        sl = slice(j * rt, (j + 1) * rt)
        xgu = jnp.dot(xs_ref[0, 0, sl, :], wgu_s[...], preferred_element_type=F32)
        hid = (jax.nn.silu(xgu[:, 0:f]) * xgu[:, f:2 * f]).astype(BF16)
        ye = jnp.dot(hid, wd_s[...], preferred_element_type=F32) * gc_ref[0, 0, sl, :]
        ye_ref[0, 0, sl, :] = ye.astype(BF16)


def _ffn_call(xs, gc, w_gate, w_up, w_down):
    bsz, n_e, cap, d = xs.shape
    f = w_gate.shape[2]
    return pl.pallas_call(
        _ffn_kernel,
        out_shape=jax.ShapeDtypeStruct((bsz, n_e, cap, d), BF16),
        grid=(n_e, bsz),
        in_specs=[
            pl.BlockSpec((1, 1, cap, d), lambda e, b: (b, e, 0, 0)),
            pl.BlockSpec((1, 1, cap, 1), lambda e, b: (b, e, 0, 0)),
            pl.BlockSpec((1, d, f), lambda e, b: (e, 0, 0)),
            pl.BlockSpec((1, d, f), lambda e, b: (e, 0, 0)),
            pl.BlockSpec((1, f, d), lambda e, b: (e, 0, 0)),
        ],
        out_specs=pl.BlockSpec((1, 1, cap, d), lambda e, b: (b, e, 0, 0)),
        scratch_shapes=[pltpu.VMEM((d, 2 * f), BF16), pltpu.VMEM((f, d), BF16)],
        compiler_params=_cparams(("arbitrary", "arbitrary")),
        name="ffn",
    )(xs, gc, w_gate, w_up, w_down)


def _combine_kernel(ws_ref, nx_ref, ye_ref, pos_ref, x1_ref, mod_ref, lnw_ref, lnb_ref, out_ref, acc_scr):
    b = pl.program_id(0)
    kb = pl.program_id(1)
    n_e = pos_ref.shape[2]
    cap = ye_ref.shape[2]
    d = D_MODEL
    win = WIN_COMBINE
    base = (b * pl.num_programs(1) + kb) * n_e
    starts = [pl.multiple_of(ws_ref[base + e], WIN_ALIGN) for e in range(n_e)]
    onehot = jnp.concatenate(
        [(_window_rows(starts[e], win) == pos_ref[0, 0, e:e + 1, :]).astype(BF16) for e in range(n_e)], axis=0)
    rows = jnp.concatenate([ye_ref[0, e, pl.ds(starts[e], win), :] for e in range(n_e)], axis=0)
    acc_scr[...] = _dot_t0(onehot, rows)

    for e in range(n_e):
        @pl.when(nx_ref[base + e] > 0)
        def _(e=e):
            prow = pos_ref[0, 0, e:e + 1, :]
            rest = prow >= starts[e] + win

            def body(w, carry):
                off, m = _sweep_window(w, cap, win, prow, rest)
                acc_scr[...] += _dot_t0(m.astype(BF16), ye_ref[0, e, pl.ds(off, win), :])
                return carry

            lax.fori_loop(0, pl.cdiv(cap, win), body, 0)

    g2 = mod_ref[0, :, 5 * d:6 * d]
    out_ref[0] = _ln(DEEPNORM_ALPHA * x1_ref[0] + g2 * acc_scr[...]) * lnw_ref[...] + lnb_ref[...]


def _combine_call(ws, nx, ye, pos_b, x1, mod3, ln2w, ln2b):
    bsz, n_e, cap, d = ye.shape
    nblk, _, tb = pos_b.shape[1:]
    grid_spec = pltpu.PrefetchScalarGridSpec(
        num_scalar_prefetch=2,
        grid=(bsz, nblk),
        in_specs=[
            pl.BlockSpec((1, n_e, cap, d), lambda b, k, ws, nx: (b, 0, 0, 0), pipeline_mode=pl.Buffered(1)),
            pl.BlockSpec((1, 1, n_e, tb), lambda b, k, ws, nx: (b, k, 0, 0)),
            pl.BlockSpec((1, tb, d), lambda b, k, ws, nx: (b, k, 0)),
            pl.BlockSpec((1, 1, 6 * d), lambda b, k, ws, nx: (b, 0, 0)),
            pl.BlockSpec((1, d), lambda b, k, ws, nx: (0, 0)),
            pl.BlockSpec((1, d), lambda b, k, ws, nx: (0, 0)),
        ],
        out_specs=pl.BlockSpec((1, tb, d), lambda b, k, ws, nx: (b, k, 0)),
        scratch_shapes=[pltpu.VMEM((tb, d), F32)],
    )
    return pl.pallas_call(
        _combine_kernel,
        out_shape=jax.ShapeDtypeStruct((bsz, nblk * tb, d), F32),
        grid_spec=grid_spec,
        compiler_params=_cparams(("parallel", "arbitrary")),
        name="combine",
    )(ws, nx, ye, pos_b, x1, mod3, ln2w, ln2b)


def _rope_tables(n_tokens):
    rows = n_tokens // GRID_W
    n_freq = RET_QK_DIM // 4
    inv = ROPE_THETA ** (-jnp.arange(n_freq, dtype=F32) / n_freq)
    ang_r = jnp.arange(rows, dtype=F32)[:, None] * inv
    ang_c = jnp.arange(GRID_W, dtype=F32)[:, None] * inv
    zr = jnp.zeros((rows, 2 * n_freq), F32)
    zc = jnp.zeros((GRID_W, 2 * n_freq), F32)
    cr, sr, cc, sc = jnp.cos(ang_r), jnp.sin(ang_r), jnp.cos(ang_c), jnp.sin(ang_c)
    cos_t = (jnp.concatenate([cr, cr, zr], axis=-1)[:, None, :]
             + jnp.concatenate([zc, cc, cc], axis=-1)[None, :, :]).reshape(n_tokens, RET_QK_DIM)
    sin_t = (jnp.concatenate([-sr, sr, zr], axis=-1)[:, None, :]
             + jnp.concatenate([zc, -sc, sc], axis=-1)[None, :, :]).reshape(n_tokens, RET_QK_DIM)
    return jnp.concatenate([cos_t, sin_t], axis=-1)


def kernel(x, c, ctx, c_ctx, w_ada, b_ada, w_in, b_in, ret_decay_f, ret_decay_b, ret_gn_w, ret_gn_b,
           sgu_ln_w, sgu_ln_b, sgu_w, sgu_b, w_pa, w_pb, w_o, b_o, ln1_w, ln1_b,
           w_router, b_router, w_gate, w_up, w_down, ln2_w, ln2_b):
    bsz, t, d = x.shape
    assert d == D_MODEL and w_ada.shape[0] == DEPTH == 1 and bsz < MOD_ROWS
    cap = EC_CAPACITY_FACTOR * t // N_EXPERTS
    for win in (WIN_DISPATCH, WIN_COMBINE):
        assert cap >= win and (cap - win) % WIN_ALIGN == 0 and win % WIN_ALIGN == 0
    assert t % TB_ROUTE == 0
    assert N_EXPERTS % DISPATCH_GROUPS == 0
    l = 0

    cvec = jnp.zeros((MOD_ROWS, d), F32).at[:bsz].set(c).at[bsz].set(c_ctx)
    mod = _mod_call(cvec.T, w_ada[l], b_ada[l][None], bsz + 1)
    mod3 = mod[:, None, :]

    w_in_b = w_in[l].astype(BF16)
    b_in2 = b_in[l][None]
    lo_kv, hi_kv = RET_QK_W, 2 * RET_QK_W + RET_V_W
    kctx, vctx = _ctx_kv_call(ctx, mod3, w_in_b[:, lo_kv:hi_kv], b_in2[:, lo_kv:hi_kv], bsz)

    z = _inproj_call(x, mod3, w_in_b, b_in2, _rope_tables(t), sgu_ln_w[l][None], sgu_ln_b[l][None])

    lg = jnp.stack([jax.nn.log_sigmoid(ret_decay_f[l].astype(F32)),
                    jax.nn.log_sigmoid(ret_decay_b[l].astype(F32))])
    states = _state_call(lg, z, kctx, vctx)

    x1, hm, logits_t = _mixer_call(
        lg, z, states, x, mod3,
        w_pa[l].astype(BF16), w_pb[l].astype(BF16), w_o[l].astype(BF16),
        sgu_w[l].astype(BF16), sgu_b[l].T, ret_gn_w[l][None], ret_gn_b[l][None], b_o[l][None],
        ln1_w[l][None], ln1_b[l][None], w_router[l].T.astype(BF16), b_router[l][:, None])

    pos_b, gate_b, cnt_b = _route_call(logits_t, cap)

    c0 = cnt_b[..., 0]
    c1 = jnp.concatenate([c0[:, 1:], jnp.full_like(c0[:, :1], cap)], axis=1)

    def windows(win):
        ws = jnp.minimum((c0 // WIN_ALIGN) * WIN_ALIGN, cap - win)
        return ws.reshape(-1), (c1 > ws + win).astype(I32).reshape(-1)

    ws_d, nx_d = windows(WIN_DISPATCH)
    ws_c, nx_c = windows(WIN_COMBINE)

    xs, gc = _gather_call(ws_d, nx_d, hm, pos_b, gate_b, cap)
    ye = _ffn_call(xs, gc, w_gate[l], w_up[l], w_down[l])
    return _combine_call(ws_c, nx_c, ye, pos_b, x1, mod3, ln2_w[l][None], ln2_b[l][None])
```

```python
import functools
import math

import jax
import jax.numpy as jnp
from jax import lax
from jax.experimental import pallas as pl
from jax.experimental.pallas import tpu as pltpu

F32 = jnp.float32
BF16 = jnp.bfloat16
I32 = jnp.int32

D_MODEL = 1024
DEPTH = 1
GRID_W = 64
RET_HEADS = 4
RET_QK_DIM = 128
RET_V_DIM = 256
RET_CHUNK = 128
ROPE_THETA = 10000.0
RET_QK_W = RET_HEADS * RET_QK_DIM
RET_V_W = RET_HEADS * RET_V_DIM
SGU_GROUPS = 4
SGU_CHUNK = 128
SGU_WIDTH = 1024
SGU_GW = SGU_WIDTH // SGU_GROUPS
N_EXPERTS = 16
EC_CAPACITY_FACTOR = 2
D_EXPERT = 1024
LN_EPS = 1e-6
DEEPNORM_ALPHA = (2.0 * DEPTH) ** 0.25
D_IN = 2 * RET_QK_W + 2 * RET_V_W + 2 * SGU_WIDTH + 2 * D_MODEL
Z_V = 2 * RET_QK_W
Z_SG = Z_V + RET_V_W
Z_U = Z_SG + RET_V_W
Z_VS = Z_U + SGU_WIDTH
Z_GA = Z_VS + SGU_WIDTH
Z_GB = Z_GA + D_MODEL
Z_BLOCK = 3 * 1024

V7X_VMEM_BYTES = 64 * 1024 * 1024
VMEM_LIMIT = 56 * 1024 * 1024
LANES = 128

MOD_ROWS = 8
TM_INPROJ = 512
INPROJ_CHUNK = 256
TT_MIXER = 512
MERGE_ROWS = 512
TB_ROUTE = 512
WIN_DISPATCH = 128
WIN_COMBINE = 96
WIN_ALIGN = 16
DISPATCH_GROUPS = 2
STATE_UNROLL = 8
FFN_ROWS = 256


def _cparams(sem):
    return pltpu.CompilerParams(dimension_semantics=sem, vmem_limit_bytes=VMEM_LIMIT)


def _ln(x):
    mu = jnp.mean(x, axis=-1, keepdims=True)
    xc = x - mu
    var = jnp.mean(xc * xc, axis=-1, keepdims=True)
    return xc * lax.rsqrt(var + LN_EPS)


_GELU_A = -2.0 * math.sqrt(2.0 / math.pi) * math.log2(math.e)
_GELU_B = _GELU_A * 0.044715


def _gelu(x):
    return x / (1.0 + jnp.exp2(x * (_GELU_A + _GELU_B * (x * x))))


def _iota_f(shape, dim):
    return lax.broadcasted_iota(I32, shape, dim).astype(F32)


def _mod_kernel(ct_ref, w_ref, b_ref, o_ref, *, n_rows):
    c = ct_ref[...]
    a = c * jax.nn.sigmoid(c)
    w = w_ref[...]
    rows = [jnp.sum(w * a[:, m:m + 1], axis=0, keepdims=True) + b_ref[...] for m in range(n_rows)]
    rows += [jnp.zeros_like(rows[0])] * (MOD_ROWS - n_rows)
    o_ref[...] = jnp.concatenate(rows, axis=0)


def _mod_call(cvec_t, w_ada, b_ada, n_rows):
    d, n = w_ada.shape
    tn = 1024
    return pl.pallas_call(
        functools.partial(_mod_kernel, n_rows=n_rows),
        out_shape=jax.ShapeDtypeStruct((MOD_ROWS, n), F32),
        grid=(n // tn,),
        in_specs=[
            pl.BlockSpec((d, MOD_ROWS), lambda j: (0, 0)),
            pl.BlockSpec((d, tn), lambda j: (0, j)),
            pl.BlockSpec((1, tn), lambda j: (0, j)),
        ],
        out_specs=pl.BlockSpec((MOD_ROWS, tn), lambda j: (0, j)),
        compiler_params=_cparams(("parallel",)),
        name="mod",
    )(cvec_t, w_ada, b_ada)


def _ctx_kv_kernel(x_ref, mod_ref, w_ref, b_ref, k_ref, v_ref):
    d = D_MODEL
    sh = mod_ref[0, :, 0:d]
    sc = mod_ref[0, :, d:2 * d]
    h = (_ln(x_ref[0]) * (1.0 + sc) + sh).astype(BF16)
    z = jnp.dot(h, w_ref[...], preferred_element_type=F32) + b_ref[...]
    k_ref[0] = (z[:, :RET_QK_W] * (RET_QK_DIM ** -0.5)).astype(BF16)
    v_ref[0] = z[:, RET_QK_W:].astype(BF16)


def _ctx_kv_call(ctx, mod3, w_kv, b_kv, ctx_row):
    bsz, n_ctx, d = ctx.shape
    wkv = w_kv.shape[1]
    return pl.pallas_call(
        _ctx_kv_kernel,
        out_shape=(jax.ShapeDtypeStruct((bsz, n_ctx, RET_QK_W), BF16),
                   jax.ShapeDtypeStruct((bsz, n_ctx, RET_V_W), BF16)),
        grid=(bsz,),
        in_specs=[
            pl.BlockSpec((1, n_ctx, d), lambda b: (b, 0, 0)),
            pl.BlockSpec((1, 1, 6 * d), lambda b: (ctx_row, 0, 0)),
            pl.BlockSpec((d, wkv), lambda b: (0, 0)),
            pl.BlockSpec((1, wkv), lambda b: (0, 0)),
        ],
        out_specs=(pl.BlockSpec((1, n_ctx, RET_QK_W), lambda b: (b, 0, 0)),
                   pl.BlockSpec((1, n_ctx, RET_V_W), lambda b: (b, 0, 0))),
        compiler_params=_cparams(("parallel",)),
        name="ctx_kv",
    )(ctx, mod3, w_kv, b_kv)


def _inproj_kernel(x_ref, mod_ref, w_ref, b_ref, rope_ref, lnw_ref, lnb_ref, z_ref):
    d = D_MODEL
    sh = mod_ref[0, :, 0:d]
    sc = mod_ref[0, :, d:2 * d]
    h = (_ln(x_ref[0]) * (1.0 + sc) + sh).astype(BF16)

    def proj(c0, width):
        return jnp.dot(h, w_ref[:, c0:c0 + width], preferred_element_type=F32) + b_ref[:, c0:c0 + width]

    zq = proj(0, 2 * RET_QK_W)
    cos = rope_ref[:, 0:LANES]
    sin = rope_ref[:, LANES:2 * LANES]
    tm = zq.shape[0]
    upper = (lax.broadcasted_iota(I32, (tm, LANES), 1) & 32) != 0
    for hb in range(2 * RET_HEADS):
        zs = zq[:, hb * LANES:(hb + 1) * LANES]
        if hb >= RET_HEADS:
            zs = zs * (RET_QK_DIM ** -0.5)
        sw = jnp.where(upper, pltpu.roll(zs, 32, 1), pltpu.roll(zs, LANES - 32, 1))
        z_ref[0, :, hb * LANES:(hb + 1) * LANES] = (zs * cos + sw * sin).astype(BF16)

    def emit(c0, width, fn):
        for cc in range(c0, c0 + width, INPROJ_CHUNK):
            z_ref[0, :, cc:cc + INPROJ_CHUNK] = fn(proj(cc, INPROJ_CHUNK)).astype(BF16)

    vs = _ln(_gelu(proj(Z_VS, SGU_WIDTH))) * lnw_ref[...] + lnb_ref[...]
    z_ref[0, :, Z_VS:Z_VS + SGU_WIDTH] = vs.astype(BF16)
    emit(Z_U, SGU_WIDTH, _gelu)
    emit(Z_SG, RET_V_W, jax.nn.silu)
    emit(Z_GA, d, jax.nn.sigmoid)
    emit(Z_GB, d, jax.nn.sigmoid)
    emit(Z_V, RET_V_W, lambda z: z)


def _inproj_call(x, mod3, w_in, b_in, rope_t, sgu_ln_w, sgu_ln_b):
    bsz, t, d = x.shape
    tm = TM_INPROJ
    row = lambda b, i: (b, i, 0)
    const2 = lambda b, i: (0, 0)
    return pl.pallas_call(
        _inproj_kernel,
        out_shape=jax.ShapeDtypeStruct((bsz, t, D_IN), BF16),
        grid=(bsz, t // tm),
        in_specs=[
            pl.BlockSpec((1, tm, d), row),
            pl.BlockSpec((1, 1, 6 * d), lambda b, i: (b, 0, 0)),
            pl.BlockSpec((d, D_IN), const2, pipeline_mode=pl.Buffered(1)),
            pl.BlockSpec((1, D_IN), const2),
            pl.BlockSpec((tm, 2 * LANES), lambda b, i: (i, 0)),
            pl.BlockSpec((1, SGU_WIDTH), const2),
            pl.BlockSpec((1, SGU_WIDTH), const2),
        ],
        out_specs=pl.BlockSpec((1, tm, D_IN), row),
        compiler_params=_cparams(("parallel", "parallel")),
        name="inproj",
    )(x, mod3, w_in, b_in, rope_t, sgu_ln_w, sgu_ln_b)


def _dot_t0(a, b):
    return lax.dot_general(a, b, (((0,), (0,)), ((), ())), preferred_element_type=F32)


def _state_kernel(lg_ref, k_ref, v_ref, kc_ref, vc_ref, s_ref):
    hd = pl.program_id(1)
    lgf = lg_ref[0, hd]
    lgb = lg_ref[1, hd]
    n_chunks = s_ref.shape[2]
    n_ctx = kc_ref.shape[1]
    c = RET_CHUNK
    li = _iota_f((c, RET_QK_DIM), 0)
    zeta_f = jnp.exp((c - 1.0 - li) * lgf)
    zeta_b = jnp.exp(li * lgb)
    one = jnp.ones((1, 1), F32)
    cd_f = jnp.exp(one * (c * lgf))
    cd_b = jnp.exp(one * (c * lgb))

    tc = _iota_f((n_ctx, RET_QK_DIM), 0)
    kc = kc_ref[0].astype(F32)
    vc = vc_ref[0]
    s0f = _dot_t0((kc * jnp.exp((n_ctx - 1.0 - tc) * lgf)).astype(BF16), vc)
    s0b = _dot_t0((kc * jnp.exp(tc * lgb)).astype(BF16), vc)

    def chunk_kv(n, zeta):
        off = pl.multiple_of(n * c, c)
        kk = k_ref[0, pl.ds(off, c), :].astype(F32)
        return _dot_t0((kk * zeta).astype(BF16), v_ref[0, pl.ds(off, c), :])

    def step(i, carry):
        sf, sb = carry
        nb = n_chunks - 1 - i
        s_ref[0, 0, i, 0:RET_QK_DIM, :] = sf.astype(BF16)
        s_ref[0, 0, nb, RET_QK_DIM:2 * RET_QK_DIM, :] = sb.astype(BF16)
        return cd_f * sf + chunk_kv(i, zeta_f), cd_b * sb + chunk_kv(nb, zeta_b)

    lax.fori_loop(0, n_chunks, step, (s0f, s0b), unroll=STATE_UNROLL)


def _state_call(lg, z, kctx, vctx):
    bsz, t, _ = z.shape
    n_ctx = kctx.shape[1]
    n_chunks = t // RET_CHUNK
    return pl.pallas_call(
        _state_kernel,
        out_shape=jax.ShapeDtypeStruct((bsz, RET_HEADS, n_chunks, 2 * RET_QK_DIM, RET_V_DIM), BF16),
        grid=(bsz, RET_HEADS),
        in_specs=[
            pl.BlockSpec(memory_space=pltpu.SMEM),
            pl.BlockSpec((1, t, RET_QK_DIM), lambda b, h: (b, 0, RET_QK_W // RET_QK_DIM + h)),
            pl.BlockSpec((1, t, RET_V_DIM), lambda b, h: (b, 0, Z_V // RET_V_DIM + h)),
            pl.BlockSpec((1, n_ctx, RET_QK_DIM), lambda b, h: (b, 0, h)),
            pl.BlockSpec((1, n_ctx, RET_V_DIM), lambda b, h: (b, 0, h)),
        ],
        out_specs=pl.BlockSpec((1, 1, n_chunks, 2 * RET_QK_DIM, RET_V_DIM), lambda b, h: (b, h, 0, 0, 0)),
        compiler_params=_cparams(("parallel", "parallel")),
        name="states",
    )(lg, z, z, kctx, vctx)


def _mixer_kernel(lg_ref, za_ref, zb_ref, gb_ref, s_ref, x_ref, mod_ref,
                  wpa_ref, wpb_ref, wo_ref, sguw_ref, sgub_ref, gnw_ref, gnb_ref, bo_ref,
                  ln1w_ref, ln1b_ref, wr_ref, br_ref,
                  x1_ref, hm_ref, lgt_ref, ret_scr, sgu_scr):
    d = D_MODEL
    c = RET_CHUNK

    n_sub = za_ref.shape[1] // c
    row = lax.broadcasted_iota(I32, (c, c), 0)
    col = lax.broadcasted_iota(I32, (c, c), 1)
    diff = (row - col).astype(F32)
    rowq = _iota_f((c, RET_QK_DIM), 0)
    for hd in range(RET_HEADS):
        lgf = lg_ref[0, hd]
        lgb = lg_ref[1, hd]
        vcols = slice(hd * RET_V_DIM, (hd + 1) * RET_V_DIM)
        mask = (jnp.where(diff >= 0, jnp.exp(jnp.maximum(diff, 0.0) * lgf), 0.0)
                + jnp.where(diff <= 0, jnp.exp(jnp.maximum(-diff, 0.0) * lgb), 0.0))
        xi_f = jnp.exp((rowq + 1.0) * lgf)
        xi_b = jnp.exp((c - rowq) * lgb)
        for ci in range(n_sub):
            r0 = ci * c
            q = za_ref[0, r0:r0 + c, hd * RET_QK_DIM:(hd + 1) * RET_QK_DIM]
            k = za_ref[0, r0:r0 + c, RET_QK_W + hd * RET_QK_DIM:RET_QK_W + (hd + 1) * RET_QK_DIM]
            s = lax.dot_general(q, k, (((1,), (1,)), ((), ())), preferred_element_type=F32)
            vv = za_ref[0, r0:r0 + c, Z_V + hd * RET_V_DIM:Z_V + (hd + 1) * RET_V_DIM]
            intra = jnp.dot((s * mask).astype(BF16), vv, preferred_element_type=F32)
            qf = q.astype(F32)
            qx = jnp.concatenate([(qf * xi_f).astype(BF16), (qf * xi_b).astype(BF16)], axis=1)
            cross = jnp.dot(qx, s_ref[0, hd, ci], preferred_element_type=F32)
            o = _ln(intra + cross) * gnw_ref[:, vcols] + gnb_ref[:, vcols]
            gate = za_ref[0, r0:r0 + c, Z_SG + hd * RET_V_DIM:Z_SG + (hd + 1) * RET_V_DIM]
            ret_scr[r0:r0 + c, vcols] = (o * gate.astype(F32)).astype(BF16)

    for g in range(SGU_GROUPS):
        gcols = slice(g * SGU_GW, (g + 1) * SGU_GW)
        wg = sguw_ref[g]
        bg = sgub_ref[:, g:g + 1]
        for ci in range(n_sub):
            r0 = ci * c
            vsb = zb_ref[0, r0:r0 + c, Z_VS - Z_BLOCK + g * SGU_GW:Z_VS - Z_BLOCK + (g + 1) * SGU_GW]
            ub = zb_ref[0, r0:r0 + c, Z_U - Z_BLOCK + g * SGU_GW:Z_U - Z_BLOCK + (g + 1) * SGU_GW]
            sp = jnp.dot(wg, vsb, preferred_element_type=F32) + bg
            sgu_scr[r0:r0 + c, gcols] = (ub.astype(F32) * sp).astype(BF16)

    g1 = mod_ref[0, :, 2 * d:3 * d]
    sh2 = mod_ref[0, :, 3 * d:4 * d]
    sc2 = mod_ref[0, :, 4 * d:5 * d]
    for r0 in range(0, za_ref.shape[1], MERGE_ROWS):
        rows = slice(r0, r0 + MERGE_ROWS)
        pa = jnp.dot(ret_scr[rows, :], wpa_ref[...], preferred_element_type=F32)
        pb = jnp.dot(sgu_scr[rows, :], wpb_ref[...], preferred_element_type=F32)
        ga = zb_ref[0, rows, Z_GA - Z_BLOCK:Z_GA - Z_BLOCK + d]
        y = (ga.astype(F32) * pa + gb_ref[0, rows, :].astype(F32) * pb).astype(BF16)
        mix = jnp.dot(y, wo_ref[...], preferred_element_type=F32) + bo_ref[...]
        x1 = _ln(DEEPNORM_ALPHA * x_ref[0, rows, :] + g1 * mix) * ln1w_ref[...] + ln1b_ref[...]
        x1_ref[0, rows, :] = x1
        hm = (_ln(x1) * (1.0 + sc2) + sh2).astype(BF16)
        hm_ref[0, rows, :] = hm
        lgt_ref[0, :, rows] = lax.dot_general(wr_ref[...], hm, (((1,), (1,)), ((), ())),
                                              preferred_element_type=F32) + br_ref[...]


def _mixer_call(lg, z, states, x, mod3, wpa, wpb, wo, sguw, sgub_t, gnw, gnb, bo,
                ln1w, ln1b, wr_t, br):
    bsz, t, d = x.shape
    tt = TT_MIXER
    n_sub = tt // RET_CHUNK
    row = lambda b, i: (b, i, 0)
    c2 = lambda b, i: (0, 0)
    c3 = lambda b, i: (0, 0, 0)
    return pl.pallas_call(
        _mixer_kernel,
        out_shape=(jax.ShapeDtypeStruct((bsz, t, d), F32),
                   jax.ShapeDtypeStruct((bsz, t, d), BF16),
                   jax.ShapeDtypeStruct((bsz, N_EXPERTS, t), F32)),
        grid=(bsz, t // tt),
        in_specs=[
            pl.BlockSpec(memory_space=pltpu.SMEM),
            pl.BlockSpec((1, tt, Z_BLOCK), lambda b, i: (b, i, 0)),
            pl.BlockSpec((1, tt, Z_BLOCK), lambda b, i: (b, i, 1)),
            pl.BlockSpec((1, tt, d), lambda b, i: (b, i, Z_GB // d)),
            pl.BlockSpec((1, RET_HEADS, n_sub, 2 * RET_QK_DIM, RET_V_DIM), lambda b, i: (b, 0, i, 0, 0)),
            pl.BlockSpec((1, tt, d), row),
            pl.BlockSpec((1, 1, 6 * d), lambda b, i: (b, 0, 0)),
            pl.BlockSpec((RET_V_W, d), c2, pipeline_mode=pl.Buffered(1)),
            pl.BlockSpec((SGU_WIDTH, d), c2, pipeline_mode=pl.Buffered(1)),
            pl.BlockSpec((d, d), c2, pipeline_mode=pl.Buffered(1)),
            pl.BlockSpec((SGU_GROUPS, SGU_CHUNK, SGU_CHUNK), c3),
            pl.BlockSpec((SGU_CHUNK, SGU_GROUPS), c2),
            pl.BlockSpec((1, RET_V_W), c2),
            pl.BlockSpec((1, RET_V_W), c2),
            pl.BlockSpec((1, d), c2),
            pl.BlockSpec((1, d), c2),
            pl.BlockSpec((1, d), c2),
            pl.BlockSpec((N_EXPERTS, d), c2),
            pl.BlockSpec((N_EXPERTS, 1), c2),
        ],
        out_specs=(pl.BlockSpec((1, tt, d), row),
                   pl.BlockSpec((1, tt, d), row),
                   pl.BlockSpec((1, N_EXPERTS, tt), lambda b, i: (b, 0, i))),
        scratch_shapes=[pltpu.VMEM((tt, RET_V_W), BF16), pltpu.VMEM((tt, SGU_WIDTH), BF16)],
        compiler_params=_cparams(("parallel", "parallel")),
        name="mixer",
    )(lg, z, z, z, states, x, mod3, wpa, wpb, wo, sguw, sgub_t, gnw, gnb, bo,
      ln1w, ln1b, wr_t, br)


def _route_kernel(lgt_ref, pos_ref, gate_ref, cnt_ref, *, cap):
    lg = lgt_ref[0]
    n_e, t = lg.shape
    tb = TB_ROUTE
    m = jnp.max(lg, axis=0, keepdims=True)
    ex = jnp.exp(lg - m)
    aff = ex / jnp.sum(ex, axis=0, keepdims=True)

    def search(i, thr_bits):
        cand = thr_bits | lax.shift_left(jnp.int32(1), 30 - i)
        cnt = jnp.sum((aff >= lax.bitcast_convert_type(cand, F32)).astype(I32), axis=1, keepdims=True)
        return jnp.where(cnt >= cap, cand, thr_bits)

    thr_bits = lax.fori_loop(0, 31, search, jnp.zeros((n_e, 1), I32))
    floor_f = lax.bitcast_convert_type(thr_bits, F32)
    thr = jnp.min(jnp.where(aff >= floor_f, aff, jnp.inf), axis=1, keepdims=True)
    need = (cap - jnp.sum((aff > thr).astype(I32), axis=1, keepdims=True)).astype(F32)

    r = lax.broadcasted_iota(I32, (tb, tb), 0)
    cc = lax.broadcasted_iota(I32, (tb, tb), 1)
    tri = (r <= cc).astype(BF16)
    carry_eq = jnp.zeros((n_e, 1), F32)
    carry_sel = jnp.zeros((n_e, 1), F32)
    for blk in range(t // tb):
        sl = slice(blk * tb, (blk + 1) * tb)
        aff_b = aff[:, sl]
        eq = aff_b == thr
        eq_b = eq.astype(BF16)
        inc_eq = jnp.dot(eq_b, tri, preferred_element_type=F32)
        before = carry_eq + inc_eq - eq_b.astype(F32)
        sel = (aff_b > thr) | (eq & (before < need))
        sel_b = sel.astype(BF16)
        inc_sel = jnp.dot(sel_b, tri, preferred_element_type=F32)
        pos = carry_sel + inc_sel - 1.0
        pos_ref[0, blk] = jnp.where(sel, pos.astype(I32), -1)
        gate_ref[0, blk] = aff_b
        cnt_ref[0, blk] = jnp.broadcast_to(carry_sel, (n_e, LANES)).astype(I32)
        carry_eq = carry_eq + inc_eq[:, tb - 1:tb]
        carry_sel = carry_sel + inc_sel[:, tb - 1:tb]


def _route_call(logits_t, cap):
    bsz, n_e, t = logits_t.shape
    nblk = t // TB_ROUTE
    return pl.pallas_call(
        functools.partial(_route_kernel, cap=cap),
        out_shape=(jax.ShapeDtypeStruct((bsz, nblk, n_e, TB_ROUTE), I32),
                   jax.ShapeDtypeStruct((bsz, nblk, n_e, TB_ROUTE), F32),
                   jax.ShapeDtypeStruct((bsz, nblk, n_e, LANES), I32)),
        grid=(bsz,),
        in_specs=[pl.BlockSpec((1, n_e, t), lambda b: (b, 0, 0))],
        out_specs=(pl.BlockSpec((1, nblk, n_e, TB_ROUTE), lambda b: (b, 0, 0, 0)),
                   pl.BlockSpec((1, nblk, n_e, TB_ROUTE), lambda b: (b, 0, 0, 0)),
                   pl.BlockSpec((1, nblk, n_e, LANES), lambda b: (b, 0, 0, 0))),
        compiler_params=_cparams(("parallel",)),
        name="route",
    )(logits_t)


def _window_rows(ws, win):
    return lax.broadcasted_iota(I32, (win, TB_ROUTE), 0) + ws


def _sweep_window(w, cap, win, prow, rest):
    off = pl.multiple_of(jnp.minimum(w * win, cap - win), WIN_ALIGN)
    rows = _window_rows(off, win)
    return off, (rows == prow) & rest & (rows >= w * win)


def _gather_kernel(ws_ref, nx_ref, hm_ref, pos_ref, gate_ref, xs_ref, gc_ref):
    b = pl.program_id(0)
    g = pl.program_id(1)
    kb = pl.program_id(2)
    eg = pos_ref.shape[2]
    n_e = eg * pl.num_programs(1)
    cap = xs_ref.shape[2]
    win = WIN_DISPATCH

    @pl.when(kb == 0)
    def _():
        xs_ref[...] = jnp.zeros_like(xs_ref)
        gc_ref[...] = jnp.zeros_like(gc_ref)

    base = (b * pl.num_programs(2) + kb) * n_e + g * eg
    starts = [pl.multiple_of(ws_ref[base + el], WIN_ALIGN) for el in range(eg)]
    for el in range(eg):
        match = _window_rows(starts[el], win) == pos_ref[0, 0, el:el + 1, :]
        sl = pl.ds(starts[el], win)
        xs_ref[0, el, sl, :] += jnp.dot(match.astype(BF16), hm_ref[0], preferred_element_type=F32).astype(BF16)
        gc_ref[0, el, sl, :] += jnp.sum(jnp.where(match, gate_ref[0, 0, el:el + 1, :], 0.0),
                                         axis=1, keepdims=True)

    for el in range(eg):
        @pl.when(nx_ref[base + el] > 0)
        def _(el=el):
            prow = pos_ref[0, 0, el:el + 1, :]
            grow = gate_ref[0, 0, el:el + 1, :]
            rest = prow >= starts[el] + win

            def body(w, carry):
                off, m = _sweep_window(w, cap, win, prow, rest)
                xs_ref[0, el, pl.ds(off, win), :] += jnp.dot(m.astype(BF16), hm_ref[0],
                                                              preferred_element_type=F32).astype(BF16)
                gc_ref[0, el, pl.ds(off, win), :] += jnp.sum(jnp.where(m, grow, 0.0), axis=1, keepdims=True)
                return carry

            lax.fori_loop(0, pl.cdiv(cap, win), body, 0)


def _gather_call(ws, nx, hm, pos_b, gate_b, cap):
    bsz, t, d = hm.shape
    nblk, n_e, tb = pos_b.shape[1:]
    eg = n_e // DISPATCH_GROUPS
    grid_spec = pltpu.PrefetchScalarGridSpec(
        num_scalar_prefetch=2,
        grid=(bsz, DISPATCH_GROUPS, nblk),
        in_specs=[
            pl.BlockSpec((1, tb, d), lambda b, g, k, ws, nx: (b, k, 0)),
            pl.BlockSpec((1, 1, eg, tb), lambda b, g, k, ws, nx: (b, k, g, 0)),
            pl.BlockSpec((1, 1, eg, tb), lambda b, g, k, ws, nx: (b, k, g, 0)),
        ],
        out_specs=(pl.BlockSpec((1, eg, cap, d), lambda b, g, k, ws, nx: (b, g, 0, 0)),
                   pl.BlockSpec((1, eg, cap, 1), lambda b, g, k, ws, nx: (b, g, 0, 0))),
    )
    return pl.pallas_call(
        _gather_kernel,
        out_shape=(jax.ShapeDtypeStruct((bsz, n_e, cap, d), BF16),
                   jax.ShapeDtypeStruct((bsz, n_e, cap, 1), F32)),
        grid_spec=grid_spec,
        compiler_params=_cparams(("parallel", "parallel", "arbitrary")),
        name="gather",
    )(ws, nx, hm, pos_b, gate_b)


def _ffn_kernel(ws_ref, xs_ref, gc_ref, wg_ref, wu_ref, wd_ref, ye_ref, yw_ref, wgu_s, wd_s):
    f = wd_s.shape[0]
    e = pl.program_id(0)
    b = pl.program_id(1)
    n_e = pl.num_programs(0)
    nblk = yw_ref.shape[1]

    @pl.when(pl.program_id(1) == 0)
    def _():
        wgu_s[:, 0:f] = wg_ref[0].astype(BF16)
        wgu_s[:, f:2 * f] = wu_ref[0].astype(BF16)
        wd_s[...] = wd_ref[0].astype(BF16)

    rt = FFN_ROWS
    for j in range(xs_ref.shape[2] // rt):
        sl = slice(j * rt, (j + 1) * rt)
        xgu = jnp.dot(xs_ref[0, 0, sl, :], wgu_s[...], preferred_element_type=F32)
        hid = (jax.nn.silu(xgu[:, 0:f]) * xgu[:, f:2 * f]).astype(BF16)
        ye = jnp.dot(hid, wd_s[...], preferred_element_type=F32) * gc_ref[0, 0, sl, :]
        ye_ref[0, 0, sl, :] = ye.astype(BF16)

    for kb in range(nblk):
        ws = pl.multiple_of(ws_ref[(b * nblk + kb) * n_e + e], WIN_ALIGN)
        yw_ref[0, kb, 0] = ye_ref[0, 0, pl.ds(ws, WIN_COMBINE), :]


def _ffn_call(ws_c, xs, gc, w_gate, w_up, w_down, nblk):
    bsz, n_e, cap, d = xs.shape
    f = w_gate.shape[2]
    grid_spec = pltpu.PrefetchScalarGridSpec(
        num_scalar_prefetch=1,
        grid=(n_e, bsz),
        in_specs=[
            pl.BlockSpec((1, 1, cap, d), lambda e, b, ws: (b, e, 0, 0)),
            pl.BlockSpec((1, 1, cap, 1), lambda e, b, ws: (b, e, 0, 0)),
            pl.BlockSpec((1, d, f), lambda e, b, ws: (e, 0, 0)),
            pl.BlockSpec((1, d, f), lambda e, b, ws: (e, 0, 0)),
            pl.BlockSpec((1, f, d), lambda e, b, ws: (e, 0, 0)),
        ],
        out_specs=(pl.BlockSpec((1, 1, cap, d), lambda e, b, ws: (b, e, 0, 0)),
                   pl.BlockSpec((1, nblk, 1, WIN_COMBINE, d), lambda e, b, ws: (b, 0, e, 0, 0))),
        scratch_shapes=[pltpu.VMEM((d, 2 * f), BF16), pltpu.VMEM((f, d), BF16)],
    )
    return pl.pallas_call(
        _ffn_kernel,
        out_shape=(jax.ShapeDtypeStruct((bsz, n_e, cap, d), BF16),
                   jax.ShapeDtypeStruct((bsz, nblk, n_e, WIN_COMBINE, d), BF16)),
        grid_spec=grid_spec,
        compiler_params=_cparams(("arbitrary", "arbitrary")),
        name="ffn",
    )(ws_c, xs, gc, w_gate, w_up, w_down)


def _combine_kernel(ws_ref, nx_ref, yw_ref, ye_hbm, pos_ref, x1_ref, mod_ref, lnw_ref, lnb_ref, out_ref,
                    acc_scr, row_scr):
    b = pl.program_id(0)
    kb = pl.program_id(1)
    n_e = pos_ref.shape[2]
    cap = ye_hbm.shape[2]
    d = D_MODEL
    win = WIN_COMBINE
    base = (b * pl.num_programs(1) + kb) * n_e
    starts = [ws_ref[base + e] for e in range(n_e)]
    onehot = jnp.concatenate(
        [(_window_rows(starts[e], win) == pos_ref[0, 0, e:e + 1, :]).astype(BF16) for e in range(n_e)], axis=0)
    rows = jnp.concatenate([yw_ref[0, 0, e] for e in range(n_e)], axis=0)
    acc_scr[...] = _dot_t0(onehot, rows)

    for e in range(n_e):
        @pl.when(nx_ref[base + e] > 0)
        def _(e=e):
            prow = pos_ref[0, 0, e:e + 1, :]
            rest = prow >= starts[e] + win

            def body(w, carry):
                off, m = _sweep_window(w, cap, win, prow, rest)
                pltpu.sync_copy(ye_hbm.at[b, e, pl.ds(off, win), :], row_scr)
                acc_scr[...] += _dot_t0(m.astype(BF16), row_scr[...])
                return carry

            lax.fori_loop(0, pl.cdiv(cap, win), body, 0)

    g2 = mod_ref[0, :, 5 * d:6 * d]
    out_ref[0] = _ln(DEEPNORM_ALPHA * x1_ref[0] + g2 * acc_scr[...]) * lnw_ref[...] + lnb_ref[...]


def _combine_call(ws, nx, yw, ye, pos_b, x1, mod3, ln2w, ln2b):
    bsz, nblk, n_e, win, d = yw.shape
    tb = pos_b.shape[3]
    grid_spec = pltpu.PrefetchScalarGridSpec(
        num_scalar_prefetch=2,
        grid=(bsz, nblk),
        in_specs=[
            pl.BlockSpec((1, 1, n_e, win, d), lambda b, k, ws, nx: (b, k, 0, 0, 0)),
            pl.BlockSpec(memory_space=pl.ANY),
            pl.BlockSpec((1, 1, n_e, tb), lambda b, k, ws, nx: (b, k, 0, 0)),
            pl.BlockSpec((1, tb, d), lambda b, k, ws, nx: (b, k, 0)),
            pl.BlockSpec((1, 1, 6 * d), lambda b, k, ws, nx: (b, 0, 0)),
            pl.BlockSpec((1, d), lambda b, k, ws, nx: (0, 0)),
            pl.BlockSpec((1, d), lambda b, k, ws, nx: (0, 0)),
        ],
        out_specs=pl.BlockSpec((1, tb, d), lambda b, k, ws, nx: (b, k, 0)),
        scratch_shapes=[pltpu.VMEM((tb, d), F32), pltpu.VMEM((win, d), BF16)],
    )
    return pl.pallas_call(
        _combine_kernel,
        out_shape=jax.ShapeDtypeStruct((bsz, nblk * tb, d), F32),
        grid_spec=grid_spec,
        compiler_params=_cparams(("parallel", "arbitrary")),
        name="combine",
    )(ws, nx, yw, ye, pos_b, x1, mod3, ln2w, ln2b)


def _rope_tables(n_tokens):
    rows = n_tokens // GRID_W
    n_freq = RET_QK_DIM // 4
    inv = ROPE_THETA ** (-jnp.arange(n_freq, dtype=F32) / n_freq)
    ang_r = jnp.arange(rows, dtype=F32)[:, None] * inv
    ang_c = jnp.arange(GRID_W, dtype=F32)[:, None] * inv
    zr = jnp.zeros((rows, 2 * n_freq), F32)
    zc = jnp.zeros((GRID_W, 2 * n_freq), F32)
    cr, sr, cc, sc = jnp.cos(ang_r), jnp.sin(ang_r), jnp.cos(ang_c), jnp.sin(ang_c)
    cos_t = (jnp.concatenate([cr, cr, zr], axis=-1)[:, None, :]
             + jnp.concatenate([zc, cc, cc], axis=-1)[None, :, :]).reshape(n_tokens, RET_QK_DIM)
    sin_t = (jnp.concatenate([-sr, sr, zr], axis=-1)[:, None, :]
             + jnp.concatenate([zc, -sc, sc], axis=-1)[None, :, :]).reshape(n_tokens, RET_QK_DIM)
    return jnp.concatenate([cos_t, sin_t], axis=-1)


def kernel(x, c, ctx, c_ctx, w_ada, b_ada, w_in, b_in, ret_decay_f, ret_decay_b, ret_gn_w, ret_gn_b,
           sgu_ln_w, sgu_ln_b, sgu_w, sgu_b, w_pa, w_pb, w_o, b_o, ln1_w, ln1_b,
           w_router, b_router, w_gate, w_up, w_down, ln2_w, ln2_b):
    bsz, t, d = x.shape
    assert d == D_MODEL and w_ada.shape[0] == DEPTH == 1 and bsz < MOD_ROWS
    cap = EC_CAPACITY_FACTOR * t // N_EXPERTS
    for win in (WIN_DISPATCH, WIN_COMBINE):
        assert cap >= win and (cap - win) % WIN_ALIGN == 0 and win % WIN_ALIGN == 0
    assert t % TB_ROUTE == 0
    assert N_EXPERTS % DISPATCH_GROUPS == 0
    l = 0

    cvec = jnp.zeros((MOD_ROWS, d), F32).at[:bsz].set(c).at[bsz].set(c_ctx)
    mod = _mod_call(cvec.T, w_ada[l], b_ada[l][None], bsz + 1)
    mod3 = mod[:, None, :]

    w_in_b = w_in[l].astype(BF16)
    b_in2 = b_in[l][None]
    lo_kv, hi_kv = RET_QK_W, 2 * RET_QK_W + RET_V_W
    kctx, vctx = _ctx_kv_call(ctx, mod3, w_in_b[:, lo_kv:hi_kv], b_in2[:, lo_kv:hi_kv], bsz)

    z = _inproj_call(x, mod3, w_in_b, b_in2, _rope_tables(t), sgu_ln_w[l][None], sgu_ln_b[l][None])

    lg = jnp.stack([jax.nn.log_sigmoid(ret_decay_f[l].astype(F32)),
                    jax.nn.log_sigmoid(ret_decay_b[l].astype(F32))])
    states = _state_call(lg, z, kctx, vctx)

    x1, hm, logits_t = _mixer_call(
        lg, z, states, x, mod3,
        w_pa[l].astype(BF16), w_pb[l].astype(BF16), w_o[l].astype(BF16),
        sgu_w[l].astype(BF16), sgu_b[l].T, ret_gn_w[l][None], ret_gn_b[l][None], b_o[l][None],
        ln1_w[l][None], ln1_b[l][None], w_router[l].T.astype(BF16), b_router[l][:, None])

    pos_b, gate_b, cnt_b = _route_call(logits_t, cap)

    c0 = cnt_b[..., 0]
    c1 = jnp.concatenate([c0[:, 1:], jnp.full_like(c0[:, :1], cap)], axis=1)

    def windows(win):
        ws = jnp.minimum((c0 // WIN_ALIGN) * WIN_ALIGN, cap - win)
        return ws.reshape(-1), (c1 > ws + win).astype(I32).reshape(-1)

    ws_d, nx_d = windows(WIN_DISPATCH)
    ws_c, nx_c = windows(WIN_COMBINE)

    xs, gc = _gather_call(ws_d, nx_d, hm, pos_b, gate_b, cap)
    ye, yw = _ffn_call(ws_c, xs, gc, w_gate[l], w_up[l], w_down[l], t // TB_ROUTE)
    return _combine_call(ws_c, nx_c, yw, ye, pos_b, x1, mod3, ln2_w[l][None], ln2_b[l][None])
```

```python
import functools
import math

import jax
import jax.numpy as jnp
from jax import lax
from jax.experimental import pallas as pl
from jax.experimental.pallas import tpu as pltpu

F32 = jnp.float32
BF16 = jnp.bfloat16
I32 = jnp.int32

D_MODEL = 1024
DEPTH = 1
GRID_W = 64
RET_HEADS = 4
RET_QK_DIM = 128
RET_V_DIM = 256
RET_CHUNK = 128
ROPE_THETA = 10000.0
RET_QK_W = RET_HEADS * RET_QK_DIM
RET_V_W = RET_HEADS * RET_V_DIM
SGU_GROUPS = 4
SGU_CHUNK = 128
SGU_WIDTH = 1024
SGU_GW = SGU_WIDTH // SGU_GROUPS
N_EXPERTS = 16
EC_CAPACITY_FACTOR = 2
D_EXPERT = 1024
LN_EPS = 1e-6
DEEPNORM_ALPHA = (2.0 * DEPTH) ** 0.25
D_IN = 2 * RET_QK_W + 2 * RET_V_W + 2 * SGU_WIDTH + 2 * D_MODEL
Z_V = 2 * RET_QK_W
Z_SG = Z_V + RET_V_W
Z_U = Z_SG + RET_V_W
Z_VS = Z_U + SGU_WIDTH
Z_GA = Z_VS + SGU_WIDTH
Z_GB = Z_GA + D_MODEL
Z_BLOCK = 3 * 1024

V7X_VMEM_BYTES = 64 * 1024 * 1024
VMEM_LIMIT = 56 * 1024 * 1024
LANES = 128

MOD_ROWS = 8
TM_INPROJ = 512
INPROJ_CHUNK = 256
TT_MIXER = 512
MERGE_ROWS = 512
TB_ROUTE = 512
WIN_DISPATCH = 128
WIN_COMBINE = 96
WIN_ALIGN = 16
DISPATCH_GROUPS = 2
DISPATCH_STEP_BLOCKS = 4
STATE_UNROLL = 8
FFN_ROWS = 256


def _cparams(sem):
    return pltpu.CompilerParams(dimension_semantics=sem, vmem_limit_bytes=VMEM_LIMIT)


def _ln(x):
    mu = jnp.mean(x, axis=-1, keepdims=True)
    xc = x - mu
    var = jnp.mean(xc * xc, axis=-1, keepdims=True)
    return xc * lax.rsqrt(var + LN_EPS)


_GELU_A = -2.0 * math.sqrt(2.0 / math.pi) * math.log2(math.e)
_GELU_B = _GELU_A * 0.044715


def _gelu(x):
    return x / (1.0 + jnp.exp2(x * (_GELU_A + _GELU_B * (x * x))))


def _iota_f(shape, dim):
    return lax.broadcasted_iota(I32, shape, dim).astype(F32)


def _mod_kernel(ct_ref, w_ref, b_ref, o_ref, *, n_rows):
    c = ct_ref[...]
    a = c * jax.nn.sigmoid(c)
    w = w_ref[...]
    rows = [jnp.sum(w * a[:, m:m + 1], axis=0, keepdims=True) + b_ref[...] for m in range(n_rows)]
    rows += [jnp.zeros_like(rows[0])] * (MOD_ROWS - n_rows)
    o_ref[...] = jnp.concatenate(rows, axis=0)


def _mod_call(cvec_t, w_ada, b_ada, n_rows):
    d, n = w_ada.shape
    tn = 1024
    return pl.pallas_call(
        functools.partial(_mod_kernel, n_rows=n_rows),
        out_shape=jax.ShapeDtypeStruct((MOD_ROWS, n), F32),
        grid=(n // tn,),
        in_specs=[
            pl.BlockSpec((d, MOD_ROWS), lambda j: (0, 0)),
            pl.BlockSpec((d, tn), lambda j: (0, j)),
            pl.BlockSpec((1, tn), lambda j: (0, j)),
        ],
        out_specs=pl.BlockSpec((MOD_ROWS, tn), lambda j: (0, j)),
        compiler_params=_cparams(("parallel",)),
        name="mod",
    )(cvec_t, w_ada, b_ada)


def _ctx_kv_kernel(x_ref, mod_ref, w_ref, b_ref, k_ref, v_ref):
    d = D_MODEL
    sh = mod_ref[0, :, 0:d]
    sc = mod_ref[0, :, d:2 * d]
    h = (_ln(x_ref[0]) * (1.0 + sc) + sh).astype(BF16)
    z = jnp.dot(h, w_ref[...], preferred_element_type=F32) + b_ref[...]
    k_ref[0] = (z[:, :RET_QK_W] * (RET_QK_DIM ** -0.5)).astype(BF16)
    v_ref[0] = z[:, RET_QK_W:].astype(BF16)


def _ctx_kv_call(ctx, mod3, w_kv, b_kv, ctx_row):
    bsz, n_ctx, d = ctx.shape
    wkv = w_kv.shape[1]
    return pl.pallas_call(
        _ctx_kv_kernel,
        out_shape=(jax.ShapeDtypeStruct((bsz, n_ctx, RET_QK_W), BF16),
                   jax.ShapeDtypeStruct((bsz, n_ctx, RET_V_W), BF16)),
        grid=(bsz,),
        in_specs=[
            pl.BlockSpec((1, n_ctx, d), lambda b: (b, 0, 0)),
            pl.BlockSpec((1, 1, 6 * d), lambda b: (ctx_row, 0, 0)),
            pl.BlockSpec((d, wkv), lambda b: (0, 0)),
            pl.BlockSpec((1, wkv), lambda b: (0, 0)),
        ],
        out_specs=(pl.BlockSpec((1, n_ctx, RET_QK_W), lambda b: (b, 0, 0)),
                   pl.BlockSpec((1, n_ctx, RET_V_W), lambda b: (b, 0, 0))),
        compiler_params=_cparams(("parallel",)),
        name="ctx_kv",
    )(ctx, mod3, w_kv, b_kv)


def _inproj_kernel(x_ref, mod_ref, w_ref, b_ref, rope_ref, lnw_ref, lnb_ref, z_ref):
    d = D_MODEL
    sh = mod_ref[0, :, 0:d]
    sc = mod_ref[0, :, d:2 * d]
    h = (_ln(x_ref[0]) * (1.0 + sc) + sh).astype(BF16)

    def proj(c0, width):
        return jnp.dot(h, w_ref[:, c0:c0 + width], preferred_element_type=F32) + b_ref[:, c0:c0 + width]

    zq = proj(0, 2 * RET_QK_W)
    cos = rope_ref[:, 0:LANES]
    sin = rope_ref[:, LANES:2 * LANES]
    tm = zq.shape[0]
    upper = (lax.broadcasted_iota(I32, (tm, LANES), 1) & 32) != 0
    for hb in range(2 * RET_HEADS):
        zs = zq[:, hb * LANES:(hb + 1) * LANES]
        if hb >= RET_HEADS:
            zs = zs * (RET_QK_DIM ** -0.5)
        sw = jnp.where(upper, pltpu.roll(zs, 32, 1), pltpu.roll(zs, LANES - 32, 1))
        z_ref[0, :, hb * LANES:(hb + 1) * LANES] = (zs * cos + sw * sin).astype(BF16)

    def emit(c0, width, fn):
        for cc in range(c0, c0 + width, INPROJ_CHUNK):
            z_ref[0, :, cc:cc + INPROJ_CHUNK] = fn(proj(cc, INPROJ_CHUNK)).astype(BF16)

    vs = _ln(_gelu(proj(Z_VS, SGU_WIDTH))) * lnw_ref[...] + lnb_ref[...]
    z_ref[0, :, Z_VS:Z_VS + SGU_WIDTH] = vs.astype(BF16)
    emit(Z_U, SGU_WIDTH, _gelu)
    emit(Z_SG, RET_V_W, jax.nn.silu)
    emit(Z_GA, d, jax.nn.sigmoid)
    emit(Z_GB, d, jax.nn.sigmoid)
    emit(Z_V, RET_V_W, lambda z: z)


def _inproj_call(x, mod3, w_in, b_in, rope_t, sgu_ln_w, sgu_ln_b):
    bsz, t, d = x.shape
    tm = TM_INPROJ
    row = lambda b, i: (b, i, 0)
    const2 = lambda b, i: (0, 0)
    return pl.pallas_call(
        _inproj_kernel,
        out_shape=jax.ShapeDtypeStruct((bsz, t, D_IN), BF16),
        grid=(bsz, t // tm),
        in_specs=[
            pl.BlockSpec((1, tm, d), row),
            pl.BlockSpec((1, 1, 6 * d), lambda b, i: (b, 0, 0)),
            pl.BlockSpec((d, D_IN), const2, pipeline_mode=pl.Buffered(1)),
            pl.BlockSpec((1, D_IN), const2),
            pl.BlockSpec((tm, 2 * LANES), lambda b, i: (i, 0)),
            pl.BlockSpec((1, SGU_WIDTH), const2),
            pl.BlockSpec((1, SGU_WIDTH), const2),
        ],
        out_specs=pl.BlockSpec((1, tm, D_IN), row),
        compiler_params=_cparams(("parallel", "parallel")),
        name="inproj",
    )(x, mod3, w_in, b_in, rope_t, sgu_ln_w, sgu_ln_b)


def _dot_t0(a, b):
    return lax.dot_general(a, b, (((0,), (0,)), ((), ())), preferred_element_type=F32)


def _state_kernel(lg_ref, k_ref, v_ref, kc_ref, vc_ref, s_ref):
    hd = pl.program_id(1)
    lgf = lg_ref[0, hd]
    lgb = lg_ref[1, hd]
    n_chunks = s_ref.shape[2]
    n_ctx = kc_ref.shape[1]
    c = RET_CHUNK
    li = _iota_f((c, RET_QK_DIM), 0)
    zeta_f = jnp.exp((c - 1.0 - li) * lgf)
    zeta_b = jnp.exp(li * lgb)
    one = jnp.ones((1, 1), F32)
    cd_f = jnp.exp(one * (c * lgf))
    cd_b = jnp.exp(one * (c * lgb))

    tc = _iota_f((n_ctx, RET_QK_DIM), 0)
    kc = kc_ref[0].astype(F32)
    vc = vc_ref[0]
    s0f = _dot_t0((kc * jnp.exp((n_ctx - 1.0 - tc) * lgf)).astype(BF16), vc)
    s0b = _dot_t0((kc * jnp.exp(tc * lgb)).astype(BF16), vc)

    def chunk_kv(n, zeta):
        off = pl.multiple_of(n * c, c)
        kk = k_ref[0, pl.ds(off, c), :].astype(F32)
        return _dot_t0((kk * zeta).astype(BF16), v_ref[0, pl.ds(off, c), :])

    def step(i, carry):
        sf, sb = carry
        nb = n_chunks - 1 - i
        s_ref[0, 0, i, 0:RET_QK_DIM, :] = sf.astype(BF16)
        s_ref[0, 0, nb, RET_QK_DIM:2 * RET_QK_DIM, :] = sb.astype(BF16)
        return cd_f * sf + chunk_kv(i, zeta_f), cd_b * sb + chunk_kv(nb, zeta_b)

    lax.fori_loop(0, n_chunks, step, (s0f, s0b), unroll=STATE_UNROLL)


def _state_call(lg, z, kctx, vctx):
    bsz, t, _ = z.shape
    n_ctx = kctx.shape[1]
    n_chunks = t // RET_CHUNK
    return pl.pallas_call(
        _state_kernel,
        out_shape=jax.ShapeDtypeStruct((bsz, RET_HEADS, n_chunks, 2 * RET_QK_DIM, RET_V_DIM), BF16),
        grid=(bsz, RET_HEADS),
        in_specs=[
            pl.BlockSpec(memory_space=pltpu.SMEM),
            pl.BlockSpec((1, t, RET_QK_DIM), lambda b, h: (b, 0, RET_QK_W // RET_QK_DIM + h)),
            pl.BlockSpec((1, t, RET_V_DIM), lambda b, h: (b, 0, Z_V // RET_V_DIM + h)),
            pl.BlockSpec((1, n_ctx, RET_QK_DIM), lambda b, h: (b, 0, h)),
            pl.BlockSpec((1, n_ctx, RET_V_DIM), lambda b, h: (b, 0, h)),
        ],
        out_specs=pl.BlockSpec((1, 1, n_chunks, 2 * RET_QK_DIM, RET_V_DIM), lambda b, h: (b, h, 0, 0, 0)),
        compiler_params=_cparams(("parallel", "parallel")),
        name="states",
    )(lg, z, z, kctx, vctx)


def _mixer_kernel(lg_ref, za_ref, zb_ref, gb_ref, s_ref, x_ref, mod_ref,
                  wpa_ref, wpb_ref, wo_ref, sguw_ref, sgub_ref, gnw_ref, gnb_ref, bo_ref,
                  ln1w_ref, ln1b_ref, wr_ref, br_ref,
                  x1_ref, hm_ref, lgt_ref, ret_scr, sgu_scr):
    d = D_MODEL
    c = RET_CHUNK

    n_sub = za_ref.shape[1] // c
    row = lax.broadcasted_iota(I32, (c, c), 0)
    col = lax.broadcasted_iota(I32, (c, c), 1)
    diff = (row - col).astype(F32)
    rowq = _iota_f((c, RET_QK_DIM), 0)
    for hd in range(RET_HEADS):
        lgf = lg_ref[0, hd]
        lgb = lg_ref[1, hd]
        vcols = slice(hd * RET_V_DIM, (hd + 1) * RET_V_DIM)
        mask = (jnp.where(diff >= 0, jnp.exp(jnp.maximum(diff, 0.0) * lgf), 0.0)
                + jnp.where(diff <= 0, jnp.exp(jnp.maximum(-diff, 0.0) * lgb), 0.0))
        xi_f = jnp.exp((rowq + 1.0) * lgf)
        xi_b = jnp.exp((c - rowq) * lgb)
        for ci in range(n_sub):
            r0 = ci * c
            q = za_ref[0, r0:r0 + c, hd * RET_QK_DIM:(hd + 1) * RET_QK_DIM]
            k = za_ref[0, r0:r0 + c, RET_QK_W + hd * RET_QK_DIM:RET_QK_W + (hd + 1) * RET_QK_DIM]
            s = lax.dot_general(q, k, (((1,), (1,)), ((), ())), preferred_element_type=F32)
            vv = za_ref[0, r0:r0 + c, Z_V + hd * RET_V_DIM:Z_V + (hd + 1) * RET_V_DIM]
            intra = jnp.dot((s * mask).astype(BF16), vv, preferred_element_type=F32)
            qf = q.astype(F32)
            qx = jnp.concatenate([(qf * xi_f).astype(BF16), (qf * xi_b).astype(BF16)], axis=1)
            cross = jnp.dot(qx, s_ref[0, hd, ci], preferred_element_type=F32)
            o = _ln(intra + cross) * gnw_ref[:, vcols] + gnb_ref[:, vcols]
            gate = za_ref[0, r0:r0 + c, Z_SG + hd * RET_V_DIM:Z_SG + (hd + 1) * RET_V_DIM]
            ret_scr[r0:r0 + c, vcols] = (o * gate.astype(F32)).astype(BF16)

    for g in range(SGU_GROUPS):
        gcols = slice(g * SGU_GW, (g + 1) * SGU_GW)
        wg = sguw_ref[g]
        bg = sgub_ref[:, g:g + 1]
        for ci in range(n_sub):
            r0 = ci * c
            vsb = zb_ref[0, r0:r0 + c, Z_VS - Z_BLOCK + g * SGU_GW:Z_VS - Z_BLOCK + (g + 1) * SGU_GW]
            ub = zb_ref[0, r0:r0 + c, Z_U - Z_BLOCK + g * SGU_GW:Z_U - Z_BLOCK + (g + 1) * SGU_GW]
            sp = jnp.dot(wg, vsb, preferred_element_type=F32) + bg
            sgu_scr[r0:r0 + c, gcols] = (ub.astype(F32) * sp).astype(BF16)

    g1 = mod_ref[0, :, 2 * d:3 * d]
    sh2 = mod_ref[0, :, 3 * d:4 * d]
    sc2 = mod_ref[0, :, 4 * d:5 * d]
    for r0 in range(0, za_ref.shape[1], MERGE_ROWS):
        rows = slice(r0, r0 + MERGE_ROWS)
        pa = jnp.dot(ret_scr[rows, :], wpa_ref[...], preferred_element_type=F32)
        pb = jnp.dot(sgu_scr[rows, :], wpb_ref[...], preferred_element_type=F32)
        ga = zb_ref[0, rows, Z_GA - Z_BLOCK:Z_GA - Z_BLOCK + d]
        y = (ga.astype(F32) * pa + gb_ref[0, rows, :].astype(F32) * pb).astype(BF16)
        mix = jnp.dot(y, wo_ref[...], preferred_element_type=F32) + bo_ref[...]
        x1 = _ln(DEEPNORM_ALPHA * x_ref[0, rows, :] + g1 * mix) * ln1w_ref[...] + ln1b_ref[...]
        x1_ref[0, rows, :] = x1
        hm = (_ln(x1) * (1.0 + sc2) + sh2).astype(BF16)
        hm_ref[0, rows, :] = hm
        lgt_ref[0, :, rows] = lax.dot_general(wr_ref[...], hm, (((1,), (1,)), ((), ())),
                                              preferred_element_type=F32) + br_ref[...]


def _mixer_call(lg, z, states, x, mod3, wpa, wpb, wo, sguw, sgub_t, gnw, gnb, bo,
                ln1w, ln1b, wr_t, br):
    bsz, t, d = x.shape
    tt = TT_MIXER
    n_sub = tt // RET_CHUNK
    row = lambda b, i: (b, i, 0)
    c2 = lambda b, i: (0, 0)
    c3 = lambda b, i: (0, 0, 0)
    return pl.pallas_call(
        _mixer_kernel,
        out_shape=(jax.ShapeDtypeStruct((bsz, t, d), F32),
                   jax.ShapeDtypeStruct((bsz, t, d), BF16),
                   jax.ShapeDtypeStruct((bsz, N_EXPERTS, t), F32)),
        grid=(bsz, t // tt),
        in_specs=[
            pl.BlockSpec(memory_space=pltpu.SMEM),
            pl.BlockSpec((1, tt, Z_BLOCK), lambda b, i: (b, i, 0)),
            pl.BlockSpec((1, tt, Z_BLOCK), lambda b, i: (b, i, 1)),
            pl.BlockSpec((1, tt, d), lambda b, i: (b, i, Z_GB // d)),
            pl.BlockSpec((1, RET_HEADS, n_sub, 2 * RET_QK_DIM, RET_V_DIM), lambda b, i: (b, 0, i, 0, 0)),
            pl.BlockSpec((1, tt, d), row),
            pl.BlockSpec((1, 1, 6 * d), lambda b, i: (b, 0, 0)),
            pl.BlockSpec((RET_V_W, d), c2, pipeline_mode=pl.Buffered(1)),
            pl.BlockSpec((SGU_WIDTH, d), c2, pipeline_mode=pl.Buffered(1)),
            pl.BlockSpec((d, d), c2, pipeline_mode=pl.Buffered(1)),
            pl.BlockSpec((SGU_GROUPS, SGU_CHUNK, SGU_CHUNK), c3),
            pl.BlockSpec((SGU_CHUNK, SGU_GROUPS), c2),
            pl.BlockSpec((1, RET_V_W), c2),
            pl.BlockSpec((1, RET_V_W), c2),
            pl.BlockSpec((1, d), c2),
            pl.BlockSpec((1, d), c2),
            pl.BlockSpec((1, d), c2),
            pl.BlockSpec((N_EXPERTS, d), c2),
            pl.BlockSpec((N_EXPERTS, 1), c2),
        ],
        out_specs=(pl.BlockSpec((1, tt, d), row),
                   pl.BlockSpec((1, tt, d), row),
                   pl.BlockSpec((1, N_EXPERTS, tt), lambda b, i: (b, 0, i))),
        scratch_shapes=[pltpu.VMEM((tt, RET_V_W), BF16), pltpu.VMEM((tt, SGU_WIDTH), BF16)],
        compiler_params=_cparams(("parallel", "parallel")),
        name="mixer",
    )(lg, z, z, z, states, x, mod3, wpa, wpb, wo, sguw, sgub_t, gnw, gnb, bo,
      ln1w, ln1b, wr_t, br)


def _route_kernel(lgt_ref, pos_ref, gate_ref, cnt_ref, *, cap):
    lg = lgt_ref[0]
    n_e, t = lg.shape
    tb = TB_ROUTE
    m = jnp.max(lg, axis=0, keepdims=True)
    ex = jnp.exp(lg - m)
    aff = ex / jnp.sum(ex, axis=0, keepdims=True)

    def search(i, thr_bits):
        cand = thr_bits | lax.shift_left(jnp.int32(1), 30 - i)
        cnt = jnp.sum((aff >= lax.bitcast_convert_type(cand, F32)).astype(I32), axis=1, keepdims=True)
        return jnp.where(cnt >= cap, cand, thr_bits)

    thr_bits = lax.fori_loop(0, 31, search, jnp.zeros((n_e, 1), I32))
    floor_f = lax.bitcast_convert_type(thr_bits, F32)
    thr = jnp.min(jnp.where(aff >= floor_f, aff, jnp.inf), axis=1, keepdims=True)
    need = (cap - jnp.sum((aff > thr).astype(I32), axis=1, keepdims=True)).astype(F32)

    r = lax.broadcasted_iota(I32, (tb, tb), 0)
    cc = lax.broadcasted_iota(I32, (tb, tb), 1)
    tri = (r <= cc).astype(BF16)
    carry_eq = jnp.zeros((n_e, 1), F32)
    carry_sel = jnp.zeros((n_e, 1), F32)
    for blk in range(t // tb):
        sl = slice(blk * tb, (blk + 1) * tb)
        aff_b = aff[:, sl]
        eq = aff_b == thr
        eq_b = eq.astype(BF16)
        inc_eq = jnp.dot(eq_b, tri, preferred_element_type=F32)
        before = carry_eq + inc_eq - eq_b.astype(F32)
        sel = (aff_b > thr) | (eq & (before < need))
        sel_b = sel.astype(BF16)
        inc_sel = jnp.dot(sel_b, tri, preferred_element_type=F32)
        pos = carry_sel + inc_sel - 1.0
        pos_ref[0, blk] = jnp.where(sel, pos.astype(I32), -1)
        gate_ref[0, blk] = aff_b
        cnt_ref[0, blk] = jnp.broadcast_to(carry_sel, (n_e, LANES)).astype(I32)
        carry_eq = carry_eq + inc_eq[:, tb - 1:tb]
        carry_sel = carry_sel + inc_sel[:, tb - 1:tb]


def _route_call(logits_t, cap):
    bsz, n_e, t = logits_t.shape
    nblk = t // TB_ROUTE
    return pl.pallas_call(
        functools.partial(_route_kernel, cap=cap),
        out_shape=(jax.ShapeDtypeStruct((bsz, nblk, n_e, TB_ROUTE), I32),
                   jax.ShapeDtypeStruct((bsz, nblk, n_e, TB_ROUTE), F32),
                   jax.ShapeDtypeStruct((bsz, nblk, n_e, LANES), I32)),
        grid=(bsz,),
        in_specs=[pl.BlockSpec((1, n_e, t), lambda b: (b, 0, 0))],
        out_specs=(pl.BlockSpec((1, nblk, n_e, TB_ROUTE), lambda b: (b, 0, 0, 0)),
                   pl.BlockSpec((1, nblk, n_e, TB_ROUTE), lambda b: (b, 0, 0, 0)),
                   pl.BlockSpec((1, nblk, n_e, LANES), lambda b: (b, 0, 0, 0))),
        compiler_params=_cparams(("parallel",)),
        name="route",
    )(logits_t)


def _window_rows(ws, win):
    return lax.broadcasted_iota(I32, (win, TB_ROUTE), 0) + ws


def _sweep_window(w, cap, win, prow, rest):
    off = pl.multiple_of(jnp.minimum(w * win, cap - win), WIN_ALIGN)
    rows = _window_rows(off, win)
    return off, (rows == prow) & rest & (rows >= w * win)


def _gather_kernel(ws_ref, nx_ref, hm_ref, pos_ref, gate_ref, xs_ref, gc_ref):
    b = pl.program_id(0)
    g = pl.program_id(1)
    step = pl.program_id(2)
    n_sub, eg, tb = pos_ref.shape[1:]
    n_e = eg * pl.num_programs(1)
    nblk = n_sub * pl.num_programs(2)
    cap = xs_ref.shape[2]
    win = WIN_DISPATCH

    @pl.when(step == 0)
    def _():
        xs_ref[...] = jnp.zeros_like(xs_ref)
        gc_ref[...] = jnp.zeros_like(gc_ref)

    def base(sub):
        return (b * nblk + step * n_sub + sub) * n_e + g * eg

    def tokens(sub):
        return hm_ref[0, sub * tb:(sub + 1) * tb, :]

    for sub in range(n_sub):
        for el in range(eg):
            ws = pl.multiple_of(ws_ref[base(sub) + el], WIN_ALIGN)
            match = _window_rows(ws, win) == pos_ref[0, sub, el:el + 1, :]
            sl = pl.ds(ws, win)
            xs_ref[0, el, sl, :] += jnp.dot(match.astype(BF16), tokens(sub),
                                            preferred_element_type=F32).astype(BF16)
            gc_ref[0, el, sl, :] += jnp.sum(jnp.where(match, gate_ref[0, sub, el:el + 1, :], 0.0),
                                             axis=1, keepdims=True)

    for sub in range(n_sub):
        for el in range(eg):
            @pl.when(nx_ref[base(sub) + el] > 0)
            def _(sub=sub, el=el):
                prow = pos_ref[0, sub, el:el + 1, :]
                grow = gate_ref[0, sub, el:el + 1, :]
                rest = prow >= ws_ref[base(sub) + el] + win

                def body(w, carry):
                    off, m = _sweep_window(w, cap, win, prow, rest)
                    xs_ref[0, el, pl.ds(off, win), :] += jnp.dot(m.astype(BF16), tokens(sub),
                                                                  preferred_element_type=F32).astype(BF16)
                    gc_ref[0, el, pl.ds(off, win), :] += jnp.sum(jnp.where(m, grow, 0.0), axis=1, keepdims=True)
                    return carry

                lax.fori_loop(0, pl.cdiv(cap, win), body, 0)


def _gather_call(ws, nx, hm, pos_b, gate_b, cap):
    bsz, t, d = hm.shape
    nblk, n_e, tb = pos_b.shape[1:]
    eg = n_e // DISPATCH_GROUPS
    n_sub = DISPATCH_STEP_BLOCKS
    grid_spec = pltpu.PrefetchScalarGridSpec(
        num_scalar_prefetch=2,
        grid=(bsz, DISPATCH_GROUPS, nblk // n_sub),
        in_specs=[
            pl.BlockSpec((1, n_sub * tb, d), lambda b, g, k, ws, nx: (b, k, 0)),
            pl.BlockSpec((1, n_sub, eg, tb), lambda b, g, k, ws, nx: (b, k, g, 0)),
            pl.BlockSpec((1, n_sub, eg, tb), lambda b, g, k, ws, nx: (b, k, g, 0)),
        ],
        out_specs=(pl.BlockSpec((1, eg, cap, d), lambda b, g, k, ws, nx: (b, g, 0, 0)),
                   pl.BlockSpec((1, eg, cap, 1), lambda b, g, k, ws, nx: (b, g, 0, 0))),
    )
    return pl.pallas_call(
        _gather_kernel,
        out_shape=(jax.ShapeDtypeStruct((bsz, n_e, cap, d), BF16),
                   jax.ShapeDtypeStruct((bsz, n_e, cap, 1), F32)),
        grid_spec=grid_spec,
        compiler_params=_cparams(("parallel", "parallel", "arbitrary")),
        name="gather",
    )(ws, nx, hm, pos_b, gate_b)


def _ffn_kernel(xs_ref, gc_ref, wg_ref, wu_ref, wd_ref, ye_ref, wgu_s, wd_s):
    f = wd_s.shape[0]
    wgu_s[:, 0:f] = wg_ref[0].astype(BF16)
    wgu_s[:, f:2 * f] = wu_ref[0].astype(BF16)
    wd_s[...] = wd_ref[0].astype(BF16)

    rt = FFN_ROWS
    for b in range(xs_ref.shape[0]):
        for j in range(xs_ref.shape[2] // rt):
            sl = slice(j * rt, (j + 1) * rt)
            xgu = jnp.dot(xs_ref[b, 0, sl, :], wgu_s[...], preferred_element_type=F32)
            hid = (jax.nn.silu(xgu[:, 0:f]) * xgu[:, f:2 * f]).astype(BF16)
            ye = jnp.dot(hid, wd_s[...], preferred_element_type=F32) * gc_ref[b, 0, sl, :]
            ye_ref[b, 0, sl, :] = ye.astype(BF16)


def _ffn_call(xs, gc, w_gate, w_up, w_down):
    bsz, n_e, cap, d = xs.shape
    f = w_gate.shape[2]
    return pl.pallas_call(
        _ffn_kernel,
        out_shape=jax.ShapeDtypeStruct((bsz, n_e, cap, d), BF16),
        grid=(n_e,),
        in_specs=[
            pl.BlockSpec((bsz, 1, cap, d), lambda e: (0, e, 0, 0)),
            pl.BlockSpec((bsz, 1, cap, 1), lambda e: (0, e, 0, 0)),
            pl.BlockSpec((1, d, f), lambda e: (e, 0, 0)),
            pl.BlockSpec((1, d, f), lambda e: (e, 0, 0)),
            pl.BlockSpec((1, f, d), lambda e: (e, 0, 0)),
        ],
        out_specs=pl.BlockSpec((bsz, 1, cap, d), lambda e: (0, e, 0, 0)),
        scratch_shapes=[pltpu.VMEM((d, 2 * f), BF16), pltpu.VMEM((f, d), BF16)],
        compiler_params=_cparams(("parallel",)),
        name="ffn",
    )(xs, gc, w_gate, w_up, w_down)


def _combine_kernel(ws_ref, nx_ref, ye_ref, pos_ref, x1_ref, mod_ref, lnw_ref, lnb_ref, out_ref, acc_scr):
    b = pl.program_id(0)
    kb = pl.program_id(1)
    n_e = pos_ref.shape[2]
    cap = ye_ref.shape[2]
    d = D_MODEL
    win = WIN_COMBINE
    base = (b * pl.num_programs(1) + kb) * n_e
    starts = [pl.multiple_of(ws_ref[base + e], WIN_ALIGN) for e in range(n_e)]
    onehot = jnp.concatenate(
        [(_window_rows(starts[e], win) == pos_ref[0, 0, e:e + 1, :]).astype(BF16) for e in range(n_e)], axis=0)
    rows = jnp.concatenate([ye_ref[0, e, pl.ds(starts[e], win), :] for e in range(n_e)], axis=0)
    acc_scr[...] = _dot_t0(onehot, rows)

    for e in range(n_e):
        @pl.when(nx_ref[base + e] > 0)
        def _(e=e):
            prow = pos_ref[0, 0, e:e + 1, :]
            rest = prow >= starts[e] + win

            def body(w, carry):
                off, m = _sweep_window(w, cap, win, prow, rest)
                acc_scr[...] += _dot_t0(m.astype(BF16), ye_ref[0, e, pl.ds(off, win), :])
                return carry

            lax.fori_loop(0, pl.cdiv(cap, win), body, 0)

    g2 = mod_ref[0, :, 5 * d:6 * d]
    out_ref[0] = _ln(DEEPNORM_ALPHA * x1_ref[0] + g2 * acc_scr[...]) * lnw_ref[...] + lnb_ref[...]


def _combine_call(ws, nx, ye, pos_b, x1, mod3, ln2w, ln2b):
    bsz, n_e, cap, d = ye.shape
    nblk, _, tb = pos_b.shape[1:]
    grid_spec = pltpu.PrefetchScalarGridSpec(
        num_scalar_prefetch=2,
        grid=(bsz, nblk),
        in_specs=[
            pl.BlockSpec((1, n_e, cap, d), lambda b, k, ws, nx: (b, 0, 0, 0), pipeline_mode=pl.Buffered(1)),
            pl.BlockSpec((1, 1, n_e, tb), lambda b, k, ws, nx: (b, k, 0, 0)),
            pl.BlockSpec((1, tb, d), lambda b, k, ws, nx: (b, k, 0)),
            pl.BlockSpec((1, 1, 6 * d), lambda b, k, ws, nx: (b, 0, 0)),
            pl.BlockSpec((1, d), lambda b, k, ws, nx: (0, 0)),
            pl.BlockSpec((1, d), lambda b, k, ws, nx: (0, 0)),
        ],
        out_specs=pl.BlockSpec((1, tb, d), lambda b, k, ws, nx: (b, k, 0)),
        scratch_shapes=[pltpu.VMEM((tb, d), F32)],
    )
    return pl.pallas_call(
        _combine_kernel,
        out_shape=jax.ShapeDtypeStruct((bsz, nblk * tb, d), F32),
        grid_spec=grid_spec,
        compiler_params=_cparams(("parallel", "arbitrary")),
        name="combine",
    )(ws, nx, ye, pos_b, x1, mod3, ln2w, ln2b)


def _rope_tables(n_tokens):
    rows = n_tokens // GRID_W
    n_freq = RET_QK_DIM // 4
    inv = ROPE_THETA ** (-jnp.arange(n_freq, dtype=F32) / n_freq)
    ang_r = jnp.arange(rows, dtype=F32)[:, None] * inv
    ang_c = jnp.arange(GRID_W, dtype=F32)[:, None] * inv
    zr = jnp.zeros((rows, 2 * n_freq), F32)
    zc = jnp.zeros((GRID_W, 2 * n_freq), F32)
    cr, sr, cc, sc = jnp.cos(ang_r), jnp.sin(ang_r), jnp.cos(ang_c), jnp.sin(ang_c)
    cos_t = (jnp.concatenate([cr, cr, zr], axis=-1)[:, None, :]
             + jnp.concatenate([zc, cc, cc], axis=-1)[None, :, :]).reshape(n_tokens, RET_QK_DIM)
    sin_t = (jnp.concatenate([-sr, sr, zr], axis=-1)[:, None, :]
             + jnp.concatenate([zc, -sc, sc], axis=-1)[None, :, :]).reshape(n_tokens, RET_QK_DIM)
    return jnp.concatenate([cos_t, sin_t], axis=-1)


def kernel(x, c, ctx, c_ctx, w_ada, b_ada, w_in, b_in, ret_decay_f, ret_decay_b, ret_gn_w, ret_gn_b,
           sgu_ln_w, sgu_ln_b, sgu_w, sgu_b, w_pa, w_pb, w_o, b_o, ln1_w, ln1_b,
           w_router, b_router, w_gate, w_up, w_down, ln2_w, ln2_b):
    bsz, t, d = x.shape
    assert d == D_MODEL and w_ada.shape[0] == DEPTH == 1 and bsz < MOD_ROWS
    cap = EC_CAPACITY_FACTOR * t // N_EXPERTS
    for win in (WIN_DISPATCH, WIN_COMBINE):
        assert cap >= win and (cap - win) % WIN_ALIGN == 0 and win % WIN_ALIGN == 0
    assert t % (TB_ROUTE * DISPATCH_STEP_BLOCKS) == 0
    assert N_EXPERTS % DISPATCH_GROUPS == 0
    l = 0

    cvec = jnp.zeros((MOD_ROWS, d), F32).at[:bsz].set(c).at[bsz].set(c_ctx)
    mod = _mod_call(cvec.T, w_ada[l], b_ada[l][None], bsz + 1)
    mod3 = mod[:, None, :]

    w_in_b = w_in[l].astype(BF16)
    b_in2 = b_in[l][None]
    lo_kv, hi_kv = RET_QK_W, 2 * RET_QK_W + RET_V_W
    kctx, vctx = _ctx_kv_call(ctx, mod3, w_in_b[:, lo_kv:hi_kv], b_in2[:, lo_kv:hi_kv], bsz)

    z = _inproj_call(x, mod3, w_in_b, b_in2, _rope_tables(t), sgu_ln_w[l][None], sgu_ln_b[l][None])

    lg = jnp.stack([jax.nn.log_sigmoid(ret_decay_f[l].astype(F32)),
                    jax.nn.log_sigmoid(ret_decay_b[l].astype(F32))])
    states = _state_call(lg, z, kctx, vctx)

    x1, hm, logits_t = _mixer_call(
        lg, z, states, x, mod3,
        w_pa[l].astype(BF16), w_pb[l].astype(BF16), w_o[l].astype(BF16),
        sgu_w[l].astype(BF16), sgu_b[l].T, ret_gn_w[l][None], ret_gn_b[l][None], b_o[l][None],
        ln1_w[l][None], ln1_b[l][None], w_router[l].T.astype(BF16), b_router[l][:, None])

    pos_b, gate_b, cnt_b = _route_call(logits_t, cap)

    c0 = cnt_b[..., 0]
    c1 = jnp.concatenate([c0[:, 1:], jnp.full_like(c0[:, :1], cap)], axis=1)

    def windows(win):
        ws = jnp.minimum((c0 // WIN_ALIGN) * WIN_ALIGN, cap - win)
        return ws.reshape(-1), (c1 > ws + win).astype(I32).reshape(-1)

    ws_d, nx_d = windows(WIN_DISPATCH)
    ws_c, nx_c = windows(WIN_COMBINE)

    xs, gc = _gather_call(ws_d, nx_d, hm, pos_b, gate_b, cap)
    ye = _ffn_call(xs, gc, w_gate[l], w_up[l], w_down[l])
    return _combine_call(ws_c, nx_c, ye, pos_b, x1, mod3, ln2_w[l][None], ln2_b[l][None])
```

```python
import functools
import math

import jax
import jax.numpy as jnp
from jax import lax
from jax.experimental import pallas as pl
from jax.experimental.pallas import tpu as pltpu

F32 = jnp.float32
BF16 = jnp.bfloat16
I32 = jnp.int32

D_MODEL = 1024
DEPTH = 1
GRID_W = 64
RET_HEADS = 4
RET_QK_DIM = 128
RET_V_DIM = 256
RET_CHUNK = 128
ROPE_THETA = 10000.0
RET_QK_W = RET_HEADS * RET_QK_DIM
RET_V_W = RET_HEADS * RET_V_DIM
SGU_GROUPS = 4
SGU_CHUNK = 128
SGU_WIDTH = 1024
SGU_GW = SGU_WIDTH // SGU_GROUPS
N_EXPERTS = 16
EC_CAPACITY_FACTOR = 2
D_EXPERT = 1024
LN_EPS = 1e-6
DEEPNORM_ALPHA = (2.0 * DEPTH) ** 0.25
D_IN = 2 * RET_QK_W + 2 * RET_V_W + 2 * SGU_WIDTH + 2 * D_MODEL
Z_V = 2 * RET_QK_W
Z_SG = Z_V + RET_V_W
Z_U = Z_SG + RET_V_W
Z_VS = Z_U + SGU_WIDTH
Z_GA = Z_VS + SGU_WIDTH
Z_GB = Z_GA + D_MODEL
Z_BLOCK = 3 * 1024

V7X_VMEM_BYTES = 64 * 1024 * 1024
VMEM_LIMIT = 60 * 1024 * 1024
LANES = 128

MOD_ROWS = 8
TM_INPROJ = 1024
INPROJ_CHUNK = 256
TT_MIXER = 512
MERGE_ROWS = 512
TB_ROUTE = 512
WIN_DISPATCH = 128
WIN_COMBINE = 96
WIN_ALIGN = 16
DISPATCH_GROUPS = 2
DISPATCH_STEP_BLOCKS = 4
COMBINE_STEP_BLOCKS = 2
STATE_UNROLL = 8
FFN_ROWS = 256


def _cparams(sem):
    return pltpu.CompilerParams(dimension_semantics=sem, vmem_limit_bytes=VMEM_LIMIT)


def _ln(x):
    mu = jnp.mean(x, axis=-1, keepdims=True)
    xc = x - mu
    var = jnp.mean(xc * xc, axis=-1, keepdims=True)
    return xc * lax.rsqrt(var + LN_EPS)


_GELU_A = -2.0 * math.sqrt(2.0 / math.pi) * math.log2(math.e)
_GELU_B = _GELU_A * 0.044715


def _gelu(x):
    return x / (1.0 + jnp.exp2(x * (_GELU_A + _GELU_B * (x * x))))


def _iota_f(shape, dim):
    return lax.broadcasted_iota(I32, shape, dim).astype(F32)


def _mod_kernel(ct_ref, w_ref, b_ref, o_ref, *, n_rows):
    c = ct_ref[...]
    a = c * jax.nn.sigmoid(c)
    w = w_ref[...]
    rows = [jnp.sum(w * a[:, m:m + 1], axis=0, keepdims=True) + b_ref[...] for m in range(n_rows)]
    rows += [jnp.zeros_like(rows[0])] * (MOD_ROWS - n_rows)
    o_ref[...] = jnp.concatenate(rows, axis=0)


def _mod_call(cvec_t, w_ada, b_ada, n_rows):
    d, n = w_ada.shape
    tn = 1024
    return pl.pallas_call(
        functools.partial(_mod_kernel, n_rows=n_rows),
        out_shape=jax.ShapeDtypeStruct((MOD_ROWS, n), F32),
        grid=(n // tn,),
        in_specs=[
            pl.BlockSpec((d, MOD_ROWS), lambda j: (0, 0)),
            pl.BlockSpec((d, tn), lambda j: (0, j)),
            pl.BlockSpec((1, tn), lambda j: (0, j)),
        ],
        out_specs=pl.BlockSpec((MOD_ROWS, tn), lambda j: (0, j)),
        compiler_params=_cparams(("parallel",)),
        name="mod",
    )(cvec_t, w_ada, b_ada)


def _ctx_kv_kernel(x_ref, mod_ref, w_ref, b_ref, k_ref, v_ref):
    d = D_MODEL
    sh = mod_ref[0, :, 0:d]
    sc = mod_ref[0, :, d:2 * d]
    h = (_ln(x_ref[0]) * (1.0 + sc) + sh).astype(BF16)
    z = jnp.dot(h, w_ref[...], preferred_element_type=F32) + b_ref[...]
    k_ref[0] = (z[:, :RET_QK_W] * (RET_QK_DIM ** -0.5)).astype(BF16)
    v_ref[0] = z[:, RET_QK_W:].astype(BF16)


def _ctx_kv_call(ctx, mod3, w_kv, b_kv, ctx_row):
    bsz, n_ctx, d = ctx.shape
    wkv = w_kv.shape[1]
    return pl.pallas_call(
        _ctx_kv_kernel,
        out_shape=(jax.ShapeDtypeStruct((bsz, n_ctx, RET_QK_W), BF16),
                   jax.ShapeDtypeStruct((bsz, n_ctx, RET_V_W), BF16)),
        grid=(bsz,),
        in_specs=[
            pl.BlockSpec((1, n_ctx, d), lambda b: (b, 0, 0)),
            pl.BlockSpec((1, 1, 6 * d), lambda b: (ctx_row, 0, 0)),
            pl.BlockSpec((d, wkv), lambda b: (0, 0)),
            pl.BlockSpec((1, wkv), lambda b: (0, 0)),
        ],
        out_specs=(pl.BlockSpec((1, n_ctx, RET_QK_W), lambda b: (b, 0, 0)),
                   pl.BlockSpec((1, n_ctx, RET_V_W), lambda b: (b, 0, 0))),
        compiler_params=_cparams(("parallel",)),
        name="ctx_kv",
    )(ctx, mod3, w_kv, b_kv)


def _inproj_kernel(x_ref, mod_ref, w_ref, b_ref, rope_ref, lnw_ref, lnb_ref, z_ref):
    d = D_MODEL
    sh = mod_ref[0, :, 0:d]
    sc = mod_ref[0, :, d:2 * d]
    h = (_ln(x_ref[0]) * (1.0 + sc) + sh).astype(BF16)

    def proj(c0, width):
        return jnp.dot(h, w_ref[:, c0:c0 + width], preferred_element_type=F32) + b_ref[:, c0:c0 + width]

    zq = proj(0, 2 * RET_QK_W)
    cos = rope_ref[:, 0:LANES]
    sin = rope_ref[:, LANES:2 * LANES]
    tm = zq.shape[0]
    upper = (lax.broadcasted_iota(I32, (tm, LANES), 1) & 32) != 0
    for hb in range(2 * RET_HEADS):
        zs = zq[:, hb * LANES:(hb + 1) * LANES]
        if hb >= RET_HEADS:
            zs = zs * (RET_QK_DIM ** -0.5)
        sw = jnp.where(upper, pltpu.roll(zs, 32, 1), pltpu.roll(zs, LANES - 32, 1))
        z_ref[0, :, hb * LANES:(hb + 1) * LANES] = (zs * cos + sw * sin).astype(BF16)

    def emit(c0, width, fn):
        for cc in range(c0, c0 + width, INPROJ_CHUNK):
            z_ref[0, :, cc:cc + INPROJ_CHUNK] = fn(proj(cc, INPROJ_CHUNK)).astype(BF16)

    vs = _ln(_gelu(proj(Z_VS, SGU_WIDTH))) * lnw_ref[...] + lnb_ref[...]
    z_ref[0, :, Z_VS:Z_VS + SGU_WIDTH] = vs.astype(BF16)
    emit(Z_U, SGU_WIDTH, _gelu)
    emit(Z_SG, RET_V_W, jax.nn.silu)
    emit(Z_GA, d, jax.nn.sigmoid)
    emit(Z_GB, d, jax.nn.sigmoid)
    emit(Z_V, RET_V_W, lambda z: z)


def _inproj_call(x, mod3, w_in, b_in, rope_t, sgu_ln_w, sgu_ln_b):
    bsz, t, d = x.shape
    tm = TM_INPROJ
    row = lambda b, i: (b, i, 0)
    const2 = lambda b, i: (0, 0)
    return pl.pallas_call(
        _inproj_kernel,
        out_shape=jax.ShapeDtypeStruct((bsz, t, D_IN), BF16),
        grid=(bsz, t // tm),
        in_specs=[
            pl.BlockSpec((1, tm, d), row),
            pl.BlockSpec((1, 1, 6 * d), lambda b, i: (b, 0, 0)),
            pl.BlockSpec((d, D_IN), const2, pipeline_mode=pl.Buffered(1)),
            pl.BlockSpec((1, D_IN), const2),
            pl.BlockSpec((tm, 2 * LANES), lambda b, i: (i, 0)),
            pl.BlockSpec((1, SGU_WIDTH), const2),
            pl.BlockSpec((1, SGU_WIDTH), const2),
        ],
        out_specs=pl.BlockSpec((1, tm, D_IN), row),
        compiler_params=_cparams(("parallel", "parallel")),
        name="inproj",
    )(x, mod3, w_in, b_in, rope_t, sgu_ln_w, sgu_ln_b)


def _dot_t0(a, b):
    return lax.dot_general(a, b, (((0,), (0,)), ((), ())), preferred_element_type=F32)


def _state_kernel(lg_ref, k_ref, v_ref, kc_ref, vc_ref, s_ref):
    hd = pl.program_id(1)
    lgf = lg_ref[0, hd]
    lgb = lg_ref[1, hd]
    n_chunks = s_ref.shape[2]
    n_ctx = kc_ref.shape[1]
    c = RET_CHUNK
    li = _iota_f((c, RET_QK_DIM), 0)
    zeta_f = jnp.exp((c - 1.0 - li) * lgf)
    zeta_b = jnp.exp(li * lgb)
    one = jnp.ones((1, 1), F32)
    cd_f = jnp.exp(one * (c * lgf))
    cd_b = jnp.exp(one * (c * lgb))

    tc = _iota_f((n_ctx, RET_QK_DIM), 0)
    kc = kc_ref[0].astype(F32)
    vc = vc_ref[0]
    s0f = _dot_t0((kc * jnp.exp((n_ctx - 1.0 - tc) * lgf)).astype(BF16), vc)
    s0b = _dot_t0((kc * jnp.exp(tc * lgb)).astype(BF16), vc)

    def chunk_kv(n, zeta):
        off = pl.multiple_of(n * c, c)
        kk = k_ref[0, pl.ds(off, c), :].astype(F32)
        return _dot_t0((kk * zeta).astype(BF16), v_ref[0, pl.ds(off, c), :])

    def step(i, carry):
        sf, sb = carry
        nb = n_chunks - 1 - i
        s_ref[0, 0, i, 0:RET_QK_DIM, :] = sf.astype(BF16)
        s_ref[0, 0, nb, RET_QK_DIM:2 * RET_QK_DIM, :] = sb.astype(BF16)
        return cd_f * sf + chunk_kv(i, zeta_f), cd_b * sb + chunk_kv(nb, zeta_b)

    lax.fori_loop(0, n_chunks, step, (s0f, s0b), unroll=STATE_UNROLL)


def _state_call(lg, z, kctx, vctx):
    bsz, t, _ = z.shape
    n_ctx = kctx.shape[1]
    n_chunks = t // RET_CHUNK
    return pl.pallas_call(
        _state_kernel,
        out_shape=jax.ShapeDtypeStruct((bsz, RET_HEADS, n_chunks, 2 * RET_QK_DIM, RET_V_DIM), BF16),
        grid=(bsz, RET_HEADS),
        in_specs=[
            pl.BlockSpec(memory_space=pltpu.SMEM),
            pl.BlockSpec((1, t, RET_QK_DIM), lambda b, h: (b, 0, RET_QK_W // RET_QK_DIM + h)),
            pl.BlockSpec((1, t, RET_V_DIM), lambda b, h: (b, 0, Z_V // RET_V_DIM + h)),
            pl.BlockSpec((1, n_ctx, RET_QK_DIM), lambda b, h: (b, 0, h)),
            pl.BlockSpec((1, n_ctx, RET_V_DIM), lambda b, h: (b, 0, h)),
        ],
        out_specs=pl.BlockSpec((1, 1, n_chunks, 2 * RET_QK_DIM, RET_V_DIM), lambda b, h: (b, h, 0, 0, 0)),
        compiler_params=_cparams(("parallel", "parallel")),
        name="states",
    )(lg, z, z, kctx, vctx)


def _mixer_kernel(lg_ref, za_ref, zb_ref, gb_ref, s_ref, x_ref, mod_ref,
                  wpa_ref, wpb_ref, wo_ref, sguw_ref, sgub_ref, gnw_ref, gnb_ref, bo_ref,
                  ln1w_ref, ln1b_ref, wr_ref, br_ref,
                  x1_ref, hm_ref, lgt_ref, ret_scr, sgu_scr):
    d = D_MODEL
    c = RET_CHUNK

    n_sub = za_ref.shape[1] // c
    row = lax.broadcasted_iota(I32, (c, c), 0)
    col = lax.broadcasted_iota(I32, (c, c), 1)
    diff = (row - col).astype(F32)
    rowq = _iota_f((c, RET_QK_DIM), 0)
    for hd in range(RET_HEADS):
        lgf = lg_ref[0, hd]
        lgb = lg_ref[1, hd]
        vcols = slice(hd * RET_V_DIM, (hd + 1) * RET_V_DIM)
        mask = (jnp.where(diff >= 0, jnp.exp(jnp.maximum(diff, 0.0) * lgf), 0.0)
                + jnp.where(diff <= 0, jnp.exp(jnp.maximum(-diff, 0.0) * lgb), 0.0))
        xi_f = jnp.exp((rowq + 1.0) * lgf)
        xi_b = jnp.exp((c - rowq) * lgb)
        for ci in range(n_sub):
            r0 = ci * c
            q = za_ref[0, r0:r0 + c, hd * RET_QK_DIM:(hd + 1) * RET_QK_DIM]
            k = za_ref[0, r0:r0 + c, RET_QK_W + hd * RET_QK_DIM:RET_QK_W + (hd + 1) * RET_QK_DIM]
            s = lax.dot_general(q, k, (((1,), (1,)), ((), ())), preferred_element_type=F32)
            vv = za_ref[0, r0:r0 + c, Z_V + hd * RET_V_DIM:Z_V + (hd + 1) * RET_V_DIM]
            intra = jnp.dot((s * mask).astype(BF16), vv, preferred_element_type=F32)
            qf = q.astype(F32)
            qx = jnp.concatenate([(qf * xi_f).astype(BF16), (qf * xi_b).astype(BF16)], axis=1)
            cross = jnp.dot(qx, s_ref[0, hd, ci], preferred_element_type=F32)
            o = _ln(intra + cross) * gnw_ref[:, vcols] + gnb_ref[:, vcols]
            gate = za_ref[0, r0:r0 + c, Z_SG + hd * RET_V_DIM:Z_SG + (hd + 1) * RET_V_DIM]
            ret_scr[r0:r0 + c, vcols] = (o * gate.astype(F32)).astype(BF16)

    for g in range(SGU_GROUPS):
        gcols = slice(g * SGU_GW, (g + 1) * SGU_GW)
        wg = sguw_ref[g]
        bg = sgub_ref[:, g:g + 1]
        for ci in range(n_sub):
            r0 = ci * c
            vsb = zb_ref[0, r0:r0 + c, Z_VS - Z_BLOCK + g * SGU_GW:Z_VS - Z_BLOCK + (g + 1) * SGU_GW]
            ub = zb_ref[0, r0:r0 + c, Z_U - Z_BLOCK + g * SGU_GW:Z_U - Z_BLOCK + (g + 1) * SGU_GW]
            sp = jnp.dot(wg, vsb, preferred_element_type=F32) + bg
            sgu_scr[r0:r0 + c, gcols] = (ub.astype(F32) * sp).astype(BF16)

    g1 = mod_ref[0, :, 2 * d:3 * d]
    sh2 = mod_ref[0, :, 3 * d:4 * d]
    sc2 = mod_ref[0, :, 4 * d:5 * d]
    for r0 in range(0, za_ref.shape[1], MERGE_ROWS):
        rows = slice(r0, r0 + MERGE_ROWS)
        pa = jnp.dot(ret_scr[rows, :], wpa_ref[...], preferred_element_type=F32)
        pb = jnp.dot(sgu_scr[rows, :], wpb_ref[...], preferred_element_type=F32)
        ga = zb_ref[0, rows, Z_GA - Z_BLOCK:Z_GA - Z_BLOCK + d]
        y = (ga.astype(F32) * pa + gb_ref[0, rows, :].astype(F32) * pb).astype(BF16)
        mix = jnp.dot(y, wo_ref[...], preferred_element_type=F32) + bo_ref[...]
        x1 = _ln(DEEPNORM_ALPHA * x_ref[0, rows, :] + g1 * mix) * ln1w_ref[...] + ln1b_ref[...]
        x1_ref[0, rows, :] = x1
        hm = (_ln(x1) * (1.0 + sc2) + sh2).astype(BF16)
        hm_ref[0, rows, :] = hm
        lgt_ref[0, :, rows] = lax.dot_general(wr_ref[...], hm, (((1,), (1,)), ((), ())),
                                              preferred_element_type=F32) + br_ref[...]


def _mixer_call(lg, z, states, x, mod3, wpa, wpb, wo, sguw, sgub_t, gnw, gnb, bo,
                ln1w, ln1b, wr_t, br):
    bsz, t, d = x.shape
    tt = TT_MIXER
    n_sub = tt // RET_CHUNK
    row = lambda b, i: (b, i, 0)
    c2 = lambda b, i: (0, 0)
    c3 = lambda b, i: (0, 0, 0)
    return pl.pallas_call(
        _mixer_kernel,
        out_shape=(jax.ShapeDtypeStruct((bsz, t, d), F32),
                   jax.ShapeDtypeStruct((bsz, t, d), BF16),
                   jax.ShapeDtypeStruct((bsz, N_EXPERTS, t), F32)),
        grid=(bsz, t // tt),
        in_specs=[
            pl.BlockSpec(memory_space=pltpu.SMEM),
            pl.BlockSpec((1, tt, Z_BLOCK), lambda b, i: (b, i, 0)),
            pl.BlockSpec((1, tt, Z_BLOCK), lambda b, i: (b, i, 1)),
            pl.BlockSpec((1, tt, d), lambda b, i: (b, i, Z_GB // d)),
            pl.BlockSpec((1, RET_HEADS, n_sub, 2 * RET_QK_DIM, RET_V_DIM), lambda b, i: (b, 0, i, 0, 0)),
            pl.BlockSpec((1, tt, d), row),
            pl.BlockSpec((1, 1, 6 * d), lambda b, i: (b, 0, 0)),
            pl.BlockSpec((RET_V_W, d), c2, pipeline_mode=pl.Buffered(1)),
            pl.BlockSpec((SGU_WIDTH, d), c2, pipeline_mode=pl.Buffered(1)),
            pl.BlockSpec((d, d), c2, pipeline_mode=pl.Buffered(1)),
            pl.BlockSpec((SGU_GROUPS, SGU_CHUNK, SGU_CHUNK), c3),
            pl.BlockSpec((SGU_CHUNK, SGU_GROUPS), c2),
            pl.BlockSpec((1, RET_V_W), c2),
            pl.BlockSpec((1, RET_V_W), c2),
            pl.BlockSpec((1, d), c2),
            pl.BlockSpec((1, d), c2),
            pl.BlockSpec((1, d), c2),
            pl.BlockSpec((N_EXPERTS, d), c2),
            pl.BlockSpec((N_EXPERTS, 1), c2),
        ],
        out_specs=(pl.BlockSpec((1, tt, d), row),
                   pl.BlockSpec((1, tt, d), row),
                   pl.BlockSpec((1, N_EXPERTS, tt), lambda b, i: (b, 0, i))),
        scratch_shapes=[pltpu.VMEM((tt, RET_V_W), BF16), pltpu.VMEM((tt, SGU_WIDTH), BF16)],
        compiler_params=_cparams(("parallel", "parallel")),
        name="mixer",
    )(lg, z, z, z, states, x, mod3, wpa, wpb, wo, sguw, sgub_t, gnw, gnb, bo,
      ln1w, ln1b, wr_t, br)


def _route_kernel(lgt_ref, pos_ref, gate_ref, cnt_ref, *, cap):
    lg = lgt_ref[0]
    n_e, t = lg.shape
    tb = TB_ROUTE
    m = jnp.max(lg, axis=0, keepdims=True)
    ex = jnp.exp(lg - m)
    aff = ex / jnp.sum(ex, axis=0, keepdims=True)

    def search(i, thr_bits):
        cand = thr_bits | lax.shift_left(jnp.int32(1), 30 - i)
        cnt = jnp.sum((aff >= lax.bitcast_convert_type(cand, F32)).astype(I32), axis=1, keepdims=True)
        return jnp.where(cnt >= cap, cand, thr_bits)

    thr_bits = lax.fori_loop(0, 31, search, jnp.zeros((n_e, 1), I32))
    floor_f = lax.bitcast_convert_type(thr_bits, F32)
    thr = jnp.min(jnp.where(aff >= floor_f, aff, jnp.inf), axis=1, keepdims=True)
    need = (cap - jnp.sum((aff > thr).astype(I32), axis=1, keepdims=True)).astype(F32)

    r = lax.broadcasted_iota(I32, (tb, tb), 0)
    cc = lax.broadcasted_iota(I32, (tb, tb), 1)
    tri = (r <= cc).astype(BF16)
    carry_eq = jnp.zeros((n_e, 1), F32)
    carry_sel = jnp.zeros((n_e, 1), F32)
    for blk in range(t // tb):
        sl = slice(blk * tb, (blk + 1) * tb)
        aff_b = aff[:, sl]
        eq = aff_b == thr
        eq_b = eq.astype(BF16)
        inc_eq = jnp.dot(eq_b, tri, preferred_element_type=F32)
        before = carry_eq + inc_eq - eq_b.astype(F32)
        sel = (aff_b > thr) | (eq & (before < need))
        sel_b = sel.astype(BF16)
        inc_sel = jnp.dot(sel_b, tri, preferred_element_type=F32)
        pos = carry_sel + inc_sel - 1.0
        pos_ref[0, blk] = jnp.where(sel, pos.astype(I32), -1)
        gate_ref[0, blk] = aff_b
        cnt_ref[0, blk] = jnp.broadcast_to(carry_sel, (n_e, LANES)).astype(I32)
        carry_eq = carry_eq + inc_eq[:, tb - 1:tb]
        carry_sel = carry_sel + inc_sel[:, tb - 1:tb]


def _route_call(logits_t, cap):
    bsz, n_e, t = logits_t.shape
    nblk = t // TB_ROUTE
    return pl.pallas_call(
        functools.partial(_route_kernel, cap=cap),
        out_shape=(jax.ShapeDtypeStruct((bsz, nblk, n_e, TB_ROUTE), I32),
                   jax.ShapeDtypeStruct((bsz, nblk, n_e, TB_ROUTE), F32),
                   jax.ShapeDtypeStruct((bsz, nblk, n_e, LANES), I32)),
        grid=(bsz,),
        in_specs=[pl.BlockSpec((1, n_e, t), lambda b: (b, 0, 0))],
        out_specs=(pl.BlockSpec((1, nblk, n_e, TB_ROUTE), lambda b: (b, 0, 0, 0)),
                   pl.BlockSpec((1, nblk, n_e, TB_ROUTE), lambda b: (b, 0, 0, 0)),
                   pl.BlockSpec((1, nblk, n_e, LANES), lambda b: (b, 0, 0, 0))),
        compiler_params=_cparams(("parallel",)),
        name="route",
    )(logits_t)


def _window_rows(ws, win):
    return lax.broadcasted_iota(I32, (win, TB_ROUTE), 0) + ws


def _sweep_window(w, cap, win, prow, rest):
    off = pl.multiple_of(jnp.minimum(w * win, cap - win), WIN_ALIGN)
    rows = _window_rows(off, win)
    return off, (rows == prow) & rest & (rows >= w * win)


def _gather_kernel(ws_ref, nx_ref, hm_ref, pos_ref, gate_ref, xs_ref, gc_ref):
    b = pl.program_id(0)
    g = pl.program_id(1)
    step = pl.program_id(2)
    n_sub, eg, tb = pos_ref.shape[1:]
    n_e = eg * pl.num_programs(1)
    nblk = n_sub * pl.num_programs(2)
    cap = xs_ref.shape[2]
    win = WIN_DISPATCH

    @pl.when(step == 0)
    def _():
        xs_ref[...] = jnp.zeros_like(xs_ref)
        gc_ref[...] = jnp.zeros_like(gc_ref)

    def base(sub):
        return (b * nblk + step * n_sub + sub) * n_e + g * eg

    def tokens(sub):
        return hm_ref[0, sub * tb:(sub + 1) * tb, :]

    for sub in range(n_sub):
        for el in range(eg):
            ws = pl.multiple_of(ws_ref[base(sub) + el], WIN_ALIGN)
            match = _window_rows(ws, win) == pos_ref[0, sub, el:el + 1, :]
            sl = pl.ds(ws, win)
            xs_ref[0, el, sl, :] += jnp.dot(match.astype(BF16), tokens(sub),
                                            preferred_element_type=F32).astype(BF16)
            gc_ref[0, el, sl, :] += jnp.sum(jnp.where(match, gate_ref[0, sub, el:el + 1, :], 0.0),
                                             axis=1, keepdims=True)

    for sub in range(n_sub):
        for el in range(eg):
            @pl.when(nx_ref[base(sub) + el] > 0)
            def _(sub=sub, el=el):
                prow = pos_ref[0, sub, el:el + 1, :]
                grow = gate_ref[0, sub, el:el + 1, :]
                rest = prow >= ws_ref[base(sub) + el] + win

                def body(w, carry):
                    off, m = _sweep_window(w, cap, win, prow, rest)
                    xs_ref[0, el, pl.ds(off, win), :] += jnp.dot(m.astype(BF16), tokens(sub),
                                                                  preferred_element_type=F32).astype(BF16)
                    gc_ref[0, el, pl.ds(off, win), :] += jnp.sum(jnp.where(m, grow, 0.0), axis=1, keepdims=True)
                    return carry

                lax.fori_loop(0, pl.cdiv(cap, win), body, 0)


def _gather_call(ws, nx, hm, pos_b, gate_b, cap):
    bsz, t, d = hm.shape
    nblk, n_e, tb = pos_b.shape[1:]
    eg = n_e // DISPATCH_GROUPS
    n_sub = DISPATCH_STEP_BLOCKS
    grid_spec = pltpu.PrefetchScalarGridSpec(
        num_scalar_prefetch=2,
        grid=(bsz, DISPATCH_GROUPS, nblk // n_sub),
        in_specs=[
            pl.BlockSpec((1, n_sub * tb, d), lambda b, g, k, ws, nx: (b, k, 0)),
            pl.BlockSpec((1, n_sub, eg, tb), lambda b, g, k, ws, nx: (b, k, g, 0)),
            pl.BlockSpec((1, n_sub, eg, tb), lambda b, g, k, ws, nx: (b, k, g, 0)),
        ],
        out_specs=(pl.BlockSpec((1, eg, cap, d), lambda b, g, k, ws, nx: (b, g, 0, 0)),
                   pl.BlockSpec((1, eg, cap, 1), lambda b, g, k, ws, nx: (b, g, 0, 0))),
    )
    return pl.pallas_call(
        _gather_kernel,
        out_shape=(jax.ShapeDtypeStruct((bsz, n_e, cap, d), BF16),
                   jax.ShapeDtypeStruct((bsz, n_e, cap, 1), F32)),
        grid_spec=grid_spec,
        compiler_params=_cparams(("parallel", "parallel", "arbitrary")),
        name="gather",
    )(ws, nx, hm, pos_b, gate_b)


def _ffn_kernel(xs_ref, gc_ref, wg_ref, wu_ref, wd_ref, ye_ref, wgu_s, wd_s):
    f = wd_s.shape[0]
    wgu_s[:, 0:f] = wg_ref[0].astype(BF16)
    wgu_s[:, f:2 * f] = wu_ref[0].astype(BF16)
    wd_s[...] = wd_ref[0].astype(BF16)

    rt = FFN_ROWS
    for b in range(xs_ref.shape[0]):
        for j in range(xs_ref.shape[2] // rt):
            sl = slice(j * rt, (j + 1) * rt)
            xgu = jnp.dot(xs_ref[b, 0, sl, :], wgu_s[...], preferred_element_type=F32)
            hid = (jax.nn.silu(xgu[:, 0:f]) * xgu[:, f:2 * f]).astype(BF16)
            ye = jnp.dot(hid, wd_s[...], preferred_element_type=F32) * gc_ref[b, 0, sl, :]
            ye_ref[b, 0, sl, :] = ye.astype(BF16)


def _ffn_call(xs, gc, w_gate, w_up, w_down):
    bsz, n_e, cap, d = xs.shape
    f = w_gate.shape[2]
    return pl.pallas_call(
        _ffn_kernel,
        out_shape=jax.ShapeDtypeStruct((bsz, n_e, cap, d), BF16),
        grid=(n_e,),
        in_specs=[
            pl.BlockSpec((bsz, 1, cap, d), lambda e: (0, e, 0, 0)),
            pl.BlockSpec((bsz, 1, cap, 1), lambda e: (0, e, 0, 0)),
            pl.BlockSpec((1, d, f), lambda e: (e, 0, 0)),
            pl.BlockSpec((1, d, f), lambda e: (e, 0, 0)),
            pl.BlockSpec((1, f, d), lambda e: (e, 0, 0)),
        ],
        out_specs=pl.BlockSpec((bsz, 1, cap, d), lambda e: (0, e, 0, 0)),
        scratch_shapes=[pltpu.VMEM((d, 2 * f), BF16), pltpu.VMEM((f, d), BF16)],
        compiler_params=_cparams(("parallel",)),
        name="ffn",
    )(xs, gc, w_gate, w_up, w_down)


def _combine_kernel(ws_ref, nx_ref, ye_ref, pos_ref, x1_ref, mod_ref, lnw_ref, lnb_ref, out_ref, acc_scr):
    b = pl.program_id(0)
    step = pl.program_id(1)
    n_sub, n_e, tb = pos_ref.shape[1:]
    nblk = n_sub * pl.num_programs(1)
    cap = ye_ref.shape[2]
    d = D_MODEL
    win = WIN_COMBINE
    g2 = mod_ref[0, :, 5 * d:6 * d]

    for sub in range(n_sub):
        base = (b * nblk + step * n_sub + sub) * n_e
        starts = [pl.multiple_of(ws_ref[base + e], WIN_ALIGN) for e in range(n_e)]
        onehot = jnp.concatenate(
            [(_window_rows(starts[e], win) == pos_ref[0, sub, e:e + 1, :]).astype(BF16) for e in range(n_e)],
            axis=0)
        rows = jnp.concatenate([ye_ref[0, e, pl.ds(starts[e], win), :] for e in range(n_e)], axis=0)
        acc_scr[...] = _dot_t0(onehot, rows)

        for e in range(n_e):
            @pl.when(nx_ref[base + e] > 0)
            def _(e=e, sub=sub, starts=starts):
                prow = pos_ref[0, sub, e:e + 1, :]
                rest = prow >= starts[e] + win

                def body(w, carry):
                    off, m = _sweep_window(w, cap, win, prow, rest)
                    acc_scr[...] += _dot_t0(m.astype(BF16), ye_ref[0, e, pl.ds(off, win), :])
                    return carry

                lax.fori_loop(0, pl.cdiv(cap, win), body, 0)

        tok = slice(sub * tb, (sub + 1) * tb)
        out_ref[0, tok, :] = (_ln(DEEPNORM_ALPHA * x1_ref[0, tok, :] + g2 * acc_scr[...]) * lnw_ref[...]
                              + lnb_ref[...])


def _combine_call(ws, nx, ye, pos_b, x1, mod3, ln2w, ln2b):
    bsz, n_e, cap, d = ye.shape
    nblk, _, tb = pos_b.shape[1:]
    n_sub = COMBINE_STEP_BLOCKS
    grid_spec = pltpu.PrefetchScalarGridSpec(
        num_scalar_prefetch=2,
        grid=(bsz, nblk // n_sub),
        in_specs=[
            pl.BlockSpec((1, n_e, cap, d), lambda b, k, ws, nx: (b, 0, 0, 0), pipeline_mode=pl.Buffered(1)),
            pl.BlockSpec((1, n_sub, n_e, tb), lambda b, k, ws, nx: (b, k, 0, 0)),
            pl.BlockSpec((1, n_sub * tb, d), lambda b, k, ws, nx: (b, k, 0)),
            pl.BlockSpec((1, 1, 6 * d), lambda b, k, ws, nx: (b, 0, 0)),
            pl.BlockSpec((1, d), lambda b, k, ws, nx: (0, 0)),
            pl.BlockSpec((1, d), lambda b, k, ws, nx: (0, 0)),
        ],
        out_specs=pl.BlockSpec((1, n_sub * tb, d), lambda b, k, ws, nx: (b, k, 0)),
        scratch_shapes=[pltpu.VMEM((tb, d), F32)],
    )
    return pl.pallas_call(
        _combine_kernel,
        out_shape=jax.ShapeDtypeStruct((bsz, nblk * tb, d), F32),
        grid_spec=grid_spec,
        compiler_params=_cparams(("parallel", "arbitrary")),
        name="combine",
    )(ws, nx, ye, pos_b, x1, mod3, ln2w, ln2b)


def _rope_tables(n_tokens):
    rows = n_tokens // GRID_W
    n_freq = RET_QK_DIM // 4
    inv = ROPE_THETA ** (-jnp.arange(n_freq, dtype=F32) / n_freq)
    ang_r = jnp.arange(rows, dtype=F32)[:, None] * inv
    ang_c = jnp.arange(GRID_W, dtype=F32)[:, None] * inv
    zr = jnp.zeros((rows, 2 * n_freq), F32)
    zc = jnp.zeros((GRID_W, 2 * n_freq), F32)
    cr, sr, cc, sc = jnp.cos(ang_r), jnp.sin(ang_r), jnp.cos(ang_c), jnp.sin(ang_c)
    cos_t = (jnp.concatenate([cr, cr, zr], axis=-1)[:, None, :]
             + jnp.concatenate([zc, cc, cc], axis=-1)[None, :, :]).reshape(n_tokens, RET_QK_DIM)
    sin_t = (jnp.concatenate([-sr, sr, zr], axis=-1)[:, None, :]
             + jnp.concatenate([zc, -sc, sc], axis=-1)[None, :, :]).reshape(n_tokens, RET_QK_DIM)
    return jnp.concatenate([cos_t, sin_t], axis=-1)


def kernel(x, c, ctx, c_ctx, w_ada, b_ada, w_in, b_in, ret_decay_f, ret_decay_b, ret_gn_w, ret_gn_b,
           sgu_ln_w, sgu_ln_b, sgu_w, sgu_b, w_pa, w_pb, w_o, b_o, ln1_w, ln1_b,
           w_router, b_router, w_gate, w_up, w_down, ln2_w, ln2_b):
    bsz, t, d = x.shape
    assert d == D_MODEL and w_ada.shape[0] == DEPTH == 1 and bsz < MOD_ROWS
    cap = EC_CAPACITY_FACTOR * t // N_EXPERTS
    for win in (WIN_DISPATCH, WIN_COMBINE):
        assert cap >= win and (cap - win) % WIN_ALIGN == 0 and win % WIN_ALIGN == 0
    assert t % (TB_ROUTE * DISPATCH_STEP_BLOCKS) == 0 and t % (TB_ROUTE * COMBINE_STEP_BLOCKS) == 0
    assert N_EXPERTS % DISPATCH_GROUPS == 0
    l = 0

    cvec = jnp.zeros((MOD_ROWS, d), F32).at[:bsz].set(c).at[bsz].set(c_ctx)
    mod = _mod_call(cvec.T, w_ada[l], b_ada[l][None], bsz + 1)
    mod3 = mod[:, None, :]

    w_in_b = w_in[l].astype(BF16)
    b_in2 = b_in[l][None]
    lo_kv, hi_kv = RET_QK_W, 2 * RET_QK_W + RET_V_W
    kctx, vctx = _ctx_kv_call(ctx, mod3, w_in_b[:, lo_kv:hi_kv], b_in2[:, lo_kv:hi_kv], bsz)

    z = _inproj_call(x, mod3, w_in_b, b_in2, _rope_tables(t), sgu_ln_w[l][None], sgu_ln_b[l][None])

    lg = jnp.stack([jax.nn.log_sigmoid(ret_decay_f[l].astype(F32)),
                    jax.nn.log_sigmoid(ret_decay_b[l].astype(F32))])
    states = _state_call(lg, z, kctx, vctx)

    x1, hm, logits_t = _mixer_call(
        lg, z, states, x, mod3,
        w_pa[l].astype(BF16), w_pb[l].astype(BF16), w_o[l].astype(BF16),
        sgu_w[l].astype(BF16), sgu_b[l].T, ret_gn_w[l][None], ret_gn_b[l][None], b_o[l][None],
        ln1_w[l][None], ln1_b[l][None], w_router[l].T.astype(BF16), b_router[l][:, None])

    pos_b, gate_b, cnt_b = _route_call(logits_t, cap)

    c0 = cnt_b[..., 0]
    c1 = jnp.concatenate([c0[:, 1:], jnp.full_like(c0[:, :1], cap)], axis=1)

    def windows(win):
        ws = jnp.minimum((c0 // WIN_ALIGN) * WIN_ALIGN, cap - win)
        return ws.reshape(-1), (c1 > ws + win).astype(I32).reshape(-1)

    ws_d, nx_d = windows(WIN_DISPATCH)
    ws_c, nx_c = windows(WIN_COMBINE)

    xs, gc = _gather_call(ws_d, nx_d, hm, pos_b, gate_b, cap)
    ye = _ffn_call(xs, gc, w_gate[l], w_up[l], w_down[l])
    return _combine_call(ws_c, nx_c, ye, pos_b, x1, mod3, ln2_w[l][None], ln2_b[l][None])
```

```python
import functools
import math

import jax
import jax.numpy as jnp
from jax import lax
from jax.experimental import pallas as pl
from jax.experimental.pallas import tpu as pltpu

F32 = jnp.float32
BF16 = jnp.bfloat16
I32 = jnp.int32

D_MODEL = 1024
DEPTH = 1
GRID_W = 64
RET_HEADS = 4
RET_QK_DIM = 128
RET_V_DIM = 256
RET_CHUNK = 128
ROPE_THETA = 10000.0
RET_QK_W = RET_HEADS * RET_QK_DIM
RET_V_W = RET_HEADS * RET_V_DIM
SGU_GROUPS = 4
SGU_CHUNK = 128
SGU_WIDTH = 1024
SGU_GW = SGU_WIDTH // SGU_GROUPS
N_EXPERTS = 16
EC_CAPACITY_FACTOR = 2
D_EXPERT = 1024
LN_EPS = 1e-6
DEEPNORM_ALPHA = (2.0 * DEPTH) ** 0.25
D_IN = 2 * RET_QK_W + 2 * RET_V_W + 2 * SGU_WIDTH + 2 * D_MODEL
Z_K = RET_QK_W
Z_V = 2 * RET_QK_W
Z_SG = Z_V + RET_V_W
Z_U = Z_SG + RET_V_W
Z_VS = Z_U + SGU_WIDTH
Z_GA = Z_VS + SGU_WIDTH
Z_GB = Z_GA + D_MODEL
Z_BLOCK = 3 * 1024

V7X_VMEM_BYTES = 64 * 1024 * 1024
VMEM_LIMIT = 60 * 1024 * 1024
LANES = 128

MOD_ROWS = 8
TM_INPROJ = 1024
INPROJ_CHUNK = 256
TT_MIXER = 512
MERGE_ROWS = 512
TB_ROUTE = 512
WIN_DISPATCH = 128
WIN_COMBINE = 96
WIN_ALIGN = 16
DISPATCH_GROUPS = 2
DISPATCH_STEP_BLOCKS = 4
COMBINE_STEP_BLOCKS = 2
STATE_UNROLL = 8
FFN_ROWS = 256


def _cparams(sem):
    return pltpu.CompilerParams(dimension_semantics=sem, vmem_limit_bytes=VMEM_LIMIT)


def _ln(x):
    mu = jnp.mean(x, axis=-1, keepdims=True)
    xc = x - mu
    var = jnp.mean(xc * xc, axis=-1, keepdims=True)
    return xc * lax.rsqrt(var + LN_EPS)


_GELU_A = -2.0 * math.sqrt(2.0 / math.pi) * math.log2(math.e)
_GELU_B = _GELU_A * 0.044715


def _gelu(x):
    return x / (1.0 + jnp.exp2(x * (_GELU_A + _GELU_B * (x * x))))


def _iota_f(shape, dim):
    return lax.broadcasted_iota(I32, shape, dim).astype(F32)


def _mod_kernel(ct_ref, w_ref, b_ref, o_ref, *, n_rows):
    c = ct_ref[...]
    a = c * jax.nn.sigmoid(c)
    w = w_ref[...]
    rows = [jnp.sum(w * a[:, m:m + 1], axis=0, keepdims=True) + b_ref[...] for m in range(n_rows)]
    rows += [jnp.zeros_like(rows[0])] * (MOD_ROWS - n_rows)
    o_ref[...] = jnp.concatenate(rows, axis=0)


def _mod_call(cvec_t, w_ada, b_ada, n_rows):
    d, n = w_ada.shape
    tn = 1024
    return pl.pallas_call(
        functools.partial(_mod_kernel, n_rows=n_rows),
        out_shape=jax.ShapeDtypeStruct((MOD_ROWS, n), F32),
        grid=(n // tn,),
        in_specs=[
            pl.BlockSpec((d, MOD_ROWS), lambda j: (0, 0)),
            pl.BlockSpec((d, tn), lambda j: (0, j)),
            pl.BlockSpec((1, tn), lambda j: (0, j)),
        ],
        out_specs=pl.BlockSpec((MOD_ROWS, tn), lambda j: (0, j)),
        compiler_params=_cparams(("parallel",)),
        name="mod",
    )(cvec_t, w_ada, b_ada)


def _ctx_kv_kernel(x_ref, mod_ref, wk_ref, wv_ref, bk_ref, bv_ref, k_ref, v_ref):
    d = D_MODEL
    sh = mod_ref[0, :, 0:d]
    sc = mod_ref[0, :, d:2 * d]
    h = (_ln(x_ref[0]) * (1.0 + sc) + sh).astype(BF16)
    k = jnp.dot(h, wk_ref[...], preferred_element_type=F32) + bk_ref[...]
    k_ref[0] = (k * (RET_QK_DIM ** -0.5)).astype(BF16)
    v_ref[0] = (jnp.dot(h, wv_ref[...], preferred_element_type=F32) + bv_ref[...]).astype(BF16)


def _ctx_kv_call(ctx, mod3, w_in, b_in, ctx_row):
    bsz, n_ctx, d = ctx.shape
    kblk = Z_K // RET_QK_W
    vblk = Z_V // RET_V_W
    return pl.pallas_call(
        _ctx_kv_kernel,
        out_shape=(jax.ShapeDtypeStruct((bsz, n_ctx, RET_QK_W), BF16),
                   jax.ShapeDtypeStruct((bsz, n_ctx, RET_V_W), BF16)),
        grid=(bsz,),
        in_specs=[
            pl.BlockSpec((1, n_ctx, d), lambda b: (b, 0, 0)),
            pl.BlockSpec((1, 1, 6 * d), lambda b: (ctx_row, 0, 0)),
            pl.BlockSpec((d, RET_QK_W), lambda b: (0, kblk)),
            pl.BlockSpec((d, RET_V_W), lambda b: (0, vblk)),
            pl.BlockSpec((1, RET_QK_W), lambda b: (0, kblk)),
            pl.BlockSpec((1, RET_V_W), lambda b: (0, vblk)),
        ],
        out_specs=(pl.BlockSpec((1, n_ctx, RET_QK_W), lambda b: (b, 0, 0)),
                   pl.BlockSpec((1, n_ctx, RET_V_W), lambda b: (b, 0, 0))),
        compiler_params=_cparams(("parallel",)),
        name="ctx_kv",
    )(ctx, mod3, w_in, w_in, b_in, b_in)


def _inproj_kernel(x_ref, mod_ref, w_ref, b_ref, rope_ref, lnw_ref, lnb_ref, z_ref):
    d = D_MODEL
    sh = mod_ref[0, :, 0:d]
    sc = mod_ref[0, :, d:2 * d]
    h = (_ln(x_ref[0]) * (1.0 + sc) + sh).astype(BF16)

    def proj(c0, width):
        return jnp.dot(h, w_ref[:, c0:c0 + width], preferred_element_type=F32) + b_ref[:, c0:c0 + width]

    zq = proj(0, 2 * RET_QK_W)
    cos = rope_ref[:, 0:LANES]
    sin = rope_ref[:, LANES:2 * LANES]
    tm = zq.shape[0]
    upper = (lax.broadcasted_iota(I32, (tm, LANES), 1) & 32) != 0
    for hb in range(2 * RET_HEADS):
        zs = zq[:, hb * LANES:(hb + 1) * LANES]
        if hb >= RET_HEADS:
            zs = zs * (RET_QK_DIM ** -0.5)
        sw = jnp.where(upper, pltpu.roll(zs, 32, 1), pltpu.roll(zs, LANES - 32, 1))
        z_ref[0, :, hb * LANES:(hb + 1) * LANES] = (zs * cos + sw * sin).astype(BF16)

    def emit(c0, width, fn):
        for cc in range(c0, c0 + width, INPROJ_CHUNK):
            z_ref[0, :, cc:cc + INPROJ_CHUNK] = fn(proj(cc, INPROJ_CHUNK)).astype(BF16)

    vs = _ln(_gelu(proj(Z_VS, SGU_WIDTH))) * lnw_ref[...] + lnb_ref[...]
    z_ref[0, :, Z_VS:Z_VS + SGU_WIDTH] = vs.astype(BF16)
    emit(Z_U, SGU_WIDTH, _gelu)
    emit(Z_SG, RET_V_W, jax.nn.silu)
    emit(Z_GA, d, jax.nn.sigmoid)
    emit(Z_GB, d, jax.nn.sigmoid)
    emit(Z_V, RET_V_W, lambda z: z)


def _inproj_call(x, mod3, w_in, b_in, rope_t, sgu_ln_w, sgu_ln_b):
    bsz, t, d = x.shape
    tm = TM_INPROJ
    row = lambda b, i: (b, i, 0)
    const2 = lambda b, i: (0, 0)
    return pl.pallas_call(
        _inproj_kernel,
        out_shape=jax.ShapeDtypeStruct((bsz, t, D_IN), BF16),
        grid=(bsz, t // tm),
        in_specs=[
            pl.BlockSpec((1, tm, d), row),
            pl.BlockSpec((1, 1, 6 * d), lambda b, i: (b, 0, 0)),
            pl.BlockSpec((d, D_IN), const2, pipeline_mode=pl.Buffered(1)),
            pl.BlockSpec((1, D_IN), const2),
            pl.BlockSpec((tm, 2 * LANES), lambda b, i: (i, 0)),
            pl.BlockSpec((1, SGU_WIDTH), const2),
            pl.BlockSpec((1, SGU_WIDTH), const2),
        ],
        out_specs=pl.BlockSpec((1, tm, D_IN), row),
        compiler_params=_cparams(("parallel", "parallel")),
        name="inproj",
    )(x, mod3, w_in, b_in, rope_t, sgu_ln_w, sgu_ln_b)


def _dot_t0(a, b):
    return lax.dot_general(a, b, (((0,), (0,)), ((), ())), preferred_element_type=F32)


def _state_kernel(lg_ref, k_ref, v_ref, kc_ref, vc_ref, s_ref):
    hd = pl.program_id(1)
    lgf = lg_ref[0, hd]
    lgb = lg_ref[1, hd]
    n_chunks = s_ref.shape[2]
    n_ctx = kc_ref.shape[1]
    c = RET_CHUNK
    li = _iota_f((c, RET_QK_DIM), 0)
    zeta_f = jnp.exp((c - 1.0 - li) * lgf)
    zeta_b = jnp.exp(li * lgb)
    one = jnp.ones((1, 1), F32)
    cd_f = jnp.exp(one * (c * lgf))
    cd_b = jnp.exp(one * (c * lgb))

    tc = _iota_f((n_ctx, RET_QK_DIM), 0)
    kc = kc_ref[0].astype(F32)
    vc = vc_ref[0]
    s0f = _dot_t0((kc * jnp.exp((n_ctx - 1.0 - tc) * lgf)).astype(BF16), vc)
    s0b = _dot_t0((kc * jnp.exp(tc * lgb)).astype(BF16), vc)

    def chunk_kv(n, zeta):
        off = pl.multiple_of(n * c, c)
        kk = k_ref[0, pl.ds(off, c), :].astype(F32)
        return _dot_t0((kk * zeta).astype(BF16), v_ref[0, pl.ds(off, c), :])

    def step(i, carry):
        sf, sb = carry
        nb = n_chunks - 1 - i
        s_ref[0, 0, i, 0:RET_QK_DIM, :] = sf.astype(BF16)
        s_ref[0, 0, nb, RET_QK_DIM:2 * RET_QK_DIM, :] = sb.astype(BF16)
        return cd_f * sf + chunk_kv(i, zeta_f), cd_b * sb + chunk_kv(nb, zeta_b)

    lax.fori_loop(0, n_chunks, step, (s0f, s0b), unroll=STATE_UNROLL)


def _state_call(lg, z, kctx, vctx):
    bsz, t, _ = z.shape
    n_ctx = kctx.shape[1]
    n_chunks = t // RET_CHUNK
    return pl.pallas_call(
        _state_kernel,
        out_shape=jax.ShapeDtypeStruct((bsz, RET_HEADS, n_chunks, 2 * RET_QK_DIM, RET_V_DIM), BF16),
        grid=(bsz, RET_HEADS),
        in_specs=[
            pl.BlockSpec(memory_space=pltpu.SMEM),
            pl.BlockSpec((1, t, RET_QK_DIM), lambda b, h: (b, 0, RET_QK_W // RET_QK_DIM + h)),
            pl.BlockSpec((1, t, RET_V_DIM), lambda b, h: (b, 0, Z_V // RET_V_DIM + h)),
            pl.BlockSpec((1, n_ctx, RET_QK_DIM), lambda b, h: (b, 0, h)),
            pl.BlockSpec((1, n_ctx, RET_V_DIM), lambda b, h: (b, 0, h)),
        ],
        out_specs=pl.BlockSpec((1, 1, n_chunks, 2 * RET_QK_DIM, RET_V_DIM), lambda b, h: (b, h, 0, 0, 0)),
        compiler_params=_cparams(("parallel", "parallel")),
        name="states",
    )(lg, z, z, kctx, vctx)


def _mixer_kernel(lg_ref, za_ref, zb_ref, gb_ref, s_ref, x_ref, mod_ref,
                  wpa_ref, wpb_ref, wo_ref, sguw_ref, sgub_ref, gnw_ref, gnb_ref, bo_ref,
                  ln1w_ref, ln1b_ref, wr_ref, br_ref,
                  x1_ref, hm_ref, lgt_ref, ret_scr, sgu_scr):
    d = D_MODEL
    c = RET_CHUNK

    n_sub = za_ref.shape[1] // c
    row = lax.broadcasted_iota(I32, (c, c), 0)
    col = lax.broadcasted_iota(I32, (c, c), 1)
    diff = (row - col).astype(F32)
    rowq = _iota_f((c, RET_QK_DIM), 0)
    for hd in range(RET_HEADS):
        lgf = lg_ref[0, hd]
        lgb = lg_ref[1, hd]
        vcols = slice(hd * RET_V_DIM, (hd + 1) * RET_V_DIM)
        mask = (jnp.where(diff >= 0, jnp.exp(jnp.maximum(diff, 0.0) * lgf), 0.0)
                + jnp.where(diff <= 0, jnp.exp(jnp.maximum(-diff, 0.0) * lgb), 0.0))
        xi_f = jnp.exp((rowq + 1.0) * lgf)
        xi_b = jnp.exp((c - rowq) * lgb)
        for ci in range(n_sub):
            r0 = ci * c
            q = za_ref[0, r0:r0 + c, hd * RET_QK_DIM:(hd + 1) * RET_QK_DIM]
            k = za_ref[0, r0:r0 + c, RET_QK_W + hd * RET_QK_DIM:RET_QK_W + (hd + 1) * RET_QK_DIM]
            s = lax.dot_general(q, k, (((1,), (1,)), ((), ())), preferred_element_type=F32)
            vv = za_ref[0, r0:r0 + c, Z_V + hd * RET_V_DIM:Z_V + (hd + 1) * RET_V_DIM]
            intra = jnp.dot((s * mask).astype(BF16), vv, preferred_element_type=F32)
            qf = q.astype(F32)
            qx = jnp.concatenate([(qf * xi_f).astype(BF16), (qf * xi_b).astype(BF16)], axis=1)
            cross = jnp.dot(qx, s_ref[0, hd, ci], preferred_element_type=F32)
            o = _ln(intra + cross) * gnw_ref[:, vcols] + gnb_ref[:, vcols]
            gate = za_ref[0, r0:r0 + c, Z_SG + hd * RET_V_DIM:Z_SG + (hd + 1) * RET_V_DIM]
            ret_scr[r0:r0 + c, vcols] = (o * gate.astype(F32)).astype(BF16)

    for g in range(SGU_GROUPS):
        gcols = slice(g * SGU_GW, (g + 1) * SGU_GW)
        wg = sguw_ref[g]
        bg = sgub_ref[:, g:g + 1]
        for ci in range(n_sub):
            r0 = ci * c
            vsb = zb_ref[0, r0:r0 + c, Z_VS - Z_BLOCK + g * SGU_GW:Z_VS - Z_BLOCK + (g + 1) * SGU_GW]
            ub = zb_ref[0, r0:r0 + c, Z_U - Z_BLOCK + g * SGU_GW:Z_U - Z_BLOCK + (g + 1) * SGU_GW]
            sp = jnp.dot(wg, vsb, preferred_element_type=F32) + bg
            sgu_scr[r0:r0 + c, gcols] = (ub.astype(F32) * sp).astype(BF16)

    g1 = mod_ref[0, :, 2 * d:3 * d]
    sh2 = mod_ref[0, :, 3 * d:4 * d]
    sc2 = mod_ref[0, :, 4 * d:5 * d]
    for r0 in range(0, za_ref.shape[1], MERGE_ROWS):
        rows = slice(r0, r0 + MERGE_ROWS)
        pa = jnp.dot(ret_scr[rows, :], wpa_ref[...], preferred_element_type=F32)
        pb = jnp.dot(sgu_scr[rows, :], wpb_ref[...], preferred_element_type=F32)
        ga = zb_ref[0, rows, Z_GA - Z_BLOCK:Z_GA - Z_BLOCK + d]
        y = (ga.astype(F32) * pa + gb_ref[0, rows, :].astype(F32) * pb).astype(BF16)
        mix = jnp.dot(y, wo_ref[...], preferred_element_type=F32) + bo_ref[...]
        x1 = _ln(DEEPNORM_ALPHA * x_ref[0, rows, :] + g1 * mix) * ln1w_ref[...] + ln1b_ref[...]
        x1_ref[0, rows, :] = x1
        hm = (_ln(x1) * (1.0 + sc2) + sh2).astype(BF16)
        hm_ref[0, rows, :] = hm
        lgt_ref[0, :, rows] = lax.dot_general(wr_ref[...], hm, (((1,), (1,)), ((), ())),
                                              preferred_element_type=F32) + br_ref[...]


def _mixer_call(lg, z, states, x, mod3, wpa, wpb, wo, sguw, sgub_t, gnw, gnb, bo,
                ln1w, ln1b, wr_t, br):
    bsz, t, d = x.shape
    tt = TT_MIXER
    n_sub = tt // RET_CHUNK
    row = lambda b, i: (b, i, 0)
    c2 = lambda b, i: (0, 0)
    c3 = lambda b, i: (0, 0, 0)
    return pl.pallas_call(
        _mixer_kernel,
        out_shape=(jax.ShapeDtypeStruct((bsz, t, d), F32),
                   jax.ShapeDtypeStruct((bsz, t, d), BF16),
                   jax.ShapeDtypeStruct((bsz, N_EXPERTS, t), F32)),
        grid=(bsz, t // tt),
        in_specs=[
            pl.BlockSpec(memory_space=pltpu.SMEM),
            pl.BlockSpec((1, tt, Z_BLOCK), lambda b, i: (b, i, 0)),
            pl.BlockSpec((1, tt, Z_BLOCK), lambda b, i: (b, i, 1)),
            pl.BlockSpec((1, tt, d), lambda b, i: (b, i, Z_GB // d)),
            pl.BlockSpec((1, RET_HEADS, n_sub, 2 * RET_QK_DIM, RET_V_DIM), lambda b, i: (b, 0, i, 0, 0)),
            pl.BlockSpec((1, tt, d), row),
            pl.BlockSpec((1, 1, 6 * d), lambda b, i: (b, 0, 0)),
            pl.BlockSpec((RET_V_W, d), c2, pipeline_mode=pl.Buffered(1)),
            pl.BlockSpec((SGU_WIDTH, d), c2, pipeline_mode=pl.Buffered(1)),
            pl.BlockSpec((d, d), c2, pipeline_mode=pl.Buffered(1)),
            pl.BlockSpec((SGU_GROUPS, SGU_CHUNK, SGU_CHUNK), c3),
            pl.BlockSpec((SGU_CHUNK, SGU_GROUPS), c2),
            pl.BlockSpec((1, RET_V_W), c2),
            pl.BlockSpec((1, RET_V_W), c2),
            pl.BlockSpec((1, d), c2),
            pl.BlockSpec((1, d), c2),
            pl.BlockSpec((1, d), c2),
            pl.BlockSpec((N_EXPERTS, d), c2),
            pl.BlockSpec((N_EXPERTS, 1), c2),
        ],
        out_specs=(pl.BlockSpec((1, tt, d), row),
                   pl.BlockSpec((1, tt, d), row),
                   pl.BlockSpec((1, N_EXPERTS, tt), lambda b, i: (b, 0, i))),
        scratch_shapes=[pltpu.VMEM((tt, RET_V_W), BF16), pltpu.VMEM((tt, SGU_WIDTH), BF16)],
        compiler_params=_cparams(("parallel", "parallel")),
        name="mixer",
    )(lg, z, z, z, states, x, mod3, wpa, wpb, wo, sguw, sgub_t, gnw, gnb, bo,
      ln1w, ln1b, wr_t, br)


def _route_kernel(lgt_ref, pos_ref, gate_ref, cnt_ref, *, cap):
    lg = lgt_ref[0]
    n_e, t = lg.shape
    tb = TB_ROUTE
    m = jnp.max(lg, axis=0, keepdims=True)
    ex = jnp.exp(lg - m)
    aff = ex / jnp.sum(ex, axis=0, keepdims=True)

    def search(i, thr_bits):
        cand = thr_bits | lax.shift_left(jnp.int32(1), 30 - i)
        cnt = jnp.sum((aff >= lax.bitcast_convert_type(cand, F32)).astype(I32), axis=1, keepdims=True)
        return jnp.where(cnt >= cap, cand, thr_bits)

    thr_bits = lax.fori_loop(0, 31, search, jnp.zeros((n_e, 1), I32))
    floor_f = lax.bitcast_convert_type(thr_bits, F32)
    thr = jnp.min(jnp.where(aff >= floor_f, aff, jnp.inf), axis=1, keepdims=True)
    need = (cap - jnp.sum((aff > thr).astype(I32), axis=1, keepdims=True)).astype(F32)

    r = lax.broadcasted_iota(I32, (tb, tb), 0)
    cc = lax.broadcasted_iota(I32, (tb, tb), 1)
    tri = (r <= cc).astype(BF16)
    carry_eq = jnp.zeros((n_e, 1), F32)
    carry_sel = jnp.zeros((n_e, 1), F32)
    for blk in range(t // tb):
        sl = slice(blk * tb, (blk + 1) * tb)
        aff_b = aff[:, sl]
        eq = aff_b == thr
        eq_b = eq.astype(BF16)
        inc_eq = jnp.dot(eq_b, tri, preferred_element_type=F32)
        before = carry_eq + inc_eq - eq_b.astype(F32)
        sel = (aff_b > thr) | (eq & (before < need))
        sel_b = sel.astype(BF16)
        inc_sel = jnp.dot(sel_b, tri, preferred_element_type=F32)
        pos = carry_sel + inc_sel - 1.0
        pos_ref[0, blk] = jnp.where(sel, pos.astype(I32), -1)
        gate_ref[0, blk] = aff_b
        cnt_ref[0, blk] = jnp.broadcast_to(carry_sel, (n_e, LANES)).astype(I32)
        carry_eq = carry_eq + inc_eq[:, tb - 1:tb]
        carry_sel = carry_sel + inc_sel[:, tb - 1:tb]


def _route_call(logits_t, cap):
    bsz, n_e, t = logits_t.shape
    nblk = t // TB_ROUTE
    return pl.pallas_call(
        functools.partial(_route_kernel, cap=cap),
        out_shape=(jax.ShapeDtypeStruct((bsz, nblk, n_e, TB_ROUTE), I32),
                   jax.ShapeDtypeStruct((bsz, nblk, n_e, TB_ROUTE), F32),
                   jax.ShapeDtypeStruct((bsz, nblk, n_e, LANES), I32)),
        grid=(bsz,),
        in_specs=[pl.BlockSpec((1, n_e, t), lambda b: (b, 0, 0))],
        out_specs=(pl.BlockSpec((1, nblk, n_e, TB_ROUTE), lambda b: (b, 0, 0, 0)),
                   pl.BlockSpec((1, nblk, n_e, TB_ROUTE), lambda b: (b, 0, 0, 0)),
                   pl.BlockSpec((1, nblk, n_e, LANES), lambda b: (b, 0, 0, 0))),
        compiler_params=_cparams(("parallel",)),
        name="route",
    )(logits_t)


def _window_rows(ws, win):
    return lax.broadcasted_iota(I32, (win, TB_ROUTE), 0) + ws


def _sweep_window(w, cap, win, prow, rest):
    off = pl.multiple_of(jnp.minimum(w * win, cap - win), WIN_ALIGN)
    rows = _window_rows(off, win)
    return off, (rows == prow) & rest & (rows >= w * win)


def _gather_kernel(ws_ref, nx_ref, hm_ref, pos_ref, gate_ref, xs_ref, gc_ref):
    b = pl.program_id(0)
    g = pl.program_id(1)
    step = pl.program_id(2)
    n_sub, eg, tb = pos_ref.shape[1:]
    n_e = eg * pl.num_programs(1)
    nblk = n_sub * pl.num_programs(2)
    cap = xs_ref.shape[2]
    win = WIN_DISPATCH

    @pl.when(step == 0)
    def _():
        xs_ref[...] = jnp.zeros_like(xs_ref)
        gc_ref[...] = jnp.zeros_like(gc_ref)

    def base(sub):
        return (b * nblk + step * n_sub + sub) * n_e + g * eg

    def tokens(sub):
        return hm_ref[0, sub * tb:(sub + 1) * tb, :]

    for sub in range(n_sub):
        for el in range(eg):
            ws = pl.multiple_of(ws_ref[base(sub) + el], WIN_ALIGN)
            match = _window_rows(ws, win) == pos_ref[0, sub, el:el + 1, :]
            sl = pl.ds(ws, win)
            xs_ref[0, el, sl, :] += jnp.dot(match.astype(BF16), tokens(sub),
                                            preferred_element_type=F32).astype(BF16)
            gc_ref[0, el, sl, :] += jnp.sum(jnp.where(match, gate_ref[0, sub, el:el + 1, :], 0.0),
                                             axis=1, keepdims=True)

    for sub in range(n_sub):
        for el in range(eg):
            @pl.when(nx_ref[base(sub) + el] > 0)
            def _(sub=sub, el=el):
                prow = pos_ref[0, sub, el:el + 1, :]
                grow = gate_ref[0, sub, el:el + 1, :]
                rest = prow >= ws_ref[base(sub) + el] + win

                def body(w, carry):
                    off, m = _sweep_window(w, cap, win, prow, rest)
                    xs_ref[0, el, pl.ds(off, win), :] += jnp.dot(m.astype(BF16), tokens(sub),
                                                                  preferred_element_type=F32).astype(BF16)
                    gc_ref[0, el, pl.ds(off, win), :] += jnp.sum(jnp.where(m, grow, 0.0), axis=1, keepdims=True)
                    return carry

                lax.fori_loop(0, pl.cdiv(cap, win), body, 0)


def _gather_call(ws, nx, hm, pos_b, gate_b, cap):
    bsz, t, d = hm.shape
    nblk, n_e, tb = pos_b.shape[1:]
    eg = n_e // DISPATCH_GROUPS
    n_sub = DISPATCH_STEP_BLOCKS
    grid_spec = pltpu.PrefetchScalarGridSpec(
        num_scalar_prefetch=2,
        grid=(bsz, DISPATCH_GROUPS, nblk // n_sub),
        in_specs=[
            pl.BlockSpec((1, n_sub * tb, d), lambda b, g, k, ws, nx: (b, k, 0)),
            pl.BlockSpec((1, n_sub, eg, tb), lambda b, g, k, ws, nx: (b, k, g, 0)),
            pl.BlockSpec((1, n_sub, eg, tb), lambda b, g, k, ws, nx: (b, k, g, 0)),
        ],
        out_specs=(pl.BlockSpec((1, eg, cap, d), lambda b, g, k, ws, nx: (b, g, 0, 0)),
                   pl.BlockSpec((1, eg, cap, 1), lambda b, g, k, ws, nx: (b, g, 0, 0))),
    )
    return pl.pallas_call(
        _gather_kernel,
        out_shape=(jax.ShapeDtypeStruct((bsz, n_e, cap, d), BF16),
                   jax.ShapeDtypeStruct((bsz, n_e, cap, 1), F32)),
        grid_spec=grid_spec,
        compiler_params=_cparams(("parallel", "parallel", "arbitrary")),
        name="gather",
    )(ws, nx, hm, pos_b, gate_b)


def _ffn_kernel(xs_ref, gc_ref, wg_ref, wu_ref, wd_ref, ye_ref, wgu_s, wd_s):
    f = wd_s.shape[0]
    wgu_s[:, 0:f] = wg_ref[0].astype(BF16)
    wgu_s[:, f:2 * f] = wu_ref[0].astype(BF16)
    wd_s[...] = wd_ref[0].astype(BF16)

    rt = FFN_ROWS
    for b in range(xs_ref.shape[0]):
        for j in range(xs_ref.shape[2] // rt):
            sl = slice(j * rt, (j + 1) * rt)
            xgu = jnp.dot(xs_ref[b, 0, sl, :], wgu_s[...], preferred_element_type=F32)
            hid = (jax.nn.silu(xgu[:, 0:f]) * xgu[:, f:2 * f]).astype(BF16)
            ye = jnp.dot(hid, wd_s[...], preferred_element_type=F32) * gc_ref[b, 0, sl, :]
            ye_ref[b, 0, sl, :] = ye.astype(BF16)


def _ffn_call(xs, gc, w_gate, w_up, w_down):
    bsz, n_e, cap, d = xs.shape
    f = w_gate.shape[2]
    return pl.pallas_call(
        _ffn_kernel,
        out_shape=jax.ShapeDtypeStruct((bsz, n_e, cap, d), BF16),
        grid=(n_e,),
        in_specs=[
            pl.BlockSpec((bsz, 1, cap, d), lambda e: (0, e, 0, 0)),
            pl.BlockSpec((bsz, 1, cap, 1), lambda e: (0, e, 0, 0)),
            pl.BlockSpec((1, d, f), lambda e: (e, 0, 0)),
            pl.BlockSpec((1, d, f), lambda e: (e, 0, 0)),
            pl.BlockSpec((1, f, d), lambda e: (e, 0, 0)),
        ],
        out_specs=pl.BlockSpec((bsz, 1, cap, d), lambda e: (0, e, 0, 0)),
        scratch_shapes=[pltpu.VMEM((d, 2 * f), BF16), pltpu.VMEM((f, d), BF16)],
        compiler_params=_cparams(("parallel",)),
        name="ffn",
    )(xs, gc, w_gate, w_up, w_down)


def _combine_kernel(ws_ref, nx_ref, ye_ref, pos_ref, x1_ref, mod_ref, lnw_ref, lnb_ref, out_ref, acc_scr):
    b = pl.program_id(0)
    step = pl.program_id(1)
    n_sub, n_e, tb = pos_ref.shape[1:]
    nblk = n_sub * pl.num_programs(1)
    cap = ye_ref.shape[2]
    d = D_MODEL
    win = WIN_COMBINE
    g2 = mod_ref[0, :, 5 * d:6 * d]

    for sub in range(n_sub):
        base = (b * nblk + step * n_sub + sub) * n_e
        starts = [pl.multiple_of(ws_ref[base + e], WIN_ALIGN) for e in range(n_e)]
        onehot = jnp.concatenate(
            [(_window_rows(starts[e], win) == pos_ref[0, sub, e:e + 1, :]).astype(BF16) for e in range(n_e)],
            axis=0)
        rows = jnp.concatenate([ye_ref[0, e, pl.ds(starts[e], win), :] for e in range(n_e)], axis=0)
        acc_scr[...] = _dot_t0(onehot, rows)

        for e in range(n_e):
            @pl.when(nx_ref[base + e] > 0)
            def _(e=e, sub=sub, starts=starts):
                prow = pos_ref[0, sub, e:e + 1, :]
                rest = prow >= starts[e] + win

                def body(w, carry):
                    off, m = _sweep_window(w, cap, win, prow, rest)
                    acc_scr[...] += _dot_t0(m.astype(BF16), ye_ref[0, e, pl.ds(off, win), :])
                    return carry

                lax.fori_loop(0, pl.cdiv(cap, win), body, 0)

        tok = slice(sub * tb, (sub + 1) * tb)
        out_ref[0, tok, :] = (_ln(DEEPNORM_ALPHA * x1_ref[0, tok, :] + g2 * acc_scr[...]) * lnw_ref[...]
                              + lnb_ref[...])


def _combine_call(ws, nx, ye, pos_b, x1, mod3, ln2w, ln2b):
    bsz, n_e, cap, d = ye.shape
    nblk, _, tb = pos_b.shape[1:]
    n_sub = COMBINE_STEP_BLOCKS
    grid_spec = pltpu.PrefetchScalarGridSpec(
        num_scalar_prefetch=2,
        grid=(bsz, nblk // n_sub),
        in_specs=[
            pl.BlockSpec((1, n_e, cap, d), lambda b, k, ws, nx: (b, 0, 0, 0), pipeline_mode=pl.Buffered(1)),
            pl.BlockSpec((1, n_sub, n_e, tb), lambda b, k, ws, nx: (b, k, 0, 0)),
            pl.BlockSpec((1, n_sub * tb, d), lambda b, k, ws, nx: (b, k, 0)),
            pl.BlockSpec((1, 1, 6 * d), lambda b, k, ws, nx: (b, 0, 0)),
            pl.BlockSpec((1, d), lambda b, k, ws, nx: (0, 0)),
            pl.BlockSpec((1, d), lambda b, k, ws, nx: (0, 0)),
        ],
        out_specs=pl.BlockSpec((1, n_sub * tb, d), lambda b, k, ws, nx: (b, k, 0)),
        scratch_shapes=[pltpu.VMEM((tb, d), F32)],
    )
    return pl.pallas_call(
        _combine_kernel,
        out_shape=jax.ShapeDtypeStruct((bsz, nblk * tb, d), F32),
        grid_spec=grid_spec,
        compiler_params=_cparams(("parallel", "arbitrary")),
        name="combine",
    )(ws, nx, ye, pos_b, x1, mod3, ln2w, ln2b)


def _rope_tables(n_tokens):
    rows = n_tokens // GRID_W
    n_freq = RET_QK_DIM // 4
    inv = ROPE_THETA ** (-jnp.arange(n_freq, dtype=F32) / n_freq)
    ang_r = jnp.arange(rows, dtype=F32)[:, None] * inv
    ang_c = jnp.arange(GRID_W, dtype=F32)[:, None] * inv
    zr = jnp.zeros((rows, 2 * n_freq), F32)
    zc = jnp.zeros((GRID_W, 2 * n_freq), F32)
    cr, sr, cc, sc = jnp.cos(ang_r), jnp.sin(ang_r), jnp.cos(ang_c), jnp.sin(ang_c)
    cos_t = (jnp.concatenate([cr, cr, zr], axis=-1)[:, None, :]
             + jnp.concatenate([zc, cc, cc], axis=-1)[None, :, :]).reshape(n_tokens, RET_QK_DIM)
    sin_t = (jnp.concatenate([-sr, sr, zr], axis=-1)[:, None, :]
             + jnp.concatenate([zc, -sc, sc], axis=-1)[None, :, :]).reshape(n_tokens, RET_QK_DIM)
    return jnp.concatenate([cos_t, sin_t], axis=-1)


def kernel(x, c, ctx, c_ctx, w_ada, b_ada, w_in, b_in, ret_decay_f, ret_decay_b, ret_gn_w, ret_gn_b,
           sgu_ln_w, sgu_ln_b, sgu_w, sgu_b, w_pa, w_pb, w_o, b_o, ln1_w, ln1_b,
           w_router, b_router, w_gate, w_up, w_down, ln2_w, ln2_b):
    bsz, t, d = x.shape
    assert d == D_MODEL and w_ada.shape[0] == DEPTH == 1 and bsz < MOD_ROWS
    cap = EC_CAPACITY_FACTOR * t // N_EXPERTS
    for win in (WIN_DISPATCH, WIN_COMBINE):
        assert cap >= win and (cap - win) % WIN_ALIGN == 0 and win % WIN_ALIGN == 0
    assert t % (TB_ROUTE * DISPATCH_STEP_BLOCKS) == 0 and t % (TB_ROUTE * COMBINE_STEP_BLOCKS) == 0
    assert N_EXPERTS % DISPATCH_GROUPS == 0
    l = 0

    cvec = jnp.concatenate([c, c_ctx[None], jnp.zeros((MOD_ROWS - bsz - 1, d), F32)], axis=0)
    mod = _mod_call(cvec.T, w_ada[l], b_ada[l][None], bsz + 1)
    mod3 = mod[:, None, :]

    w_in_b = w_in[l].astype(BF16)
    b_in2 = b_in[l][None]
    kctx, vctx = _ctx_kv_call(ctx, mod3, w_in_b, b_in2, bsz)

    z = _inproj_call(x, mod3, w_in_b, b_in2, _rope_tables(t), sgu_ln_w[l][None], sgu_ln_b[l][None])

    lg = jnp.stack([jax.nn.log_sigmoid(ret_decay_f[l].astype(F32)),
                    jax.nn.log_sigmoid(ret_decay_b[l].astype(F32))])
    states = _state_call(lg, z, kctx, vctx)

    x1, hm, logits_t = _mixer_call(
        lg, z, states, x, mod3,
        w_pa[l].astype(BF16), w_pb[l].astype(BF16), w_o[l].astype(BF16),
        sgu_w[l].astype(BF16), sgu_b[l].T, ret_gn_w[l][None], ret_gn_b[l][None], b_o[l][None],
        ln1_w[l][None], ln1_b[l][None], w_router[l].T.astype(BF16), b_router[l][:, None])

    pos_b, gate_b, cnt_b = _route_call(logits_t, cap)

    c0 = cnt_b[..., 0]
    c1 = jnp.concatenate([c0[:, 1:], jnp.full_like(c0[:, :1], cap)], axis=1)

    def windows(win):
        ws = jnp.minimum((c0 // WIN_ALIGN) * WIN_ALIGN, cap - win)
        return ws.reshape(-1), (c1 > ws + win).astype(I32).reshape(-1)

    ws_d, nx_d = windows(WIN_DISPATCH)
    ws_c, nx_c = windows(WIN_COMBINE)

    xs, gc = _gather_call(ws_d, nx_d, hm, pos_b, gate_b, cap)
    ye = _ffn_call(xs, gc, w_gate[l], w_up[l], w_down[l])
    return _combine_call(ws_c, nx_c, ye, pos_b, x1, mod3, ln2_w[l][None], ln2_b[l][None])
```

```python
import functools
import math

import jax
import jax.numpy as jnp
from jax import lax
from jax.experimental import pallas as pl
from jax.experimental.pallas import tpu as pltpu

F32 = jnp.float32
BF16 = jnp.bfloat16
I32 = jnp.int32

D_MODEL = 1024
DEPTH = 1
GRID_W = 64
RET_HEADS = 4
RET_QK_DIM = 128
RET_V_DIM = 256
RET_CHUNK = 128
ROPE_THETA = 10000.0
RET_QK_W = RET_HEADS * RET_QK_DIM
RET_V_W = RET_HEADS * RET_V_DIM
SGU_GROUPS = 4
SGU_CHUNK = 128
SGU_WIDTH = 1024
SGU_GW = SGU_WIDTH // SGU_GROUPS
N_EXPERTS = 16
EC_CAPACITY_FACTOR = 2
LN_EPS = 1e-6
DEEPNORM_ALPHA = (2.0 * DEPTH) ** 0.25
D_IN = 2 * RET_QK_W + 2 * RET_V_W + 2 * SGU_WIDTH + 2 * D_MODEL
Z_K = RET_QK_W
Z_V = 2 * RET_QK_W
Z_SG = Z_V + RET_V_W
Z_U = Z_SG + RET_V_W
Z_VS = Z_U + SGU_WIDTH
Z_GA = Z_VS + SGU_WIDTH
Z_GB = Z_GA + D_MODEL
Z_BLOCK = 3 * 1024

V7X_VMEM_BYTES = 64 * 1024 * 1024
VMEM_LIMIT = V7X_VMEM_BYTES - 4 * 1024 * 1024
LANES = 128

MOD_ROWS = 8
MOD_COLS = 2048
TM_INPROJ = 1024
INPROJ_CHUNK = 256
TT_MIXER = 512
MERGE_ROWS = 512
TB_ROUTE = 512
WIN_DISPATCH = 128
WIN_COMBINE = 96
WIN_ALIGN = 16
DISPATCH_GROUPS = 2
DISPATCH_STEP_BLOCKS = 4
COMBINE_STEP_BLOCKS = 2
STATE_UNROLL = 8
FFN_ROWS = 256


def _cparams(sem):
    return pltpu.CompilerParams(dimension_semantics=sem, vmem_limit_bytes=VMEM_LIMIT)


def _ln(x):
    mu = jnp.mean(x, axis=-1, keepdims=True)
    xc = x - mu
    var = jnp.mean(xc * xc, axis=-1, keepdims=True)
    return xc * lax.rsqrt(var + LN_EPS)


_GELU_A = -2.0 * math.sqrt(2.0 / math.pi) * math.log2(math.e)
_GELU_B = _GELU_A * 0.044715


def _gelu(x):
    return x / (1.0 + jnp.exp2(x * (_GELU_A + _GELU_B * (x * x))))


def _iota_f(shape, dim):
    return lax.broadcasted_iota(I32, shape, dim).astype(F32)


def _mod_kernel(ct_ref, w_ref, b_ref, o_ref, *, n_rows):
    c = ct_ref[...]
    a = c * jax.nn.sigmoid(c)
    w = w_ref[...]
    rows = [jnp.sum(w * a[:, m:m + 1], axis=0, keepdims=True) + b_ref[...] for m in range(n_rows)]
    rows += [jnp.zeros_like(rows[0])] * (MOD_ROWS - n_rows)
    o_ref[...] = jnp.concatenate(rows, axis=0)


def _mod_call(cvec_t, w_ada, b_ada, n_rows):
    d, n = w_ada.shape
    tn = MOD_COLS
    return pl.pallas_call(
        functools.partial(_mod_kernel, n_rows=n_rows),
        out_shape=jax.ShapeDtypeStruct((MOD_ROWS, n), F32),
        grid=(n // tn,),
        in_specs=[
            pl.BlockSpec((d, MOD_ROWS), lambda j: (0, 0)),
            pl.BlockSpec((d, tn), lambda j: (0, j)),
            pl.BlockSpec((1, tn), lambda j: (0, j)),
        ],
        out_specs=pl.BlockSpec((MOD_ROWS, tn), lambda j: (0, j)),
        compiler_params=_cparams(("parallel",)),
        name="mod",
    )(cvec_t, w_ada, b_ada)


def _ctx_kv_kernel(x_ref, mod_ref, wk_ref, wv_ref, bk_ref, bv_ref, k_ref, v_ref):
    d = D_MODEL
    sh = mod_ref[0, :, 0:d]
    sc = mod_ref[0, :, d:2 * d]
    h = (_ln(x_ref[0]) * (1.0 + sc) + sh).astype(BF16)
    k = jnp.dot(h, wk_ref[...], preferred_element_type=F32) + bk_ref[...]
    k_ref[0] = (k * (RET_QK_DIM ** -0.5)).astype(BF16)
    v_ref[0] = (jnp.dot(h, wv_ref[...], preferred_element_type=F32) + bv_ref[...]).astype(BF16)


def _ctx_kv_call(ctx, mod3, w_in, b_in, ctx_row):
    bsz, n_ctx, d = ctx.shape
    kblk = Z_K // RET_QK_W
    vblk = Z_V // RET_V_W
    return pl.pallas_call(
        _ctx_kv_kernel,
        out_shape=(jax.ShapeDtypeStruct((bsz, n_ctx, RET_QK_W), BF16),
                   jax.ShapeDtypeStruct((bsz, n_ctx, RET_V_W), BF16)),
        grid=(bsz,),
        in_specs=[
            pl.BlockSpec((1, n_ctx, d), lambda b: (b, 0, 0)),
            pl.BlockSpec((1, 1, 6 * d), lambda b: (ctx_row, 0, 0)),
            pl.BlockSpec((d, RET_QK_W), lambda b: (0, kblk)),
            pl.BlockSpec((d, RET_V_W), lambda b: (0, vblk)),
            pl.BlockSpec((1, RET_QK_W), lambda b: (0, kblk)),
            pl.BlockSpec((1, RET_V_W), lambda b: (0, vblk)),
        ],
        out_specs=(pl.BlockSpec((1, n_ctx, RET_QK_W), lambda b: (b, 0, 0)),
                   pl.BlockSpec((1, n_ctx, RET_V_W), lambda b: (b, 0, 0))),
        compiler_params=_cparams(("parallel",)),
        name="ctx_kv",
    )(ctx, mod3, w_in, w_in, b_in, b_in)


def _inproj_kernel(x_ref, mod_ref, w_ref, b_ref, rope_ref, lnw_ref, lnb_ref, z_ref):
    d = D_MODEL
    sh = mod_ref[0, :, 0:d]
    sc = mod_ref[0, :, d:2 * d]
    h = (_ln(x_ref[0]) * (1.0 + sc) + sh).astype(BF16)

    def proj(c0, width):
        return jnp.dot(h, w_ref[:, c0:c0 + width], preferred_element_type=F32) + b_ref[:, c0:c0 + width]

    zq = proj(0, 2 * RET_QK_W)
    cos = rope_ref[:, 0:LANES]
    sin = rope_ref[:, LANES:2 * LANES]
    tm = zq.shape[0]
    upper = (lax.broadcasted_iota(I32, (tm, LANES), 1) & 32) != 0
    for hb in range(2 * RET_HEADS):
        zs = zq[:, hb * LANES:(hb + 1) * LANES]
        if hb >= RET_HEADS:
            zs = zs * (RET_QK_DIM ** -0.5)
        sw = jnp.where(upper, pltpu.roll(zs, 32, 1), pltpu.roll(zs, LANES - 32, 1))
        z_ref[0, :, hb * LANES:(hb + 1) * LANES] = (zs * cos + sw * sin).astype(BF16)

    def emit(c0, width, fn):
        for cc in range(c0, c0 + width, INPROJ_CHUNK):
            z_ref[0, :, cc:cc + INPROJ_CHUNK] = fn(proj(cc, INPROJ_CHUNK)).astype(BF16)

    vs = _ln(_gelu(proj(Z_VS, SGU_WIDTH))) * lnw_ref[...] + lnb_ref[...]
    z_ref[0, :, Z_VS:Z_VS + SGU_WIDTH] = vs.astype(BF16)
    emit(Z_U, SGU_WIDTH, _gelu)
    emit(Z_SG, RET_V_W, jax.nn.silu)
    emit(Z_GA, d, jax.nn.sigmoid)
    emit(Z_GB, d, jax.nn.sigmoid)
    emit(Z_V, RET_V_W, lambda z: z)


def _inproj_call(x, mod3, w_in, b_in, rope_t, sgu_ln_w, sgu_ln_b):
    bsz, t, d = x.shape
    tm = TM_INPROJ
    row = lambda b, i: (b, i, 0)
    const2 = lambda b, i: (0, 0)
    return pl.pallas_call(
        _inproj_kernel,
        out_shape=jax.ShapeDtypeStruct((bsz, t, D_IN), BF16),
        grid=(bsz, t // tm),
        in_specs=[
            pl.BlockSpec((1, tm, d), row),
            pl.BlockSpec((1, 1, 6 * d), lambda b, i: (b, 0, 0)),
            pl.BlockSpec((d, D_IN), const2, pipeline_mode=pl.Buffered(1)),
            pl.BlockSpec((1, D_IN), const2),
            pl.BlockSpec((tm, 2 * LANES), lambda b, i: (i, 0)),
            pl.BlockSpec((1, SGU_WIDTH), const2),
            pl.BlockSpec((1, SGU_WIDTH), const2),
        ],
        out_specs=pl.BlockSpec((1, tm, D_IN), row),
        compiler_params=_cparams(("parallel", "parallel")),
        name="inproj",
    )(x, mod3, w_in, b_in, rope_t, sgu_ln_w, sgu_ln_b)


def _dot_t0(a, b):
    return lax.dot_general(a, b, (((0,), (0,)), ((), ())), preferred_element_type=F32)


def _state_kernel(lg_ref, k_ref, v_ref, kc_ref, vc_ref, s_ref):
    hd = pl.program_id(1)
    lgf = lg_ref[0, hd]
    lgb = lg_ref[1, hd]
    n_chunks = s_ref.shape[2]
    n_ctx = kc_ref.shape[1]
    c = RET_CHUNK
    li = _iota_f((c, RET_QK_DIM), 0)
    zeta_f = jnp.exp((c - 1.0 - li) * lgf)
    zeta_b = jnp.exp(li * lgb)
    one = jnp.ones((1, 1), F32)
    cd_f = jnp.exp(one * (c * lgf))
    cd_b = jnp.exp(one * (c * lgb))

    tc = _iota_f((n_ctx, RET_QK_DIM), 0)
    kc = kc_ref[0].astype(F32)
    vc = vc_ref[0]
    s0f = _dot_t0((kc * jnp.exp((n_ctx - 1.0 - tc) * lgf)).astype(BF16), vc)
    s0b = _dot_t0((kc * jnp.exp(tc * lgb)).astype(BF16), vc)

    def chunk_kv(n, zeta):
        off = pl.multiple_of(n * c, c)
        kk = k_ref[0, pl.ds(off, c), :].astype(F32)
        return _dot_t0((kk * zeta).astype(BF16), v_ref[0, pl.ds(off, c), :])

    def step(i, carry):
        sf, sb = carry
        nb = n_chunks - 1 - i
        s_ref[0, 0, i, 0:RET_QK_DIM, :] = sf.astype(BF16)
        s_ref[0, 0, nb, RET_QK_DIM:2 * RET_QK_DIM, :] = sb.astype(BF16)
        return cd_f * sf + chunk_kv(i, zeta_f), cd_b * sb + chunk_kv(nb, zeta_b)

    lax.fori_loop(0, n_chunks, step, (s0f, s0b), unroll=STATE_UNROLL)


def _state_call(lg, z, kctx, vctx):
    bsz, t, _ = z.shape
    n_ctx = kctx.shape[1]
    n_chunks = t // RET_CHUNK
    return pl.pallas_call(
        _state_kernel,
        out_shape=jax.ShapeDtypeStruct((bsz, RET_HEADS, n_chunks, 2 * RET_QK_DIM, RET_V_DIM), BF16),
        grid=(bsz, RET_HEADS),
        in_specs=[
            pl.BlockSpec(memory_space=pltpu.SMEM),
            pl.BlockSpec((1, t, RET_QK_DIM), lambda b, h: (b, 0, RET_QK_W // RET_QK_DIM + h)),
            pl.BlockSpec((1, t, RET_V_DIM), lambda b, h: (b, 0, Z_V // RET_V_DIM + h)),
            pl.BlockSpec((1, n_ctx, RET_QK_DIM), lambda b, h: (b, 0, h)),
            pl.BlockSpec((1, n_ctx, RET_V_DIM), lambda b, h: (b, 0, h)),
        ],
        out_specs=pl.BlockSpec((1, 1, n_chunks, 2 * RET_QK_DIM, RET_V_DIM), lambda b, h: (b, h, 0, 0, 0)),
        compiler_params=_cparams(("parallel", "parallel")),
        name="states",
    )(lg, z, z, kctx, vctx)


def _mixer_kernel(lg_ref, za_ref, zb_ref, gb_ref, s_ref, x_ref, mod_ref,
                  wpa_ref, wpb_ref, wo_ref, sguw_ref, sgub_ref, gnw_ref, gnb_ref, bo_ref,
                  ln1w_ref, ln1b_ref, wr_ref, br_ref,
                  x1_ref, hm_ref, lgt_ref, ret_scr, sgu_scr):
    d = D_MODEL
    c = RET_CHUNK

    n_sub = za_ref.shape[1] // c
    row = lax.broadcasted_iota(I32, (c, c), 0)
    col = lax.broadcasted_iota(I32, (c, c), 1)
    diff = (row - col).astype(F32)
    rowq = _iota_f((c, RET_QK_DIM), 0)
    for hd in range(RET_HEADS):
        lgf = lg_ref[0, hd]
        lgb = lg_ref[1, hd]
        vcols = slice(hd * RET_V_DIM, (hd + 1) * RET_V_DIM)
        mask = (jnp.where(diff >= 0, jnp.exp(jnp.maximum(diff, 0.0) * lgf), 0.0)
                + jnp.where(diff <= 0, jnp.exp(jnp.maximum(-diff, 0.0) * lgb), 0.0))
        xi_f = jnp.exp((rowq + 1.0) * lgf)
        xi_b = jnp.exp((c - rowq) * lgb)
        for ci in range(n_sub):
            r0 = ci * c
            q = za_ref[0, r0:r0 + c, hd * RET_QK_DIM:(hd + 1) * RET_QK_DIM]
            k = za_ref[0, r0:r0 + c, RET_QK_W + hd * RET_QK_DIM:RET_QK_W + (hd + 1) * RET_QK_DIM]
            s = lax.dot_general(q, k, (((1,), (1,)), ((), ())), preferred_element_type=F32)
            vv = za_ref[0, r0:r0 + c, Z_V + hd * RET_V_DIM:Z_V + (hd + 1) * RET_V_DIM]
            intra = jnp.dot((s * mask).astype(BF16), vv, preferred_element_type=F32)
            qf = q.astype(F32)
            qx = jnp.concatenate([(qf * xi_f).astype(BF16), (qf * xi_b).astype(BF16)], axis=1)
            cross = jnp.dot(qx, s_ref[0, hd, ci], preferred_element_type=F32)
            o = _ln(intra + cross) * gnw_ref[:, vcols] + gnb_ref[:, vcols]
            gate = za_ref[0, r0:r0 + c, Z_SG + hd * RET_V_DIM:Z_SG + (hd + 1) * RET_V_DIM]
            ret_scr[r0:r0 + c, vcols] = (o * gate.astype(F32)).astype(BF16)

    for g in range(SGU_GROUPS):
        gcols = slice(g * SGU_GW, (g + 1) * SGU_GW)
        wg = sguw_ref[g]
        bg = sgub_ref[:, g:g + 1]
        for ci in range(n_sub):
            r0 = ci * c
            vsb = zb_ref[0, r0:r0 + c, Z_VS - Z_BLOCK + g * SGU_GW:Z_VS - Z_BLOCK + (g + 1) * SGU_GW]
            ub = zb_ref[0, r0:r0 + c, Z_U - Z_BLOCK + g * SGU_GW:Z_U - Z_BLOCK + (g + 1) * SGU_GW]
            sp = jnp.dot(wg, vsb, preferred_element_type=F32) + bg
            sgu_scr[r0:r0 + c, gcols] = (ub.astype(F32) * sp).astype(BF16)

    g1 = mod_ref[0, :, 2 * d:3 * d]
    sh2 = mod_ref[0, :, 3 * d:4 * d]
    sc2 = mod_ref[0, :, 4 * d:5 * d]
    for r0 in range(0, za_ref.shape[1], MERGE_ROWS):
        rows = slice(r0, r0 + MERGE_ROWS)
        pa = jnp.dot(ret_scr[rows, :], wpa_ref[...], preferred_element_type=F32)
        pb = jnp.dot(sgu_scr[rows, :], wpb_ref[...], preferred_element_type=F32)
        ga = zb_ref[0, rows, Z_GA - Z_BLOCK:Z_GA - Z_BLOCK + d]
        y = (ga.astype(F32) * pa + gb_ref[0, rows, :].astype(F32) * pb).astype(BF16)
        mix = jnp.dot(y, wo_ref[...], preferred_element_type=F32) + bo_ref[...]
        x1 = _ln(DEEPNORM_ALPHA * x_ref[0, rows, :] + g1 * mix) * ln1w_ref[...] + ln1b_ref[...]
        x1_ref[0, rows, :] = x1
        hm = (_ln(x1) * (1.0 + sc2) + sh2).astype(BF16)
        hm_ref[0, rows, :] = hm
        lgt_ref[0, :, rows] = lax.dot_general(wr_ref[...], hm, (((1,), (1,)), ((), ())),
                                              preferred_element_type=F32) + br_ref[...]


def _mixer_call(lg, z, states, x, mod3, wpa, wpb, wo, sguw, sgub_t, gnw, gnb, bo,
                ln1w, ln1b, wr_t, br):
    bsz, t, d = x.shape
    tt = TT_MIXER
    n_sub = tt // RET_CHUNK
    row = lambda b, i: (b, i, 0)
    c2 = lambda b, i: (0, 0)
    c3 = lambda b, i: (0, 0, 0)
    return pl.pallas_call(
        _mixer_kernel,
        out_shape=(jax.ShapeDtypeStruct((bsz, t, d), F32),
                   jax.ShapeDtypeStruct((bsz, t, d), BF16),
                   jax.ShapeDtypeStruct((bsz, N_EXPERTS, t), F32)),
        grid=(bsz, t // tt),
        in_specs=[
            pl.BlockSpec(memory_space=pltpu.SMEM),
            pl.BlockSpec((1, tt, Z_BLOCK), lambda b, i: (b, i, 0)),
            pl.BlockSpec((1, tt, Z_BLOCK), lambda b, i: (b, i, 1)),
            pl.BlockSpec((1, tt, d), lambda b, i: (b, i, Z_GB // d)),
            pl.BlockSpec((1, RET_HEADS, n_sub, 2 * RET_QK_DIM, RET_V_DIM), lambda b, i: (b, 0, i, 0, 0)),
            pl.BlockSpec((1, tt, d), row),
            pl.BlockSpec((1, 1, 6 * d), lambda b, i: (b, 0, 0)),
            pl.BlockSpec((RET_V_W, d), c2, pipeline_mode=pl.Buffered(1)),
            pl.BlockSpec((SGU_WIDTH, d), c2, pipeline_mode=pl.Buffered(1)),
            pl.BlockSpec((d, d), c2, pipeline_mode=pl.Buffered(1)),
            pl.BlockSpec((SGU_GROUPS, SGU_CHUNK, SGU_CHUNK), c3),
            pl.BlockSpec((SGU_CHUNK, SGU_GROUPS), c2),
            pl.BlockSpec((1, RET_V_W), c2),
            pl.BlockSpec((1, RET_V_W), c2),
            pl.BlockSpec((1, d), c2),
            pl.BlockSpec((1, d), c2),
            pl.BlockSpec((1, d), c2),
            pl.BlockSpec((N_EXPERTS, d), c2),
            pl.BlockSpec((N_EXPERTS, 1), c2),
        ],
        out_specs=(pl.BlockSpec((1, tt, d), row),
                   pl.BlockSpec((1, tt, d), row),
                   pl.BlockSpec((1, N_EXPERTS, tt), lambda b, i: (b, 0, i))),
        scratch_shapes=[pltpu.VMEM((tt, RET_V_W), BF16), pltpu.VMEM((tt, SGU_WIDTH), BF16)],
        compiler_params=_cparams(("parallel", "parallel")),
        name="mixer",
    )(lg, z, z, z, states, x, mod3, wpa, wpb, wo, sguw, sgub_t, gnw, gnb, bo,
      ln1w, ln1b, wr_t, br)


def _route_kernel(lgt_ref, pos_ref, gate_ref, cnt_ref, *, cap):
    lg = lgt_ref[0]
    n_e, t = lg.shape
    tb = TB_ROUTE
    m = jnp.max(lg, axis=0, keepdims=True)
    ex = jnp.exp(lg - m)
    aff = ex / jnp.sum(ex, axis=0, keepdims=True)

    def search(i, thr_bits):
        cand = thr_bits | lax.shift_left(jnp.int32(1), 30 - i)
        cnt = jnp.sum((aff >= lax.bitcast_convert_type(cand, F32)).astype(I32), axis=1, keepdims=True)
        return jnp.where(cnt >= cap, cand, thr_bits)

    thr_bits = lax.fori_loop(0, 31, search, jnp.zeros((n_e, 1), I32))
    floor_f = lax.bitcast_convert_type(thr_bits, F32)
    thr = jnp.min(jnp.where(aff >= floor_f, aff, jnp.inf), axis=1, keepdims=True)
    need = (cap - jnp.sum((aff > thr).astype(I32), axis=1, keepdims=True)).astype(F32)

    r = lax.broadcasted_iota(I32, (tb, tb), 0)
    cc = lax.broadcasted_iota(I32, (tb, tb), 1)
    tri = (r <= cc).astype(BF16)
    carry_eq = jnp.zeros((n_e, 1), F32)
    carry_sel = jnp.zeros((n_e, 1), F32)
    for blk in range(t // tb):
        sl = slice(blk * tb, (blk + 1) * tb)
        aff_b = aff[:, sl]
        eq = aff_b == thr
        eq_b = eq.astype(BF16)
        inc_eq = jnp.dot(eq_b, tri, preferred_element_type=F32)
        before = carry_eq + inc_eq - eq_b.astype(F32)
        sel = (aff_b > thr) | (eq & (before < need))
        sel_b = sel.astype(BF16)
        inc_sel = jnp.dot(sel_b, tri, preferred_element_type=F32)
        pos = carry_sel + inc_sel - 1.0
        pos_ref[0, blk] = jnp.where(sel, pos.astype(I32), -1)
        gate_ref[0, blk] = aff_b
        cnt_ref[0, blk] = jnp.broadcast_to(carry_sel, (n_e, LANES)).astype(I32)
        carry_eq = carry_eq + inc_eq[:, tb - 1:tb]
        carry_sel = carry_sel + inc_sel[:, tb - 1:tb]


def _route_call(logits_t, cap):
    bsz, n_e, t = logits_t.shape
    nblk = t // TB_ROUTE
    return pl.pallas_call(
        functools.partial(_route_kernel, cap=cap),
        out_shape=(jax.ShapeDtypeStruct((bsz, nblk, n_e, TB_ROUTE), I32),
                   jax.ShapeDtypeStruct((bsz, nblk, n_e, TB_ROUTE), F32),
                   jax.ShapeDtypeStruct((bsz, nblk, n_e, LANES), I32)),
        grid=(bsz,),
        in_specs=[pl.BlockSpec((1, n_e, t), lambda b: (b, 0, 0))],
        out_specs=(pl.BlockSpec((1, nblk, n_e, TB_ROUTE), lambda b: (b, 0, 0, 0)),
                   pl.BlockSpec((1, nblk, n_e, TB_ROUTE), lambda b: (b, 0, 0, 0)),
                   pl.BlockSpec((1, nblk, n_e, LANES), lambda b: (b, 0, 0, 0))),
        compiler_params=_cparams(("parallel",)),
        name="route",
    )(logits_t)


def _window_rows(ws, win):
    return lax.broadcasted_iota(I32, (win, TB_ROUTE), 0) + ws


def _sweep_window(w, cap, win, prow, rest):
    off = pl.multiple_of(jnp.minimum(w * win, cap - win), WIN_ALIGN)
    rows = _window_rows(off, win)
    return off, (rows == prow) & rest & (rows >= w * win)


def _gather_kernel(ws_ref, nx_ref, hm_ref, pos_ref, gate_ref, xs_ref, gc_ref):
    b = pl.program_id(0)
    g = pl.program_id(1)
    step = pl.program_id(2)
    n_sub, eg, tb = pos_ref.shape[1:]
    n_e = eg * pl.num_programs(1)
    nblk = n_sub * pl.num_programs(2)
    cap = xs_ref.shape[2]
    win = WIN_DISPATCH

    @pl.when(step == 0)
    def _():
        xs_ref[...] = jnp.zeros_like(xs_ref)
        gc_ref[...] = jnp.zeros_like(gc_ref)

    def base(sub):
        return (b * nblk + step * n_sub + sub) * n_e + g * eg

    def tokens(sub):
        return hm_ref[0, sub * tb:(sub + 1) * tb, :]

    for sub in range(n_sub):
        for el in range(eg):
            ws = pl.multiple_of(ws_ref[base(sub) + el], WIN_ALIGN)
            match = _window_rows(ws, win) == pos_ref[0, sub, el:el + 1, :]
            sl = pl.ds(ws, win)
            xs_ref[0, el, sl, :] += jnp.dot(match.astype(BF16), tokens(sub),
                                            preferred_element_type=F32).astype(BF16)
            gc_ref[0, el, sl, :] += jnp.sum(jnp.where(match, gate_ref[0, sub, el:el + 1, :], 0.0),
                                             axis=1, keepdims=True)

    for sub in range(n_sub):
        for el in range(eg):
            @pl.when(nx_ref[base(sub) + el] > 0)
            def _(sub=sub, el=el):
                prow = pos_ref[0, sub, el:el + 1, :]
                grow = gate_ref[0, sub, el:el + 1, :]
                rest = prow >= ws_ref[base(sub) + el] + win

                def body(w, carry):
                    off, m = _sweep_window(w, cap, win, prow, rest)
                    xs_ref[0, el, pl.ds(off, win), :] += jnp.dot(m.astype(BF16), tokens(sub),
                                                                  preferred_element_type=F32).astype(BF16)
                    gc_ref[0, el, pl.ds(off, win), :] += jnp.sum(jnp.where(m, grow, 0.0), axis=1, keepdims=True)
                    return carry

                lax.fori_loop(0, pl.cdiv(cap, win), body, 0)


def _gather_call(ws, nx, hm, pos_b, gate_b, cap):
    bsz, t, d = hm.shape
    nblk, n_e, tb = pos_b.shape[1:]
    eg = n_e // DISPATCH_GROUPS
    n_sub = DISPATCH_STEP_BLOCKS
    grid_spec = pltpu.PrefetchScalarGridSpec(
        num_scalar_prefetch=2,
        grid=(bsz, DISPATCH_GROUPS, nblk // n_sub),
        in_specs=[
            pl.BlockSpec((1, n_sub * tb, d), lambda b, g, k, ws, nx: (b, k, 0)),
            pl.BlockSpec((1, n_sub, eg, tb), lambda b, g, k, ws, nx: (b, k, g, 0)),
            pl.BlockSpec((1, n_sub, eg, tb), lambda b, g, k, ws, nx: (b, k, g, 0)),
        ],
        out_specs=(pl.BlockSpec((1, eg, cap, d), lambda b, g, k, ws, nx: (b, g, 0, 0)),
                   pl.BlockSpec((1, eg, cap, 1), lambda b, g, k, ws, nx: (b, g, 0, 0))),
    )
    return pl.pallas_call(
        _gather_kernel,
        out_shape=(jax.ShapeDtypeStruct((bsz, n_e, cap, d), BF16),
                   jax.ShapeDtypeStruct((bsz, n_e, cap, 1), F32)),
        grid_spec=grid_spec,
        compiler_params=_cparams(("parallel", "parallel", "arbitrary")),
        name="gather",
    )(ws, nx, hm, pos_b, gate_b)


def _ffn_kernel(xs_ref, gc_ref, wg_ref, wu_ref, wd_ref, ye_ref, wgu_s, wd_s):
    f = wd_s.shape[0]
    wgu_s[:, 0:f] = wg_ref[0].astype(BF16)
    wgu_s[:, f:2 * f] = wu_ref[0].astype(BF16)
    wd_s[...] = wd_ref[0].astype(BF16)

    rt = FFN_ROWS
    for b in range(xs_ref.shape[0]):
        for j in range(xs_ref.shape[2] // rt):
            sl = slice(j * rt, (j + 1) * rt)
            xgu = jnp.dot(xs_ref[b, 0, sl, :], wgu_s[...], preferred_element_type=F32)
            hid = (jax.nn.silu(xgu[:, 0:f]) * xgu[:, f:2 * f]).astype(BF16)
            ye = jnp.dot(hid, wd_s[...], preferred_element_type=F32) * gc_ref[b, 0, sl, :]
            ye_ref[b, 0, sl, :] = ye.astype(BF16)


def _ffn_call(xs, gc, w_gate, w_up, w_down):
    bsz, n_e, cap, d = xs.shape
    f = w_gate.shape[2]
    return pl.pallas_call(
        _ffn_kernel,
        out_shape=jax.ShapeDtypeStruct((bsz, n_e, cap, d), BF16),
        grid=(n_e,),
        in_specs=[
            pl.BlockSpec((bsz, 1, cap, d), lambda e: (0, e, 0, 0)),
            pl.BlockSpec((bsz, 1, cap, 1), lambda e: (0, e, 0, 0)),
            pl.BlockSpec((1, d, f), lambda e: (e, 0, 0)),
            pl.BlockSpec((1, d, f), lambda e: (e, 0, 0)),
            pl.BlockSpec((1, f, d), lambda e: (e, 0, 0)),
        ],
        out_specs=pl.BlockSpec((bsz, 1, cap, d), lambda e: (0, e, 0, 0)),
        scratch_shapes=[pltpu.VMEM((d, 2 * f), BF16), pltpu.VMEM((f, d), BF16)],
        compiler_params=_cparams(("parallel",)),
        name="ffn",
    )(xs, gc, w_gate, w_up, w_down)


def _combine_kernel(ws_ref, nx_ref, ye_ref, pos_ref, x1_ref, mod_ref, lnw_ref, lnb_ref, out_ref, acc_scr):
    b = pl.program_id(0)
    step = pl.program_id(1)
    n_sub, n_e, tb = pos_ref.shape[1:]
    nblk = n_sub * pl.num_programs(1)
    cap = ye_ref.shape[2]
    d = D_MODEL
    win = WIN_COMBINE
    g2 = mod_ref[0, :, 5 * d:6 * d]

    for sub in range(n_sub):
        base = (b * nblk + step * n_sub + sub) * n_e
        starts = [pl.multiple_of(ws_ref[base + e], WIN_ALIGN) for e in range(n_e)]
        onehot = jnp.concatenate(
            [(_window_rows(starts[e], win) == pos_ref[0, sub, e:e + 1, :]).astype(BF16) for e in range(n_e)],
            axis=0)
        rows = jnp.concatenate([ye_ref[0, e, pl.ds(starts[e], win), :] for e in range(n_e)], axis=0)
        acc_scr[...] = _dot_t0(onehot, rows)

        for e in range(n_e):
            @pl.when(nx_ref[base + e] > 0)
            def _(e=e, sub=sub, starts=starts):
                prow = pos_ref[0, sub, e:e + 1, :]
                rest = prow >= starts[e] + win

                def body(w, carry):
                    off, m = _sweep_window(w, cap, win, prow, rest)
                    acc_scr[...] += _dot_t0(m.astype(BF16), ye_ref[0, e, pl.ds(off, win), :])
                    return carry

                lax.fori_loop(0, pl.cdiv(cap, win), body, 0)

        tok = slice(sub * tb, (sub + 1) * tb)
        out_ref[0, tok, :] = (_ln(DEEPNORM_ALPHA * x1_ref[0, tok, :] + g2 * acc_scr[...]) * lnw_ref[...]
                              + lnb_ref[...])


def _combine_call(ws, nx, ye, pos_b, x1, mod3, ln2w, ln2b):
    bsz, n_e, cap, d = ye.shape
    nblk, _, tb = pos_b.shape[1:]
    n_sub = COMBINE_STEP_BLOCKS
    grid_spec = pltpu.PrefetchScalarGridSpec(
        num_scalar_prefetch=2,
        grid=(bsz, nblk // n_sub),
        in_specs=[
            pl.BlockSpec((1, n_e, cap, d), lambda b, k, ws, nx: (b, 0, 0, 0), pipeline_mode=pl.Buffered(1)),
            pl.BlockSpec((1, n_sub, n_e, tb), lambda b, k, ws, nx: (b, k, 0, 0)),
            pl.BlockSpec((1, n_sub * tb, d), lambda b, k, ws, nx: (b, k, 0)),
            pl.BlockSpec((1, 1, 6 * d), lambda b, k, ws, nx: (b, 0, 0)),
            pl.BlockSpec((1, d), lambda b, k, ws, nx: (0, 0)),
            pl.BlockSpec((1, d), lambda b, k, ws, nx: (0, 0)),
        ],
        out_specs=pl.BlockSpec((1, n_sub * tb, d), lambda b, k, ws, nx: (b, k, 0)),
        scratch_shapes=[pltpu.VMEM((tb, d), F32)],
    )
    return pl.pallas_call(
        _combine_kernel,
        out_shape=jax.ShapeDtypeStruct((bsz, nblk * tb, d), F32),
        grid_spec=grid_spec,
        compiler_params=_cparams(("parallel", "arbitrary")),
        name="combine",
    )(ws, nx, ye, pos_b, x1, mod3, ln2w, ln2b)


def _rope_tables(n_tokens):
    rows = n_tokens // GRID_W
    n_freq = RET_QK_DIM // 4
    inv = ROPE_THETA ** (-jnp.arange(n_freq, dtype=F32) / n_freq)
    ang_r = jnp.arange(rows, dtype=F32)[:, None] * inv
    ang_c = jnp.arange(GRID_W, dtype=F32)[:, None] * inv
    zr = jnp.zeros((rows, 2 * n_freq), F32)
    zc = jnp.zeros((GRID_W, 2 * n_freq), F32)
    cr, sr, cc, sc = jnp.cos(ang_r), jnp.sin(ang_r), jnp.cos(ang_c), jnp.sin(ang_c)
    cos_t = (jnp.concatenate([cr, cr, zr], axis=-1)[:, None, :]
             + jnp.concatenate([zc, cc, cc], axis=-1)[None, :, :]).reshape(n_tokens, RET_QK_DIM)
    sin_t = (jnp.concatenate([-sr, sr, zr], axis=-1)[:, None, :]
             + jnp.concatenate([zc, -sc, sc], axis=-1)[None, :, :]).reshape(n_tokens, RET_QK_DIM)
    return jnp.concatenate([cos_t, sin_t], axis=-1)


def kernel(x, c, ctx, c_ctx, w_ada, b_ada, w_in, b_in, ret_decay_f, ret_decay_b, ret_gn_w, ret_gn_b,
           sgu_ln_w, sgu_ln_b, sgu_w, sgu_b, w_pa, w_pb, w_o, b_o, ln1_w, ln1_b,
           w_router, b_router, w_gate, w_up, w_down, ln2_w, ln2_b):
    bsz, t, d = x.shape
    assert d == D_MODEL and w_ada.shape[0] == DEPTH == 1 and bsz < MOD_ROWS
    cap = EC_CAPACITY_FACTOR * t // N_EXPERTS
    for win in (WIN_DISPATCH, WIN_COMBINE):
        assert cap >= win and (cap - win) % WIN_ALIGN == 0 and win % WIN_ALIGN == 0
    assert t % (TB_ROUTE * DISPATCH_STEP_BLOCKS) == 0 and t % (TB_ROUTE * COMBINE_STEP_BLOCKS) == 0
    assert N_EXPERTS % DISPATCH_GROUPS == 0
    l = 0

    cvec = jnp.concatenate([c, c_ctx[None], jnp.zeros((MOD_ROWS - bsz - 1, d), F32)], axis=0)
    mod = _mod_call(cvec.T, w_ada[l], b_ada[l][None], bsz + 1)
    mod3 = mod[:, None, :]

    w_in_b = w_in[l].astype(BF16)
    b_in2 = b_in[l][None]
    kctx, vctx = _ctx_kv_call(ctx, mod3, w_in_b, b_in2, bsz)

    z = _inproj_call(x, mod3, w_in_b, b_in2, _rope_tables(t), sgu_ln_w[l][None], sgu_ln_b[l][None])

    lg = jnp.stack([jax.nn.log_sigmoid(ret_decay_f[l].astype(F32)),
                    jax.nn.log_sigmoid(ret_decay_b[l].astype(F32))])
    states = _state_call(lg, z, kctx, vctx)

    x1, hm, logits_t = _mixer_call(
        lg, z, states, x, mod3,
        w_pa[l].astype(BF16), w_pb[l].astype(BF16), w_o[l].astype(BF16),
        sgu_w[l].astype(BF16), sgu_b[l].T, ret_gn_w[l][None], ret_gn_b[l][None], b_o[l][None],
        ln1_w[l][None], ln1_b[l][None], w_router[l].T.astype(BF16), b_router[l][:, None])

    pos_b, gate_b, cnt_b = _route_call(logits_t, cap)

    c0 = cnt_b[..., 0]
    c1 = jnp.concatenate([c0[:, 1:], jnp.full_like(c0[:, :1], cap)], axis=1)

    def windows(win):
        ws = jnp.minimum((c0 // WIN_ALIGN) * WIN_ALIGN, cap - win)
        return ws.reshape(-1), (c1 > ws + win).astype(I32).reshape(-1)

    ws_d, nx_d = windows(WIN_DISPATCH)
    ws_c, nx_c = windows(WIN_COMBINE)

    xs, gc = _gather_call(ws_d, nx_d, hm, pos_b, gate_b, cap)
    ye = _ffn_call(xs, gc, w_gate[l], w_up[l], w_down[l])
    return _combine_call(ws_c, nx_c, ye, pos_b, x1, mod3, ln2_w[l][None], ln2_b[l][None])
```

```python
import functools
import math

import jax
import jax.numpy as jnp
from jax import lax
from jax.experimental import pallas as pl
from jax.experimental.pallas import tpu as pltpu

F32 = jnp.float32
BF16 = jnp.bfloat16
I32 = jnp.int32

D_MODEL = 1024
DEPTH = 1
GRID_W = 64
RET_HEADS = 4
RET_QK_DIM = 128
RET_V_DIM = 256
RET_CHUNK = 128
ROPE_THETA = 10000.0
RET_QK_W = RET_HEADS * RET_QK_DIM
RET_V_W = RET_HEADS * RET_V_DIM
SGU_GROUPS = 4
SGU_CHUNK = 128
SGU_WIDTH = 1024
SGU_GW = SGU_WIDTH // SGU_GROUPS
N_EXPERTS = 16
EC_CAPACITY_FACTOR = 2
LN_EPS = 1e-6
DEEPNORM_ALPHA = (2.0 * DEPTH) ** 0.25
D_IN = 2 * RET_QK_W + 2 * RET_V_W + 2 * SGU_WIDTH + 2 * D_MODEL
Z_K = RET_QK_W
Z_V = 2 * RET_QK_W
Z_SG = Z_V + RET_V_W
Z_U = Z_SG + RET_V_W
Z_VS = Z_U + SGU_WIDTH
Z_GA = Z_VS + SGU_WIDTH
Z_GB = Z_GA + D_MODEL
Z_BLOCK = 3 * 1024

V7X_VMEM_BYTES = 64 * 1024 * 1024
VMEM_LIMIT = V7X_VMEM_BYTES - 4 * 1024 * 1024
LANES = 128

MOD_ROWS = 8
MOD_COLS = 2048
TM_INPROJ = 1024
INPROJ_CHUNK = 256
TT_MIXER = 512
MERGE_ROWS = 512
TB_ROUTE = 256
WIN_DISPATCH = 64
WIN_COMBINE = 64
WIN_ALIGN = 16
DISPATCH_GROUPS = 2
DISPATCH_STEP_BLOCKS = 8
COMBINE_STEP_BLOCKS = 4
STATE_UNROLL = 8
FFN_ROWS = 256


def _cparams(sem):
    return pltpu.CompilerParams(dimension_semantics=sem, vmem_limit_bytes=VMEM_LIMIT)


def _ln(x):
    mu = jnp.mean(x, axis=-1, keepdims=True)
    xc = x - mu
    var = jnp.mean(xc * xc, axis=-1, keepdims=True)
    return xc * lax.rsqrt(var + LN_EPS)


_GELU_A = -2.0 * math.sqrt(2.0 / math.pi) * math.log2(math.e)
_GELU_B = _GELU_A * 0.044715


def _gelu(x):
    return x / (1.0 + jnp.exp2(x * (_GELU_A + _GELU_B * (x * x))))


def _iota_f(shape, dim):
    return lax.broadcasted_iota(I32, shape, dim).astype(F32)


def _mod_kernel(ct_ref, w_ref, b_ref, o_ref, *, n_rows):
    c = ct_ref[...]
    a = c * jax.nn.sigmoid(c)
    w = w_ref[...]
    rows = [jnp.sum(w * a[:, m:m + 1], axis=0, keepdims=True) + b_ref[...] for m in range(n_rows)]
    rows += [jnp.zeros_like(rows[0])] * (MOD_ROWS - n_rows)
    o_ref[...] = jnp.concatenate(rows, axis=0)


def _mod_call(cvec_t, w_ada, b_ada, n_rows):
    d, n = w_ada.shape
    tn = MOD_COLS
    return pl.pallas_call(
        functools.partial(_mod_kernel, n_rows=n_rows),
        out_shape=jax.ShapeDtypeStruct((MOD_ROWS, n), F32),
        grid=(n // tn,),
        in_specs=[
            pl.BlockSpec((d, MOD_ROWS), lambda j: (0, 0)),
            pl.BlockSpec((d, tn), lambda j: (0, j)),
            pl.BlockSpec((1, tn), lambda j: (0, j)),
        ],
        out_specs=pl.BlockSpec((MOD_ROWS, tn), lambda j: (0, j)),
        compiler_params=_cparams(("parallel",)),
        name="mod",
    )(cvec_t, w_ada, b_ada)


def _ctx_kv_kernel(x_ref, mod_ref, wk_ref, wv_ref, bk_ref, bv_ref, k_ref, v_ref):
    d = D_MODEL
    sh = mod_ref[0, :, 0:d]
    sc = mod_ref[0, :, d:2 * d]
    h = (_ln(x_ref[0]) * (1.0 + sc) + sh).astype(BF16)
    k = jnp.dot(h, wk_ref[...], preferred_element_type=F32) + bk_ref[...]
    k_ref[0] = (k * (RET_QK_DIM ** -0.5)).astype(BF16)
    v_ref[0] = (jnp.dot(h, wv_ref[...], preferred_element_type=F32) + bv_ref[...]).astype(BF16)


def _ctx_kv_call(ctx, mod3, w_in, b_in, ctx_row):
    bsz, n_ctx, d = ctx.shape
    kblk = Z_K // RET_QK_W
    vblk = Z_V // RET_V_W
    return pl.pallas_call(
        _ctx_kv_kernel,
        out_shape=(jax.ShapeDtypeStruct((bsz, n_ctx, RET_QK_W), BF16),
                   jax.ShapeDtypeStruct((bsz, n_ctx, RET_V_W), BF16)),
        grid=(bsz,),
        in_specs=[
            pl.BlockSpec((1, n_ctx, d), lambda b: (b, 0, 0)),
            pl.BlockSpec((1, 1, 6 * d), lambda b: (ctx_row, 0, 0)),
            pl.BlockSpec((d, RET_QK_W), lambda b: (0, kblk)),
            pl.BlockSpec((d, RET_V_W), lambda b: (0, vblk)),
            pl.BlockSpec((1, RET_QK_W), lambda b: (0, kblk)),
            pl.BlockSpec((1, RET_V_W), lambda b: (0, vblk)),
        ],
        out_specs=(pl.BlockSpec((1, n_ctx, RET_QK_W), lambda b: (b, 0, 0)),
                   pl.BlockSpec((1, n_ctx, RET_V_W), lambda b: (b, 0, 0))),
        compiler_params=_cparams(("parallel",)),
        name="ctx_kv",
    )(ctx, mod3, w_in, w_in, b_in, b_in)


def _inproj_kernel(x_ref, mod_ref, w_ref, b_ref, rope_ref, lnw_ref, lnb_ref, z_ref):
    d = D_MODEL
    sh = mod_ref[0, :, 0:d]
    sc = mod_ref[0, :, d:2 * d]
    h = (_ln(x_ref[0]) * (1.0 + sc) + sh).astype(BF16)

    def proj(c0, width):
        return jnp.dot(h, w_ref[:, c0:c0 + width], preferred_element_type=F32) + b_ref[:, c0:c0 + width]

    zq = proj(0, 2 * RET_QK_W)
    cos = rope_ref[:, 0:LANES]
    sin = rope_ref[:, LANES:2 * LANES]
    tm = zq.shape[0]
    upper = (lax.broadcasted_iota(I32, (tm, LANES), 1) & 32) != 0
    for hb in range(2 * RET_HEADS):
        zs = zq[:, hb * LANES:(hb + 1) * LANES]
        if hb >= RET_HEADS:
            zs = zs * (RET_QK_DIM ** -0.5)
        sw = jnp.where(upper, pltpu.roll(zs, 32, 1), pltpu.roll(zs, LANES - 32, 1))
        z_ref[0, :, hb * LANES:(hb + 1) * LANES] = (zs * cos + sw * sin).astype(BF16)

    def emit(c0, width, fn):
        for cc in range(c0, c0 + width, INPROJ_CHUNK):
            z_ref[0, :, cc:cc + INPROJ_CHUNK] = fn(proj(cc, INPROJ_CHUNK)).astype(BF16)

    vs = _ln(_gelu(proj(Z_VS, SGU_WIDTH))) * lnw_ref[...] + lnb_ref[...]
    z_ref[0, :, Z_VS:Z_VS + SGU_WIDTH] = vs.astype(BF16)
    emit(Z_U, SGU_WIDTH, _gelu)
    emit(Z_SG, RET_V_W, jax.nn.silu)
    emit(Z_GA, d, jax.nn.sigmoid)
    emit(Z_GB, d, jax.nn.sigmoid)
    emit(Z_V, RET_V_W, lambda z: z)


def _inproj_call(x, mod3, w_in, b_in, rope_t, sgu_ln_w, sgu_ln_b):
    bsz, t, d = x.shape
    tm = TM_INPROJ
    row = lambda b, i: (b, i, 0)
    const2 = lambda b, i: (0, 0)
    return pl.pallas_call(
        _inproj_kernel,
        out_shape=jax.ShapeDtypeStruct((bsz, t, D_IN), BF16),
        grid=(bsz, t // tm),
        in_specs=[
            pl.BlockSpec((1, tm, d), row),
            pl.BlockSpec((1, 1, 6 * d), lambda b, i: (b, 0, 0)),
            pl.BlockSpec((d, D_IN), const2, pipeline_mode=pl.Buffered(1)),
            pl.BlockSpec((1, D_IN), const2),
            pl.BlockSpec((tm, 2 * LANES), lambda b, i: (i, 0)),
            pl.BlockSpec((1, SGU_WIDTH), const2),
            pl.BlockSpec((1, SGU_WIDTH), const2),
        ],
        out_specs=pl.BlockSpec((1, tm, D_IN), row),
        compiler_params=_cparams(("parallel", "parallel")),
        name="inproj",
    )(x, mod3, w_in, b_in, rope_t, sgu_ln_w, sgu_ln_b)


def _dot_t0(a, b):
    return lax.dot_general(a, b, (((0,), (0,)), ((), ())), preferred_element_type=F32)


def _state_kernel(lg_ref, k_ref, v_ref, kc_ref, vc_ref, s_ref):
    hd = pl.program_id(1)
    lgf = lg_ref[0, hd]
    lgb = lg_ref[1, hd]
    n_chunks = s_ref.shape[2]
    n_ctx = kc_ref.shape[1]
    c = RET_CHUNK
    li = _iota_f((c, RET_QK_DIM), 0)
    zeta_f = jnp.exp((c - 1.0 - li) * lgf)
    zeta_b = jnp.exp(li * lgb)
    one = jnp.ones((1, 1), F32)
    cd_f = jnp.exp(one * (c * lgf))
    cd_b = jnp.exp(one * (c * lgb))

    tc = _iota_f((n_ctx, RET_QK_DIM), 0)
    kc = kc_ref[0].astype(F32)
    vc = vc_ref[0]
    s0f = _dot_t0((kc * jnp.exp((n_ctx - 1.0 - tc) * lgf)).astype(BF16), vc)
    s0b = _dot_t0((kc * jnp.exp(tc * lgb)).astype(BF16), vc)

    def chunk_kv(n, zeta):
        off = pl.multiple_of(n * c, c)
        kk = k_ref[0, pl.ds(off, c), :].astype(F32)
        return _dot_t0((kk * zeta).astype(BF16), v_ref[0, pl.ds(off, c), :])

    def step(i, carry):
        sf, sb = carry
        nb = n_chunks - 1 - i
        s_ref[0, 0, i, 0:RET_QK_DIM, :] = sf.astype(BF16)
        s_ref[0, 0, nb, RET_QK_DIM:2 * RET_QK_DIM, :] = sb.astype(BF16)
        return cd_f * sf + chunk_kv(i, zeta_f), cd_b * sb + chunk_kv(nb, zeta_b)

    lax.fori_loop(0, n_chunks, step, (s0f, s0b), unroll=STATE_UNROLL)


def _state_call(lg, z, kctx, vctx):
    bsz, t, _ = z.shape
    n_ctx = kctx.shape[1]
    n_chunks = t // RET_CHUNK
    return pl.pallas_call(
        _state_kernel,
        out_shape=jax.ShapeDtypeStruct((bsz, RET_HEADS, n_chunks, 2 * RET_QK_DIM, RET_V_DIM), BF16),
        grid=(bsz, RET_HEADS),
        in_specs=[
            pl.BlockSpec(memory_space=pltpu.SMEM),
            pl.BlockSpec((1, t, RET_QK_DIM), lambda b, h: (b, 0, RET_QK_W // RET_QK_DIM + h)),
            pl.BlockSpec((1, t, RET_V_DIM), lambda b, h: (b, 0, Z_V // RET_V_DIM + h)),
            pl.BlockSpec((1, n_ctx, RET_QK_DIM), lambda b, h: (b, 0, h)),
            pl.BlockSpec((1, n_ctx, RET_V_DIM), lambda b, h: (b, 0, h)),
        ],
        out_specs=pl.BlockSpec((1, 1, n_chunks, 2 * RET_QK_DIM, RET_V_DIM), lambda b, h: (b, h, 0, 0, 0)),
        compiler_params=_cparams(("parallel", "parallel")),
        name="states",
    )(lg, z, z, kctx, vctx)


def _mixer_kernel(lg_ref, za_ref, zb_ref, gb_ref, s_ref, x_ref, mod_ref,
                  wpa_ref, wpb_ref, wo_ref, sguw_ref, sgub_ref, gnw_ref, gnb_ref, bo_ref,
                  ln1w_ref, ln1b_ref, wr_ref, br_ref,
                  x1_ref, hm_ref, lgt_ref, ret_scr, sgu_scr):
    d = D_MODEL
    c = RET_CHUNK

    n_sub = za_ref.shape[1] // c
    row = lax.broadcasted_iota(I32, (c, c), 0)
    col = lax.broadcasted_iota(I32, (c, c), 1)
    diff = (row - col).astype(F32)
    rowq = _iota_f((c, RET_QK_DIM), 0)
    for hd in range(RET_HEADS):
        lgf = lg_ref[0, hd]
        lgb = lg_ref[1, hd]
        vcols = slice(hd * RET_V_DIM, (hd + 1) * RET_V_DIM)
        mask = (jnp.where(diff >= 0, jnp.exp(jnp.maximum(diff, 0.0) * lgf), 0.0)
                + jnp.where(diff <= 0, jnp.exp(jnp.maximum(-diff, 0.0) * lgb), 0.0))
        xi_f = jnp.exp((rowq + 1.0) * lgf)
        xi_b = jnp.exp((c - rowq) * lgb)
        for ci in range(n_sub):
            r0 = ci * c
            q = za_ref[0, r0:r0 + c, hd * RET_QK_DIM:(hd + 1) * RET_QK_DIM]
            k = za_ref[0, r0:r0 + c, RET_QK_W + hd * RET_QK_DIM:RET_QK_W + (hd + 1) * RET_QK_DIM]
            s = lax.dot_general(q, k, (((1,), (1,)), ((), ())), preferred_element_type=F32)
            vv = za_ref[0, r0:r0 + c, Z_V + hd * RET_V_DIM:Z_V + (hd + 1) * RET_V_DIM]
            intra = jnp.dot((s * mask).astype(BF16), vv, preferred_element_type=F32)
            qf = q.astype(F32)
            qx = jnp.concatenate([(qf * xi_f).astype(BF16), (qf * xi_b).astype(BF16)], axis=1)
            cross = jnp.dot(qx, s_ref[0, hd, ci], preferred_element_type=F32)
            o = _ln(intra + cross) * gnw_ref[:, vcols] + gnb_ref[:, vcols]
            gate = za_ref[0, r0:r0 + c, Z_SG + hd * RET_V_DIM:Z_SG + (hd + 1) * RET_V_DIM]
            ret_scr[r0:r0 + c, vcols] = (o * gate.astype(F32)).astype(BF16)

    for g in range(SGU_GROUPS):
        gcols = slice(g * SGU_GW, (g + 1) * SGU_GW)
        wg = sguw_ref[g]
        bg = sgub_ref[:, g:g + 1]
        for ci in range(n_sub):
            r0 = ci * c
            vsb = zb_ref[0, r0:r0 + c, Z_VS - Z_BLOCK + g * SGU_GW:Z_VS - Z_BLOCK + (g + 1) * SGU_GW]
            ub = zb_ref[0, r0:r0 + c, Z_U - Z_BLOCK + g * SGU_GW:Z_U - Z_BLOCK + (g + 1) * SGU_GW]
            sp = jnp.dot(wg, vsb, preferred_element_type=F32) + bg
            sgu_scr[r0:r0 + c, gcols] = (ub.astype(F32) * sp).astype(BF16)

    g1 = mod_ref[0, :, 2 * d:3 * d]
    sh2 = mod_ref[0, :, 3 * d:4 * d]
    sc2 = mod_ref[0, :, 4 * d:5 * d]
    for r0 in range(0, za_ref.shape[1], MERGE_ROWS):
        rows = slice(r0, r0 + MERGE_ROWS)
        pa = jnp.dot(ret_scr[rows, :], wpa_ref[...], preferred_element_type=F32)
        pb = jnp.dot(sgu_scr[rows, :], wpb_ref[...], preferred_element_type=F32)
        ga = zb_ref[0, rows, Z_GA - Z_BLOCK:Z_GA - Z_BLOCK + d]
        y = (ga.astype(F32) * pa + gb_ref[0, rows, :].astype(F32) * pb).astype(BF16)
        mix = jnp.dot(y, wo_ref[...], preferred_element_type=F32) + bo_ref[...]
        x1 = _ln(DEEPNORM_ALPHA * x_ref[0, rows, :] + g1 * mix) * ln1w_ref[...] + ln1b_ref[...]
        x1_ref[0, rows, :] = x1
        hm = (_ln(x1) * (1.0 + sc2) + sh2).astype(BF16)
        hm_ref[0, rows, :] = hm
        lgt_ref[0, :, rows] = lax.dot_general(wr_ref[...], hm, (((1,), (1,)), ((), ())),
                                              preferred_element_type=F32) + br_ref[...]


def _mixer_call(lg, z, states, x, mod3, wpa, wpb, wo, sguw, sgub_t, gnw, gnb, bo,
                ln1w, ln1b, wr_t, br):
    bsz, t, d = x.shape
    tt = TT_MIXER
    n_sub = tt // RET_CHUNK
    row = lambda b, i: (b, i, 0)
    c2 = lambda b, i: (0, 0)
    c3 = lambda b, i: (0, 0, 0)
    return pl.pallas_call(
        _mixer_kernel,
        out_shape=(jax.ShapeDtypeStruct((bsz, t, d), F32),
                   jax.ShapeDtypeStruct((bsz, t, d), BF16),
                   jax.ShapeDtypeStruct((bsz, N_EXPERTS, t), F32)),
        grid=(bsz, t // tt),
        in_specs=[
            pl.BlockSpec(memory_space=pltpu.SMEM),
            pl.BlockSpec((1, tt, Z_BLOCK), lambda b, i: (b, i, 0)),
            pl.BlockSpec((1, tt, Z_BLOCK), lambda b, i: (b, i, 1)),
            pl.BlockSpec((1, tt, d), lambda b, i: (b, i, Z_GB // d)),
            pl.BlockSpec((1, RET_HEADS, n_sub, 2 * RET_QK_DIM, RET_V_DIM), lambda b, i: (b, 0, i, 0, 0)),
            pl.BlockSpec((1, tt, d), row),
            pl.BlockSpec((1, 1, 6 * d), lambda b, i: (b, 0, 0)),
            pl.BlockSpec((RET_V_W, d), c2, pipeline_mode=pl.Buffered(1)),
            pl.BlockSpec((SGU_WIDTH, d), c2, pipeline_mode=pl.Buffered(1)),
            pl.BlockSpec((d, d), c2, pipeline_mode=pl.Buffered(1)),
            pl.BlockSpec((SGU_GROUPS, SGU_CHUNK, SGU_CHUNK), c3),
            pl.BlockSpec((SGU_CHUNK, SGU_GROUPS), c2),
            pl.BlockSpec((1, RET_V_W), c2),
            pl.BlockSpec((1, RET_V_W), c2),
            pl.BlockSpec((1, d), c2),
            pl.BlockSpec((1, d), c2),
            pl.BlockSpec((1, d), c2),
            pl.BlockSpec((N_EXPERTS, d), c2),
            pl.BlockSpec((N_EXPERTS, 1), c2),
        ],
        out_specs=(pl.BlockSpec((1, tt, d), row),
                   pl.BlockSpec((1, tt, d), row),
                   pl.BlockSpec((1, N_EXPERTS, tt), lambda b, i: (b, 0, i))),
        scratch_shapes=[pltpu.VMEM((tt, RET_V_W), BF16), pltpu.VMEM((tt, SGU_WIDTH), BF16)],
        compiler_params=_cparams(("parallel", "parallel")),
        name="mixer",
    )(lg, z, z, z, states, x, mod3, wpa, wpb, wo, sguw, sgub_t, gnw, gnb, bo,
      ln1w, ln1b, wr_t, br)


def _route_kernel(lgt_ref, pos_ref, gate_ref, cnt_ref, *, cap):
    lg = lgt_ref[0]
    n_e, t = lg.shape
    tb = TB_ROUTE
    m = jnp.max(lg, axis=0, keepdims=True)
    ex = jnp.exp(lg - m)
    aff = ex / jnp.sum(ex, axis=0, keepdims=True)

    def search(i, thr_bits):
        cand = thr_bits | lax.shift_left(jnp.int32(1), 30 - i)
        cnt = jnp.sum((aff >= lax.bitcast_convert_type(cand, F32)).astype(I32), axis=1, keepdims=True)
        return jnp.where(cnt >= cap, cand, thr_bits)

    thr_bits = lax.fori_loop(0, 31, search, jnp.zeros((n_e, 1), I32))
    floor_f = lax.bitcast_convert_type(thr_bits, F32)
    thr = jnp.min(jnp.where(aff >= floor_f, aff, jnp.inf), axis=1, keepdims=True)
    need = (cap - jnp.sum((aff > thr).astype(I32), axis=1, keepdims=True)).astype(F32)

    r = lax.broadcasted_iota(I32, (tb, tb), 0)
    cc = lax.broadcasted_iota(I32, (tb, tb), 1)
    tri = (r <= cc).astype(BF16)
    carry_eq = jnp.zeros((n_e, 1), F32)
    carry_sel = jnp.zeros((n_e, 1), F32)
    for blk in range(t // tb):
        sl = slice(blk * tb, (blk + 1) * tb)
        aff_b = aff[:, sl]
        eq = aff_b == thr
        eq_b = eq.astype(BF16)
        inc_eq = jnp.dot(eq_b, tri, preferred_element_type=F32)
        before = carry_eq + inc_eq - eq_b.astype(F32)
        sel = (aff_b > thr) | (eq & (before < need))
        sel_b = sel.astype(BF16)
        inc_sel = jnp.dot(sel_b, tri, preferred_element_type=F32)
        pos = carry_sel + inc_sel - 1.0
        pos_ref[0, blk] = jnp.where(sel, pos.astype(I32), -1)
        gate_ref[0, blk] = aff_b
        cnt_ref[0, blk] = jnp.broadcast_to(carry_sel, (n_e, LANES)).astype(I32)
        carry_eq = carry_eq + inc_eq[:, tb - 1:tb]
        carry_sel = carry_sel + inc_sel[:, tb - 1:tb]


def _route_call(logits_t, cap):
    bsz, n_e, t = logits_t.shape
    nblk = t // TB_ROUTE
    return pl.pallas_call(
        functools.partial(_route_kernel, cap=cap),
        out_shape=(jax.ShapeDtypeStruct((bsz, nblk, n_e, TB_ROUTE), I32),
                   jax.ShapeDtypeStruct((bsz, nblk, n_e, TB_ROUTE), F32),
                   jax.ShapeDtypeStruct((bsz, nblk, n_e, LANES), I32)),
        grid=(bsz,),
        in_specs=[pl.BlockSpec((1, n_e, t), lambda b: (b, 0, 0))],
        out_specs=(pl.BlockSpec((1, nblk, n_e, TB_ROUTE), lambda b: (b, 0, 0, 0)),
                   pl.BlockSpec((1, nblk, n_e, TB_ROUTE), lambda b: (b, 0, 0, 0)),
                   pl.BlockSpec((1, nblk, n_e, LANES), lambda b: (b, 0, 0, 0))),
        compiler_params=_cparams(("parallel",)),
        name="route",
    )(logits_t)


def _window_rows(ws, win):
    return lax.broadcasted_iota(I32, (win, TB_ROUTE), 0) + ws


def _sweep_window(w, cap, win, prow, rest):
    off = pl.multiple_of(jnp.minimum(w * win, cap - win), WIN_ALIGN)
    rows = _window_rows(off, win)
    return off, (rows == prow) & rest & (rows >= w * win)


def _gather_kernel(ws_ref, nx_ref, hm_ref, pos_ref, gate_ref, xs_ref, gc_ref):
    b = pl.program_id(0)
    g = pl.program_id(1)
    step = pl.program_id(2)
    n_sub, eg, tb = pos_ref.shape[1:]
    n_e = eg * pl.num_programs(1)
    nblk = n_sub * pl.num_programs(2)
    cap = xs_ref.shape[2]
    win = WIN_DISPATCH

    @pl.when(step == 0)
    def _():
        xs_ref[...] = jnp.zeros_like(xs_ref)
        gc_ref[...] = jnp.zeros_like(gc_ref)

    def base(sub):
        return (b * nblk + step * n_sub + sub) * n_e + g * eg

    def tokens(sub):
        return hm_ref[0, sub * tb:(sub + 1) * tb, :]

    for sub in range(n_sub):
        for el in range(eg):
            ws = pl.multiple_of(ws_ref[base(sub) + el], WIN_ALIGN)
            match = _window_rows(ws, win) == pos_ref[0, sub, el:el + 1, :]
            sl = pl.ds(ws, win)
            xs_ref[0, el, sl, :] += jnp.dot(match.astype(BF16), tokens(sub),
                                            preferred_element_type=F32).astype(BF16)
            gc_ref[0, el, sl, :] += jnp.sum(jnp.where(match, gate_ref[0, sub, el:el + 1, :], 0.0),
                                             axis=1, keepdims=True)

    for sub in range(n_sub):
        for el in range(eg):
            @pl.when(nx_ref[base(sub) + el] > 0)
            def _(sub=sub, el=el):
                prow = pos_ref[0, sub, el:el + 1, :]
                grow = gate_ref[0, sub, el:el + 1, :]
                rest = prow >= ws_ref[base(sub) + el] + win

                def body(w, carry):
                    off, m = _sweep_window(w, cap, win, prow, rest)
                    xs_ref[0, el, pl.ds(off, win), :] += jnp.dot(m.astype(BF16), tokens(sub),
                                                                  preferred_element_type=F32).astype(BF16)
                    gc_ref[0, el, pl.ds(off, win), :] += jnp.sum(jnp.where(m, grow, 0.0), axis=1, keepdims=True)
                    return carry

                lax.fori_loop(0, pl.cdiv(cap, win), body, 0)


def _gather_call(ws, nx, hm, pos_b, gate_b, cap):
    bsz, t, d = hm.shape
    nblk, n_e, tb = pos_b.shape[1:]
    eg = n_e // DISPATCH_GROUPS
    n_sub = DISPATCH_STEP_BLOCKS
    grid_spec = pltpu.PrefetchScalarGridSpec(
        num_scalar_prefetch=2,
        grid=(bsz, DISPATCH_GROUPS, nblk // n_sub),
        in_specs=[
            pl.BlockSpec((1, n_sub * tb, d), lambda b, g, k, ws, nx: (b, k, 0)),
            pl.BlockSpec((1, n_sub, eg, tb), lambda b, g, k, ws, nx: (b, k, g, 0)),
            pl.BlockSpec((1, n_sub, eg, tb), lambda b, g, k, ws, nx: (b, k, g, 0)),
        ],
        out_specs=(pl.BlockSpec((1, eg, cap, d), lambda b, g, k, ws, nx: (b, g, 0, 0)),
                   pl.BlockSpec((1, eg, cap, 1), lambda b, g, k, ws, nx: (b, g, 0, 0))),
    )
    return pl.pallas_call(
        _gather_kernel,
        out_shape=(jax.ShapeDtypeStruct((bsz, n_e, cap, d), BF16),
                   jax.ShapeDtypeStruct((bsz, n_e, cap, 1), F32)),
        grid_spec=grid_spec,
        compiler_params=_cparams(("parallel", "parallel", "arbitrary")),
        name="gather",
    )(ws, nx, hm, pos_b, gate_b)


def _ffn_kernel(xs_ref, gc_ref, wg_ref, wu_ref, wd_ref, ye_ref, wgu_s, wd_s):
    f = wd_s.shape[0]
    wgu_s[:, 0:f] = wg_ref[0].astype(BF16)
    wgu_s[:, f:2 * f] = wu_ref[0].astype(BF16)
    wd_s[...] = wd_ref[0].astype(BF16)

    rt = FFN_ROWS
    for b in range(xs_ref.shape[0]):
        for j in range(xs_ref.shape[2] // rt):
            sl = slice(j * rt, (j + 1) * rt)
            xgu = jnp.dot(xs_ref[b, 0, sl, :], wgu_s[...], preferred_element_type=F32)
            hid = (jax.nn.silu(xgu[:, 0:f]) * xgu[:, f:2 * f]).astype(BF16)
            ye = jnp.dot(hid, wd_s[...], preferred_element_type=F32) * gc_ref[b, 0, sl, :]
            ye_ref[b, 0, sl, :] = ye.astype(BF16)


def _ffn_call(xs, gc, w_gate, w_up, w_down):
    bsz, n_e, cap, d = xs.shape
    f = w_gate.shape[2]
    return pl.pallas_call(
        _ffn_kernel,
        out_shape=jax.ShapeDtypeStruct((bsz, n_e, cap, d), BF16),
        grid=(n_e,),
        in_specs=[
            pl.BlockSpec((bsz, 1, cap, d), lambda e: (0, e, 0, 0)),
            pl.BlockSpec((bsz, 1, cap, 1), lambda e: (0, e, 0, 0)),
            pl.BlockSpec((1, d, f), lambda e: (e, 0, 0)),
            pl.BlockSpec((1, d, f), lambda e: (e, 0, 0)),
            pl.BlockSpec((1, f, d), lambda e: (e, 0, 0)),
        ],
        out_specs=pl.BlockSpec((bsz, 1, cap, d), lambda e: (0, e, 0, 0)),
        scratch_shapes=[pltpu.VMEM((d, 2 * f), BF16), pltpu.VMEM((f, d), BF16)],
        compiler_params=_cparams(("parallel",)),
        name="ffn",
    )(xs, gc, w_gate, w_up, w_down)


def _combine_kernel(ws_ref, nx_ref, ye_ref, pos_ref, x1_ref, mod_ref, lnw_ref, lnb_ref, out_ref, acc_scr):
    b = pl.program_id(0)
    step = pl.program_id(1)
    n_sub, n_e, tb = pos_ref.shape[1:]
    nblk = n_sub * pl.num_programs(1)
    cap = ye_ref.shape[2]
    d = D_MODEL
    win = WIN_COMBINE
    g2 = mod_ref[0, :, 5 * d:6 * d]

    for sub in range(n_sub):
        base = (b * nblk + step * n_sub + sub) * n_e
        starts = [pl.multiple_of(ws_ref[base + e], WIN_ALIGN) for e in range(n_e)]
        onehot = jnp.concatenate(
            [(_window_rows(starts[e], win) == pos_ref[0, sub, e:e + 1, :]).astype(BF16) for e in range(n_e)],
            axis=0)
        rows = jnp.concatenate([ye_ref[0, e, pl.ds(starts[e], win), :] for e in range(n_e)], axis=0)
        acc_scr[...] = _dot_t0(onehot, rows)

        for e in range(n_e):
            @pl.when(nx_ref[base + e] > 0)
            def _(e=e, sub=sub, starts=starts):
                prow = pos_ref[0, sub, e:e + 1, :]
                rest = prow >= starts[e] + win

                def body(w, carry):
                    off, m = _sweep_window(w, cap, win, prow, rest)
                    acc_scr[...] += _dot_t0(m.astype(BF16), ye_ref[0, e, pl.ds(off, win), :])
                    return carry

                lax.fori_loop(0, pl.cdiv(cap, win), body, 0)

        tok = slice(sub * tb, (sub + 1) * tb)
        out_ref[0, tok, :] = (_ln(DEEPNORM_ALPHA * x1_ref[0, tok, :] + g2 * acc_scr[...]) * lnw_ref[...]
                              + lnb_ref[...])


def _combine_call(ws, nx, ye, pos_b, x1, mod3, ln2w, ln2b):
    bsz, n_e, cap, d = ye.shape
    nblk, _, tb = pos_b.shape[1:]
    n_sub = COMBINE_STEP_BLOCKS
    grid_spec = pltpu.PrefetchScalarGridSpec(
        num_scalar_prefetch=2,
        grid=(bsz, nblk // n_sub),
        in_specs=[
            pl.BlockSpec((1, n_e, cap, d), lambda b, k, ws, nx: (b, 0, 0, 0), pipeline_mode=pl.Buffered(1)),
            pl.BlockSpec((1, n_sub, n_e, tb), lambda b, k, ws, nx: (b, k, 0, 0)),
            pl.BlockSpec((1, n_sub * tb, d), lambda b, k, ws, nx: (b, k, 0)),
            pl.BlockSpec((1, 1, 6 * d), lambda b, k, ws, nx: (b, 0, 0)),
            pl.BlockSpec((1, d), lambda b, k, ws, nx: (0, 0)),
            pl.BlockSpec((1, d), lambda b, k, ws, nx: (0, 0)),
        ],
        out_specs=pl.BlockSpec((1, n_sub * tb, d), lambda b, k, ws, nx: (b, k, 0)),
        scratch_shapes=[pltpu.VMEM((tb, d), F32)],
    )
    return pl.pallas_call(
        _combine_kernel,
        out_shape=jax.ShapeDtypeStruct((bsz, nblk * tb, d), F32),
        grid_spec=grid_spec,
        compiler_params=_cparams(("parallel", "arbitrary")),
        name="combine",
    )(ws, nx, ye, pos_b, x1, mod3, ln2w, ln2b)


def _rope_tables(n_tokens):
    rows = n_tokens // GRID_W
    n_freq = RET_QK_DIM // 4
    inv = ROPE_THETA ** (-jnp.arange(n_freq, dtype=F32) / n_freq)
    ang_r = jnp.arange(rows, dtype=F32)[:, None] * inv
    ang_c = jnp.arange(GRID_W, dtype=F32)[:, None] * inv
    zr = jnp.zeros((rows, 2 * n_freq), F32)
    zc = jnp.zeros((GRID_W, 2 * n_freq), F32)
    cr, sr, cc, sc = jnp.cos(ang_r), jnp.sin(ang_r), jnp.cos(ang_c), jnp.sin(ang_c)
    cos_t = (jnp.concatenate([cr, cr, zr], axis=-1)[:, None, :]
             + jnp.concatenate([zc, cc, cc], axis=-1)[None, :, :]).reshape(n_tokens, RET_QK_DIM)
    sin_t = (jnp.concatenate([-sr, sr, zr], axis=-1)[:, None, :]
             + jnp.concatenate([zc, -sc, sc], axis=-1)[None, :, :]).reshape(n_tokens, RET_QK_DIM)
    return jnp.concatenate([cos_t, sin_t], axis=-1)


def kernel(x, c, ctx, c_ctx, w_ada, b_ada, w_in, b_in, ret_decay_f, ret_decay_b, ret_gn_w, ret_gn_b,
           sgu_ln_w, sgu_ln_b, sgu_w, sgu_b, w_pa, w_pb, w_o, b_o, ln1_w, ln1_b,
           w_router, b_router, w_gate, w_up, w_down, ln2_w, ln2_b):
    bsz, t, d = x.shape
    assert d == D_MODEL and w_ada.shape[0] == DEPTH == 1 and bsz < MOD_ROWS
    cap = EC_CAPACITY_FACTOR * t // N_EXPERTS
    for win in (WIN_DISPATCH, WIN_COMBINE):
        assert cap >= win and (cap - win) % WIN_ALIGN == 0 and win % WIN_ALIGN == 0
    assert t % (TB_ROUTE * DISPATCH_STEP_BLOCKS) == 0 and t % (TB_ROUTE * COMBINE_STEP_BLOCKS) == 0
    assert N_EXPERTS % DISPATCH_GROUPS == 0
    l = 0

    cvec = jnp.concatenate([c, c_ctx[None], jnp.zeros((MOD_ROWS - bsz - 1, d), F32)], axis=0)
    mod = _mod_call(cvec.T, w_ada[l], b_ada[l][None], bsz + 1)
    mod3 = mod[:, None, :]

    w_in_b = w_in[l].astype(BF16)
    b_in2 = b_in[l][None]
    kctx, vctx = _ctx_kv_call(ctx, mod3, w_in_b, b_in2, bsz)

    z = _inproj_call(x, mod3, w_in_b, b_in2, _rope_tables(t), sgu_ln_w[l][None], sgu_ln_b[l][None])

    lg = jnp.stack([jax.nn.log_sigmoid(ret_decay_f[l].astype(F32)),
                    jax.nn.log_sigmoid(ret_decay_b[l].astype(F32))])
    states = _state_call(lg, z, kctx, vctx)

    x1, hm, logits_t = _mixer_call(
        lg, z, states, x, mod3,
        w_pa[l].astype(BF16), w_pb[l].astype(BF16), w_o[l].astype(BF16),
        sgu_w[l].astype(BF16), sgu_b[l].T, ret_gn_w[l][None], ret_gn_b[l][None], b_o[l][None],
        ln1_w[l][None], ln1_b[l][None], w_router[l].T.astype(BF16), b_router[l][:, None])

    pos_b, gate_b, cnt_b = _route_call(logits_t, cap)

    c0 = cnt_b[..., 0]
    c1 = jnp.concatenate([c0[:, 1:], jnp.full_like(c0[:, :1], cap)], axis=1)

    def windows(win):
        ws = jnp.minimum((c0 // WIN_ALIGN) * WIN_ALIGN, cap - win)
        return ws.reshape(-1), (c1 > ws + win).astype(I32).reshape(-1)

    ws_d, nx_d = windows(WIN_DISPATCH)
    ws_c, nx_c = windows(WIN_COMBINE)

    xs, gc = _gather_call(ws_d, nx_d, hm, pos_b, gate_b, cap)
    ye = _ffn_call(xs, gc, w_gate[l], w_up[l], w_down[l])
    return _combine_call(ws_c, nx_c, ye, pos_b, x1, mod3, ln2_w[l][None], ln2_b[l][None])
```

```python
import functools
import math

import jax
import jax.numpy as jnp
from jax import lax
from jax.experimental import pallas as pl
from jax.experimental.pallas import tpu as pltpu

F32 = jnp.float32
BF16 = jnp.bfloat16
I32 = jnp.int32

D_MODEL = 1024
DEPTH = 1
GRID_W = 64
RET_HEADS = 4
RET_QK_DIM = 128
RET_V_DIM = 256
RET_CHUNK = 128
ROPE_THETA = 10000.0
RET_QK_W = RET_HEADS * RET_QK_DIM
RET_V_W = RET_HEADS * RET_V_DIM
SGU_GROUPS = 4
SGU_CHUNK = 128
SGU_WIDTH = 1024
SGU_GW = SGU_WIDTH // SGU_GROUPS
N_EXPERTS = 16
EC_CAPACITY_FACTOR = 2
LN_EPS = 1e-6
DEEPNORM_ALPHA = (2.0 * DEPTH) ** 0.25
D_IN = 2 * RET_QK_W + 2 * RET_V_W + 2 * SGU_WIDTH + 2 * D_MODEL
Z_K = RET_QK_W
Z_V = 2 * RET_QK_W
Z_SG = Z_V + RET_V_W
Z_U = Z_SG + RET_V_W
Z_VS = Z_U + SGU_WIDTH
Z_GA = Z_VS + SGU_WIDTH
Z_GB = Z_GA + D_MODEL
Z_BLOCK = 3 * 1024

V7X_VMEM_BYTES = 64 * 1024 * 1024
VMEM_LIMIT = V7X_VMEM_BYTES - 4 * 1024 * 1024
LANES = 128

MOD_ROWS = 8
MOD_COLS = 2048
TM_INPROJ = 1024
INPROJ_CHUNK = 256
TT_MIXER = 512
MERGE_ROWS = 512
TB_ROUTE = 256
DISPATCH_MERGE = 2
WIN_DISPATCH = 128
WIN_COMBINE = 64
WIN_ALIGN = 16
DISPATCH_GROUPS = 2
DISPATCH_STEP_WINDOWS = 4
COMBINE_STEP_BLOCKS = 4
STATE_UNROLL = 8
FFN_ROWS = 256


def _cparams(sem):
    return pltpu.CompilerParams(dimension_semantics=sem, vmem_limit_bytes=VMEM_LIMIT)


def _ln(x):
    mu = jnp.mean(x, axis=-1, keepdims=True)
    xc = x - mu
    var = jnp.mean(xc * xc, axis=-1, keepdims=True)
    return xc * lax.rsqrt(var + LN_EPS)


_GELU_A = -2.0 * math.sqrt(2.0 / math.pi) * math.log2(math.e)
_GELU_B = _GELU_A * 0.044715


def _gelu(x):
    return x / (1.0 + jnp.exp2(x * (_GELU_A + _GELU_B * (x * x))))


def _iota_f(shape, dim):
    return lax.broadcasted_iota(I32, shape, dim).astype(F32)


def _mod_kernel(ct_ref, w_ref, b_ref, o_ref, *, n_rows):
    c = ct_ref[...]
    a = c * jax.nn.sigmoid(c)
    w = w_ref[...]
    rows = [jnp.sum(w * a[:, m:m + 1], axis=0, keepdims=True) + b_ref[...] for m in range(n_rows)]
    rows += [jnp.zeros_like(rows[0])] * (MOD_ROWS - n_rows)
    o_ref[...] = jnp.concatenate(rows, axis=0)


def _mod_call(cvec_t, w_ada, b_ada, n_rows):
    d, n = w_ada.shape
    tn = MOD_COLS
    return pl.pallas_call(
        functools.partial(_mod_kernel, n_rows=n_rows),
        out_shape=jax.ShapeDtypeStruct((MOD_ROWS, n), F32),
        grid=(n // tn,),
        in_specs=[
            pl.BlockSpec((d, MOD_ROWS), lambda j: (0, 0)),
            pl.BlockSpec((d, tn), lambda j: (0, j)),
            pl.BlockSpec((1, tn), lambda j: (0, j)),
        ],
        out_specs=pl.BlockSpec((MOD_ROWS, tn), lambda j: (0, j)),
        compiler_params=_cparams(("parallel",)),
        name="mod",
    )(cvec_t, w_ada, b_ada)


def _ctx_kv_kernel(x_ref, mod_ref, wk_ref, wv_ref, bk_ref, bv_ref, k_ref, v_ref):
    d = D_MODEL
    sh = mod_ref[0, :, 0:d]
    sc = mod_ref[0, :, d:2 * d]
    h = (_ln(x_ref[0]) * (1.0 + sc) + sh).astype(BF16)
    k = jnp.dot(h, wk_ref[...], preferred_element_type=F32) + bk_ref[...]
    k_ref[0] = (k * (RET_QK_DIM ** -0.5)).astype(BF16)
    v_ref[0] = (jnp.dot(h, wv_ref[...], preferred_element_type=F32) + bv_ref[...]).astype(BF16)


def _ctx_kv_call(ctx, mod3, w_in, b_in, ctx_row):
    bsz, n_ctx, d = ctx.shape
    kblk = Z_K // RET_QK_W
    vblk = Z_V // RET_V_W
    return pl.pallas_call(
        _ctx_kv_kernel,
        out_shape=(jax.ShapeDtypeStruct((bsz, n_ctx, RET_QK_W), BF16),
                   jax.ShapeDtypeStruct((bsz, n_ctx, RET_V_W), BF16)),
        grid=(bsz,),
        in_specs=[
            pl.BlockSpec((1, n_ctx, d), lambda b: (b, 0, 0)),
            pl.BlockSpec((1, 1, 6 * d), lambda b: (ctx_row, 0, 0)),
            pl.BlockSpec((d, RET_QK_W), lambda b: (0, kblk)),
            pl.BlockSpec((d, RET_V_W), lambda b: (0, vblk)),
            pl.BlockSpec((1, RET_QK_W), lambda b: (0, kblk)),
            pl.BlockSpec((1, RET_V_W), lambda b: (0, vblk)),
        ],
        out_specs=(pl.BlockSpec((1, n_ctx, RET_QK_W), lambda b: (b, 0, 0)),
                   pl.BlockSpec((1, n_ctx, RET_V_W), lambda b: (b, 0, 0))),
        compiler_params=_cparams(("parallel",)),
        name="ctx_kv",
    )(ctx, mod3, w_in, w_in, b_in, b_in)


def _inproj_kernel(x_ref, mod_ref, w_ref, b_ref, rope_ref, lnw_ref, lnb_ref, z_ref):
    d = D_MODEL
    sh = mod_ref[0, :, 0:d]
    sc = mod_ref[0, :, d:2 * d]
    h = (_ln(x_ref[0]) * (1.0 + sc) + sh).astype(BF16)

    def proj(c0, width):
        return jnp.dot(h, w_ref[:, c0:c0 + width], preferred_element_type=F32) + b_ref[:, c0:c0 + width]

    zq = proj(0, 2 * RET_QK_W)
    cos = rope_ref[:, 0:LANES]
    sin = rope_ref[:, LANES:2 * LANES]
    tm = zq.shape[0]
    upper = (lax.broadcasted_iota(I32, (tm, LANES), 1) & 32) != 0
    for hb in range(2 * RET_HEADS):
        zs = zq[:, hb * LANES:(hb + 1) * LANES]
        if hb >= RET_HEADS:
            zs = zs * (RET_QK_DIM ** -0.5)
        sw = jnp.where(upper, pltpu.roll(zs, 32, 1), pltpu.roll(zs, LANES - 32, 1))
        z_ref[0, :, hb * LANES:(hb + 1) * LANES] = (zs * cos + sw * sin).astype(BF16)

    def emit(c0, width, fn):
        for cc in range(c0, c0 + width, INPROJ_CHUNK):
            z_ref[0, :, cc:cc + INPROJ_CHUNK] = fn(proj(cc, INPROJ_CHUNK)).astype(BF16)

    vs = _ln(_gelu(proj(Z_VS, SGU_WIDTH))) * lnw_ref[...] + lnb_ref[...]
    z_ref[0, :, Z_VS:Z_VS + SGU_WIDTH] = vs.astype(BF16)
    emit(Z_U, SGU_WIDTH, _gelu)
    emit(Z_SG, RET_V_W, jax.nn.silu)
    emit(Z_GA, d, jax.nn.sigmoid)
    emit(Z_GB, d, jax.nn.sigmoid)
    emit(Z_V, RET_V_W, lambda z: z)


def _inproj_call(x, mod3, w_in, b_in, rope_t, sgu_ln_w, sgu_ln_b):
    bsz, t, d = x.shape
    tm = TM_INPROJ
    row = lambda b, i: (b, i, 0)
    const2 = lambda b, i: (0, 0)
    return pl.pallas_call(
        _inproj_kernel,
        out_shape=jax.ShapeDtypeStruct((bsz, t, D_IN), BF16),
        grid=(bsz, t // tm),
        in_specs=[
            pl.BlockSpec((1, tm, d), row),
            pl.BlockSpec((1, 1, 6 * d), lambda b, i: (b, 0, 0)),
            pl.BlockSpec((d, D_IN), const2, pipeline_mode=pl.Buffered(1)),
            pl.BlockSpec((1, D_IN), const2),
            pl.BlockSpec((tm, 2 * LANES), lambda b, i: (i, 0)),
            pl.BlockSpec((1, SGU_WIDTH), const2),
            pl.BlockSpec((1, SGU_WIDTH), const2),
        ],
        out_specs=pl.BlockSpec((1, tm, D_IN), row),
        compiler_params=_cparams(("parallel", "parallel")),
        name="inproj",
    )(x, mod3, w_in, b_in, rope_t, sgu_ln_w, sgu_ln_b)


def _dot_t0(a, b):
    return lax.dot_general(a, b, (((0,), (0,)), ((), ())), preferred_element_type=F32)


def _state_kernel(lg_ref, k_ref, v_ref, kc_ref, vc_ref, s_ref):
    hd = pl.program_id(1)
    lgf = lg_ref[0, hd]
    lgb = lg_ref[1, hd]
    n_chunks = s_ref.shape[2]
    n_ctx = kc_ref.shape[1]
    c = RET_CHUNK
    li = _iota_f((c, RET_QK_DIM), 0)
    zeta_f = jnp.exp((c - 1.0 - li) * lgf)
    zeta_b = jnp.exp(li * lgb)
    one = jnp.ones((1, 1), F32)
    cd_f = jnp.exp(one * (c * lgf))
    cd_b = jnp.exp(one * (c * lgb))

    tc = _iota_f((n_ctx, RET_QK_DIM), 0)
    kc = kc_ref[0].astype(F32)
    vc = vc_ref[0]
    s0f = _dot_t0((kc * jnp.exp((n_ctx - 1.0 - tc) * lgf)).astype(BF16), vc)
    s0b = _dot_t0((kc * jnp.exp(tc * lgb)).astype(BF16), vc)

    def chunk_kv(n, zeta):
        off = pl.multiple_of(n * c, c)
        kk = k_ref[0, pl.ds(off, c), :].astype(F32)
        return _dot_t0((kk * zeta).astype(BF16), v_ref[0, pl.ds(off, c), :])

    def step(i, carry):
        sf, sb = carry
        nb = n_chunks - 1 - i
        s_ref[0, 0, i, 0:RET_QK_DIM, :] = sf.astype(BF16)
        s_ref[0, 0, nb, RET_QK_DIM:2 * RET_QK_DIM, :] = sb.astype(BF16)
        return cd_f * sf + chunk_kv(i, zeta_f), cd_b * sb + chunk_kv(nb, zeta_b)

    lax.fori_loop(0, n_chunks, step, (s0f, s0b), unroll=STATE_UNROLL)


def _state_call(lg, z, kctx, vctx):
    bsz, t, _ = z.shape
    n_ctx = kctx.shape[1]
    n_chunks = t // RET_CHUNK
    return pl.pallas_call(
        _state_kernel,
        out_shape=jax.ShapeDtypeStruct((bsz, RET_HEADS, n_chunks, 2 * RET_QK_DIM, RET_V_DIM), BF16),
        grid=(bsz, RET_HEADS),
        in_specs=[
            pl.BlockSpec(memory_space=pltpu.SMEM),
            pl.BlockSpec((1, t, RET_QK_DIM), lambda b, h: (b, 0, RET_QK_W // RET_QK_DIM + h)),
            pl.BlockSpec((1, t, RET_V_DIM), lambda b, h: (b, 0, Z_V // RET_V_DIM + h)),
            pl.BlockSpec((1, n_ctx, RET_QK_DIM), lambda b, h: (b, 0, h)),
            pl.BlockSpec((1, n_ctx, RET_V_DIM), lambda b, h: (b, 0, h)),
        ],
        out_specs=pl.BlockSpec((1, 1, n_chunks, 2 * RET_QK_DIM, RET_V_DIM), lambda b, h: (b, h, 0, 0, 0)),
        compiler_params=_cparams(("parallel", "parallel")),
        name="states",
    )(lg, z, z, kctx, vctx)


def _mixer_kernel(lg_ref, za_ref, zb_ref, gb_ref, s_ref, x_ref, mod_ref,
                  wpa_ref, wpb_ref, wo_ref, sguw_ref, sgub_ref, gnw_ref, gnb_ref, bo_ref,
                  ln1w_ref, ln1b_ref, wr_ref, br_ref,
                  x1_ref, hm_ref, lgt_ref, ret_scr, sgu_scr):
    d = D_MODEL
    c = RET_CHUNK

    n_sub = za_ref.shape[1] // c
    row = lax.broadcasted_iota(I32, (c, c), 0)
    col = lax.broadcasted_iota(I32, (c, c), 1)
    diff = (row - col).astype(F32)
    rowq = _iota_f((c, RET_QK_DIM), 0)
    for hd in range(RET_HEADS):
        lgf = lg_ref[0, hd]
        lgb = lg_ref[1, hd]
        vcols = slice(hd * RET_V_DIM, (hd + 1) * RET_V_DIM)
        mask = (jnp.where(diff >= 0, jnp.exp(jnp.maximum(diff, 0.0) * lgf), 0.0)
                + jnp.where(diff <= 0, jnp.exp(jnp.maximum(-diff, 0.0) * lgb), 0.0))
        xi_f = jnp.exp((rowq + 1.0) * lgf)
        xi_b = jnp.exp((c - rowq) * lgb)
        for ci in range(n_sub):
            r0 = ci * c
            q = za_ref[0, r0:r0 + c, hd * RET_QK_DIM:(hd + 1) * RET_QK_DIM]
            k = za_ref[0, r0:r0 + c, RET_QK_W + hd * RET_QK_DIM:RET_QK_W + (hd + 1) * RET_QK_DIM]
            s = lax.dot_general(q, k, (((1,), (1,)), ((), ())), preferred_element_type=F32)
            vv = za_ref[0, r0:r0 + c, Z_V + hd * RET_V_DIM:Z_V + (hd + 1) * RET_V_DIM]
            intra = jnp.dot((s * mask).astype(BF16), vv, preferred_element_type=F32)
            qf = q.astype(F32)
            qx = jnp.concatenate([(qf * xi_f).astype(BF16), (qf * xi_b).astype(BF16)], axis=1)
            cross = jnp.dot(qx, s_ref[0, hd, ci], preferred_element_type=F32)
            o = _ln(intra + cross) * gnw_ref[:, vcols] + gnb_ref[:, vcols]
            gate = za_ref[0, r0:r0 + c, Z_SG + hd * RET_V_DIM:Z_SG + (hd + 1) * RET_V_DIM]
            ret_scr[r0:r0 + c, vcols] = (o * gate.astype(F32)).astype(BF16)

    for g in range(SGU_GROUPS):
        gcols = slice(g * SGU_GW, (g + 1) * SGU_GW)
        wg = sguw_ref[g]
        bg = sgub_ref[:, g:g + 1]
        for ci in range(n_sub):
            r0 = ci * c
            vsb = zb_ref[0, r0:r0 + c, Z_VS - Z_BLOCK + g * SGU_GW:Z_VS - Z_BLOCK + (g + 1) * SGU_GW]
            ub = zb_ref[0, r0:r0 + c, Z_U - Z_BLOCK + g * SGU_GW:Z_U - Z_BLOCK + (g + 1) * SGU_GW]
            sp = jnp.dot(wg, vsb, preferred_element_type=F32) + bg
            sgu_scr[r0:r0 + c, gcols] = (ub.astype(F32) * sp).astype(BF16)

    g1 = mod_ref[0, :, 2 * d:3 * d]
    sh2 = mod_ref[0, :, 3 * d:4 * d]
    sc2 = mod_ref[0, :, 4 * d:5 * d]
    for r0 in range(0, za_ref.shape[1], MERGE_ROWS):
        rows = slice(r0, r0 + MERGE_ROWS)
        pa = jnp.dot(ret_scr[rows, :], wpa_ref[...], preferred_element_type=F32)
        pb = jnp.dot(sgu_scr[rows, :], wpb_ref[...], preferred_element_type=F32)
        ga = zb_ref[0, rows, Z_GA - Z_BLOCK:Z_GA - Z_BLOCK + d]
        y = (ga.astype(F32) * pa + gb_ref[0, rows, :].astype(F32) * pb).astype(BF16)
        mix = jnp.dot(y, wo_ref[...], preferred_element_type=F32) + bo_ref[...]
        x1 = _ln(DEEPNORM_ALPHA * x_ref[0, rows, :] + g1 * mix) * ln1w_ref[...] + ln1b_ref[...]
        x1_ref[0, rows, :] = x1
        hm = (_ln(x1) * (1.0 + sc2) + sh2).astype(BF16)
        hm_ref[0, rows, :] = hm
        lgt_ref[0, :, rows] = lax.dot_general(wr_ref[...], hm, (((1,), (1,)), ((), ())),
                                              preferred_element_type=F32) + br_ref[...]


def _mixer_call(lg, z, states, x, mod3, wpa, wpb, wo, sguw, sgub_t, gnw, gnb, bo,
                ln1w, ln1b, wr_t, br):
    bsz, t, d = x.shape
    tt = TT_MIXER
    n_sub = tt // RET_CHUNK
    row = lambda b, i: (b, i, 0)
    c2 = lambda b, i: (0, 0)
    c3 = lambda b, i: (0, 0, 0)
    return pl.pallas_call(
        _mixer_kernel,
        out_shape=(jax.ShapeDtypeStruct((bsz, t, d), F32),
                   jax.ShapeDtypeStruct((bsz, t, d), BF16),
                   jax.ShapeDtypeStruct((bsz, N_EXPERTS, t), F32)),
        grid=(bsz, t // tt),
        in_specs=[
            pl.BlockSpec(memory_space=pltpu.SMEM),
            pl.BlockSpec((1, tt, Z_BLOCK), lambda b, i: (b, i, 0)),
            pl.BlockSpec((1, tt, Z_BLOCK), lambda b, i: (b, i, 1)),
            pl.BlockSpec((1, tt, d), lambda b, i: (b, i, Z_GB // d)),
            pl.BlockSpec((1, RET_HEADS, n_sub, 2 * RET_QK_DIM, RET_V_DIM), lambda b, i: (b, 0, i, 0, 0)),
            pl.BlockSpec((1, tt, d), row),
            pl.BlockSpec((1, 1, 6 * d), lambda b, i: (b, 0, 0)),
            pl.BlockSpec((RET_V_W, d), c2, pipeline_mode=pl.Buffered(1)),
            pl.BlockSpec((SGU_WIDTH, d), c2, pipeline_mode=pl.Buffered(1)),
            pl.BlockSpec((d, d), c2, pipeline_mode=pl.Buffered(1)),
            pl.BlockSpec((SGU_GROUPS, SGU_CHUNK, SGU_CHUNK), c3),
            pl.BlockSpec((SGU_CHUNK, SGU_GROUPS), c2),
            pl.BlockSpec((1, RET_V_W), c2),
            pl.BlockSpec((1, RET_V_W), c2),
            pl.BlockSpec((1, d), c2),
            pl.BlockSpec((1, d), c2),
            pl.BlockSpec((1, d), c2),
            pl.BlockSpec((N_EXPERTS, d), c2),
            pl.BlockSpec((N_EXPERTS, 1), c2),
        ],
        out_specs=(pl.BlockSpec((1, tt, d), row),
                   pl.BlockSpec((1, tt, d), row),
                   pl.BlockSpec((1, N_EXPERTS, tt), lambda b, i: (b, 0, i))),
        scratch_shapes=[pltpu.VMEM((tt, RET_V_W), BF16), pltpu.VMEM((tt, SGU_WIDTH), BF16)],
        compiler_params=_cparams(("parallel", "parallel")),
        name="mixer",
    )(lg, z, z, z, states, x, mod3, wpa, wpb, wo, sguw, sgub_t, gnw, gnb, bo,
      ln1w, ln1b, wr_t, br)


def _route_kernel(lgt_ref, pos_ref, gate_ref, cnt_ref, *, cap):
    lg = lgt_ref[0]
    n_e, t = lg.shape
    tb = TB_ROUTE
    m = jnp.max(lg, axis=0, keepdims=True)
    ex = jnp.exp(lg - m)
    aff = ex / jnp.sum(ex, axis=0, keepdims=True)

    def search(i, thr_bits):
        cand = thr_bits | lax.shift_left(jnp.int32(1), 30 - i)
        cnt = jnp.sum((aff >= lax.bitcast_convert_type(cand, F32)).astype(I32), axis=1, keepdims=True)
        return jnp.where(cnt >= cap, cand, thr_bits)

    thr_bits = lax.fori_loop(0, 31, search, jnp.zeros((n_e, 1), I32))
    floor_f = lax.bitcast_convert_type(thr_bits, F32)
    thr = jnp.min(jnp.where(aff >= floor_f, aff, jnp.inf), axis=1, keepdims=True)
    need = (cap - jnp.sum((aff > thr).astype(I32), axis=1, keepdims=True)).astype(F32)

    r = lax.broadcasted_iota(I32, (tb, tb), 0)
    cc = lax.broadcasted_iota(I32, (tb, tb), 1)
    tri = (r <= cc).astype(BF16)
    carry_eq = jnp.zeros((n_e, 1), F32)
    carry_sel = jnp.zeros((n_e, 1), F32)
    for blk in range(t // tb):
        sl = slice(blk * tb, (blk + 1) * tb)
        aff_b = aff[:, sl]
        eq = aff_b == thr
        eq_b = eq.astype(BF16)
        inc_eq = jnp.dot(eq_b, tri, preferred_element_type=F32)
        before = carry_eq + inc_eq - eq_b.astype(F32)
        sel = (aff_b > thr) | (eq & (before < need))
        sel_b = sel.astype(BF16)
        inc_sel = jnp.dot(sel_b, tri, preferred_element_type=F32)
        pos = carry_sel + inc_sel - 1.0
        pos_ref[0, blk] = jnp.where(sel, pos.astype(I32), -1)
        gate_ref[0, blk] = aff_b
        cnt_ref[0, blk] = jnp.broadcast_to(carry_sel, (n_e, LANES)).astype(I32)
        carry_eq = carry_eq + inc_eq[:, tb - 1:tb]
        carry_sel = carry_sel + inc_sel[:, tb - 1:tb]


def _route_call(logits_t, cap):
    bsz, n_e, t = logits_t.shape
    nblk = t // TB_ROUTE
    return pl.pallas_call(
        functools.partial(_route_kernel, cap=cap),
        out_shape=(jax.ShapeDtypeStruct((bsz, nblk, n_e, TB_ROUTE), I32),
                   jax.ShapeDtypeStruct((bsz, nblk, n_e, TB_ROUTE), F32),
                   jax.ShapeDtypeStruct((bsz, nblk, n_e, LANES), I32)),
        grid=(bsz,),
        in_specs=[pl.BlockSpec((1, n_e, t), lambda b: (b, 0, 0))],
        out_specs=(pl.BlockSpec((1, nblk, n_e, TB_ROUTE), lambda b: (b, 0, 0, 0)),
                   pl.BlockSpec((1, nblk, n_e, TB_ROUTE), lambda b: (b, 0, 0, 0)),
                   pl.BlockSpec((1, nblk, n_e, LANES), lambda b: (b, 0, 0, 0))),
        compiler_params=_cparams(("parallel",)),
        name="route",
    )(logits_t)


def _window_rows(ws, win, n_tok):
    return lax.broadcasted_iota(I32, (win, n_tok), 0) + ws


def _sweep_window(w, cap, win, prow, rest):
    off = pl.multiple_of(jnp.minimum(w * win, cap - win), WIN_ALIGN)
    rows = _window_rows(off, win, prow.shape[-1])
    return off, (rows == prow) & rest & (rows >= w * win)


def _gather_kernel(ws_ref, nx_ref, hm_ref, pos_ref, gate_ref, xs_ref, gc_ref):
    b = pl.program_id(0)
    g = pl.program_id(1)
    step = pl.program_id(2)
    merge = DISPATCH_MERGE
    eg, tb = pos_ref.shape[2:]
    n_sub = pos_ref.shape[1] // merge
    n_e = eg * pl.num_programs(1)
    n_win = n_sub * pl.num_programs(2)
    cap = xs_ref.shape[2]
    win = WIN_DISPATCH

    @pl.when(step == 0)
    def _():
        xs_ref[...] = jnp.zeros_like(xs_ref)
        gc_ref[...] = jnp.zeros_like(gc_ref)

    def base(sub):
        return (b * n_win + step * n_sub + sub) * n_e + g * eg

    def tokens(sub):
        return hm_ref[0, sub * merge * tb:(sub + 1) * merge * tb, :]

    def lanes(ref, sub, el):
        return jnp.concatenate([ref[0, sub * merge + m, el:el + 1, :] for m in range(merge)], axis=-1)

    for sub in range(n_sub):
        for el in range(eg):
            ws = pl.multiple_of(ws_ref[base(sub) + el], WIN_ALIGN)
            match = _window_rows(ws, win, merge * tb) == lanes(pos_ref, sub, el)
            sl = pl.ds(ws, win)
            xs_ref[0, el, sl, :] += jnp.dot(match.astype(BF16), tokens(sub),
                                            preferred_element_type=F32).astype(BF16)
            gc_ref[0, el, sl, :] += jnp.sum(jnp.where(match, lanes(gate_ref, sub, el), 0.0),
                                             axis=1, keepdims=True)

    for sub in range(n_sub):
        for el in range(eg):
            @pl.when(nx_ref[base(sub) + el] > 0)
            def _(sub=sub, el=el):
                prow = lanes(pos_ref, sub, el)
                grow = lanes(gate_ref, sub, el)
                rest = prow >= ws_ref[base(sub) + el] + win

                def body(w, carry):
                    off, m = _sweep_window(w, cap, win, prow, rest)
                    xs_ref[0, el, pl.ds(off, win), :] += jnp.dot(m.astype(BF16), tokens(sub),
                                                                  preferred_element_type=F32).astype(BF16)
                    gc_ref[0, el, pl.ds(off, win), :] += jnp.sum(jnp.where(m, grow, 0.0), axis=1, keepdims=True)
                    return carry

                lax.fori_loop(0, pl.cdiv(cap, win), body, 0)


def _gather_call(ws, nx, hm, pos_b, gate_b, cap):
    bsz, t, d = hm.shape
    nblk, n_e, tb = pos_b.shape[1:]
    eg = n_e // DISPATCH_GROUPS
    blocks = DISPATCH_STEP_WINDOWS * DISPATCH_MERGE
    grid_spec = pltpu.PrefetchScalarGridSpec(
        num_scalar_prefetch=2,
        grid=(bsz, DISPATCH_GROUPS, nblk // blocks),
        in_specs=[
            pl.BlockSpec((1, blocks * tb, d), lambda b, g, k, ws, nx: (b, k, 0)),
            pl.BlockSpec((1, blocks, eg, tb), lambda b, g, k, ws, nx: (b, k, g, 0)),
            pl.BlockSpec((1, blocks, eg, tb), lambda b, g, k, ws, nx: (b, k, g, 0)),
        ],
        out_specs=(pl.BlockSpec((1, eg, cap, d), lambda b, g, k, ws, nx: (b, g, 0, 0)),
                   pl.BlockSpec((1, eg, cap, 1), lambda b, g, k, ws, nx: (b, g, 0, 0))),
    )
    return pl.pallas_call(
        _gather_kernel,
        out_shape=(jax.ShapeDtypeStruct((bsz, n_e, cap, d), BF16),
                   jax.ShapeDtypeStruct((bsz, n_e, cap, 1), F32)),
        grid_spec=grid_spec,
        compiler_params=_cparams(("parallel", "parallel", "arbitrary")),
        name="gather",
    )(ws, nx, hm, pos_b, gate_b)


def _ffn_kernel(xs_ref, gc_ref, wg_ref, wu_ref, wd_ref, ye_ref, wgu_s, wd_s):
    f = wd_s.shape[0]
    wgu_s[:, 0:f] = wg_ref[0].astype(BF16)
    wgu_s[:, f:2 * f] = wu_ref[0].astype(BF16)
    wd_s[...] = wd_ref[0].astype(BF16)

    rt = FFN_ROWS
    for b in range(xs_ref.shape[0]):
        for j in range(xs_ref.shape[2] // rt):
            sl = slice(j * rt, (j + 1) * rt)
            xgu = jnp.dot(xs_ref[b, 0, sl, :], wgu_s[...], preferred_element_type=F32)
            hid = (jax.nn.silu(xgu[:, 0:f]) * xgu[:, f:2 * f]).astype(BF16)
            ye = jnp.dot(hid, wd_s[...], preferred_element_type=F32) * gc_ref[b, 0, sl, :]
            ye_ref[b, 0, sl, :] = ye.astype(BF16)


def _ffn_call(xs, gc, w_gate, w_up, w_down):
    bsz, n_e, cap, d = xs.shape
    f = w_gate.shape[2]
    return pl.pallas_call(
        _ffn_kernel,
        out_shape=jax.ShapeDtypeStruct((bsz, n_e, cap, d), BF16),
        grid=(n_e,),
        in_specs=[
            pl.BlockSpec((bsz, 1, cap, d), lambda e: (0, e, 0, 0)),
            pl.BlockSpec((bsz, 1, cap, 1), lambda e: (0, e, 0, 0)),
            pl.BlockSpec((1, d, f), lambda e: (e, 0, 0)),
            pl.BlockSpec((1, d, f), lambda e: (e, 0, 0)),
            pl.BlockSpec((1, f, d), lambda e: (e, 0, 0)),
        ],
        out_specs=pl.BlockSpec((bsz, 1, cap, d), lambda e: (0, e, 0, 0)),
        scratch_shapes=[pltpu.VMEM((d, 2 * f), BF16), pltpu.VMEM((f, d), BF16)],
        compiler_params=_cparams(("parallel",)),
        name="ffn",
    )(xs, gc, w_gate, w_up, w_down)


def _combine_kernel(ws_ref, nx_ref, ye_ref, pos_ref, x1_ref, mod_ref, lnw_ref, lnb_ref, out_ref, acc_scr):
    b = pl.program_id(0)
    step = pl.program_id(1)
    n_sub, n_e, tb = pos_ref.shape[1:]
    nblk = n_sub * pl.num_programs(1)
    cap = ye_ref.shape[2]
    d = D_MODEL
    win = WIN_COMBINE
    g2 = mod_ref[0, :, 5 * d:6 * d]

    for sub in range(n_sub):
        base = (b * nblk + step * n_sub + sub) * n_e
        starts = [pl.multiple_of(ws_ref[base + e], WIN_ALIGN) for e in range(n_e)]
        onehot = jnp.concatenate(
            [(_window_rows(starts[e], win, tb) == pos_ref[0, sub, e:e + 1, :]).astype(BF16) for e in range(n_e)],
            axis=0)
        rows = jnp.concatenate([ye_ref[0, e, pl.ds(starts[e], win), :] for e in range(n_e)], axis=0)
        acc_scr[...] = _dot_t0(onehot, rows)

        for e in range(n_e):
            @pl.when(nx_ref[base + e] > 0)
            def _(e=e, sub=sub, starts=starts):
                prow = pos_ref[0, sub, e:e + 1, :]
                rest = prow >= starts[e] + win

                def body(w, carry):
                    off, m = _sweep_window(w, cap, win, prow, rest)
                    acc_scr[...] += _dot_t0(m.astype(BF16), ye_ref[0, e, pl.ds(off, win), :])
                    return carry

                lax.fori_loop(0, pl.cdiv(cap, win), body, 0)

        tok = slice(sub * tb, (sub + 1) * tb)
        out_ref[0, tok, :] = (_ln(DEEPNORM_ALPHA * x1_ref[0, tok, :] + g2 * acc_scr[...]) * lnw_ref[...]
                              + lnb_ref[...])


def _combine_call(ws, nx, ye, pos_b, x1, mod3, ln2w, ln2b):
    bsz, n_e, cap, d = ye.shape
    nblk, _, tb = pos_b.shape[1:]
    n_sub = COMBINE_STEP_BLOCKS
    grid_spec = pltpu.PrefetchScalarGridSpec(
        num_scalar_prefetch=2,
        grid=(bsz, nblk // n_sub),
        in_specs=[
            pl.BlockSpec((1, n_e, cap, d), lambda b, k, ws, nx: (b, 0, 0, 0), pipeline_mode=pl.Buffered(1)),
            pl.BlockSpec((1, n_sub, n_e, tb), lambda b, k, ws, nx: (b, k, 0, 0)),
            pl.BlockSpec((1, n_sub * tb, d), lambda b, k, ws, nx: (b, k, 0)),
            pl.BlockSpec((1, 1, 6 * d), lambda b, k, ws, nx: (b, 0, 0)),
            pl.BlockSpec((1, d), lambda b, k, ws, nx: (0, 0)),
            pl.BlockSpec((1, d), lambda b, k, ws, nx: (0, 0)),
        ],
        out_specs=pl.BlockSpec((1, n_sub * tb, d), lambda b, k, ws, nx: (b, k, 0)),
        scratch_shapes=[pltpu.VMEM((tb, d), F32)],
    )
    return pl.pallas_call(
        _combine_kernel,
        out_shape=jax.ShapeDtypeStruct((bsz, nblk * tb, d), F32),
        grid_spec=grid_spec,
        compiler_params=_cparams(("parallel", "arbitrary")),
        name="combine",
    )(ws, nx, ye, pos_b, x1, mod3, ln2w, ln2b)


def _rope_tables(n_tokens):
    rows = n_tokens // GRID_W
    n_freq = RET_QK_DIM // 4
    inv = ROPE_THETA ** (-jnp.arange(n_freq, dtype=F32) / n_freq)
    ang_r = jnp.arange(rows, dtype=F32)[:, None] * inv
    ang_c = jnp.arange(GRID_W, dtype=F32)[:, None] * inv
    zr = jnp.zeros((rows, 2 * n_freq), F32)
    zc = jnp.zeros((GRID_W, 2 * n_freq), F32)
    cr, sr, cc, sc = jnp.cos(ang_r), jnp.sin(ang_r), jnp.cos(ang_c), jnp.sin(ang_c)
    cos_t = (jnp.concatenate([cr, cr, zr], axis=-1)[:, None, :]
             + jnp.concatenate([zc, cc, cc], axis=-1)[None, :, :]).reshape(n_tokens, RET_QK_DIM)
    sin_t = (jnp.concatenate([-sr, sr, zr], axis=-1)[:, None, :]
             + jnp.concatenate([zc, -sc, sc], axis=-1)[None, :, :]).reshape(n_tokens, RET_QK_DIM)
    return jnp.concatenate([cos_t, sin_t], axis=-1)


def kernel(x, c, ctx, c_ctx, w_ada, b_ada, w_in, b_in, ret_decay_f, ret_decay_b, ret_gn_w, ret_gn_b,
           sgu_ln_w, sgu_ln_b, sgu_w, sgu_b, w_pa, w_pb, w_o, b_o, ln1_w, ln1_b,
           w_router, b_router, w_gate, w_up, w_down, ln2_w, ln2_b):
    bsz, t, d = x.shape
    assert d == D_MODEL and w_ada.shape[0] == DEPTH == 1 and bsz < MOD_ROWS
    cap = EC_CAPACITY_FACTOR * t // N_EXPERTS
    for win in (WIN_DISPATCH, WIN_COMBINE):
        assert cap >= win and (cap - win) % WIN_ALIGN == 0 and win % WIN_ALIGN == 0
    assert t % (TB_ROUTE * DISPATCH_MERGE * DISPATCH_STEP_WINDOWS) == 0 and t % (TB_ROUTE * COMBINE_STEP_BLOCKS) == 0
    assert N_EXPERTS % DISPATCH_GROUPS == 0
    l = 0

    cvec = jnp.concatenate([c, c_ctx[None], jnp.zeros((MOD_ROWS - bsz - 1, d), F32)], axis=0)
    mod = _mod_call(cvec.T, w_ada[l], b_ada[l][None], bsz + 1)
    mod3 = mod[:, None, :]

    w_in_b = w_in[l].astype(BF16)
    b_in2 = b_in[l][None]
    kctx, vctx = _ctx_kv_call(ctx, mod3, w_in_b, b_in2, bsz)

    z = _inproj_call(x, mod3, w_in_b, b_in2, _rope_tables(t), sgu_ln_w[l][None], sgu_ln_b[l][None])

    lg = jnp.stack([jax.nn.log_sigmoid(ret_decay_f[l].astype(F32)),
                    jax.nn.log_sigmoid(ret_decay_b[l].astype(F32))])
    states = _state_call(lg, z, kctx, vctx)

    x1, hm, logits_t = _mixer_call(
        lg, z, states, x, mod3,
        w_pa[l].astype(BF16), w_pb[l].astype(BF16), w_o[l].astype(BF16),
        sgu_w[l].astype(BF16), sgu_b[l].T, ret_gn_w[l][None], ret_gn_b[l][None], b_o[l][None],
        ln1_w[l][None], ln1_b[l][None], w_router[l].T.astype(BF16), b_router[l][:, None])

    pos_b, gate_b, cnt_b = _route_call(logits_t, cap)

    c0 = cnt_b[..., 0]
    c1 = jnp.concatenate([c0[:, 1:], jnp.full_like(c0[:, :1], cap)], axis=1)

    def windows(start, end, win):
        ws = jnp.minimum((start // WIN_ALIGN) * WIN_ALIGN, cap - win)
        return ws.reshape(-1), (end > ws + win).astype(I32).reshape(-1)

    ws_d, nx_d = windows(c0[:, ::DISPATCH_MERGE], c1[:, DISPATCH_MERGE - 1::DISPATCH_MERGE], WIN_DISPATCH)
    ws_c, nx_c = windows(c0, c1, WIN_COMBINE)

    xs, gc = _gather_call(ws_d, nx_d, hm, pos_b, gate_b, cap)
    ye = _ffn_call(xs, gc, w_gate[l], w_up[l], w_down[l])
    return _combine_call(ws_c, nx_c, ye, pos_b, x1, mod3, ln2_w[l][None], ln2_b[l][None])
```

```python
import functools
import math

import jax
import jax.numpy as jnp
from jax import lax
from jax.experimental import pallas as pl
from jax.experimental.pallas import tpu as pltpu

F32 = jnp.float32
BF16 = jnp.bfloat16
I32 = jnp.int32

D_MODEL = 1024
DEPTH = 1
GRID_W = 64
RET_HEADS = 4
RET_QK_DIM = 128
RET_V_DIM = 256
RET_CHUNK = 128
ROPE_THETA = 10000.0
RET_QK_W = RET_HEADS * RET_QK_DIM
RET_V_W = RET_HEADS * RET_V_DIM
SGU_GROUPS = 4
SGU_CHUNK = 128
SGU_WIDTH = 1024
SGU_GW = SGU_WIDTH // SGU_GROUPS
N_EXPERTS = 16
EC_CAPACITY_FACTOR = 2
LN_EPS = 1e-6
DEEPNORM_ALPHA = (2.0 * DEPTH) ** 0.25
D_IN = 2 * RET_QK_W + 2 * RET_V_W + 2 * SGU_WIDTH + 2 * D_MODEL
Z_K = RET_QK_W
Z_V = 2 * RET_QK_W
Z_SG = Z_V + RET_V_W
Z_U = Z_SG + RET_V_W
Z_VS = Z_U + SGU_WIDTH
Z_GA = Z_VS + SGU_WIDTH
Z_GB = Z_GA + D_MODEL
Z_BLOCK = 3 * 1024

V7X_VMEM_BYTES = 64 * 1024 * 1024
VMEM_LIMIT = V7X_VMEM_BYTES - 4 * 1024 * 1024
LANES = 128

MOD_ROWS = 8
MOD_COLS = 1024
TM_INPROJ = 1024
INPROJ_CHUNK = 256
TT_MIXER = 512
MERGE_ROWS = 512
TB_ROUTE = 256
DISPATCH_MERGE = 2
WIN_DISPATCH = 128
WIN_COMBINE = 64
WIN_ALIGN = 16
DISPATCH_GROUPS = 2
DISPATCH_STEP_WINDOWS = 4
COMBINE_STEP_BLOCKS = 4
STATE_UNROLL = 8
FFN_ROWS = 256


def _cparams(sem):
    return pltpu.CompilerParams(dimension_semantics=sem, vmem_limit_bytes=VMEM_LIMIT)


def _ln(x):
    mu = jnp.mean(x, axis=-1, keepdims=True)
    xc = x - mu
    var = jnp.mean(xc * xc, axis=-1, keepdims=True)
    return xc * lax.rsqrt(var + LN_EPS)


_GELU_A = -2.0 * math.sqrt(2.0 / math.pi) * math.log2(math.e)
_GELU_B = _GELU_A * 0.044715


def _gelu(x):
    return x / (1.0 + jnp.exp2(x * (_GELU_A + _GELU_B * (x * x))))


def _iota_f(shape, dim):
    return lax.broadcasted_iota(I32, shape, dim).astype(F32)


def _mod_kernel(ct_ref, w_ref, b_ref, o_ref, *, n_rows):
    c = ct_ref[...]
    a = c * jax.nn.sigmoid(c)
    w = w_ref[...]
    rows = [jnp.sum(w * a[:, m:m + 1], axis=0, keepdims=True) + b_ref[...] for m in range(n_rows)]
    rows += [jnp.zeros_like(rows[0])] * (MOD_ROWS - n_rows)
    o_ref[...] = jnp.concatenate(rows, axis=0)


def _mod_call(cvec_t, w_ada, b_ada, n_rows):
    d, n = w_ada.shape
    tn = MOD_COLS
    return pl.pallas_call(
        functools.partial(_mod_kernel, n_rows=n_rows),
        out_shape=jax.ShapeDtypeStruct((MOD_ROWS, n), F32),
        grid=(n // tn,),
        in_specs=[
            pl.BlockSpec((d, MOD_ROWS), lambda j: (0, 0)),
            pl.BlockSpec((d, tn), lambda j: (0, j)),
            pl.BlockSpec((1, tn), lambda j: (0, j)),
        ],
        out_specs=pl.BlockSpec((MOD_ROWS, tn), lambda j: (0, j)),
        compiler_params=_cparams(("parallel",)),
        name="mod",
    )(cvec_t, w_ada, b_ada)


def _ctx_kv_kernel(x_ref, mod_ref, wk_ref, wv_ref, bk_ref, bv_ref, k_ref, v_ref):
    d = D_MODEL
    sh = mod_ref[0, :, 0:d]
    sc = mod_ref[0, :, d:2 * d]
    h = (_ln(x_ref[0]) * (1.0 + sc) + sh).astype(BF16)
    k = jnp.dot(h, wk_ref[...], preferred_element_type=F32) + bk_ref[...]
    k_ref[0] = (k * (RET_QK_DIM ** -0.5)).astype(BF16)
    v_ref[0] = (jnp.dot(h, wv_ref[...], preferred_element_type=F32) + bv_ref[...]).astype(BF16)


def _ctx_kv_call(ctx, mod3, w_in, b_in, ctx_row):
    bsz, n_ctx, d = ctx.shape
    kblk = Z_K // RET_QK_W
    vblk = Z_V // RET_V_W
    return pl.pallas_call(
        _ctx_kv_kernel,
        out_shape=(jax.ShapeDtypeStruct((bsz, n_ctx, RET_QK_W), BF16),
                   jax.ShapeDtypeStruct((bsz, n_ctx, RET_V_W), BF16)),
        grid=(bsz,),
        in_specs=[
            pl.BlockSpec((1, n_ctx, d), lambda b: (b, 0, 0)),
            pl.BlockSpec((1, 1, 6 * d), lambda b: (ctx_row, 0, 0)),
            pl.BlockSpec((d, RET_QK_W), lambda b: (0, kblk)),
            pl.BlockSpec((d, RET_V_W), lambda b: (0, vblk)),
            pl.BlockSpec((1, RET_QK_W), lambda b: (0, kblk)),
            pl.BlockSpec((1, RET_V_W), lambda b: (0, vblk)),
        ],
        out_specs=(pl.BlockSpec((1, n_ctx, RET_QK_W), lambda b: (b, 0, 0)),
                   pl.BlockSpec((1, n_ctx, RET_V_W), lambda b: (b, 0, 0))),
        compiler_params=_cparams(("parallel",)),
        name="ctx_kv",
    )(ctx, mod3, w_in, w_in, b_in, b_in)


def _inproj_kernel(x_ref, mod_ref, w_ref, b_ref, rope_ref, lnw_ref, lnb_ref, z_ref):
    d = D_MODEL
    sh = mod_ref[0, :, 0:d]
    sc = mod_ref[0, :, d:2 * d]
    h = (_ln(x_ref[0]) * (1.0 + sc) + sh).astype(BF16)

    def proj(c0, width):
        return jnp.dot(h, w_ref[:, c0:c0 + width], preferred_element_type=F32) + b_ref[:, c0:c0 + width]

    zq = proj(0, 2 * RET_QK_W)
    cos = rope_ref[:, 0:LANES]
    sin = rope_ref[:, LANES:2 * LANES]
    tm = zq.shape[0]
    upper = (lax.broadcasted_iota(I32, (tm, LANES), 1) & 32) != 0
    for hb in range(2 * RET_HEADS):
        zs = zq[:, hb * LANES:(hb + 1) * LANES]
        if hb >= RET_HEADS:
            zs = zs * (RET_QK_DIM ** -0.5)
        sw = jnp.where(upper, pltpu.roll(zs, 32, 1), pltpu.roll(zs, LANES - 32, 1))
        z_ref[0, :, hb * LANES:(hb + 1) * LANES] = (zs * cos + sw * sin).astype(BF16)

    def emit(c0, width, fn):
        for cc in range(c0, c0 + width, INPROJ_CHUNK):
            z_ref[0, :, cc:cc + INPROJ_CHUNK] = fn(proj(cc, INPROJ_CHUNK)).astype(BF16)

    vs = _ln(_gelu(proj(Z_VS, SGU_WIDTH))) * lnw_ref[...] + lnb_ref[...]
    z_ref[0, :, Z_VS:Z_VS + SGU_WIDTH] = vs.astype(BF16)
    emit(Z_U, SGU_WIDTH, _gelu)
    emit(Z_SG, RET_V_W, jax.nn.silu)
    emit(Z_GA, d, jax.nn.sigmoid)
    emit(Z_GB, d, jax.nn.sigmoid)
    emit(Z_V, RET_V_W, lambda z: z)


def _inproj_call(x, mod3, w_in, b_in, rope_t, sgu_ln_w, sgu_ln_b):
    bsz, t, d = x.shape
    tm = TM_INPROJ
    row = lambda b, i: (b, i, 0)
    const2 = lambda b, i: (0, 0)
    return pl.pallas_call(
        _inproj_kernel,
        out_shape=jax.ShapeDtypeStruct((bsz, t, D_IN), BF16),
        grid=(bsz, t // tm),
        in_specs=[
            pl.BlockSpec((1, tm, d), row),
            pl.BlockSpec((1, 1, 6 * d), lambda b, i: (b, 0, 0)),
            pl.BlockSpec((d, D_IN), const2, pipeline_mode=pl.Buffered(1)),
            pl.BlockSpec((1, D_IN), const2),
            pl.BlockSpec((tm, 2 * LANES), lambda b, i: (i, 0)),
            pl.BlockSpec((1, SGU_WIDTH), const2),
            pl.BlockSpec((1, SGU_WIDTH), const2),
        ],
        out_specs=pl.BlockSpec((1, tm, D_IN), row),
        compiler_params=_cparams(("parallel", "parallel")),
        name="inproj",
    )(x, mod3, w_in, b_in, rope_t, sgu_ln_w, sgu_ln_b)


def _dot_t0(a, b):
    return lax.dot_general(a, b, (((0,), (0,)), ((), ())), preferred_element_type=F32)


def _state_kernel(lg_ref, k_ref, v_ref, kc_ref, vc_ref, s_ref):
    hd = pl.program_id(1)
    lgf = lg_ref[0, hd]
    lgb = lg_ref[1, hd]
    n_chunks = s_ref.shape[2]
    n_ctx = kc_ref.shape[1]
    c = RET_CHUNK
    li = _iota_f((c, RET_QK_DIM), 0)
    zeta_f = jnp.exp((c - 1.0 - li) * lgf)
    zeta_b = jnp.exp(li * lgb)
    one = jnp.ones((1, 1), F32)
    cd_f = jnp.exp(one * (c * lgf))
    cd_b = jnp.exp(one * (c * lgb))

    tc = _iota_f((n_ctx, RET_QK_DIM), 0)
    kc = kc_ref[0].astype(F32)
    vc = vc_ref[0]
    s0f = _dot_t0((kc * jnp.exp((n_ctx - 1.0 - tc) * lgf)).astype(BF16), vc)
    s0b = _dot_t0((kc * jnp.exp(tc * lgb)).astype(BF16), vc)

    def chunk_kv(n, zeta):
        off = pl.multiple_of(n * c, c)
        kk = k_ref[0, pl.ds(off, c), :].astype(F32)
        return _dot_t0((kk * zeta).astype(BF16), v_ref[0, pl.ds(off, c), :])

    def step(i, carry):
        sf, sb = carry
        nb = n_chunks - 1 - i
        s_ref[0, 0, i, 0:RET_QK_DIM, :] = sf.astype(BF16)
        s_ref[0, 0, nb, RET_QK_DIM:2 * RET_QK_DIM, :] = sb.astype(BF16)
        return cd_f * sf + chunk_kv(i, zeta_f), cd_b * sb + chunk_kv(nb, zeta_b)

    lax.fori_loop(0, n_chunks, step, (s0f, s0b), unroll=STATE_UNROLL)


def _state_call(lg, z, kctx, vctx):
    bsz, t, _ = z.shape
    n_ctx = kctx.shape[1]
    n_chunks = t // RET_CHUNK
    return pl.pallas_call(
        _state_kernel,
        out_shape=jax.ShapeDtypeStruct((bsz, RET_HEADS, n_chunks, 2 * RET_QK_DIM, RET_V_DIM), BF16),
        grid=(bsz, RET_HEADS),
        in_specs=[
            pl.BlockSpec(memory_space=pltpu.SMEM),
            pl.BlockSpec((1, t, RET_QK_DIM), lambda b, h: (b, 0, RET_QK_W // RET_QK_DIM + h)),
            pl.BlockSpec((1, t, RET_V_DIM), lambda b, h: (b, 0, Z_V // RET_V_DIM + h)),
            pl.BlockSpec((1, n_ctx, RET_QK_DIM), lambda b, h: (b, 0, h)),
            pl.BlockSpec((1, n_ctx, RET_V_DIM), lambda b, h: (b, 0, h)),
        ],
        out_specs=pl.BlockSpec((1, 1, n_chunks, 2 * RET_QK_DIM, RET_V_DIM), lambda b, h: (b, h, 0, 0, 0)),
        compiler_params=_cparams(("parallel", "parallel")),
        name="states",
    )(lg, z, z, kctx, vctx)


def _mixer_kernel(lg_ref, za_ref, zb_ref, gb_ref, s_ref, x_ref, mod_ref,
                  wpa_ref, wpb_ref, wo_ref, sguw_ref, sgub_ref, gnw_ref, gnb_ref, bo_ref,
                  ln1w_ref, ln1b_ref, wr_ref, br_ref,
                  x1_ref, hm_ref, lgt_ref, ret_scr, sgu_scr):
    d = D_MODEL
    c = RET_CHUNK

    n_sub = za_ref.shape[1] // c
    row = lax.broadcasted_iota(I32, (c, c), 0)
    col = lax.broadcasted_iota(I32, (c, c), 1)
    diff = (row - col).astype(F32)
    rowq = _iota_f((c, RET_QK_DIM), 0)
    for hd in range(RET_HEADS):
        lgf = lg_ref[0, hd]
        lgb = lg_ref[1, hd]
        vcols = slice(hd * RET_V_DIM, (hd + 1) * RET_V_DIM)
        mask = (jnp.where(diff >= 0, jnp.exp(jnp.maximum(diff, 0.0) * lgf), 0.0)
                + jnp.where(diff <= 0, jnp.exp(jnp.maximum(-diff, 0.0) * lgb), 0.0))
        xi_f = jnp.exp((rowq + 1.0) * lgf)
        xi_b = jnp.exp((c - rowq) * lgb)
        for ci in range(n_sub):
            r0 = ci * c
            q = za_ref[0, r0:r0 + c, hd * RET_QK_DIM:(hd + 1) * RET_QK_DIM]
            k = za_ref[0, r0:r0 + c, RET_QK_W + hd * RET_QK_DIM:RET_QK_W + (hd + 1) * RET_QK_DIM]
            s = lax.dot_general(q, k, (((1,), (1,)), ((), ())), preferred_element_type=F32)
            vv = za_ref[0, r0:r0 + c, Z_V + hd * RET_V_DIM:Z_V + (hd + 1) * RET_V_DIM]
            intra = jnp.dot((s * mask).astype(BF16), vv, preferred_element_type=F32)
            qf = q.astype(F32)
            qx = jnp.concatenate([(qf * xi_f).astype(BF16), (qf * xi_b).astype(BF16)], axis=1)
            cross = jnp.dot(qx, s_ref[0, hd, ci], preferred_element_type=F32)
            o = _ln(intra + cross) * gnw_ref[:, vcols] + gnb_ref[:, vcols]
            gate = za_ref[0, r0:r0 + c, Z_SG + hd * RET_V_DIM:Z_SG + (hd + 1) * RET_V_DIM]
            ret_scr[r0:r0 + c, vcols] = (o * gate.astype(F32)).astype(BF16)

    for g in range(SGU_GROUPS):
        gcols = slice(g * SGU_GW, (g + 1) * SGU_GW)
        wg = sguw_ref[g]
        bg = sgub_ref[:, g:g + 1]
        for ci in range(n_sub):
            r0 = ci * c
            vsb = zb_ref[0, r0:r0 + c, Z_VS - Z_BLOCK + g * SGU_GW:Z_VS - Z_BLOCK + (g + 1) * SGU_GW]
            ub = zb_ref[0, r0:r0 + c, Z_U - Z_BLOCK + g * SGU_GW:Z_U - Z_BLOCK + (g + 1) * SGU_GW]
            sp = jnp.dot(wg, vsb, preferred_element_type=F32) + bg
            sgu_scr[r0:r0 + c, gcols] = (ub.astype(F32) * sp).astype(BF16)

    g1 = mod_ref[0, :, 2 * d:3 * d]
    sh2 = mod_ref[0, :, 3 * d:4 * d]
    sc2 = mod_ref[0, :, 4 * d:5 * d]
    for r0 in range(0, za_ref.shape[1], MERGE_ROWS):
        rows = slice(r0, r0 + MERGE_ROWS)
        pa = jnp.dot(ret_scr[rows, :], wpa_ref[...], preferred_element_type=F32)
        pb = jnp.dot(sgu_scr[rows, :], wpb_ref[...], preferred_element_type=F32)
        ga = zb_ref[0, rows, Z_GA - Z_BLOCK:Z_GA - Z_BLOCK + d]
        y = (ga.astype(F32) * pa + gb_ref[0, rows, :].astype(F32) * pb).astype(BF16)
        mix = jnp.dot(y, wo_ref[...], preferred_element_type=F32) + bo_ref[...]
        x1 = _ln(DEEPNORM_ALPHA * x_ref[0, rows, :] + g1 * mix) * ln1w_ref[...] + ln1b_ref[...]
        x1_ref[0, rows, :] = x1
        hm = (_ln(x1) * (1.0 + sc2) + sh2).astype(BF16)
        hm_ref[0, rows, :] = hm
        lgt_ref[0, :, rows] = lax.dot_general(wr_ref[...], hm, (((1,), (1,)), ((), ())),
                                              preferred_element_type=F32) + br_ref[...]


def _mixer_call(lg, z, states, x, mod3, wpa, wpb, wo, sguw, sgub_t, gnw, gnb, bo,
                ln1w, ln1b, wr_t, br):
    bsz, t, d = x.shape
    tt = TT_MIXER
    n_sub = tt // RET_CHUNK
    row = lambda b, i: (b, i, 0)
    c2 = lambda b, i: (0, 0)
    c3 = lambda b, i: (0, 0, 0)
    return pl.pallas_call(
        _mixer_kernel,
        out_shape=(jax.ShapeDtypeStruct((bsz, t, d), F32),
                   jax.ShapeDtypeStruct((bsz, t, d), BF16),
                   jax.ShapeDtypeStruct((bsz, N_EXPERTS, t), F32)),
        grid=(bsz, t // tt),
        in_specs=[
            pl.BlockSpec(memory_space=pltpu.SMEM),
            pl.BlockSpec((1, tt, Z_BLOCK), lambda b, i: (b, i, 0)),
            pl.BlockSpec((1, tt, Z_BLOCK), lambda b, i: (b, i, 1)),
            pl.BlockSpec((1, tt, d), lambda b, i: (b, i, Z_GB // d)),
            pl.BlockSpec((1, RET_HEADS, n_sub, 2 * RET_QK_DIM, RET_V_DIM), lambda b, i: (b, 0, i, 0, 0)),
            pl.BlockSpec((1, tt, d), row),
            pl.BlockSpec((1, 1, 6 * d), lambda b, i: (b, 0, 0)),
            pl.BlockSpec((RET_V_W, d), c2, pipeline_mode=pl.Buffered(1)),
            pl.BlockSpec((SGU_WIDTH, d), c2, pipeline_mode=pl.Buffered(1)),
            pl.BlockSpec((d, d), c2, pipeline_mode=pl.Buffered(1)),
            pl.BlockSpec((SGU_GROUPS, SGU_CHUNK, SGU_CHUNK), c3),
            pl.BlockSpec((SGU_CHUNK, SGU_GROUPS), c2),
            pl.BlockSpec((1, RET_V_W), c2),
            pl.BlockSpec((1, RET_V_W), c2),
            pl.BlockSpec((1, d), c2),
            pl.BlockSpec((1, d), c2),
            pl.BlockSpec((1, d), c2),
            pl.BlockSpec((N_EXPERTS, d), c2),
            pl.BlockSpec((N_EXPERTS, 1), c2),
        ],
        out_specs=(pl.BlockSpec((1, tt, d), row),
                   pl.BlockSpec((1, tt, d), row),
                   pl.BlockSpec((1, N_EXPERTS, tt), lambda b, i: (b, 0, i))),
        scratch_shapes=[pltpu.VMEM((tt, RET_V_W), BF16), pltpu.VMEM((tt, SGU_WIDTH), BF16)],
        compiler_params=pltpu.CompilerParams(
            dimension_semantics=("parallel", "parallel"), vmem_limit_bytes=VMEM_LIMIT,
            allow_input_fusion=[i in (7, 8, 9, 10, 11, 17) for i in range(19)]),
        name="mixer",
    )(lg, z, z, z, states, x, mod3, wpa, wpb, wo, sguw, sgub_t, gnw, gnb, bo,
      ln1w, ln1b, wr_t, br)


def _route_kernel(lgt_ref, pos_ref, gate_ref, cnt_ref, *, cap):
    lg = lgt_ref[0]
    n_e, t = lg.shape
    tb = TB_ROUTE
    m = jnp.max(lg, axis=0, keepdims=True)
    ex = jnp.exp(lg - m)
    aff = ex / jnp.sum(ex, axis=0, keepdims=True)

    def search(i, thr_bits):
        cand = thr_bits | lax.shift_left(jnp.int32(1), 30 - i)
        cnt = jnp.sum((aff >= lax.bitcast_convert_type(cand, F32)).astype(I32), axis=1, keepdims=True)
        return jnp.where(cnt >= cap, cand, thr_bits)

    thr_bits = lax.fori_loop(0, 31, search, jnp.zeros((n_e, 1), I32))
    floor_f = lax.bitcast_convert_type(thr_bits, F32)
    thr = jnp.min(jnp.where(aff >= floor_f, aff, jnp.inf), axis=1, keepdims=True)
    need = (cap - jnp.sum((aff > thr).astype(I32), axis=1, keepdims=True)).astype(F32)

    r = lax.broadcasted_iota(I32, (tb, tb), 0)
    cc = lax.broadcasted_iota(I32, (tb, tb), 1)
    tri = (r <= cc).astype(BF16)
    carry_eq = jnp.zeros((n_e, 1), F32)
    carry_sel = jnp.zeros((n_e, 1), F32)
    for blk in range(t // tb):
        sl = slice(blk * tb, (blk + 1) * tb)
        aff_b = aff[:, sl]
        eq = aff_b == thr
        eq_b = eq.astype(BF16)
        inc_eq = jnp.dot(eq_b, tri, preferred_element_type=F32)
        before = carry_eq + inc_eq - eq_b.astype(F32)
        sel = (aff_b > thr) | (eq & (before < need))
        sel_b = sel.astype(BF16)
        inc_sel = jnp.dot(sel_b, tri, preferred_element_type=F32)
        pos = carry_sel + inc_sel - 1.0
        pos_ref[0, blk] = jnp.where(sel, pos.astype(I32), -1)
        gate_ref[0, blk] = aff_b
        cnt_ref[0, blk] = jnp.broadcast_to(carry_sel, (n_e, LANES)).astype(I32)
        carry_eq = carry_eq + inc_eq[:, tb - 1:tb]
        carry_sel = carry_sel + inc_sel[:, tb - 1:tb]


def _route_call(logits_t, cap):
    bsz, n_e, t = logits_t.shape
    nblk = t // TB_ROUTE
    return pl.pallas_call(
        functools.partial(_route_kernel, cap=cap),
        out_shape=(jax.ShapeDtypeStruct((bsz, nblk, n_e, TB_ROUTE), I32),
                   jax.ShapeDtypeStruct((bsz, nblk, n_e, TB_ROUTE), F32),
                   jax.ShapeDtypeStruct((bsz, nblk, n_e, LANES), I32)),
        grid=(bsz,),
        in_specs=[pl.BlockSpec((1, n_e, t), lambda b: (b, 0, 0))],
        out_specs=(pl.BlockSpec((1, nblk, n_e, TB_ROUTE), lambda b: (b, 0, 0, 0)),
                   pl.BlockSpec((1, nblk, n_e, TB_ROUTE), lambda b: (b, 0, 0, 0)),
                   pl.BlockSpec((1, nblk, n_e, LANES), lambda b: (b, 0, 0, 0))),
        compiler_params=_cparams(("parallel",)),
        name="route",
    )(logits_t)


def _window_rows(ws, win, n_tok):
    return lax.broadcasted_iota(I32, (win, n_tok), 0) + ws


def _sweep_window(w, cap, win, prow, rest):
    off = pl.multiple_of(jnp.minimum(w * win, cap - win), WIN_ALIGN)
    rows = _window_rows(off, win, prow.shape[-1])
    return off, (rows == prow) & rest & (rows >= w * win)


def _gather_kernel(ws_ref, nx_ref, hm_ref, pos_ref, gate_ref, xs_ref, gc_ref):
    b = pl.program_id(0)
    g = pl.program_id(1)
    step = pl.program_id(2)
    merge = DISPATCH_MERGE
    eg, tb = pos_ref.shape[2:]
    n_sub = pos_ref.shape[1] // merge
    n_e = eg * pl.num_programs(1)
    n_win = n_sub * pl.num_programs(2)
    cap = xs_ref.shape[2]
    win = WIN_DISPATCH

    @pl.when(step == 0)
    def _():
        xs_ref[...] = jnp.zeros_like(xs_ref)
        gc_ref[...] = jnp.zeros_like(gc_ref)

    def base(sub):
        return (b * n_win + step * n_sub + sub) * n_e + g * eg

    def tokens(sub):
        return hm_ref[0, sub * merge * tb:(sub + 1) * merge * tb, :]

    def lanes(ref, sub, el):
        return jnp.concatenate([ref[0, sub * merge + m, el:el + 1, :] for m in range(merge)], axis=-1)

    for sub in range(n_sub):
        for el in range(eg):
            ws = pl.multiple_of(ws_ref[base(sub) + el], WIN_ALIGN)
            match = _window_rows(ws, win, merge * tb) == lanes(pos_ref, sub, el)
            sl = pl.ds(ws, win)
            xs_ref[0, el, sl, :] += jnp.dot(match.astype(BF16), tokens(sub),
                                            preferred_element_type=F32).astype(BF16)
            gc_ref[0, el, sl, :] += jnp.sum(jnp.where(match, lanes(gate_ref, sub, el), 0.0),
                                             axis=1, keepdims=True)

    for sub in range(n_sub):
        for el in range(eg):
            @pl.when(nx_ref[base(sub) + el] > 0)
            def _(sub=sub, el=el):
                prow = lanes(pos_ref, sub, el)
                grow = lanes(gate_ref, sub, el)
                rest = prow >= ws_ref[base(sub) + el] + win

                def body(w, carry):
                    off, m = _sweep_window(w, cap, win, prow, rest)
                    xs_ref[0, el, pl.ds(off, win), :] += jnp.dot(m.astype(BF16), tokens(sub),
                                                                  preferred_element_type=F32).astype(BF16)
                    gc_ref[0, el, pl.ds(off, win), :] += jnp.sum(jnp.where(m, grow, 0.0), axis=1, keepdims=True)
                    return carry

                lax.fori_loop(0, pl.cdiv(cap, win), body, 0)


def _gather_call(ws, nx, hm, pos_b, gate_b, cap):
    bsz, t, d = hm.shape
    nblk, n_e, tb = pos_b.shape[1:]
    eg = n_e // DISPATCH_GROUPS
    blocks = DISPATCH_STEP_WINDOWS * DISPATCH_MERGE
    grid_spec = pltpu.PrefetchScalarGridSpec(
        num_scalar_prefetch=2,
        grid=(bsz, DISPATCH_GROUPS, nblk // blocks),
        in_specs=[
            pl.BlockSpec((1, blocks * tb, d), lambda b, g, k, ws, nx: (b, k, 0)),
            pl.BlockSpec((1, blocks, eg, tb), lambda b, g, k, ws, nx: (b, k, g, 0)),
            pl.BlockSpec((1, blocks, eg, tb), lambda b, g, k, ws, nx: (b, k, g, 0)),
        ],
        out_specs=(pl.BlockSpec((1, eg, cap, d), lambda b, g, k, ws, nx: (b, g, 0, 0)),
                   pl.BlockSpec((1, eg, cap, 1), lambda b, g, k, ws, nx: (b, g, 0, 0))),
    )
    return pl.pallas_call(
        _gather_kernel,
        out_shape=(jax.ShapeDtypeStruct((bsz, n_e, cap, d), BF16),
                   jax.ShapeDtypeStruct((bsz, n_e, cap, 1), F32)),
        grid_spec=grid_spec,
        compiler_params=_cparams(("parallel", "parallel", "arbitrary")),
        name="gather",
    )(ws, nx, hm, pos_b, gate_b)


def _ffn_kernel(xs_ref, gc_ref, wg_ref, wu_ref, wd_ref, ye_ref, wgu_s, wd_s):
    f = wd_s.shape[0]
    wgu_s[:, 0:f] = wg_ref[0].astype(BF16)
    wgu_s[:, f:2 * f] = wu_ref[0].astype(BF16)
    wd_s[...] = wd_ref[0].astype(BF16)

    rt = FFN_ROWS
    for b in range(xs_ref.shape[0]):
        for j in range(xs_ref.shape[2] // rt):
            sl = slice(j * rt, (j + 1) * rt)
            xgu = jnp.dot(xs_ref[b, 0, sl, :], wgu_s[...], preferred_element_type=F32)
            hid = (jax.nn.silu(xgu[:, 0:f]) * xgu[:, f:2 * f]).astype(BF16)
            ye = jnp.dot(hid, wd_s[...], preferred_element_type=F32) * gc_ref[b, 0, sl, :]
            ye_ref[b, 0, sl, :] = ye.astype(BF16)


def _ffn_call(xs, gc, w_gate, w_up, w_down):
    bsz, n_e, cap, d = xs.shape
    f = w_gate.shape[2]
    return pl.pallas_call(
        _ffn_kernel,
        out_shape=jax.ShapeDtypeStruct((bsz, n_e, cap, d), BF16),
        grid=(n_e,),
        in_specs=[
            pl.BlockSpec((bsz, 1, cap, d), lambda e: (0, e, 0, 0)),
            pl.BlockSpec((bsz, 1, cap, 1), lambda e: (0, e, 0, 0)),
            pl.BlockSpec((1, d, f), lambda e: (e, 0, 0)),
            pl.BlockSpec((1, d, f), lambda e: (e, 0, 0)),
            pl.BlockSpec((1, f, d), lambda e: (e, 0, 0)),
        ],
        out_specs=pl.BlockSpec((bsz, 1, cap, d), lambda e: (0, e, 0, 0)),
        scratch_shapes=[pltpu.VMEM((d, 2 * f), BF16), pltpu.VMEM((f, d), BF16)],
        compiler_params=_cparams(("parallel",)),
        name="ffn",
    )(xs, gc, w_gate, w_up, w_down)


def _combine_kernel(ws_ref, nx_ref, ye_ref, pos_ref, x1_ref, mod_ref, lnw_ref, lnb_ref, out_ref, acc_scr):
    b = pl.program_id(0)
    step = pl.program_id(1)
    n_sub, n_e, tb = pos_ref.shape[1:]
    nblk = n_sub * pl.num_programs(1)
    cap = ye_ref.shape[2]
    d = D_MODEL
    win = WIN_COMBINE
    g2 = mod_ref[0, :, 5 * d:6 * d]

    for sub in range(n_sub):
        base = (b * nblk + step * n_sub + sub) * n_e
        starts = [pl.multiple_of(ws_ref[base + e], WIN_ALIGN) for e in range(n_e)]
        onehot = jnp.concatenate(
            [(_window_rows(starts[e], win, tb) == pos_ref[0, sub, e:e + 1, :]).astype(BF16) for e in range(n_e)],
            axis=0)
        rows = jnp.concatenate([ye_ref[0, e, pl.ds(starts[e], win), :] for e in range(n_e)], axis=0)
        acc_scr[...] = _dot_t0(onehot, rows)

        for e in range(n_e):
            @pl.when(nx_ref[base + e] > 0)
            def _(e=e, sub=sub, starts=starts):
                prow = pos_ref[0, sub, e:e + 1, :]
                rest = prow >= starts[e] + win

                def body(w, carry):
                    off, m = _sweep_window(w, cap, win, prow, rest)
                    acc_scr[...] += _dot_t0(m.astype(BF16), ye_ref[0, e, pl.ds(off, win), :])
                    return carry

                lax.fori_loop(0, pl.cdiv(cap, win), body, 0)

        tok = slice(sub * tb, (sub + 1) * tb)
        out_ref[0, tok, :] = (_ln(DEEPNORM_ALPHA * x1_ref[0, tok, :] + g2 * acc_scr[...]) * lnw_ref[...]
                              + lnb_ref[...])


def _combine_call(ws, nx, ye, pos_b, x1, mod3, ln2w, ln2b):
    bsz, n_e, cap, d = ye.shape
    nblk, _, tb = pos_b.shape[1:]
    n_sub = COMBINE_STEP_BLOCKS
    grid_spec = pltpu.PrefetchScalarGridSpec(
        num_scalar_prefetch=2,
        grid=(bsz, nblk // n_sub),
        in_specs=[
            pl.BlockSpec((1, n_e, cap, d), lambda b, k, ws, nx: (b, 0, 0, 0), pipeline_mode=pl.Buffered(1)),
            pl.BlockSpec((1, n_sub, n_e, tb), lambda b, k, ws, nx: (b, k, 0, 0)),
            pl.BlockSpec((1, n_sub * tb, d), lambda b, k, ws, nx: (b, k, 0)),
            pl.BlockSpec((1, 1, 6 * d), lambda b, k, ws, nx: (b, 0, 0)),
            pl.BlockSpec((1, d), lambda b, k, ws, nx: (0, 0)),
            pl.BlockSpec((1, d), lambda b, k, ws, nx: (0, 0)),
        ],
        out_specs=pl.BlockSpec((1, n_sub * tb, d), lambda b, k, ws, nx: (b, k, 0)),
        scratch_shapes=[pltpu.VMEM((tb, d), F32)],
    )
    return pl.pallas_call(
        _combine_kernel,
        out_shape=jax.ShapeDtypeStruct((bsz, nblk * tb, d), F32),
        grid_spec=grid_spec,
        compiler_params=_cparams(("parallel", "arbitrary")),
        name="combine",
    )(ws, nx, ye, pos_b, x1, mod3, ln2w, ln2b)


def _rope_tables(n_tokens):
    rows = n_tokens // GRID_W
    n_freq = RET_QK_DIM // 4
    inv = ROPE_THETA ** (-jnp.arange(n_freq, dtype=F32) / n_freq)
    ang_r = jnp.arange(rows, dtype=F32)[:, None] * inv
    ang_c = jnp.arange(GRID_W, dtype=F32)[:, None] * inv
    zr = jnp.zeros((rows, 2 * n_freq), F32)
    zc = jnp.zeros((GRID_W, 2 * n_freq), F32)
    cr, sr, cc, sc = jnp.cos(ang_r), jnp.sin(ang_r), jnp.cos(ang_c), jnp.sin(ang_c)
    cos_t = (jnp.concatenate([cr, cr, zr], axis=-1)[:, None, :]
             + jnp.concatenate([zc, cc, cc], axis=-1)[None, :, :]).reshape(n_tokens, RET_QK_DIM)
    sin_t = (jnp.concatenate([-sr, sr, zr], axis=-1)[:, None, :]
             + jnp.concatenate([zc, -sc, sc], axis=-1)[None, :, :]).reshape(n_tokens, RET_QK_DIM)
    return jnp.concatenate([cos_t, sin_t], axis=-1)


def kernel(x, c, ctx, c_ctx, w_ada, b_ada, w_in, b_in, ret_decay_f, ret_decay_b, ret_gn_w, ret_gn_b,
           sgu_ln_w, sgu_ln_b, sgu_w, sgu_b, w_pa, w_pb, w_o, b_o, ln1_w, ln1_b,
           w_router, b_router, w_gate, w_up, w_down, ln2_w, ln2_b):
    bsz, t, d = x.shape
    assert d == D_MODEL and w_ada.shape[0] == DEPTH == 1 and bsz < MOD_ROWS
    cap = EC_CAPACITY_FACTOR * t // N_EXPERTS
    for win in (WIN_DISPATCH, WIN_COMBINE):
        assert cap >= win and (cap - win) % WIN_ALIGN == 0 and win % WIN_ALIGN == 0
    assert t % (TB_ROUTE * DISPATCH_MERGE * DISPATCH_STEP_WINDOWS) == 0 and t % (TB_ROUTE * COMBINE_STEP_BLOCKS) == 0
    assert N_EXPERTS % DISPATCH_GROUPS == 0
    l = 0

    cvec = jnp.concatenate([c, c_ctx[None], jnp.zeros((MOD_ROWS - bsz - 1, d), F32)], axis=0)
    mod = _mod_call(cvec.T, w_ada[l], b_ada[l][None], bsz + 1)
    mod3 = mod[:, None, :]

    w_in_b = w_in[l].astype(BF16)
    b_in2 = b_in[l][None]
    kctx, vctx = _ctx_kv_call(ctx, mod3, w_in_b, b_in2, bsz)

    z = _inproj_call(x, mod3, w_in_b, b_in2, _rope_tables(t), sgu_ln_w[l][None], sgu_ln_b[l][None])

    lg = jnp.stack([jax.nn.log_sigmoid(ret_decay_f[l].astype(F32)),
                    jax.nn.log_sigmoid(ret_decay_b[l].astype(F32))])
    states = _state_call(lg, z, kctx, vctx)

    x1, hm, logits_t = _mixer_call(
        lg, z, states, x, mod3,
        w_pa[l].astype(BF16), w_pb[l].astype(BF16), w_o[l].astype(BF16),
        sgu_w[l].astype(BF16), sgu_b[l].T, ret_gn_w[l][None], ret_gn_b[l][None], b_o[l][None],
        ln1_w[l][None], ln1_b[l][None], w_router[l].T.astype(BF16), b_router[l][:, None])

    pos_b, gate_b, cnt_b = _route_call(logits_t, cap)

    c0 = cnt_b[..., 0]
    c1 = jnp.concatenate([c0[:, 1:], jnp.full_like(c0[:, :1], cap)], axis=1)

    def windows(start, end, win):
        ws = jnp.minimum((start // WIN_ALIGN) * WIN_ALIGN, cap - win)
        return ws.reshape(-1), (end > ws + win).astype(I32).reshape(-1)

    ws_d, nx_d = windows(c0[:, ::DISPATCH_MERGE], c1[:, DISPATCH_MERGE - 1::DISPATCH_MERGE], WIN_DISPATCH)
    ws_c, nx_c = windows(c0, c1, WIN_COMBINE)

    xs, gc = _gather_call(ws_d, nx_d, hm, pos_b, gate_b, cap)
    ye = _ffn_call(xs, gc, w_gate[l], w_up[l], w_down[l])
    return _combine_call(ws_c, nx_c, ye, pos_b, x1, mod3, ln2_w[l][None], ln2_b[l][None])
```

```python
import functools
import math

import jax
import jax.numpy as jnp
from jax import lax
from jax.experimental import pallas as pl
from jax.experimental.pallas import tpu as pltpu

F32 = jnp.float32
BF16 = jnp.bfloat16
I32 = jnp.int32

D_MODEL = 1024
DEPTH = 1
GRID_W = 64
RET_HEADS = 4
RET_QK_DIM = 128
RET_V_DIM = 256
RET_CHUNK = 128
ROPE_THETA = 10000.0
RET_QK_W = RET_HEADS * RET_QK_DIM
RET_V_W = RET_HEADS * RET_V_DIM
SGU_GROUPS = 4
SGU_CHUNK = 128
SGU_WIDTH = 1024
SGU_GW = SGU_WIDTH // SGU_GROUPS
N_EXPERTS = 16
EC_CAPACITY_FACTOR = 2
LN_EPS = 1e-6
DEEPNORM_ALPHA = (2.0 * DEPTH) ** 0.25
D_IN = 2 * RET_QK_W + 2 * RET_V_W + 2 * SGU_WIDTH + 2 * D_MODEL
Z_K = RET_QK_W
Z_V = 2 * RET_QK_W
Z_SG = Z_V + RET_V_W
Z_U = Z_SG + RET_V_W
Z_VS = Z_U + SGU_WIDTH
Z_GA = Z_VS + SGU_WIDTH
Z_GB = Z_GA + D_MODEL
Z_BLOCK = 3 * 1024

V7X_VMEM_BYTES = 64 * 1024 * 1024
VMEM_LIMIT = V7X_VMEM_BYTES - 4 * 1024 * 1024
LANES = 128

MOD_ROWS = 8
MOD_COLS = 2048
TM_INPROJ = 1024
INPROJ_CHUNK = 256
TT_MIXER = 512
MERGE_ROWS = 512
TB_ROUTE = 256
DISPATCH_MERGE = 2
WIN_DISPATCH = 128
WIN_COMBINE = 64
WIN_ALIGN = 16
DISPATCH_GROUPS = 2
DISPATCH_STEP_WINDOWS = 4
COMBINE_STEP_BLOCKS = 4
STATE_HEADS = 2
STATE_UNROLL = 4
FFN_ROWS = 256


def _cparams(sem):
    return pltpu.CompilerParams(dimension_semantics=sem, vmem_limit_bytes=VMEM_LIMIT)


def _ln(x):
    mu = jnp.mean(x, axis=-1, keepdims=True)
    xc = x - mu
    var = jnp.mean(xc * xc, axis=-1, keepdims=True)
    return xc * lax.rsqrt(var + LN_EPS)


_GELU_A = -2.0 * math.sqrt(2.0 / math.pi) * math.log2(math.e)
_GELU_B = _GELU_A * 0.044715


def _gelu(x):
    return x / (1.0 + jnp.exp2(x * (_GELU_A + _GELU_B * (x * x))))


def _iota_f(shape, dim):
    return lax.broadcasted_iota(I32, shape, dim).astype(F32)


def _mod_kernel(ct_ref, w_ref, b_ref, o_ref, *, n_rows):
    c = ct_ref[...]
    a = c * jax.nn.sigmoid(c)
    w = w_ref[...]
    rows = [jnp.sum(w * a[:, m:m + 1], axis=0, keepdims=True) + b_ref[...] for m in range(n_rows)]
    rows += [jnp.zeros_like(rows[0])] * (MOD_ROWS - n_rows)
    o_ref[...] = jnp.concatenate(rows, axis=0)


def _mod_call(cvec_t, w_ada, b_ada, n_rows):
    d, n = w_ada.shape
    tn = MOD_COLS
    return pl.pallas_call(
        functools.partial(_mod_kernel, n_rows=n_rows),
        out_shape=jax.ShapeDtypeStruct((MOD_ROWS, n), F32),
        grid=(n // tn,),
        in_specs=[
            pl.BlockSpec((d, MOD_ROWS), lambda j: (0, 0)),
            pl.BlockSpec((d, tn), lambda j: (0, j)),
            pl.BlockSpec((1, tn), lambda j: (0, j)),
        ],
        out_specs=pl.BlockSpec((MOD_ROWS, tn), lambda j: (0, j)),
        compiler_params=_cparams(("parallel",)),
        name="mod",
    )(cvec_t, w_ada, b_ada)


def _ctx_kv_kernel(x_ref, mod_ref, wk_ref, wv_ref, bk_ref, bv_ref, k_ref, v_ref):
    d = D_MODEL
    sh = mod_ref[0, :, 0:d]
    sc = mod_ref[0, :, d:2 * d]
    h = (_ln(x_ref[0]) * (1.0 + sc) + sh).astype(BF16)
    k = jnp.dot(h, wk_ref[...], preferred_element_type=F32) + bk_ref[...]
    k_ref[0] = (k * (RET_QK_DIM ** -0.5)).astype(BF16)
    v_ref[0] = (jnp.dot(h, wv_ref[...], preferred_element_type=F32) + bv_ref[...]).astype(BF16)


def _ctx_kv_call(ctx, mod3, w_in, b_in, ctx_row):
    bsz, n_ctx, d = ctx.shape
    kblk = Z_K // RET_QK_W
    vblk = Z_V // RET_V_W
    return pl.pallas_call(
        _ctx_kv_kernel,
        out_shape=(jax.ShapeDtypeStruct((bsz, n_ctx, RET_QK_W), BF16),
                   jax.ShapeDtypeStruct((bsz, n_ctx, RET_V_W), BF16)),
        grid=(bsz,),
        in_specs=[
            pl.BlockSpec((1, n_ctx, d), lambda b: (b, 0, 0)),
            pl.BlockSpec((1, 1, 6 * d), lambda b: (ctx_row, 0, 0)),
            pl.BlockSpec((d, RET_QK_W), lambda b: (0, kblk)),
            pl.BlockSpec((d, RET_V_W), lambda b: (0, vblk)),
            pl.BlockSpec((1, RET_QK_W), lambda b: (0, kblk)),
            pl.BlockSpec((1, RET_V_W), lambda b: (0, vblk)),
        ],
        out_specs=(pl.BlockSpec((1, n_ctx, RET_QK_W), lambda b: (b, 0, 0)),
                   pl.BlockSpec((1, n_ctx, RET_V_W), lambda b: (b, 0, 0))),
        compiler_params=_cparams(("parallel",)),
        name="ctx_kv",
    )(ctx, mod3, w_in, w_in, b_in, b_in)


def _inproj_kernel(x_ref, mod_ref, w_ref, b_ref, rope_ref, lnw_ref, lnb_ref, z_ref):
    d = D_MODEL
    sh = mod_ref[0, :, 0:d]
    sc = mod_ref[0, :, d:2 * d]
    h = (_ln(x_ref[0]) * (1.0 + sc) + sh).astype(BF16)

    def proj(c0, width):
        return jnp.dot(h, w_ref[:, c0:c0 + width], preferred_element_type=F32) + b_ref[:, c0:c0 + width]

    zq = proj(0, 2 * RET_QK_W)
    cos = rope_ref[:, 0:LANES]
    sin = rope_ref[:, LANES:2 * LANES]
    tm = zq.shape[0]
    upper = (lax.broadcasted_iota(I32, (tm, LANES), 1) & 32) != 0
    for hb in range(2 * RET_HEADS):
        zs = zq[:, hb * LANES:(hb + 1) * LANES]
        if hb >= RET_HEADS:
            zs = zs * (RET_QK_DIM ** -0.5)
        sw = jnp.where(upper, pltpu.roll(zs, 32, 1), pltpu.roll(zs, LANES - 32, 1))
        z_ref[0, :, hb * LANES:(hb + 1) * LANES] = (zs * cos + sw * sin).astype(BF16)

    def emit(c0, width, fn):
        for cc in range(c0, c0 + width, INPROJ_CHUNK):
            z_ref[0, :, cc:cc + INPROJ_CHUNK] = fn(proj(cc, INPROJ_CHUNK)).astype(BF16)

    vs = _ln(_gelu(proj(Z_VS, SGU_WIDTH))) * lnw_ref[...] + lnb_ref[...]
    z_ref[0, :, Z_VS:Z_VS + SGU_WIDTH] = vs.astype(BF16)
    emit(Z_U, SGU_WIDTH, _gelu)
    emit(Z_SG, RET_V_W, jax.nn.silu)
    emit(Z_GA, d, jax.nn.sigmoid)
    emit(Z_GB, d, jax.nn.sigmoid)
    emit(Z_V, RET_V_W, lambda z: z)


def _inproj_call(x, mod3, w_in, b_in, rope_t, sgu_ln_w, sgu_ln_b):
    bsz, t, d = x.shape
    tm = TM_INPROJ
    row = lambda b, i: (b, i, 0)
    const2 = lambda b, i: (0, 0)
    return pl.pallas_call(
        _inproj_kernel,
        out_shape=jax.ShapeDtypeStruct((bsz, t, D_IN), BF16),
        grid=(bsz, t // tm),
        in_specs=[
            pl.BlockSpec((1, tm, d), row),
            pl.BlockSpec((1, 1, 6 * d), lambda b, i: (b, 0, 0)),
            pl.BlockSpec((d, D_IN), const2, pipeline_mode=pl.Buffered(1)),
            pl.BlockSpec((1, D_IN), const2),
            pl.BlockSpec((tm, 2 * LANES), lambda b, i: (i, 0)),
            pl.BlockSpec((1, SGU_WIDTH), const2),
            pl.BlockSpec((1, SGU_WIDTH), const2),
        ],
        out_specs=pl.BlockSpec((1, tm, D_IN), row),
        compiler_params=_cparams(("parallel", "parallel")),
        name="inproj",
    )(x, mod3, w_in, b_in, rope_t, sgu_ln_w, sgu_ln_b)


def _dot_t0(a, b):
    return lax.dot_general(a, b, (((0,), (0,)), ((), ())), preferred_element_type=F32)


def _state_kernel(lg_ref, k_ref, v_ref, kc_ref, vc_ref, s_ref):
    hp = pl.program_id(1)
    heads = s_ref.shape[1]
    n_chunks = s_ref.shape[2]
    n_ctx = kc_ref.shape[1]
    c = RET_CHUNK
    li = _iota_f((c, RET_QK_DIM), 0)
    tc = _iota_f((n_ctx, RET_QK_DIM), 0)
    one = jnp.ones((1, 1), F32)

    consts, init = [], []
    for hh in range(heads):
        lgf = lg_ref[0, hp * heads + hh]
        lgb = lg_ref[1, hp * heads + hh]
        kcols = slice(hh * RET_QK_DIM, (hh + 1) * RET_QK_DIM)
        vcols = slice(hh * RET_V_DIM, (hh + 1) * RET_V_DIM)
        consts.append((jnp.exp((c - 1.0 - li) * lgf), jnp.exp(li * lgb),
                       jnp.exp(one * (c * lgf)), jnp.exp(one * (c * lgb)), kcols, vcols))
        kc = kc_ref[0, :, kcols].astype(F32)
        vc = vc_ref[0, :, vcols]
        init.append(_dot_t0((kc * jnp.exp((n_ctx - 1.0 - tc) * lgf)).astype(BF16), vc))
        init.append(_dot_t0((kc * jnp.exp(tc * lgb)).astype(BF16), vc))

    def chunk_kv(n, zeta, kcols, vcols):
        off = pl.multiple_of(n * c, c)
        kk = k_ref[0, pl.ds(off, c), kcols].astype(F32)
        return _dot_t0((kk * zeta).astype(BF16), v_ref[0, pl.ds(off, c), vcols])

    def step(i, carry):
        nb = n_chunks - 1 - i
        out = []
        for hh in range(heads):
            zeta_f, zeta_b, cd_f, cd_b, kcols, vcols = consts[hh]
            sf, sb = carry[2 * hh], carry[2 * hh + 1]
            s_ref[0, hh, i, 0:RET_QK_DIM, :] = sf.astype(BF16)
            s_ref[0, hh, nb, RET_QK_DIM:2 * RET_QK_DIM, :] = sb.astype(BF16)
            out.append(cd_f * sf + chunk_kv(i, zeta_f, kcols, vcols))
            out.append(cd_b * sb + chunk_kv(nb, zeta_b, kcols, vcols))
        return tuple(out)

    lax.fori_loop(0, n_chunks, step, tuple(init), unroll=STATE_UNROLL)


def _state_call(lg, z, kctx, vctx):
    bsz, t, _ = z.shape
    n_ctx = kctx.shape[1]
    n_chunks = t // RET_CHUNK
    hs = STATE_HEADS
    kw, vw = hs * RET_QK_DIM, hs * RET_V_DIM
    return pl.pallas_call(
        _state_kernel,
        out_shape=jax.ShapeDtypeStruct((bsz, RET_HEADS, n_chunks, 2 * RET_QK_DIM, RET_V_DIM), BF16),
        grid=(bsz, RET_HEADS // hs),
        in_specs=[
            pl.BlockSpec(memory_space=pltpu.SMEM),
            pl.BlockSpec((1, t, kw), lambda b, h: (b, 0, Z_K // kw + h)),
            pl.BlockSpec((1, t, vw), lambda b, h: (b, 0, Z_V // vw + h)),
            pl.BlockSpec((1, n_ctx, kw), lambda b, h: (b, 0, h)),
            pl.BlockSpec((1, n_ctx, vw), lambda b, h: (b, 0, h)),
        ],
        out_specs=pl.BlockSpec((1, hs, n_chunks, 2 * RET_QK_DIM, RET_V_DIM), lambda b, h: (b, h, 0, 0, 0)),
        compiler_params=_cparams(("parallel", "parallel")),
        name="states",
    )(lg, z, z, kctx, vctx)


def _mixer_kernel(lg_ref, za_ref, zb_ref, gb_ref, s_ref, x_ref, mod_ref,
                  wpa_ref, wpb_ref, wo_ref, sguw_ref, sgub_ref, gnw_ref, gnb_ref, bo_ref,
                  ln1w_ref, ln1b_ref, wr_ref, br_ref,
                  x1_ref, hm_ref, lgt_ref, ret_scr, sgu_scr):
    d = D_MODEL
    c = RET_CHUNK

    n_sub = za_ref.shape[1] // c
    row = lax.broadcasted_iota(I32, (c, c), 0)
    col = lax.broadcasted_iota(I32, (c, c), 1)
    diff = (row - col).astype(F32)
    rowq = _iota_f((c, RET_QK_DIM), 0)
    for hd in range(RET_HEADS):
        lgf = lg_ref[0, hd]
        lgb = lg_ref[1, hd]
        vcols = slice(hd * RET_V_DIM, (hd + 1) * RET_V_DIM)
        mask = (jnp.where(diff >= 0, jnp.exp(jnp.maximum(diff, 0.0) * lgf), 0.0)
                + jnp.where(diff <= 0, jnp.exp(jnp.maximum(-diff, 0.0) * lgb), 0.0))
        xi_f = jnp.exp((rowq + 1.0) * lgf)
        xi_b = jnp.exp((c - rowq) * lgb)
        for ci in range(n_sub):
            r0 = ci * c
            q = za_ref[0, r0:r0 + c, hd * RET_QK_DIM:(hd + 1) * RET_QK_DIM]
            k = za_ref[0, r0:r0 + c, RET_QK_W + hd * RET_QK_DIM:RET_QK_W + (hd + 1) * RET_QK_DIM]
            s = lax.dot_general(q, k, (((1,), (1,)), ((), ())), preferred_element_type=F32)
            vv = za_ref[0, r0:r0 + c, Z_V + hd * RET_V_DIM:Z_V + (hd + 1) * RET_V_DIM]
            intra = jnp.dot((s * mask).astype(BF16), vv, preferred_element_type=F32)
            qf = q.astype(F32)
            qx = jnp.concatenate([(qf * xi_f).astype(BF16), (qf * xi_b).astype(BF16)], axis=1)
            cross = jnp.dot(qx, s_ref[0, hd, ci], preferred_element_type=F32)
            o = _ln(intra + cross) * gnw_ref[:, vcols] + gnb_ref[:, vcols]
            gate = za_ref[0, r0:r0 + c, Z_SG + hd * RET_V_DIM:Z_SG + (hd + 1) * RET_V_DIM]
            ret_scr[r0:r0 + c, vcols] = (o * gate.astype(F32)).astype(BF16)

    for g in range(SGU_GROUPS):
        gcols = slice(g * SGU_GW, (g + 1) * SGU_GW)
        wg = sguw_ref[g]
        bg = sgub_ref[:, g:g + 1]
        for ci in range(n_sub):
            r0 = ci * c
            vsb = zb_ref[0, r0:r0 + c, Z_VS - Z_BLOCK + g * SGU_GW:Z_VS - Z_BLOCK + (g + 1) * SGU_GW]
            ub = zb_ref[0, r0:r0 + c, Z_U - Z_BLOCK + g * SGU_GW:Z_U - Z_BLOCK + (g + 1) * SGU_GW]
            sp = jnp.dot(wg, vsb, preferred_element_type=F32) + bg
            sgu_scr[r0:r0 + c, gcols] = (ub.astype(F32) * sp).astype(BF16)

    g1 = mod_ref[0, :, 2 * d:3 * d]
    sh2 = mod_ref[0, :, 3 * d:4 * d]
    sc2 = mod_ref[0, :, 4 * d:5 * d]
    for r0 in range(0, za_ref.shape[1], MERGE_ROWS):
        rows = slice(r0, r0 + MERGE_ROWS)
        pa = jnp.dot(ret_scr[rows, :], wpa_ref[...], preferred_element_type=F32)
        pb = jnp.dot(sgu_scr[rows, :], wpb_ref[...], preferred_element_type=F32)
        ga = zb_ref[0, rows, Z_GA - Z_BLOCK:Z_GA - Z_BLOCK + d]
        y = (ga.astype(F32) * pa + gb_ref[0, rows, :].astype(F32) * pb).astype(BF16)
        mix = jnp.dot(y, wo_ref[...], preferred_element_type=F32) + bo_ref[...]
        x1 = _ln(DEEPNORM_ALPHA * x_ref[0, rows, :] + g1 * mix) * ln1w_ref[...] + ln1b_ref[...]
        x1_ref[0, rows, :] = x1
        hm = (_ln(x1) * (1.0 + sc2) + sh2).astype(BF16)
        hm_ref[0, rows, :] = hm
        lgt_ref[0, :, rows] = lax.dot_general(wr_ref[...], hm, (((1,), (1,)), ((), ())),
                                              preferred_element_type=F32) + br_ref[...]


def _mixer_call(lg, z, states, x, mod3, wpa, wpb, wo, sguw, sgub_t, gnw, gnb, bo,
                ln1w, ln1b, wr_t, br):
    bsz, t, d = x.shape
    tt = TT_MIXER
    n_sub = tt // RET_CHUNK
    row = lambda b, i: (b, i, 0)
    c2 = lambda b, i: (0, 0)
    c3 = lambda b, i: (0, 0, 0)
    return pl.pallas_call(
        _mixer_kernel,
        out_shape=(jax.ShapeDtypeStruct((bsz, t, d), F32),
                   jax.ShapeDtypeStruct((bsz, t, d), BF16),
                   jax.ShapeDtypeStruct((bsz, N_EXPERTS, t), F32)),
        grid=(bsz, t // tt),
        in_specs=[
            pl.BlockSpec(memory_space=pltpu.SMEM),
            pl.BlockSpec((1, tt, Z_BLOCK), lambda b, i: (b, i, 0)),
            pl.BlockSpec((1, tt, Z_BLOCK), lambda b, i: (b, i, 1)),
            pl.BlockSpec((1, tt, d), lambda b, i: (b, i, Z_GB // d)),
            pl.BlockSpec((1, RET_HEADS, n_sub, 2 * RET_QK_DIM, RET_V_DIM), lambda b, i: (b, 0, i, 0, 0)),
            pl.BlockSpec((1, tt, d), row),
            pl.BlockSpec((1, 1, 6 * d), lambda b, i: (b, 0, 0)),
            pl.BlockSpec((RET_V_W, d), c2, pipeline_mode=pl.Buffered(1)),
            pl.BlockSpec((SGU_WIDTH, d), c2, pipeline_mode=pl.Buffered(1)),
            pl.BlockSpec((d, d), c2, pipeline_mode=pl.Buffered(1)),
            pl.BlockSpec((SGU_GROUPS, SGU_CHUNK, SGU_CHUNK), c3),
            pl.BlockSpec((SGU_CHUNK, SGU_GROUPS), c2),
            pl.BlockSpec((1, RET_V_W), c2),
            pl.BlockSpec((1, RET_V_W), c2),
            pl.BlockSpec((1, d), c2),
            pl.BlockSpec((1, d), c2),
            pl.BlockSpec((1, d), c2),
            pl.BlockSpec((N_EXPERTS, d), c2),
            pl.BlockSpec((N_EXPERTS, 1), c2),
        ],
        out_specs=(pl.BlockSpec((1, tt, d), row),
                   pl.BlockSpec((1, tt, d), row),
                   pl.BlockSpec((1, N_EXPERTS, tt), lambda b, i: (b, 0, i))),
        scratch_shapes=[pltpu.VMEM((tt, RET_V_W), BF16), pltpu.VMEM((tt, SGU_WIDTH), BF16)],
        compiler_params=_cparams(("parallel", "parallel")),
        name="mixer",
    )(lg, z, z, z, states, x, mod3, wpa, wpb, wo, sguw, sgub_t, gnw, gnb, bo,
      ln1w, ln1b, wr_t, br)


def _route_kernel(lgt_ref, pos_ref, gate_ref, cnt_ref, *, cap):
    lg = lgt_ref[0]
    n_e, t = lg.shape
    tb = TB_ROUTE
    m = jnp.max(lg, axis=0, keepdims=True)
    ex = jnp.exp(lg - m)
    aff = ex / jnp.sum(ex, axis=0, keepdims=True)

    def search(i, thr_bits):
        cand = thr_bits | lax.shift_left(jnp.int32(1), 30 - i)
        cnt = jnp.sum((aff >= lax.bitcast_convert_type(cand, F32)).astype(I32), axis=1, keepdims=True)
        return jnp.where(cnt >= cap, cand, thr_bits)

    thr_bits = lax.fori_loop(0, 31, search, jnp.zeros((n_e, 1), I32))
    floor_f = lax.bitcast_convert_type(thr_bits, F32)
    thr = jnp.min(jnp.where(aff >= floor_f, aff, jnp.inf), axis=1, keepdims=True)
    need = (cap - jnp.sum((aff > thr).astype(I32), axis=1, keepdims=True)).astype(F32)

    r = lax.broadcasted_iota(I32, (tb, tb), 0)
    cc = lax.broadcasted_iota(I32, (tb, tb), 1)
    tri = (r <= cc).astype(BF16)
    carry_eq = jnp.zeros((n_e, 1), F32)
    carry_sel = jnp.zeros((n_e, 1), F32)
    for blk in range(t // tb):
        sl = slice(blk * tb, (blk + 1) * tb)
        aff_b = aff[:, sl]
        eq = aff_b == thr
        eq_b = eq.astype(BF16)
        inc_eq = jnp.dot(eq_b, tri, preferred_element_type=F32)
        before = carry_eq + inc_eq - eq_b.astype(F32)
        sel = (aff_b > thr) | (eq & (before < need))
        sel_b = sel.astype(BF16)
        inc_sel = jnp.dot(sel_b, tri, preferred_element_type=F32)
        pos = carry_sel + inc_sel - 1.0
        pos_ref[0, blk] = jnp.where(sel, pos.astype(I32), -1)
        gate_ref[0, blk] = aff_b
        cnt_ref[0, blk] = jnp.broadcast_to(carry_sel, (n_e, LANES)).astype(I32)
        carry_eq = carry_eq + inc_eq[:, tb - 1:tb]
        carry_sel = carry_sel + inc_sel[:, tb - 1:tb]


def _route_call(logits_t, cap):
    bsz, n_e, t = logits_t.shape
    nblk = t // TB_ROUTE
    return pl.pallas_call(
        functools.partial(_route_kernel, cap=cap),
        out_shape=(jax.ShapeDtypeStruct((bsz, nblk, n_e, TB_ROUTE), I32),
                   jax.ShapeDtypeStruct((bsz, nblk, n_e, TB_ROUTE), F32),
                   jax.ShapeDtypeStruct((bsz, nblk, n_e, LANES), I32)),
        grid=(bsz,),
        in_specs=[pl.BlockSpec((1, n_e, t), lambda b: (b, 0, 0))],
        out_specs=(pl.BlockSpec((1, nblk, n_e, TB_ROUTE), lambda b: (b, 0, 0, 0)),
                   pl.BlockSpec((1, nblk, n_e, TB_ROUTE), lambda b: (b, 0, 0, 0)),
                   pl.BlockSpec((1, nblk, n_e, LANES), lambda b: (b, 0, 0, 0))),
        compiler_params=_cparams(("parallel",)),
        name="route",
    )(logits_t)


def _window_rows(ws, win, n_tok):
    return lax.broadcasted_iota(I32, (win, n_tok), 0) + ws


def _sweep_window(w, cap, win, prow, rest):
    off = pl.multiple_of(jnp.minimum(w * win, cap - win), WIN_ALIGN)
    rows = _window_rows(off, win, prow.shape[-1])
    return off, (rows == prow) & rest & (rows >= w * win)


def _gather_kernel(ws_ref, nx_ref, hm_ref, pos_ref, gate_ref, xs_ref, gc_ref):
    b = pl.program_id(0)
    g = pl.program_id(1)
    step = pl.program_id(2)
    merge = DISPATCH_MERGE
    eg, tb = pos_ref.shape[2:]
    n_sub = pos_ref.shape[1] // merge
    n_e = eg * pl.num_programs(1)
    n_win = n_sub * pl.num_programs(2)
    cap = xs_ref.shape[2]
    win = WIN_DISPATCH

    @pl.when(step == 0)
    def _():
        xs_ref[...] = jnp.zeros_like(xs_ref)
        gc_ref[...] = jnp.zeros_like(gc_ref)

    def base(sub):
        return (b * n_win + step * n_sub + sub) * n_e + g * eg

    def tokens(sub):
        return hm_ref[0, sub * merge * tb:(sub + 1) * merge * tb, :]

    def lanes(ref, sub, el):
        return jnp.concatenate([ref[0, sub * merge + m, el:el + 1, :] for m in range(merge)], axis=-1)

    for sub in range(n_sub):
        for el in range(eg):
            ws = pl.multiple_of(ws_ref[base(sub) + el], WIN_ALIGN)
            match = _window_rows(ws, win, merge * tb) == lanes(pos_ref, sub, el)
            sl = pl.ds(ws, win)
            xs_ref[0, el, sl, :] += jnp.dot(match.astype(BF16), tokens(sub),
                                            preferred_element_type=F32).astype(BF16)
            gc_ref[0, el, sl, :] += jnp.sum(jnp.where(match, lanes(gate_ref, sub, el), 0.0),
                                             axis=1, keepdims=True)

    for sub in range(n_sub):
        for el in range(eg):
            @pl.when(nx_ref[base(sub) + el] > 0)
            def _(sub=sub, el=el):
                prow = lanes(pos_ref, sub, el)
                grow = lanes(gate_ref, sub, el)
                rest = prow >= ws_ref[base(sub) + el] + win

                def body(w, carry):
                    off, m = _sweep_window(w, cap, win, prow, rest)
                    xs_ref[0, el, pl.ds(off, win), :] += jnp.dot(m.astype(BF16), tokens(sub),
                                                                  preferred_element_type=F32).astype(BF16)
                    gc_ref[0, el, pl.ds(off, win), :] += jnp.sum(jnp.where(m, grow, 0.0), axis=1, keepdims=True)
                    return carry

                lax.fori_loop(0, pl.cdiv(cap, win), body, 0)


def _gather_call(ws, nx, hm, pos_b, gate_b, cap):
    bsz, t, d = hm.shape
    nblk, n_e, tb = pos_b.shape[1:]
    eg = n_e // DISPATCH_GROUPS
    blocks = DISPATCH_STEP_WINDOWS * DISPATCH_MERGE
    grid_spec = pltpu.PrefetchScalarGridSpec(
        num_scalar_prefetch=2,
        grid=(bsz, DISPATCH_GROUPS, nblk // blocks),
        in_specs=[
            pl.BlockSpec((1, blocks * tb, d), lambda b, g, k, ws, nx: (b, k, 0)),
            pl.BlockSpec((1, blocks, eg, tb), lambda b, g, k, ws, nx: (b, k, g, 0)),
            pl.BlockSpec((1, blocks, eg, tb), lambda b, g, k, ws, nx: (b, k, g, 0)),
        ],
        out_specs=(pl.BlockSpec((1, eg, cap, d), lambda b, g, k, ws, nx: (b, g, 0, 0)),
                   pl.BlockSpec((1, eg, cap, 1), lambda b, g, k, ws, nx: (b, g, 0, 0))),
    )
    return pl.pallas_call(
        _gather_kernel,
        out_shape=(jax.ShapeDtypeStruct((bsz, n_e, cap, d), BF16),
                   jax.ShapeDtypeStruct((bsz, n_e, cap, 1), F32)),
        grid_spec=grid_spec,
        compiler_params=_cparams(("parallel", "parallel", "arbitrary")),
        name="gather",
    )(ws, nx, hm, pos_b, gate_b)


def _ffn_kernel(xs_ref, gc_ref, wg_ref, wu_ref, wd_ref, ye_ref, wgu_s, wd_s):
    f = wd_s.shape[0]
    wgu_s[:, 0:f] = wg_ref[0].astype(BF16)
    wgu_s[:, f:2 * f] = wu_ref[0].astype(BF16)
    wd_s[...] = wd_ref[0].astype(BF16)

    rt = FFN_ROWS
    for b in range(xs_ref.shape[0]):
        for j in range(xs_ref.shape[2] // rt):
            sl = slice(j * rt, (j + 1) * rt)
            xgu = jnp.dot(xs_ref[b, 0, sl, :], wgu_s[...], preferred_element_type=F32)
            hid = (jax.nn.silu(xgu[:, 0:f]) * xgu[:, f:2 * f]).astype(BF16)
            ye = jnp.dot(hid, wd_s[...], preferred_element_type=F32) * gc_ref[b, 0, sl, :]
            ye_ref[b, 0, sl, :] = ye.astype(BF16)


def _ffn_call(xs, gc, w_gate, w_up, w_down):
    bsz, n_e, cap, d = xs.shape
    f = w_gate.shape[2]
    return pl.pallas_call(
        _ffn_kernel,
        out_shape=jax.ShapeDtypeStruct((bsz, n_e, cap, d), BF16),
        grid=(n_e,),
        in_specs=[
            pl.BlockSpec((bsz, 1, cap, d), lambda e: (0, e, 0, 0)),
            pl.BlockSpec((bsz, 1, cap, 1), lambda e: (0, e, 0, 0)),
            pl.BlockSpec((1, d, f), lambda e: (e, 0, 0)),
            pl.BlockSpec((1, d, f), lambda e: (e, 0, 0)),
            pl.BlockSpec((1, f, d), lambda e: (e, 0, 0)),
        ],
        out_specs=pl.BlockSpec((bsz, 1, cap, d), lambda e: (0, e, 0, 0)),
        scratch_shapes=[pltpu.VMEM((d, 2 * f), BF16), pltpu.VMEM((f, d), BF16)],
        compiler_params=_cparams(("parallel",)),
        name="ffn",
    )(xs, gc, w_gate, w_up, w_down)


def _combine_kernel(ws_ref, nx_ref, ye_ref, pos_ref, x1_ref, mod_ref, lnw_ref, lnb_ref, out_ref, acc_scr):
    b = pl.program_id(0)
    step = pl.program_id(1)
    n_sub, n_e, tb = pos_ref.shape[1:]
    nblk = n_sub * pl.num_programs(1)
    cap = ye_ref.shape[2]
    d = D_MODEL
    win = WIN_COMBINE
    g2 = mod_ref[0, :, 5 * d:6 * d]

    for sub in range(n_sub):
        base = (b * nblk + step * n_sub + sub) * n_e
        starts = [pl.multiple_of(ws_ref[base + e], WIN_ALIGN) for e in range(n_e)]
        onehot = jnp.concatenate(
            [(_window_rows(starts[e], win, tb) == pos_ref[0, sub, e:e + 1, :]).astype(BF16) for e in range(n_e)],
            axis=0)
        rows = jnp.concatenate([ye_ref[0, e, pl.ds(starts[e], win), :] for e in range(n_e)], axis=0)
        acc_scr[...] = _dot_t0(onehot, rows)

        for e in range(n_e):
            @pl.when(nx_ref[base + e] > 0)
            def _(e=e, sub=sub, starts=starts):
                prow = pos_ref[0, sub, e:e + 1, :]
                rest = prow >= starts[e] + win

                def body(w, carry):
                    off, m = _sweep_window(w, cap, win, prow, rest)
                    acc_scr[...] += _dot_t0(m.astype(BF16), ye_ref[0, e, pl.ds(off, win), :])
                    return carry

                lax.fori_loop(0, pl.cdiv(cap, win), body, 0)

        tok = slice(sub * tb, (sub + 1) * tb)
        out_ref[0, tok, :] = (_ln(DEEPNORM_ALPHA * x1_ref[0, tok, :] + g2 * acc_scr[...]) * lnw_ref[...]
                              + lnb_ref[...])


def _combine_call(ws, nx, ye, pos_b, x1, mod3, ln2w, ln2b):
    bsz, n_e, cap, d = ye.shape
    nblk, _, tb = pos_b.shape[1:]
    n_sub = COMBINE_STEP_BLOCKS
    grid_spec = pltpu.PrefetchScalarGridSpec(
        num_scalar_prefetch=2,
        grid=(bsz, nblk // n_sub),
        in_specs=[
            pl.BlockSpec((1, n_e, cap, d), lambda b, k, ws, nx: (b, 0, 0, 0), pipeline_mode=pl.Buffered(1)),
            pl.BlockSpec((1, n_sub, n_e, tb), lambda b, k, ws, nx: (b, k, 0, 0)),
            pl.BlockSpec((1, n_sub * tb, d), lambda b, k, ws, nx: (b, k, 0)),
            pl.BlockSpec((1, 1, 6 * d), lambda b, k, ws, nx: (b, 0, 0)),
            pl.BlockSpec((1, d), lambda b, k, ws, nx: (0, 0)),
            pl.BlockSpec((1, d), lambda b, k, ws, nx: (0, 0)),
        ],
        out_specs=pl.BlockSpec((1, n_sub * tb, d), lambda b, k, ws, nx: (b, k, 0)),
        scratch_shapes=[pltpu.VMEM((tb, d), F32)],
    )
    return pl.pallas_call(
        _combine_kernel,
        out_shape=jax.ShapeDtypeStruct((bsz, nblk * tb, d), F32),
        grid_spec=grid_spec,
        compiler_params=_cparams(("parallel", "arbitrary")),
        name="combine",
    )(ws, nx, ye, pos_b, x1, mod3, ln2w, ln2b)


def _rope_tables(n_tokens):
    rows = n_tokens // GRID_W
    n_freq = RET_QK_DIM // 4
    inv = ROPE_THETA ** (-jnp.arange(n_freq, dtype=F32) / n_freq)
    ang_r = jnp.arange(rows, dtype=F32)[:, None] * inv
    ang_c = jnp.arange(GRID_W, dtype=F32)[:, None] * inv
    zr = jnp.zeros((rows, 2 * n_freq), F32)
    zc = jnp.zeros((GRID_W, 2 * n_freq), F32)
    cr, sr, cc, sc = jnp.cos(ang_r), jnp.sin(ang_r), jnp.cos(ang_c), jnp.sin(ang_c)
    cos_t = (jnp.concatenate([cr, cr, zr], axis=-1)[:, None, :]
             + jnp.concatenate([zc, cc, cc], axis=-1)[None, :, :]).reshape(n_tokens, RET_QK_DIM)
    sin_t = (jnp.concatenate([-sr, sr, zr], axis=-1)[:, None, :]
             + jnp.concatenate([zc, -sc, sc], axis=-1)[None, :, :]).reshape(n_tokens, RET_QK_DIM)
    return jnp.concatenate([cos_t, sin_t], axis=-1)


def kernel(x, c, ctx, c_ctx, w_ada, b_ada, w_in, b_in, ret_decay_f, ret_decay_b, ret_gn_w, ret_gn_b,
           sgu_ln_w, sgu_ln_b, sgu_w, sgu_b, w_pa, w_pb, w_o, b_o, ln1_w, ln1_b,
           w_router, b_router, w_gate, w_up, w_down, ln2_w, ln2_b):
    bsz, t, d = x.shape
    assert d == D_MODEL and w_ada.shape[0] == DEPTH == 1 and bsz < MOD_ROWS
    cap = EC_CAPACITY_FACTOR * t // N_EXPERTS
    for win in (WIN_DISPATCH, WIN_COMBINE):
        assert cap >= win and (cap - win) % WIN_ALIGN == 0 and win % WIN_ALIGN == 0
    assert t % (TB_ROUTE * DISPATCH_MERGE * DISPATCH_STEP_WINDOWS) == 0 and t % (TB_ROUTE * COMBINE_STEP_BLOCKS) == 0
    assert N_EXPERTS % DISPATCH_GROUPS == 0
    l = 0

    cvec = jnp.concatenate([c, c_ctx[None], jnp.zeros((MOD_ROWS - bsz - 1, d), F32)], axis=0)
    mod = _mod_call(cvec.T, w_ada[l], b_ada[l][None], bsz + 1)
    mod3 = mod[:, None, :]

    w_in_b = w_in[l].astype(BF16)
    b_in2 = b_in[l][None]
    kctx, vctx = _ctx_kv_call(ctx, mod3, w_in_b, b_in2, bsz)

    z = _inproj_call(x, mod3, w_in_b, b_in2, _rope_tables(t), sgu_ln_w[l][None], sgu_ln_b[l][None])

    lg = jnp.stack([jax.nn.log_sigmoid(ret_decay_f[l].astype(F32)),
                    jax.nn.log_sigmoid(ret_decay_b[l].astype(F32))])
    states = _state_call(lg, z, kctx, vctx)

    x1, hm, logits_t = _mixer_call(
        lg, z, states, x, mod3,
        w_pa[l].astype(BF16), w_pb[l].astype(BF16), w_o[l].astype(BF16),
        sgu_w[l].astype(BF16), sgu_b[l].T, ret_gn_w[l][None], ret_gn_b[l][None], b_o[l][None],
        ln1_w[l][None], ln1_b[l][None], w_router[l].T.astype(BF16), b_router[l][:, None])

    pos_b, gate_b, cnt_b = _route_call(logits_t, cap)

    c0 = cnt_b[..., 0]
    c1 = jnp.concatenate([c0[:, 1:], jnp.full_like(c0[:, :1], cap)], axis=1)

    def windows(start, end, win):
        ws = jnp.minimum((start // WIN_ALIGN) * WIN_ALIGN, cap - win)
        return ws.reshape(-1), (end > ws + win).astype(I32).reshape(-1)

    ws_d, nx_d = windows(c0[:, ::DISPATCH_MERGE], c1[:, DISPATCH_MERGE - 1::DISPATCH_MERGE], WIN_DISPATCH)
    ws_c, nx_c = windows(c0, c1, WIN_COMBINE)

    xs, gc = _gather_call(ws_d, nx_d, hm, pos_b, gate_b, cap)
    ye = _ffn_call(xs, gc, w_gate[l], w_up[l], w_down[l])
    return _combine_call(ws_c, nx_c, ye, pos_b, x1, mod3, ln2_w[l][None], ln2_b[l][None])
```

```python
import functools
import math

import jax
import jax.numpy as jnp
from jax import lax
from jax.experimental import pallas as pl
from jax.experimental.pallas import tpu as pltpu

F32 = jnp.float32
BF16 = jnp.bfloat16
I32 = jnp.int32

D_MODEL = 1024
DEPTH = 1
GRID_W = 64
RET_HEADS = 4
RET_QK_DIM = 128
RET_V_DIM = 256
RET_CHUNK = 128
ROPE_THETA = 10000.0
RET_QK_W = RET_HEADS * RET_QK_DIM
RET_V_W = RET_HEADS * RET_V_DIM
SGU_GROUPS = 4
SGU_CHUNK = 128
SGU_WIDTH = 1024
SGU_GW = SGU_WIDTH // SGU_GROUPS
N_EXPERTS = 16
EC_CAPACITY_FACTOR = 2
LN_EPS = 1e-6
DEEPNORM_ALPHA = (2.0 * DEPTH) ** 0.25
D_IN = 2 * RET_QK_W + 2 * RET_V_W + 2 * SGU_WIDTH + 2 * D_MODEL
Z_K = RET_QK_W
Z_V = 2 * RET_QK_W
Z_SG = Z_V + RET_V_W
Z_U = Z_SG + RET_V_W
Z_VS = Z_U + SGU_WIDTH
Z_GA = Z_VS + SGU_WIDTH
Z_GB = Z_GA + D_MODEL
Z_BLOCK = 3 * 1024

V7X_VMEM_BYTES = 64 * 1024 * 1024
VMEM_LIMIT = V7X_VMEM_BYTES - 4 * 1024 * 1024
LANES = 128

MOD_ROWS = 8
MOD_COLS = 1024
TM_INPROJ = 1024
INPROJ_CHUNK = 256
INPROJ_FIRST_ROWS = 256
TT_MIXER = 512
MERGE_ROWS = 512
TB_ROUTE = 256
DISPATCH_MERGE = 2
WIN_DISPATCH = 128
WIN_COMBINE = 64
WIN_ALIGN = 16
DISPATCH_GROUPS = 2
DISPATCH_STEP_WINDOWS = 4
COMBINE_STEP_BLOCKS = 4
STATE_UNROLL = 8
FFN_ROWS = 256


def _cparams(sem):
    return pltpu.CompilerParams(dimension_semantics=sem, vmem_limit_bytes=VMEM_LIMIT)


def _ln(x):
    mu = jnp.mean(x, axis=-1, keepdims=True)
    xc = x - mu
    var = jnp.mean(xc * xc, axis=-1, keepdims=True)
    return xc * lax.rsqrt(var + LN_EPS)


_GELU_A = -2.0 * math.sqrt(2.0 / math.pi) * math.log2(math.e)
_GELU_B = _GELU_A * 0.044715


def _gelu(x):
    return x / (1.0 + jnp.exp2(x * (_GELU_A + _GELU_B * (x * x))))


def _iota_f(shape, dim):
    return lax.broadcasted_iota(I32, shape, dim).astype(F32)


def _mod_kernel(ct_ref, w_ref, b_ref, o_ref, *, n_rows):
    c = ct_ref[...]
    a = c * jax.nn.sigmoid(c)
    w = w_ref[...]
    rows = [jnp.sum(w * a[:, m:m + 1], axis=0, keepdims=True) + b_ref[...] for m in range(n_rows)]
    rows += [jnp.zeros_like(rows[0])] * (MOD_ROWS - n_rows)
    o_ref[...] = jnp.concatenate(rows, axis=0)


def _mod_call(cvec_t, w_ada, b_ada, n_rows):
    d, n = w_ada.shape
    tn = MOD_COLS
    return pl.pallas_call(
        functools.partial(_mod_kernel, n_rows=n_rows),
        out_shape=jax.ShapeDtypeStruct((MOD_ROWS, n), F32),
        grid=(n // tn,),
        in_specs=[
            pl.BlockSpec((d, MOD_ROWS), lambda j: (0, 0)),
            pl.BlockSpec((d, tn), lambda j: (0, j)),
            pl.BlockSpec((1, tn), lambda j: (0, j)),
        ],
        out_specs=pl.BlockSpec((MOD_ROWS, tn), lambda j: (0, j)),
        compiler_params=_cparams(("parallel",)),
        name="mod",
    )(cvec_t, w_ada, b_ada)


def _ctx_kv_kernel(x_ref, mod_ref, wk_ref, wv_ref, bk_ref, bv_ref, k_ref, v_ref):
    d = D_MODEL
    sh = mod_ref[0, :, 0:d]
    sc = mod_ref[0, :, d:2 * d]
    h = (_ln(x_ref[0]) * (1.0 + sc) + sh).astype(BF16)
    k = jnp.dot(h, wk_ref[...], preferred_element_type=F32) + bk_ref[...]
    k_ref[0] = (k * (RET_QK_DIM ** -0.5)).astype(BF16)
    v_ref[0] = (jnp.dot(h, wv_ref[...], preferred_element_type=F32) + bv_ref[...]).astype(BF16)


def _ctx_kv_call(ctx, mod3, w_in, b_in, ctx_row):
    bsz, n_ctx, d = ctx.shape
    kblk = Z_K // RET_QK_W
    vblk = Z_V // RET_V_W
    return pl.pallas_call(
        _ctx_kv_kernel,
        out_shape=(jax.ShapeDtypeStruct((bsz, n_ctx, RET_QK_W), BF16),
                   jax.ShapeDtypeStruct((bsz, n_ctx, RET_V_W), BF16)),
        grid=(bsz,),
        in_specs=[
            pl.BlockSpec((1, n_ctx, d), lambda b: (b, 0, 0)),
            pl.BlockSpec((1, 1, 6 * d), lambda b: (ctx_row, 0, 0)),
            pl.BlockSpec((d, RET_QK_W), lambda b: (0, kblk)),
            pl.BlockSpec((d, RET_V_W), lambda b: (0, vblk)),
            pl.BlockSpec((1, RET_QK_W), lambda b: (0, kblk)),
            pl.BlockSpec((1, RET_V_W), lambda b: (0, vblk)),
        ],
        out_specs=(pl.BlockSpec((1, n_ctx, RET_QK_W), lambda b: (b, 0, 0)),
                   pl.BlockSpec((1, n_ctx, RET_V_W), lambda b: (b, 0, 0))),
        compiler_params=_cparams(("parallel",)),
        name="ctx_kv",
    )(ctx, mod3, w_in, w_in, b_in, b_in)


def _inproj_kernel(x_ref, mod_ref, w_ref, b_ref, rope_ref, lnw_ref, lnb_ref, z_ref):
    d = D_MODEL
    sh = mod_ref[0, :, 0:d]
    sc = mod_ref[0, :, d:2 * d]
    h = (_ln(x_ref[0]) * (1.0 + sc) + sh).astype(BF16)

    def proj(c0, width):
        return jnp.dot(h, w_ref[:, c0:c0 + width], preferred_element_type=F32) + b_ref[:, c0:c0 + width]

    rb = INPROJ_FIRST_ROWS
    upper = (lax.broadcasted_iota(I32, (rb, LANES), 1) & 32) != 0
    for r0 in range(0, h.shape[0], rb):
        zq = (jnp.dot(h[r0:r0 + rb], w_ref[:, 0:2 * RET_QK_W], preferred_element_type=F32)
              + b_ref[:, 0:2 * RET_QK_W])
        cos = rope_ref[r0:r0 + rb, 0:LANES]
        sin = rope_ref[r0:r0 + rb, LANES:2 * LANES]
        for hb in range(2 * RET_HEADS):
            zs = zq[:, hb * LANES:(hb + 1) * LANES]
            if hb >= RET_HEADS:
                zs = zs * (RET_QK_DIM ** -0.5)
            sw = jnp.where(upper, pltpu.roll(zs, 32, 1), pltpu.roll(zs, LANES - 32, 1))
            z_ref[0, r0:r0 + rb, hb * LANES:(hb + 1) * LANES] = (zs * cos + sw * sin).astype(BF16)

    def emit(c0, width, fn):
        for cc in range(c0, c0 + width, INPROJ_CHUNK):
            z_ref[0, :, cc:cc + INPROJ_CHUNK] = fn(proj(cc, INPROJ_CHUNK)).astype(BF16)

    vs = _ln(_gelu(proj(Z_VS, SGU_WIDTH))) * lnw_ref[...] + lnb_ref[...]
    z_ref[0, :, Z_VS:Z_VS + SGU_WIDTH] = vs.astype(BF16)
    emit(Z_U, SGU_WIDTH, _gelu)
    emit(Z_SG, RET_V_W, jax.nn.silu)
    emit(Z_GA, d, jax.nn.sigmoid)
    emit(Z_GB, d, jax.nn.sigmoid)
    emit(Z_V, RET_V_W, lambda z: z)


def _inproj_call(x, mod3, w_in, b_in, rope_t, sgu_ln_w, sgu_ln_b):
    bsz, t, d = x.shape
    tm = TM_INPROJ
    row = lambda b, i: (b, i, 0)
    const2 = lambda b, i: (0, 0)
    return pl.pallas_call(
        _inproj_kernel,
        out_shape=jax.ShapeDtypeStruct((bsz, t, D_IN), BF16),
        grid=(bsz, t // tm),
        in_specs=[
            pl.BlockSpec((1, tm, d), row),
            pl.BlockSpec((1, 1, 6 * d), lambda b, i: (b, 0, 0)),
            pl.BlockSpec((d, D_IN), const2, pipeline_mode=pl.Buffered(1)),
            pl.BlockSpec((1, D_IN), const2),
            pl.BlockSpec((tm, 2 * LANES), lambda b, i: (i, 0)),
            pl.BlockSpec((1, SGU_WIDTH), const2),
            pl.BlockSpec((1, SGU_WIDTH), const2),
        ],
        out_specs=pl.BlockSpec((1, tm, D_IN), row),
        compiler_params=_cparams(("parallel", "parallel")),
        name="inproj",
    )(x, mod3, w_in, b_in, rope_t, sgu_ln_w, sgu_ln_b)


def _dot_t0(a, b):
    return lax.dot_general(a, b, (((0,), (0,)), ((), ())), preferred_element_type=F32)


def _state_kernel(lg_ref, k_ref, v_ref, kc_ref, vc_ref, s_ref):
    hd = pl.program_id(1)
    lgf = lg_ref[0, hd]
    lgb = lg_ref[1, hd]
    n_chunks = s_ref.shape[2]
    n_ctx = kc_ref.shape[1]
    c = RET_CHUNK
    li = _iota_f((c, RET_QK_DIM), 0)
    zeta_f = jnp.exp((c - 1.0 - li) * lgf)
    zeta_b = jnp.exp(li * lgb)
    one = jnp.ones((1, 1), F32)
    cd_f = jnp.exp(one * (c * lgf))
    cd_b = jnp.exp(one * (c * lgb))

    tc = _iota_f((n_ctx, RET_QK_DIM), 0)
    kc = kc_ref[0].astype(F32)
    vc = vc_ref[0]
    s0f = _dot_t0((kc * jnp.exp((n_ctx - 1.0 - tc) * lgf)).astype(BF16), vc)
    s0b = _dot_t0((kc * jnp.exp(tc * lgb)).astype(BF16), vc)

    def chunk_kv(n, zeta):
        off = pl.multiple_of(n * c, c)
        kk = k_ref[0, pl.ds(off, c), :].astype(F32)
        return _dot_t0((kk * zeta).astype(BF16), v_ref[0, pl.ds(off, c), :])

    def step(i, carry):
        sf, sb = carry
        nb = n_chunks - 1 - i
        s_ref[0, 0, i, 0:RET_QK_DIM, :] = sf.astype(BF16)
        s_ref[0, 0, nb, RET_QK_DIM:2 * RET_QK_DIM, :] = sb.astype(BF16)
        return cd_f * sf + chunk_kv(i, zeta_f), cd_b * sb + chunk_kv(nb, zeta_b)

    lax.fori_loop(0, n_chunks, step, (s0f, s0b), unroll=STATE_UNROLL)


def _state_call(lg, z, kctx, vctx):
    bsz, t, _ = z.shape
    n_ctx = kctx.shape[1]
    n_chunks = t // RET_CHUNK
    return pl.pallas_call(
        _state_kernel,
        out_shape=jax.ShapeDtypeStruct((bsz, RET_HEADS, n_chunks, 2 * RET_QK_DIM, RET_V_DIM), BF16),
        grid=(bsz, RET_HEADS),
        in_specs=[
            pl.BlockSpec(memory_space=pltpu.SMEM),
            pl.BlockSpec((1, t, RET_QK_DIM), lambda b, h: (b, 0, RET_QK_W // RET_QK_DIM + h)),
            pl.BlockSpec((1, t, RET_V_DIM), lambda b, h: (b, 0, Z_V // RET_V_DIM + h)),
            pl.BlockSpec((1, n_ctx, RET_QK_DIM), lambda b, h: (b, 0, h)),
            pl.BlockSpec((1, n_ctx, RET_V_DIM), lambda b, h: (b, 0, h)),
        ],
        out_specs=pl.BlockSpec((1, 1, n_chunks, 2 * RET_QK_DIM, RET_V_DIM), lambda b, h: (b, h, 0, 0, 0)),
        compiler_params=_cparams(("parallel", "parallel")),
        name="states",
    )(lg, z, z, kctx, vctx)


def _mixer_kernel(lg_ref, za_ref, zb_ref, gb_ref, s_ref, x_ref, mod_ref,
                  wpa_ref, wpb_ref, wo_ref, sguw_ref, sgub_ref, gnw_ref, gnb_ref, bo_ref,
                  ln1w_ref, ln1b_ref, wr_ref, br_ref,
                  x1_ref, hm_ref, lgt_ref, ret_scr, sgu_scr):
    d = D_MODEL
    c = RET_CHUNK

    n_sub = za_ref.shape[1] // c
    row = lax.broadcasted_iota(I32, (c, c), 0)
    col = lax.broadcasted_iota(I32, (c, c), 1)
    diff = (row - col).astype(F32)
    rowq = _iota_f((c, RET_QK_DIM), 0)
    for hd in range(RET_HEADS):
        lgf = lg_ref[0, hd]
        lgb = lg_ref[1, hd]
        vcols = slice(hd * RET_V_DIM, (hd + 1) * RET_V_DIM)
        mask = (jnp.where(diff >= 0, jnp.exp(jnp.maximum(diff, 0.0) * lgf), 0.0)
                + jnp.where(diff <= 0, jnp.exp(jnp.maximum(-diff, 0.0) * lgb), 0.0))
        xi_f = jnp.exp((rowq + 1.0) * lgf)
        xi_b = jnp.exp((c - rowq) * lgb)
        for ci in range(n_sub):
            r0 = ci * c
            q = za_ref[0, r0:r0 + c, hd * RET_QK_DIM:(hd + 1) * RET_QK_DIM]
            k = za_ref[0, r0:r0 + c, RET_QK_W + hd * RET_QK_DIM:RET_QK_W + (hd + 1) * RET_QK_DIM]
            s = lax.dot_general(q, k, (((1,), (1,)), ((), ())), preferred_element_type=F32)
            vv = za_ref[0, r0:r0 + c, Z_V + hd * RET_V_DIM:Z_V + (hd + 1) * RET_V_DIM]
            intra = jnp.dot((s * mask).astype(BF16), vv, preferred_element_type=F32)
            qf = q.astype(F32)
            qx = jnp.concatenate([(qf * xi_f).astype(BF16), (qf * xi_b).astype(BF16)], axis=1)
            cross = jnp.dot(qx, s_ref[0, hd, ci], preferred_element_type=F32)
            o = _ln(intra + cross) * gnw_ref[:, vcols] + gnb_ref[:, vcols]
            gate = za_ref[0, r0:r0 + c, Z_SG + hd * RET_V_DIM:Z_SG + (hd + 1) * RET_V_DIM]
            ret_scr[r0:r0 + c, vcols] = (o * gate.astype(F32)).astype(BF16)

    for g in range(SGU_GROUPS):
        gcols = slice(g * SGU_GW, (g + 1) * SGU_GW)
        wg = sguw_ref[g]
        bg = sgub_ref[:, g:g + 1]
        for ci in range(n_sub):
            r0 = ci * c
            vsb = zb_ref[0, r0:r0 + c, Z_VS - Z_BLOCK + g * SGU_GW:Z_VS - Z_BLOCK + (g + 1) * SGU_GW]
            ub = zb_ref[0, r0:r0 + c, Z_U - Z_BLOCK + g * SGU_GW:Z_U - Z_BLOCK + (g + 1) * SGU_GW]
            sp = jnp.dot(wg, vsb, preferred_element_type=F32) + bg
            sgu_scr[r0:r0 + c, gcols] = (ub.astype(F32) * sp).astype(BF16)

    g1 = mod_ref[0, :, 2 * d:3 * d]
    sh2 = mod_ref[0, :, 3 * d:4 * d]
    sc2 = mod_ref[0, :, 4 * d:5 * d]
    for r0 in range(0, za_ref.shape[1], MERGE_ROWS):
        rows = slice(r0, r0 + MERGE_ROWS)
        pa = jnp.dot(ret_scr[rows, :], wpa_ref[...], preferred_element_type=F32)
        pb = jnp.dot(sgu_scr[rows, :], wpb_ref[...], preferred_element_type=F32)
        ga = zb_ref[0, rows, Z_GA - Z_BLOCK:Z_GA - Z_BLOCK + d]
        y = (ga.astype(F32) * pa + gb_ref[0, rows, :].astype(F32) * pb).astype(BF16)
        mix = jnp.dot(y, wo_ref[...], preferred_element_type=F32) + bo_ref[...]
        x1 = _ln(DEEPNORM_ALPHA * x_ref[0, rows, :] + g1 * mix) * ln1w_ref[...] + ln1b_ref[...]
        x1_ref[0, rows, :] = x1
        hm = (_ln(x1) * (1.0 + sc2) + sh2).astype(BF16)
        hm_ref[0, rows, :] = hm
        lgt_ref[0, :, rows] = lax.dot_general(wr_ref[...], hm, (((1,), (1,)), ((), ())),
                                              preferred_element_type=F32) + br_ref[...]


def _mixer_call(lg, z, states, x, mod3, wpa, wpb, wo, sguw, sgub_t, gnw, gnb, bo,
                ln1w, ln1b, wr_t, br):
    bsz, t, d = x.shape
    tt = TT_MIXER
    n_sub = tt // RET_CHUNK
    row = lambda b, i: (b, i, 0)
    c2 = lambda b, i: (0, 0)
    c3 = lambda b, i: (0, 0, 0)
    return pl.pallas_call(
        _mixer_kernel,
        out_shape=(jax.ShapeDtypeStruct((bsz, t, d), F32),
                   jax.ShapeDtypeStruct((bsz, t, d), BF16),
                   jax.ShapeDtypeStruct((bsz, N_EXPERTS, t), F32)),
        grid=(bsz, t // tt),
        in_specs=[
            pl.BlockSpec(memory_space=pltpu.SMEM),
            pl.BlockSpec((1, tt, Z_BLOCK), lambda b, i: (b, i, 0)),
            pl.BlockSpec((1, tt, Z_BLOCK), lambda b, i: (b, i, 1)),
            pl.BlockSpec((1, tt, d), lambda b, i: (b, i, Z_GB // d)),
            pl.BlockSpec((1, RET_HEADS, n_sub, 2 * RET_QK_DIM, RET_V_DIM), lambda b, i: (b, 0, i, 0, 0)),
            pl.BlockSpec((1, tt, d), row),
            pl.BlockSpec((1, 1, 6 * d), lambda b, i: (b, 0, 0)),
            pl.BlockSpec((RET_V_W, d), c2, pipeline_mode=pl.Buffered(1)),
            pl.BlockSpec((SGU_WIDTH, d), c2, pipeline_mode=pl.Buffered(1)),
            pl.BlockSpec((d, d), c2, pipeline_mode=pl.Buffered(1)),
            pl.BlockSpec((SGU_GROUPS, SGU_CHUNK, SGU_CHUNK), c3),
            pl.BlockSpec((SGU_CHUNK, SGU_GROUPS), c2),
            pl.BlockSpec((1, RET_V_W), c2),
            pl.BlockSpec((1, RET_V_W), c2),
            pl.BlockSpec((1, d), c2),
            pl.BlockSpec((1, d), c2),
            pl.BlockSpec((1, d), c2),
            pl.BlockSpec((N_EXPERTS, d), c2),
            pl.BlockSpec((N_EXPERTS, 1), c2),
        ],
        out_specs=(pl.BlockSpec((1, tt, d), row),
                   pl.BlockSpec((1, tt, d), row),
                   pl.BlockSpec((1, N_EXPERTS, tt), lambda b, i: (b, 0, i))),
        scratch_shapes=[pltpu.VMEM((tt, RET_V_W), BF16), pltpu.VMEM((tt, SGU_WIDTH), BF16)],
        compiler_params=_cparams(("parallel", "parallel")),
        name="mixer",
    )(lg, z, z, z, states, x, mod3, wpa, wpb, wo, sguw, sgub_t, gnw, gnb, bo,
      ln1w, ln1b, wr_t, br)


def _route_kernel(lgt_ref, pos_ref, gate_ref, cnt_ref, *, cap):
    lg = lgt_ref[0]
    n_e, t = lg.shape
    tb = TB_ROUTE
    m = jnp.max(lg, axis=0, keepdims=True)
    ex = jnp.exp(lg - m)
    aff = ex / jnp.sum(ex, axis=0, keepdims=True)

    def search(i, thr_bits):
        cand = thr_bits | lax.shift_left(jnp.int32(1), 30 - i)
        cnt = jnp.sum((aff >= lax.bitcast_convert_type(cand, F32)).astype(I32), axis=1, keepdims=True)
        return jnp.where(cnt >= cap, cand, thr_bits)

    thr_bits = lax.fori_loop(0, 31, search, jnp.zeros((n_e, 1), I32))
    floor_f = lax.bitcast_convert_type(thr_bits, F32)
    thr = jnp.min(jnp.where(aff >= floor_f, aff, jnp.inf), axis=1, keepdims=True)
    need = (cap - jnp.sum((aff > thr).astype(I32), axis=1, keepdims=True)).astype(F32)

    r = lax.broadcasted_iota(I32, (tb, tb), 0)
    cc = lax.broadcasted_iota(I32, (tb, tb), 1)
    tri = (r <= cc).astype(BF16)
    carry_eq = jnp.zeros((n_e, 1), F32)
    carry_sel = jnp.zeros((n_e, 1), F32)
    for blk in range(t // tb):
        sl = slice(blk * tb, (blk + 1) * tb)
        aff_b = aff[:, sl]
        eq = aff_b == thr
        eq_b = eq.astype(BF16)
        inc_eq = jnp.dot(eq_b, tri, preferred_element_type=F32)
        before = carry_eq + inc_eq - eq_b.astype(F32)
        sel = (aff_b > thr) | (eq & (before < need))
        sel_b = sel.astype(BF16)
        inc_sel = jnp.dot(sel_b, tri, preferred_element_type=F32)
        pos = carry_sel + inc_sel - 1.0
        pos_ref[0, blk] = jnp.where(sel, pos.astype(I32), -1)
        gate_ref[0, blk] = aff_b
        cnt_ref[0, blk] = jnp.broadcast_to(carry_sel, (n_e, LANES)).astype(I32)
        carry_eq = carry_eq + inc_eq[:, tb - 1:tb]
        carry_sel = carry_sel + inc_sel[:, tb - 1:tb]


def _route_call(logits_t, cap):
    bsz, n_e, t = logits_t.shape
    nblk = t // TB_ROUTE
    return pl.pallas_call(
        functools.partial(_route_kernel, cap=cap),
        out_shape=(jax.ShapeDtypeStruct((bsz, nblk, n_e, TB_ROUTE), I32),
                   jax.ShapeDtypeStruct((bsz, nblk, n_e, TB_ROUTE), F32),
                   jax.ShapeDtypeStruct((bsz, nblk, n_e, LANES), I32)),
        grid=(bsz,),
        in_specs=[pl.BlockSpec((1, n_e, t), lambda b: (b, 0, 0))],
        out_specs=(pl.BlockSpec((1, nblk, n_e, TB_ROUTE), lambda b: (b, 0, 0, 0)),
                   pl.BlockSpec((1, nblk, n_e, TB_ROUTE), lambda b: (b, 0, 0, 0)),
                   pl.BlockSpec((1, nblk, n_e, LANES), lambda b: (b, 0, 0, 0))),
        compiler_params=_cparams(("parallel",)),
        name="route",
    )(logits_t)


def _window_rows(ws, win, n_tok):
    return lax.broadcasted_iota(I32, (win, n_tok), 0) + ws


def _sweep_window(w, cap, win, prow, rest):
    off = pl.multiple_of(jnp.minimum(w * win, cap - win), WIN_ALIGN)
    rows = _window_rows(off, win, prow.shape[-1])
    return off, (rows == prow) & rest & (rows >= w * win)


def _gather_kernel(ws_ref, nx_ref, hm_ref, pos_ref, gate_ref, xs_ref, gc_ref):
    b = pl.program_id(0)
    g = pl.program_id(1)
    step = pl.program_id(2)
    merge = DISPATCH_MERGE
    eg, tb = pos_ref.shape[2:]
    n_sub = pos_ref.shape[1] // merge
    n_e = eg * pl.num_programs(1)
    n_win = n_sub * pl.num_programs(2)
    cap = xs_ref.shape[2]
    win = WIN_DISPATCH

    @pl.when(step == 0)
    def _():
        xs_ref[...] = jnp.zeros_like(xs_ref)
        gc_ref[...] = jnp.zeros_like(gc_ref)

    def base(sub):
        return (b * n_win + step * n_sub + sub) * n_e + g * eg

    def tokens(sub):
        return hm_ref[0, sub * merge * tb:(sub + 1) * merge * tb, :]

    def lanes(ref, sub, el):
        return jnp.concatenate([ref[0, sub * merge + m, el:el + 1, :] for m in range(merge)], axis=-1)

    for sub in range(n_sub):
        for el in range(eg):
            ws = pl.multiple_of(ws_ref[base(sub) + el], WIN_ALIGN)
            match = _window_rows(ws, win, merge * tb) == lanes(pos_ref, sub, el)
            sl = pl.ds(ws, win)
            xs_ref[0, el, sl, :] += jnp.dot(match.astype(BF16), tokens(sub),
                                            preferred_element_type=F32).astype(BF16)
            gc_ref[0, el, sl, :] += jnp.sum(jnp.where(match, lanes(gate_ref, sub, el), 0.0),
                                             axis=1, keepdims=True)

    for sub in range(n_sub):
        for el in range(eg):
            @pl.when(nx_ref[base(sub) + el] > 0)
            def _(sub=sub, el=el):
                prow = lanes(pos_ref, sub, el)
                grow = lanes(gate_ref, sub, el)
                rest = prow >= ws_ref[base(sub) + el] + win

                def body(w, carry):
                    off, m = _sweep_window(w, cap, win, prow, rest)
                    xs_ref[0, el, pl.ds(off, win), :] += jnp.dot(m.astype(BF16), tokens(sub),
                                                                  preferred_element_type=F32).astype(BF16)
                    gc_ref[0, el, pl.ds(off, win), :] += jnp.sum(jnp.where(m, grow, 0.0), axis=1, keepdims=True)
                    return carry

                lax.fori_loop(0, pl.cdiv(cap, win), body, 0)


def _gather_call(ws, nx, hm, pos_b, gate_b, cap):
    bsz, t, d = hm.shape
    nblk, n_e, tb = pos_b.shape[1:]
    eg = n_e // DISPATCH_GROUPS
    blocks = DISPATCH_STEP_WINDOWS * DISPATCH_MERGE
    grid_spec = pltpu.PrefetchScalarGridSpec(
        num_scalar_prefetch=2,
        grid=(bsz, DISPATCH_GROUPS, nblk // blocks),
        in_specs=[
            pl.BlockSpec((1, blocks * tb, d), lambda b, g, k, ws, nx: (b, k, 0)),
            pl.BlockSpec((1, blocks, eg, tb), lambda b, g, k, ws, nx: (b, k, g, 0)),
            pl.BlockSpec((1, blocks, eg, tb), lambda b, g, k, ws, nx: (b, k, g, 0)),
        ],
        out_specs=(pl.BlockSpec((1, eg, cap, d), lambda b, g, k, ws, nx: (b, g, 0, 0)),
                   pl.BlockSpec((1, eg, cap, 1), lambda b, g, k, ws, nx: (b, g, 0, 0))),
    )
    return pl.pallas_call(
        _gather_kernel,
        out_shape=(jax.ShapeDtypeStruct((bsz, n_e, cap, d), BF16),
                   jax.ShapeDtypeStruct((bsz, n_e, cap, 1), F32)),
        grid_spec=grid_spec,
        compiler_params=_cparams(("parallel", "parallel", "arbitrary")),
        name="gather",
    )(ws, nx, hm, pos_b, gate_b)


def _ffn_kernel(xs_ref, gc_ref, wg_ref, wu_ref, wd_ref, ye_ref, wgu_s, wd_s):
    f = wd_s.shape[0]
    wgu_s[:, 0:f] = wg_ref[0].astype(BF16)
    wgu_s[:, f:2 * f] = wu_ref[0].astype(BF16)
    wd_s[...] = wd_ref[0].astype(BF16)

    rt = FFN_ROWS
    for b in range(xs_ref.shape[0]):
        for j in range(xs_ref.shape[2] // rt):
            sl = slice(j * rt, (j + 1) * rt)
            xgu = jnp.dot(xs_ref[b, 0, sl, :], wgu_s[...], preferred_element_type=F32)
            hid = (jax.nn.silu(xgu[:, 0:f]) * xgu[:, f:2 * f]).astype(BF16)
            ye = jnp.dot(hid, wd_s[...], preferred_element_type=F32) * gc_ref[b, 0, sl, :]
            ye_ref[b, 0, sl, :] = ye.astype(BF16)


def _ffn_call(xs, gc, w_gate, w_up, w_down):
    bsz, n_e, cap, d = xs.shape
    f = w_gate.shape[2]
    return pl.pallas_call(
        _ffn_kernel,
        out_shape=jax.ShapeDtypeStruct((bsz, n_e, cap, d), BF16),
        grid=(n_e,),
        in_specs=[
            pl.BlockSpec((bsz, 1, cap, d), lambda e: (0, e, 0, 0)),
            pl.BlockSpec((bsz, 1, cap, 1), lambda e: (0, e, 0, 0)),
            pl.BlockSpec((1, d, f), lambda e: (e, 0, 0)),
            pl.BlockSpec((1, d, f), lambda e: (e, 0, 0)),
            pl.BlockSpec((1, f, d), lambda e: (e, 0, 0)),
        ],
        out_specs=pl.BlockSpec((bsz, 1, cap, d), lambda e: (0, e, 0, 0)),
        scratch_shapes=[pltpu.VMEM((d, 2 * f), BF16), pltpu.VMEM((f, d), BF16)],
        compiler_params=_cparams(("parallel",)),
        name="ffn",
    )(xs, gc, w_gate, w_up, w_down)


def _combine_kernel(ws_ref, nx_ref, ye_ref, pos_ref, x1_ref, mod_ref, lnw_ref, lnb_ref, out_ref, acc_scr):
    b = pl.program_id(0)
    step = pl.program_id(1)
    n_sub, n_e, tb = pos_ref.shape[1:]
    nblk = n_sub * pl.num_programs(1)
    cap = ye_ref.shape[2]
    d = D_MODEL
    win = WIN_COMBINE
    g2 = mod_ref[0, :, 5 * d:6 * d]

    for sub in range(n_sub):
        base = (b * nblk + step * n_sub + sub) * n_e
        starts = [pl.multiple_of(ws_ref[base + e], WIN_ALIGN) for e in range(n_e)]
        onehot = jnp.concatenate(
            [(_window_rows(starts[e], win, tb) == pos_ref[0, sub, e:e + 1, :]).astype(BF16) for e in range(n_e)],
            axis=0)
        rows = jnp.concatenate([ye_ref[0, e, pl.ds(starts[e], win), :] for e in range(n_e)], axis=0)
        acc_scr[...] = _dot_t0(onehot, rows)

        for e in range(n_e):
            @pl.when(nx_ref[base + e] > 0)
            def _(e=e, sub=sub, starts=starts):
                prow = pos_ref[0, sub, e:e + 1, :]
                rest = prow >= starts[e] + win

                def body(w, carry):
                    off, m = _sweep_window(w, cap, win, prow, rest)
                    acc_scr[...] += _dot_t0(m.astype(BF16), ye_ref[0, e, pl.ds(off, win), :])
                    return carry

                lax.fori_loop(0, pl.cdiv(cap, win), body, 0)

        tok = slice(sub * tb, (sub + 1) * tb)
        out_ref[0, tok, :] = (_ln(DEEPNORM_ALPHA * x1_ref[0, tok, :] + g2 * acc_scr[...]) * lnw_ref[...]
                              + lnb_ref[...])


def _combine_call(ws, nx, ye, pos_b, x1, mod3, ln2w, ln2b):
    bsz, n_e, cap, d = ye.shape
    nblk, _, tb = pos_b.shape[1:]
    n_sub = COMBINE_STEP_BLOCKS
    grid_spec = pltpu.PrefetchScalarGridSpec(
        num_scalar_prefetch=2,
        grid=(bsz, nblk // n_sub),
        in_specs=[
            pl.BlockSpec((1, n_e, cap, d), lambda b, k, ws, nx: (b, 0, 0, 0), pipeline_mode=pl.Buffered(1)),
            pl.BlockSpec((1, n_sub, n_e, tb), lambda b, k, ws, nx: (b, k, 0, 0)),
            pl.BlockSpec((1, n_sub * tb, d), lambda b, k, ws, nx: (b, k, 0)),
            pl.BlockSpec((1, 1, 6 * d), lambda b, k, ws, nx: (b, 0, 0)),
            pl.BlockSpec((1, d), lambda b, k, ws, nx: (0, 0)),
            pl.BlockSpec((1, d), lambda b, k, ws, nx: (0, 0)),
        ],
        out_specs=pl.BlockSpec((1, n_sub * tb, d), lambda b, k, ws, nx: (b, k, 0)),
        scratch_shapes=[pltpu.VMEM((tb, d), F32)],
    )
    return pl.pallas_call(
        _combine_kernel,
        out_shape=jax.ShapeDtypeStruct((bsz, nblk * tb, d), F32),
        grid_spec=grid_spec,
        compiler_params=_cparams(("parallel", "arbitrary")),
        name="combine",
    )(ws, nx, ye, pos_b, x1, mod3, ln2w, ln2b)


def _rope_tables(n_tokens):
    rows = n_tokens // GRID_W
    n_freq = RET_QK_DIM // 4
    inv = ROPE_THETA ** (-jnp.arange(n_freq, dtype=F32) / n_freq)
    ang_r = jnp.arange(rows, dtype=F32)[:, None] * inv
    ang_c = jnp.arange(GRID_W, dtype=F32)[:, None] * inv
    zr = jnp.zeros((rows, 2 * n_freq), F32)
    zc = jnp.zeros((GRID_W, 2 * n_freq), F32)
    cr, sr, cc, sc = jnp.cos(ang_r), jnp.sin(ang_r), jnp.cos(ang_c), jnp.sin(ang_c)
    cos_t = (jnp.concatenate([cr, cr, zr], axis=-1)[:, None, :]
             + jnp.concatenate([zc, cc, cc], axis=-1)[None, :, :]).reshape(n_tokens, RET_QK_DIM)
    sin_t = (jnp.concatenate([-sr, sr, zr], axis=-1)[:, None, :]
             + jnp.concatenate([zc, -sc, sc], axis=-1)[None, :, :]).reshape(n_tokens, RET_QK_DIM)
    return jnp.concatenate([cos_t, sin_t], axis=-1)


def kernel(x, c, ctx, c_ctx, w_ada, b_ada, w_in, b_in, ret_decay_f, ret_decay_b, ret_gn_w, ret_gn_b,
           sgu_ln_w, sgu_ln_b, sgu_w, sgu_b, w_pa, w_pb, w_o, b_o, ln1_w, ln1_b,
           w_router, b_router, w_gate, w_up, w_down, ln2_w, ln2_b):
    bsz, t, d = x.shape
    assert d == D_MODEL and w_ada.shape[0] == DEPTH == 1 and bsz < MOD_ROWS
    cap = EC_CAPACITY_FACTOR * t // N_EXPERTS
    for win in (WIN_DISPATCH, WIN_COMBINE):
        assert cap >= win and (cap - win) % WIN_ALIGN == 0 and win % WIN_ALIGN == 0
    assert t % (TB_ROUTE * DISPATCH_MERGE * DISPATCH_STEP_WINDOWS) == 0 and t % (TB_ROUTE * COMBINE_STEP_BLOCKS) == 0
    assert N_EXPERTS % DISPATCH_GROUPS == 0
    l = 0

    cvec = jnp.concatenate([c, c_ctx[None], jnp.zeros((MOD_ROWS - bsz - 1, d), F32)], axis=0)
    mod = _mod_call(cvec.T, w_ada[l], b_ada[l][None], bsz + 1)
    mod3 = mod[:, None, :]

    w_in_b = w_in[l].astype(BF16)
    b_in2 = b_in[l][None]
    kctx, vctx = _ctx_kv_call(ctx, mod3, w_in_b, b_in2, bsz)

    z = _inproj_call(x, mod3, w_in_b, b_in2, _rope_tables(t), sgu_ln_w[l][None], sgu_ln_b[l][None])

    lg = jnp.stack([jax.nn.log_sigmoid(ret_decay_f[l].astype(F32)),
                    jax.nn.log_sigmoid(ret_decay_b[l].astype(F32))])
    states = _state_call(lg, z, kctx, vctx)

    x1, hm, logits_t = _mixer_call(
        lg, z, states, x, mod3,
        w_pa[l].astype(BF16), w_pb[l].astype(BF16), w_o[l].astype(BF16),
        sgu_w[l].astype(BF16), sgu_b[l].T, ret_gn_w[l][None], ret_gn_b[l][None], b_o[l][None],
        ln1_w[l][None], ln1_b[l][None], w_router[l].T.astype(BF16), b_router[l][:, None])

    pos_b, gate_b, cnt_b = _route_call(logits_t, cap)

    c0 = cnt_b[..., 0]
    c1 = jnp.concatenate([c0[:, 1:], jnp.full_like(c0[:, :1], cap)], axis=1)

    def windows(start, end, win):
        ws = jnp.minimum((start // WIN_ALIGN) * WIN_ALIGN, cap - win)
        return ws.reshape(-1), (end > ws + win).astype(I32).reshape(-1)

    ws_d, nx_d = windows(c0[:, ::DISPATCH_MERGE], c1[:, DISPATCH_MERGE - 1::DISPATCH_MERGE], WIN_DISPATCH)
    ws_c, nx_c = windows(c0, c1, WIN_COMBINE)

    xs, gc = _gather_call(ws_d, nx_d, hm, pos_b, gate_b, cap)
    ye = _ffn_call(xs, gc, w_gate[l], w_up[l], w_down[l])
    return _combine_call(ws_c, nx_c, ye, pos_b, x1, mod3, ln2_w[l][None], ln2_b[l][None])
```

```python
import functools
import math

import jax
import jax.numpy as jnp
from jax import lax
from jax.experimental import pallas as pl
from jax.experimental.pallas import tpu as pltpu

F32 = jnp.float32
BF16 = jnp.bfloat16
I32 = jnp.int32

D_MODEL = 1024
DEPTH = 1
GRID_W = 64
RET_HEADS = 4
RET_QK_DIM = 128
RET_V_DIM = 256
RET_CHUNK = 128
ROPE_THETA = 10000.0
RET_QK_W = RET_HEADS * RET_QK_DIM
RET_V_W = RET_HEADS * RET_V_DIM
SGU_GROUPS = 4
SGU_CHUNK = 128
SGU_WIDTH = 1024
SGU_GW = SGU_WIDTH // SGU_GROUPS
N_EXPERTS = 16
EC_CAPACITY_FACTOR = 2
LN_EPS = 1e-6
DEEPNORM_ALPHA = (2.0 * DEPTH) ** 0.25
D_IN = 2 * RET_QK_W + 2 * RET_V_W + 2 * SGU_WIDTH + 2 * D_MODEL
Z_K = RET_QK_W
Z_V = 2 * RET_QK_W
Z_SG = Z_V + RET_V_W
Z_U = Z_SG + RET_V_W
Z_VS = Z_U + SGU_WIDTH
Z_GA = Z_VS + SGU_WIDTH
Z_GB = Z_GA + D_MODEL
Z_BLOCK = 3 * 1024

V7X_VMEM_BYTES = 64 * 1024 * 1024
VMEM_LIMIT = V7X_VMEM_BYTES - 4 * 1024 * 1024
LANES = 128

MOD_ROWS = 8
MOD_COLS = 2048
TM_INPROJ = 1024
INPROJ_CHUNK = 256
TT_MIXER = 512
MERGE_ROWS = 512
TB_ROUTE = 256
DISPATCH_MERGE = 2
WIN_DISPATCH = 128
WIN_COMBINE = 64
WIN_ALIGN = 16
DISPATCH_GROUPS = 2
DISPATCH_STEP_WINDOWS = 4
COMBINE_STEP_BLOCKS = 4
STATE_UNROLL = 16
FFN_ROWS = 256


def _cparams(sem):
    return pltpu.CompilerParams(dimension_semantics=sem, vmem_limit_bytes=VMEM_LIMIT)


def _ln(x):
    mu = jnp.mean(x, axis=-1, keepdims=True)
    xc = x - mu
    var = jnp.mean(xc * xc, axis=-1, keepdims=True)
    return xc * lax.rsqrt(var + LN_EPS)


_GELU_A = -2.0 * math.sqrt(2.0 / math.pi) * math.log2(math.e)
_GELU_B = _GELU_A * 0.044715


def _gelu(x):
    return x / (1.0 + jnp.exp2(x * (_GELU_A + _GELU_B * (x * x))))


def _iota_f(shape, dim):
    return lax.broadcasted_iota(I32, shape, dim).astype(F32)


def _mod_kernel(ct_ref, w_ref, b_ref, o_ref, *, n_rows):
    c = ct_ref[...]
    a = c * jax.nn.sigmoid(c)
    w = w_ref[...]
    rows = [jnp.sum(w * a[:, m:m + 1], axis=0, keepdims=True) + b_ref[...] for m in range(n_rows)]
    rows += [jnp.zeros_like(rows[0])] * (MOD_ROWS - n_rows)
    o_ref[...] = jnp.concatenate(rows, axis=0)


def _mod_call(cvec_t, w_ada, b_ada, n_rows):
    d, n = w_ada.shape
    tn = MOD_COLS
    return pl.pallas_call(
        functools.partial(_mod_kernel, n_rows=n_rows),
        out_shape=jax.ShapeDtypeStruct((MOD_ROWS, n), F32),
        grid=(n // tn,),
        in_specs=[
            pl.BlockSpec((d, MOD_ROWS), lambda j: (0, 0)),
            pl.BlockSpec((d, tn), lambda j: (0, j)),
            pl.BlockSpec((1, tn), lambda j: (0, j)),
        ],
        out_specs=pl.BlockSpec((MOD_ROWS, tn), lambda j: (0, j)),
        compiler_params=_cparams(("parallel",)),
        name="mod",
    )(cvec_t, w_ada, b_ada)


def _ctx_kv_kernel(x_ref, mod_ref, wk_ref, wv_ref, bk_ref, bv_ref, k_ref, v_ref):
    d = D_MODEL
    sh = mod_ref[0, :, 0:d]
    sc = mod_ref[0, :, d:2 * d]
    h = (_ln(x_ref[0]) * (1.0 + sc) + sh).astype(BF16)
    k = jnp.dot(h, wk_ref[...], preferred_element_type=F32) + bk_ref[...]
    k_ref[0] = (k * (RET_QK_DIM ** -0.5)).astype(BF16)
    v_ref[0] = (jnp.dot(h, wv_ref[...], preferred_element_type=F32) + bv_ref[...]).astype(BF16)


def _ctx_kv_call(ctx, mod3, w_in, b_in, ctx_row):
    bsz, n_ctx, d = ctx.shape
    kblk = Z_K // RET_QK_W
    vblk = Z_V // RET_V_W
    return pl.pallas_call(
        _ctx_kv_kernel,
        out_shape=(jax.ShapeDtypeStruct((bsz, n_ctx, RET_QK_W), BF16),
                   jax.ShapeDtypeStruct((bsz, n_ctx, RET_V_W), BF16)),
        grid=(bsz,),
        in_specs=[
            pl.BlockSpec((1, n_ctx, d), lambda b: (b, 0, 0)),
            pl.BlockSpec((1, 1, 6 * d), lambda b: (ctx_row, 0, 0)),
            pl.BlockSpec((d, RET_QK_W), lambda b: (0, kblk)),
            pl.BlockSpec((d, RET_V_W), lambda b: (0, vblk)),
            pl.BlockSpec((1, RET_QK_W), lambda b: (0, kblk)),
            pl.BlockSpec((1, RET_V_W), lambda b: (0, vblk)),
        ],
        out_specs=(pl.BlockSpec((1, n_ctx, RET_QK_W), lambda b: (b, 0, 0)),
                   pl.BlockSpec((1, n_ctx, RET_V_W), lambda b: (b, 0, 0))),
        compiler_params=_cparams(("parallel",)),
        name="ctx_kv",
    )(ctx, mod3, w_in, w_in, b_in, b_in)


def _inproj_kernel(x_ref, mod_ref, w_ref, b_ref, rope_ref, lnw_ref, lnb_ref, z_ref):
    d = D_MODEL
    sh = mod_ref[0, :, 0:d]
    sc = mod_ref[0, :, d:2 * d]
    h = (_ln(x_ref[0]) * (1.0 + sc) + sh).astype(BF16)

    def proj(c0, width):
        return jnp.dot(h, w_ref[:, c0:c0 + width], preferred_element_type=F32) + b_ref[:, c0:c0 + width]

    zq = proj(0, 2 * RET_QK_W)
    cos = rope_ref[:, 0:LANES]
    sin = rope_ref[:, LANES:2 * LANES]
    tm = zq.shape[0]
    upper = (lax.broadcasted_iota(I32, (tm, LANES), 1) & 32) != 0
    for hb in range(2 * RET_HEADS):
        zs = zq[:, hb * LANES:(hb + 1) * LANES]
        if hb >= RET_HEADS:
            zs = zs * (RET_QK_DIM ** -0.5)
        sw = jnp.where(upper, pltpu.roll(zs, 32, 1), pltpu.roll(zs, LANES - 32, 1))
        z_ref[0, :, hb * LANES:(hb + 1) * LANES] = (zs * cos + sw * sin).astype(BF16)

    def emit(c0, width, fn):
        for cc in range(c0, c0 + width, INPROJ_CHUNK):
            z_ref[0, :, cc:cc + INPROJ_CHUNK] = fn(proj(cc, INPROJ_CHUNK)).astype(BF16)

    vs = _ln(_gelu(proj(Z_VS, SGU_WIDTH))) * lnw_ref[...] + lnb_ref[...]
    z_ref[0, :, Z_VS:Z_VS + SGU_WIDTH] = vs.astype(BF16)
    emit(Z_U, SGU_WIDTH, _gelu)
    emit(Z_SG, RET_V_W, jax.nn.silu)
    emit(Z_GA, d, jax.nn.sigmoid)
    emit(Z_GB, d, jax.nn.sigmoid)
    emit(Z_V, RET_V_W, lambda z: z)


def _inproj_call(x, mod3, w_in, b_in, rope_t, sgu_ln_w, sgu_ln_b):
    bsz, t, d = x.shape
    tm = TM_INPROJ
    row = lambda b, i: (b, i, 0)
    const2 = lambda b, i: (0, 0)
    return pl.pallas_call(
        _inproj_kernel,
        out_shape=jax.ShapeDtypeStruct((bsz, t, D_IN), BF16),
        grid=(bsz, t // tm),
        in_specs=[
            pl.BlockSpec((1, tm, d), row),
            pl.BlockSpec((1, 1, 6 * d), lambda b, i: (b, 0, 0)),
            pl.BlockSpec((d, D_IN), const2, pipeline_mode=pl.Buffered(1)),
            pl.BlockSpec((1, D_IN), const2),
            pl.BlockSpec((tm, 2 * LANES), lambda b, i: (i, 0)),
            pl.BlockSpec((1, SGU_WIDTH), const2),
            pl.BlockSpec((1, SGU_WIDTH), const2),
        ],
        out_specs=pl.BlockSpec((1, tm, D_IN), row),
        compiler_params=_cparams(("parallel", "parallel")),
        name="inproj",
    )(x, mod3, w_in, b_in, rope_t, sgu_ln_w, sgu_ln_b)


def _dot_t0(a, b):
    return lax.dot_general(a, b, (((0,), (0,)), ((), ())), preferred_element_type=F32)


def _state_kernel(lg_ref, k_ref, v_ref, kc_ref, vc_ref, s_ref):
    hd = pl.program_id(1)
    lgf = lg_ref[0, hd]
    lgb = lg_ref[1, hd]
    n_chunks = s_ref.shape[2]
    n_ctx = kc_ref.shape[1]
    c = RET_CHUNK
    li = _iota_f((c, RET_QK_DIM), 0)
    zeta_f = jnp.exp((c - 1.0 - li) * lgf)
    zeta_b = jnp.exp(li * lgb)
    one = jnp.ones((1, 1), F32)
    cd_f = jnp.exp(one * (c * lgf))
    cd_b = jnp.exp(one * (c * lgb))

    tc = _iota_f((n_ctx, RET_QK_DIM), 0)
    kc = kc_ref[0].astype(F32)
    vc = vc_ref[0]
    s0f = _dot_t0((kc * jnp.exp((n_ctx - 1.0 - tc) * lgf)).astype(BF16), vc)
    s0b = _dot_t0((kc * jnp.exp(tc * lgb)).astype(BF16), vc)

    def chunk_kv(n, zeta):
        off = pl.multiple_of(n * c, c)
        kk = k_ref[0, pl.ds(off, c), :].astype(F32)
        return _dot_t0((kk * zeta).astype(BF16), v_ref[0, pl.ds(off, c), :])

    def step(i, carry):
        sf, sb = carry
        nb = n_chunks - 1 - i
        s_ref[0, 0, i, 0:RET_QK_DIM, :] = sf.astype(BF16)
        s_ref[0, 0, nb, RET_QK_DIM:2 * RET_QK_DIM, :] = sb.astype(BF16)
        return cd_f * sf + chunk_kv(i, zeta_f), cd_b * sb + chunk_kv(nb, zeta_b)

    lax.fori_loop(0, n_chunks, step, (s0f, s0b), unroll=STATE_UNROLL)


def _state_call(lg, z, kctx, vctx):
    bsz, t, _ = z.shape
    n_ctx = kctx.shape[1]
    n_chunks = t // RET_CHUNK
    return pl.pallas_call(
        _state_kernel,
        out_shape=jax.ShapeDtypeStruct((bsz, RET_HEADS, n_chunks, 2 * RET_QK_DIM, RET_V_DIM), BF16),
        grid=(bsz, RET_HEADS),
        in_specs=[
            pl.BlockSpec(memory_space=pltpu.SMEM),
            pl.BlockSpec((1, t, RET_QK_DIM), lambda b, h: (b, 0, RET_QK_W // RET_QK_DIM + h)),
            pl.BlockSpec((1, t, RET_V_DIM), lambda b, h: (b, 0, Z_V // RET_V_DIM + h)),
            pl.BlockSpec((1, n_ctx, RET_QK_DIM), lambda b, h: (b, 0, h)),
            pl.BlockSpec((1, n_ctx, RET_V_DIM), lambda b, h: (b, 0, h)),
        ],
        out_specs=pl.BlockSpec((1, 1, n_chunks, 2 * RET_QK_DIM, RET_V_DIM), lambda b, h: (b, h, 0, 0, 0)),
        compiler_params=_cparams(("parallel", "parallel")),
        name="states",
    )(lg, z, z, kctx, vctx)


def _mixer_kernel(lg_ref, za_ref, zb_ref, gb_ref, s_ref, x_ref, mod_ref,
                  wpa_ref, wpb_ref, wo_ref, sguw_ref, sgub_ref, gnw_ref, gnb_ref, bo_ref,
                  ln1w_ref, ln1b_ref, wr_ref, br_ref,
                  x1_ref, hm_ref, lgt_ref, ret_scr, sgu_scr):
    d = D_MODEL
    c = RET_CHUNK

    n_sub = za_ref.shape[1] // c
    row = lax.broadcasted_iota(I32, (c, c), 0)
    col = lax.broadcasted_iota(I32, (c, c), 1)
    diff = (row - col).astype(F32)
    rowq = _iota_f((c, RET_QK_DIM), 0)
    for hd in range(RET_HEADS):
        lgf = lg_ref[0, hd]
        lgb = lg_ref[1, hd]
        vcols = slice(hd * RET_V_DIM, (hd + 1) * RET_V_DIM)
        mask = (jnp.where(diff >= 0, jnp.exp(jnp.maximum(diff, 0.0) * lgf), 0.0)
                + jnp.where(diff <= 0, jnp.exp(jnp.maximum(-diff, 0.0) * lgb), 0.0))
        xi_f = jnp.exp((rowq + 1.0) * lgf)
        xi_b = jnp.exp((c - rowq) * lgb)
        for ci in range(n_sub):
            r0 = ci * c
            q = za_ref[0, r0:r0 + c, hd * RET_QK_DIM:(hd + 1) * RET_QK_DIM]
            k = za_ref[0, r0:r0 + c, RET_QK_W + hd * RET_QK_DIM:RET_QK_W + (hd + 1) * RET_QK_DIM]
            s = lax.dot_general(q, k, (((1,), (1,)), ((), ())), preferred_element_type=F32)
            vv = za_ref[0, r0:r0 + c, Z_V + hd * RET_V_DIM:Z_V + (hd + 1) * RET_V_DIM]
            intra = jnp.dot((s * mask).astype(BF16), vv, preferred_element_type=F32)
            qf = q.astype(F32)
            qx = jnp.concatenate([(qf * xi_f).astype(BF16), (qf * xi_b).astype(BF16)], axis=1)
            cross = jnp.dot(qx, s_ref[0, hd, ci], preferred_element_type=F32)
            o = _ln(intra + cross) * gnw_ref[:, vcols] + gnb_ref[:, vcols]
            gate = za_ref[0, r0:r0 + c, Z_SG + hd * RET_V_DIM:Z_SG + (hd + 1) * RET_V_DIM]
            ret_scr[r0:r0 + c, vcols] = (o * gate.astype(F32)).astype(BF16)

    for g in range(SGU_GROUPS):
        gcols = slice(g * SGU_GW, (g + 1) * SGU_GW)
        wg = sguw_ref[g]
        bg = sgub_ref[:, g:g + 1]
        for ci in range(n_sub):
            r0 = ci * c
            vsb = zb_ref[0, r0:r0 + c, Z_VS - Z_BLOCK + g * SGU_GW:Z_VS - Z_BLOCK + (g + 1) * SGU_GW]
            ub = zb_ref[0, r0:r0 + c, Z_U - Z_BLOCK + g * SGU_GW:Z_U - Z_BLOCK + (g + 1) * SGU_GW]
            sp = jnp.dot(wg, vsb, preferred_element_type=F32) + bg
            sgu_scr[r0:r0 + c, gcols] = (ub.astype(F32) * sp).astype(BF16)

    g1 = mod_ref[0, :, 2 * d:3 * d]
    sh2 = mod_ref[0, :, 3 * d:4 * d]
    sc2 = mod_ref[0, :, 4 * d:5 * d]
    for r0 in range(0, za_ref.shape[1], MERGE_ROWS):
        rows = slice(r0, r0 + MERGE_ROWS)
        pa = jnp.dot(ret_scr[rows, :], wpa_ref[...], preferred_element_type=F32)
        pb = jnp.dot(sgu_scr[rows, :], wpb_ref[...], preferred_element_type=F32)
        ga = zb_ref[0, rows, Z_GA - Z_BLOCK:Z_GA - Z_BLOCK + d]
        y = (ga.astype(F32) * pa + gb_ref[0, rows, :].astype(F32) * pb).astype(BF16)
        mix = jnp.dot(y, wo_ref[...], preferred_element_type=F32) + bo_ref[...]
        x1 = _ln(DEEPNORM_ALPHA * x_ref[0, rows, :] + g1 * mix) * ln1w_ref[...] + ln1b_ref[...]
        x1_ref[0, rows, :] = x1
        hm = (_ln(x1) * (1.0 + sc2) + sh2).astype(BF16)
        hm_ref[0, rows, :] = hm
        lgt_ref[0, :, rows] = lax.dot_general(wr_ref[...], hm, (((1,), (1,)), ((), ())),
                                              preferred_element_type=F32) + br_ref[...]


def _mixer_call(lg, z, states, x, mod3, wpa, wpb, wo, sguw, sgub_t, gnw, gnb, bo,
                ln1w, ln1b, wr_t, br):
    bsz, t, d = x.shape
    tt = TT_MIXER
    n_sub = tt // RET_CHUNK
    row = lambda b, i: (b, i, 0)
    c2 = lambda b, i: (0, 0)
    c3 = lambda b, i: (0, 0, 0)
    return pl.pallas_call(
        _mixer_kernel,
        out_shape=(jax.ShapeDtypeStruct((bsz, t, d), F32),
                   jax.ShapeDtypeStruct((bsz, t, d), BF16),
                   jax.ShapeDtypeStruct((bsz, N_EXPERTS, t), F32)),
        grid=(bsz, t // tt),
        in_specs=[
            pl.BlockSpec(memory_space=pltpu.SMEM),
            pl.BlockSpec((1, tt, Z_BLOCK), lambda b, i: (b, i, 0)),
            pl.BlockSpec((1, tt, Z_BLOCK), lambda b, i: (b, i, 1)),
            pl.BlockSpec((1, tt, d), lambda b, i: (b, i, Z_GB // d)),
            pl.BlockSpec((1, RET_HEADS, n_sub, 2 * RET_QK_DIM, RET_V_DIM), lambda b, i: (b, 0, i, 0, 0)),
            pl.BlockSpec((1, tt, d), row),
            pl.BlockSpec((1, 1, 6 * d), lambda b, i: (b, 0, 0)),
            pl.BlockSpec((RET_V_W, d), c2, pipeline_mode=pl.Buffered(1)),
            pl.BlockSpec((SGU_WIDTH, d), c2, pipeline_mode=pl.Buffered(1)),
            pl.BlockSpec((d, d), c2, pipeline_mode=pl.Buffered(1)),
            pl.BlockSpec((SGU_GROUPS, SGU_CHUNK, SGU_CHUNK), c3),
            pl.BlockSpec((SGU_CHUNK, SGU_GROUPS), c2),
            pl.BlockSpec((1, RET_V_W), c2),
            pl.BlockSpec((1, RET_V_W), c2),
            pl.BlockSpec((1, d), c2),
            pl.BlockSpec((1, d), c2),
            pl.BlockSpec((1, d), c2),
            pl.BlockSpec((N_EXPERTS, d), c2),
            pl.BlockSpec((N_EXPERTS, 1), c2),
        ],
        out_specs=(pl.BlockSpec((1, tt, d), row),
                   pl.BlockSpec((1, tt, d), row),
                   pl.BlockSpec((1, N_EXPERTS, tt), lambda b, i: (b, 0, i))),
        scratch_shapes=[pltpu.VMEM((tt, RET_V_W), BF16), pltpu.VMEM((tt, SGU_WIDTH), BF16)],
        compiler_params=_cparams(("parallel", "parallel")),
        name="mixer",
    )(lg, z, z, z, states, x, mod3, wpa, wpb, wo, sguw, sgub_t, gnw, gnb, bo,
      ln1w, ln1b, wr_t, br)


def _route_kernel(lgt_ref, pos_ref, gate_ref, cnt_ref, *, cap):
    lg = lgt_ref[0]
    n_e, t = lg.shape
    tb = TB_ROUTE
    m = jnp.max(lg, axis=0, keepdims=True)
    ex = jnp.exp(lg - m)
    aff = ex / jnp.sum(ex, axis=0, keepdims=True)

    def search(i, thr_bits):
        cand = thr_bits | lax.shift_left(jnp.int32(1), 30 - i)
        cnt = jnp.sum((aff >= lax.bitcast_convert_type(cand, F32)).astype(I32), axis=1, keepdims=True)
        return jnp.where(cnt >= cap, cand, thr_bits)

    thr_bits = lax.fori_loop(0, 31, search, jnp.zeros((n_e, 1), I32))
    floor_f = lax.bitcast_convert_type(thr_bits, F32)
    thr = jnp.min(jnp.where(aff >= floor_f, aff, jnp.inf), axis=1, keepdims=True)
    need = (cap - jnp.sum((aff > thr).astype(I32), axis=1, keepdims=True)).astype(F32)

    r = lax.broadcasted_iota(I32, (tb, tb), 0)
    cc = lax.broadcasted_iota(I32, (tb, tb), 1)
    tri = (r <= cc).astype(BF16)
    carry_eq = jnp.zeros((n_e, 1), F32)
    carry_sel = jnp.zeros((n_e, 1), F32)
    for blk in range(t // tb):
        sl = slice(blk * tb, (blk + 1) * tb)
        aff_b = aff[:, sl]
        eq = aff_b == thr
        eq_b = eq.astype(BF16)
        inc_eq = jnp.dot(eq_b, tri, preferred_element_type=F32)
        before = carry_eq + inc_eq - eq_b.astype(F32)
        sel = (aff_b > thr) | (eq & (before < need))
        sel_b = sel.astype(BF16)
        inc_sel = jnp.dot(sel_b, tri, preferred_element_type=F32)
        pos = carry_sel + inc_sel - 1.0
        pos_ref[0, blk] = jnp.where(sel, pos.astype(I32), -1)
        gate_ref[0, blk] = aff_b
        cnt_ref[0, blk] = jnp.broadcast_to(carry_sel, (n_e, LANES)).astype(I32)
        carry_eq = carry_eq + inc_eq[:, tb - 1:tb]
        carry_sel = carry_sel + inc_sel[:, tb - 1:tb]


def _route_call(logits_t, cap):
    bsz, n_e, t = logits_t.shape
    nblk = t // TB_ROUTE
    return pl.pallas_call(
        functools.partial(_route_kernel, cap=cap),
        out_shape=(jax.ShapeDtypeStruct((bsz, nblk, n_e, TB_ROUTE), I32),
                   jax.ShapeDtypeStruct((bsz, nblk, n_e, TB_ROUTE), F32),
                   jax.ShapeDtypeStruct((bsz, nblk, n_e, LANES), I32)),
        grid=(bsz,),
        in_specs=[pl.BlockSpec((1, n_e, t), lambda b: (b, 0, 0))],
        out_specs=(pl.BlockSpec((1, nblk, n_e, TB_ROUTE), lambda b: (b, 0, 0, 0)),
                   pl.BlockSpec((1, nblk, n_e, TB_ROUTE), lambda b: (b, 0, 0, 0)),
                   pl.BlockSpec((1, nblk, n_e, LANES), lambda b: (b, 0, 0, 0))),
        compiler_params=_cparams(("parallel",)),
        name="route",
    )(logits_t)


def _window_rows(ws, win, n_tok):
    return lax.broadcasted_iota(I32, (win, n_tok), 0) + ws


def _sweep_window(w, cap, win, prow, rest):
    off = pl.multiple_of(jnp.minimum(w * win, cap - win), WIN_ALIGN)
    rows = _window_rows(off, win, prow.shape[-1])
    return off, (rows == prow) & rest & (rows >= w * win)


def _gather_kernel(ws_ref, nx_ref, hm_ref, pos_ref, gate_ref, xs_ref, gc_ref):
    b = pl.program_id(0)
    g = pl.program_id(1)
    step = pl.program_id(2)
    merge = DISPATCH_MERGE
    eg, tb = pos_ref.shape[2:]
    n_sub = pos_ref.shape[1] // merge
    n_e = eg * pl.num_programs(1)
    n_win = n_sub * pl.num_programs(2)
    cap = xs_ref.shape[2]
    win = WIN_DISPATCH

    @pl.when(step == 0)
    def _():
        xs_ref[...] = jnp.zeros_like(xs_ref)
        gc_ref[...] = jnp.zeros_like(gc_ref)

    def base(sub):
        return (b * n_win + step * n_sub + sub) * n_e + g * eg

    def tokens(sub):
        return hm_ref[0, sub * merge * tb:(sub + 1) * merge * tb, :]

    def lanes(ref, sub, el):
        return jnp.concatenate([ref[0, sub * merge + m, el:el + 1, :] for m in range(merge)], axis=-1)

    for sub in range(n_sub):
        for el in range(eg):
            ws = pl.multiple_of(ws_ref[base(sub) + el], WIN_ALIGN)
            match = _window_rows(ws, win, merge * tb) == lanes(pos_ref, sub, el)
            sl = pl.ds(ws, win)
            xs_ref[0, el, sl, :] += jnp.dot(match.astype(BF16), tokens(sub),
                                            preferred_element_type=F32).astype(BF16)
            gc_ref[0, el, sl, :] += jnp.sum(jnp.where(match, lanes(gate_ref, sub, el), 0.0),
                                             axis=1, keepdims=True)

    for sub in range(n_sub):
        for el in range(eg):
            @pl.when(nx_ref[base(sub) + el] > 0)
            def _(sub=sub, el=el):
                prow = lanes(pos_ref, sub, el)
                grow = lanes(gate_ref, sub, el)
                rest = prow >= ws_ref[base(sub) + el] + win

                def body(w, carry):
                    off, m = _sweep_window(w, cap, win, prow, rest)
                    xs_ref[0, el, pl.ds(off, win), :] += jnp.dot(m.astype(BF16), tokens(sub),
                                                                  preferred_element_type=F32).astype(BF16)
                    gc_ref[0, el, pl.ds(off, win), :] += jnp.sum(jnp.where(m, grow, 0.0), axis=1, keepdims=True)
                    return carry

                lax.fori_loop(0, pl.cdiv(cap, win), body, 0)


def _gather_call(ws, nx, hm, pos_b, gate_b, cap):
    bsz, t, d = hm.shape
    nblk, n_e, tb = pos_b.shape[1:]
    eg = n_e // DISPATCH_GROUPS
    blocks = DISPATCH_STEP_WINDOWS * DISPATCH_MERGE
    grid_spec = pltpu.PrefetchScalarGridSpec(
        num_scalar_prefetch=2,
        grid=(bsz, DISPATCH_GROUPS, nblk // blocks),
        in_specs=[
            pl.BlockSpec((1, blocks * tb, d), lambda b, g, k, ws, nx: (b, k, 0)),
            pl.BlockSpec((1, blocks, eg, tb), lambda b, g, k, ws, nx: (b, k, g, 0)),
            pl.BlockSpec((1, blocks, eg, tb), lambda b, g, k, ws, nx: (b, k, g, 0)),
        ],
        out_specs=(pl.BlockSpec((1, eg, cap, d), lambda b, g, k, ws, nx: (b, g, 0, 0)),
                   pl.BlockSpec((1, eg, cap, 1), lambda b, g, k, ws, nx: (b, g, 0, 0))),
    )
    return pl.pallas_call(
        _gather_kernel,
        out_shape=(jax.ShapeDtypeStruct((bsz, n_e, cap, d), BF16),
                   jax.ShapeDtypeStruct((bsz, n_e, cap, 1), F32)),
        grid_spec=grid_spec,
        compiler_params=_cparams(("parallel", "parallel", "arbitrary")),
        name="gather",
    )(ws, nx, hm, pos_b, gate_b)


def _ffn_kernel(xs_ref, gc_ref, wg_ref, wu_ref, wd_ref, ye_ref, wgu_s, wd_s):
    f = wd_s.shape[0]
    wgu_s[:, 0:f] = wg_ref[0].astype(BF16)
    wgu_s[:, f:2 * f] = wu_ref[0].astype(BF16)
    wd_s[...] = wd_ref[0].astype(BF16)

    rt = FFN_ROWS
    for b in range(xs_ref.shape[0]):
        for j in range(xs_ref.shape[2] // rt):
            sl = slice(j * rt, (j + 1) * rt)
            xgu = jnp.dot(xs_ref[b, 0, sl, :], wgu_s[...], preferred_element_type=F32)
            hid = (jax.nn.silu(xgu[:, 0:f]) * xgu[:, f:2 * f]).astype(BF16)
            ye = jnp.dot(hid, wd_s[...], preferred_element_type=F32) * gc_ref[b, 0, sl, :]
            ye_ref[b, 0, sl, :] = ye.astype(BF16)


def _ffn_call(xs, gc, w_gate, w_up, w_down):
    bsz, n_e, cap, d = xs.shape
    f = w_gate.shape[2]
    return pl.pallas_call(
        _ffn_kernel,
        out_shape=jax.ShapeDtypeStruct((bsz, n_e, cap, d), BF16),
        grid=(n_e,),
        in_specs=[
            pl.BlockSpec((bsz, 1, cap, d), lambda e: (0, e, 0, 0)),
            pl.BlockSpec((bsz, 1, cap, 1), lambda e: (0, e, 0, 0)),
            pl.BlockSpec((1, d, f), lambda e: (e, 0, 0)),
            pl.BlockSpec((1, d, f), lambda e: (e, 0, 0)),
            pl.BlockSpec((1, f, d), lambda e: (e, 0, 0)),
        ],
        out_specs=pl.BlockSpec((bsz, 1, cap, d), lambda e: (0, e, 0, 0)),
        scratch_shapes=[pltpu.VMEM((d, 2 * f), BF16), pltpu.VMEM((f, d), BF16)],
        compiler_params=_cparams(("parallel",)),
        name="ffn",
    )(xs, gc, w_gate, w_up, w_down)


def _combine_kernel(ws_ref, nx_ref, ye_ref, pos_ref, x1_ref, mod_ref, lnw_ref, lnb_ref, out_ref, acc_scr):
    b = pl.program_id(0)
    step = pl.program_id(1)
    n_sub, n_e, tb = pos_ref.shape[1:]
    nblk = n_sub * pl.num_programs(1)
    cap = ye_ref.shape[2]
    d = D_MODEL
    win = WIN_COMBINE
    g2 = mod_ref[0, :, 5 * d:6 * d]

    for sub in range(n_sub):
        base = (b * nblk + step * n_sub + sub) * n_e
        starts = [pl.multiple_of(ws_ref[base + e], WIN_ALIGN) for e in range(n_e)]
        onehot = jnp.concatenate(
            [(_window_rows(starts[e], win, tb) == pos_ref[0, sub, e:e + 1, :]).astype(BF16) for e in range(n_e)],
            axis=0)
        rows = jnp.concatenate([ye_ref[0, e, pl.ds(starts[e], win), :] for e in range(n_e)], axis=0)
        acc_scr[...] = _dot_t0(onehot, rows)

        for e in range(n_e):
            @pl.when(nx_ref[base + e] > 0)
            def _(e=e, sub=sub, starts=starts):
                prow = pos_ref[0, sub, e:e + 1, :]
                rest = prow >= starts[e] + win

                def body(w, carry):
                    off, m = _sweep_window(w, cap, win, prow, rest)
                    acc_scr[...] += _dot_t0(m.astype(BF16), ye_ref[0, e, pl.ds(off, win), :])
                    return carry

                lax.fori_loop(0, pl.cdiv(cap, win), body, 0)

        tok = slice(sub * tb, (sub + 1) * tb)
        out_ref[0, tok, :] = (_ln(DEEPNORM_ALPHA * x1_ref[0, tok, :] + g2 * acc_scr[...]) * lnw_ref[...]
                              + lnb_ref[...])


def _combine_call(ws, nx, ye, pos_b, x1, mod3, ln2w, ln2b):
    bsz, n_e, cap, d = ye.shape
    nblk, _, tb = pos_b.shape[1:]
    n_sub = COMBINE_STEP_BLOCKS
    grid_spec = pltpu.PrefetchScalarGridSpec(
        num_scalar_prefetch=2,
        grid=(bsz, nblk // n_sub),
        in_specs=[
            pl.BlockSpec((1, n_e, cap, d), lambda b, k, ws, nx: (b, 0, 0, 0), pipeline_mode=pl.Buffered(1)),
            pl.BlockSpec((1, n_sub, n_e, tb), lambda b, k, ws, nx: (b, k, 0, 0)),
            pl.BlockSpec((1, n_sub * tb, d), lambda b, k, ws, nx: (b, k, 0)),
            pl.BlockSpec((1, 1, 6 * d), lambda b, k, ws, nx: (b, 0, 0)),
            pl.BlockSpec((1, d), lambda b, k, ws, nx: (0, 0)),
            pl.BlockSpec((1, d), lambda b, k, ws, nx: (0, 0)),
        ],
        out_specs=pl.BlockSpec((1, n_sub * tb, d), lambda b, k, ws, nx: (b, k, 0)),
        scratch_shapes=[pltpu.VMEM((tb, d), F32)],
    )
    return pl.pallas_call(
        _combine_kernel,
        out_shape=jax.ShapeDtypeStruct((bsz, nblk * tb, d), F32),
        grid_spec=grid_spec,
        compiler_params=_cparams(("parallel", "arbitrary")),
        name="combine",
    )(ws, nx, ye, pos_b, x1, mod3, ln2w, ln2b)


def _rope_tables(n_tokens):
    rows = n_tokens // GRID_W
    n_freq = RET_QK_DIM // 4
    inv = ROPE_THETA ** (-jnp.arange(n_freq, dtype=F32) / n_freq)
    ang_r = jnp.arange(rows, dtype=F32)[:, None] * inv
    ang_c = jnp.arange(GRID_W, dtype=F32)[:, None] * inv
    zr = jnp.zeros((rows, 2 * n_freq), F32)
    zc = jnp.zeros((GRID_W, 2 * n_freq), F32)
    cr, sr, cc, sc = jnp.cos(ang_r), jnp.sin(ang_r), jnp.cos(ang_c), jnp.sin(ang_c)
    cos_t = (jnp.concatenate([cr, cr, zr], axis=-1)[:, None, :]
             + jnp.concatenate([zc, cc, cc], axis=-1)[None, :, :]).reshape(n_tokens, RET_QK_DIM)
    sin_t = (jnp.concatenate([-sr, sr, zr], axis=-1)[:, None, :]
             + jnp.concatenate([zc, -sc, sc], axis=-1)[None, :, :]).reshape(n_tokens, RET_QK_DIM)
    return jnp.concatenate([cos_t, sin_t], axis=-1)


def kernel(x, c, ctx, c_ctx, w_ada, b_ada, w_in, b_in, ret_decay_f, ret_decay_b, ret_gn_w, ret_gn_b,
           sgu_ln_w, sgu_ln_b, sgu_w, sgu_b, w_pa, w_pb, w_o, b_o, ln1_w, ln1_b,
           w_router, b_router, w_gate, w_up, w_down, ln2_w, ln2_b):
    bsz, t, d = x.shape
    assert d == D_MODEL and w_ada.shape[0] == DEPTH == 1 and bsz < MOD_ROWS
    cap = EC_CAPACITY_FACTOR * t // N_EXPERTS
    for win in (WIN_DISPATCH, WIN_COMBINE):
        assert cap >= win and (cap - win) % WIN_ALIGN == 0 and win % WIN_ALIGN == 0
    assert t % (TB_ROUTE * DISPATCH_MERGE * DISPATCH_STEP_WINDOWS) == 0 and t % (TB_ROUTE * COMBINE_STEP_BLOCKS) == 0
    assert N_EXPERTS % DISPATCH_GROUPS == 0
    l = 0

    cvec = jnp.concatenate([c, c_ctx[None], jnp.zeros((MOD_ROWS - bsz - 1, d), F32)], axis=0)
    mod = _mod_call(cvec.T, w_ada[l], b_ada[l][None], bsz + 1)
    mod3 = mod[:, None, :]

    w_in_b = w_in[l].astype(BF16)
    b_in2 = b_in[l][None]
    kctx, vctx = _ctx_kv_call(ctx, mod3, w_in_b, b_in2, bsz)

    z = _inproj_call(x, mod3, w_in_b, b_in2, _rope_tables(t), sgu_ln_w[l][None], sgu_ln_b[l][None])

    lg = jnp.stack([jax.nn.log_sigmoid(ret_decay_f[l].astype(F32)),
                    jax.nn.log_sigmoid(ret_decay_b[l].astype(F32))])
    states = _state_call(lg, z, kctx, vctx)

    x1, hm, logits_t = _mixer_call(
        lg, z, states, x, mod3,
        w_pa[l].astype(BF16), w_pb[l].astype(BF16), w_o[l].astype(BF16),
        sgu_w[l].astype(BF16), sgu_b[l].T, ret_gn_w[l][None], ret_gn_b[l][None], b_o[l][None],
        ln1_w[l][None], ln1_b[l][None], w_router[l].T.astype(BF16), b_router[l][:, None])

    pos_b, gate_b, cnt_b = _route_call(logits_t, cap)

    c0 = cnt_b[..., 0]
    c1 = jnp.concatenate([c0[:, 1:], jnp.full_like(c0[:, :1], cap)], axis=1)

    def windows(start, end, win):
        ws = jnp.minimum((start // WIN_ALIGN) * WIN_ALIGN, cap - win)
        return ws.reshape(-1), (end > ws + win).astype(I32).reshape(-1)

    ws_d, nx_d = windows(c0[:, ::DISPATCH_MERGE], c1[:, DISPATCH_MERGE - 1::DISPATCH_MERGE], WIN_DISPATCH)
    ws_c, nx_c = windows(c0, c1, WIN_COMBINE)

    xs, gc = _gather_call(ws_d, nx_d, hm, pos_b, gate_b, cap)
    ye = _ffn_call(xs, gc, w_gate[l], w_up[l], w_down[l])
    return _combine_call(ws_c, nx_c, ye, pos_b, x1, mod3, ln2_w[l][None], ln2_b[l][None])
```

```python
import functools
import math

import jax
import jax.numpy as jnp
from jax import lax
from jax.experimental import pallas as pl
from jax.experimental.pallas import tpu as pltpu

F32 = jnp.float32
BF16 = jnp.bfloat16
I32 = jnp.int32

D_MODEL = 1024
DEPTH = 1
GRID_W = 64
RET_HEADS = 4
RET_QK_DIM = 128
RET_V_DIM = 256
RET_CHUNK = 128
ROPE_THETA = 10000.0
RET_QK_W = RET_HEADS * RET_QK_DIM
RET_V_W = RET_HEADS * RET_V_DIM
SGU_GROUPS = 4
SGU_CHUNK = 128
SGU_WIDTH = 1024
SGU_GW = SGU_WIDTH // SGU_GROUPS
N_EXPERTS = 16
EC_CAPACITY_FACTOR = 2
LN_EPS = 1e-6
DEEPNORM_ALPHA = (2.0 * DEPTH) ** 0.25
D_IN = 2 * RET_QK_W + 2 * RET_V_W + 2 * SGU_WIDTH + 2 * D_MODEL
Z_K = RET_QK_W
Z_V = 2 * RET_QK_W
Z_SG = Z_V + RET_V_W
Z_U = Z_SG + RET_V_W
Z_VS = Z_U + SGU_WIDTH
Z_GA = Z_VS + SGU_WIDTH
Z_GB = Z_GA + D_MODEL
Z_BLOCK = 3 * 1024

V7X_VMEM_BYTES = 64 * 1024 * 1024
VMEM_LIMIT = V7X_VMEM_BYTES - 4 * 1024 * 1024
LANES = 128

MOD_ROWS = 8
MOD_COLS = 2048
TM_INPROJ = 1024
INPROJ_CHUNK = 256
TT_MIXER = 512
MERGE_ROWS = 512
TB_ROUTE = 256
DISPATCH_MERGE = 2
WIN_DISPATCH = 128
WIN_COMBINE = 64
WIN_ALIGN = 16
DISPATCH_GROUPS = 2
DISPATCH_STEP_WINDOWS = 4
COMBINE_STEP_BLOCKS = 4
STATE_UNROLL = 16
FFN_ROWS = 256


def _cparams(sem):
    return pltpu.CompilerParams(dimension_semantics=sem, vmem_limit_bytes=VMEM_LIMIT)


def _ln(x):
    mu = jnp.mean(x, axis=-1, keepdims=True)
    xc = x - mu
    var = jnp.mean(xc * xc, axis=-1, keepdims=True)
    return xc * lax.rsqrt(var + LN_EPS)


_GELU_A = -2.0 * math.sqrt(2.0 / math.pi) * math.log2(math.e)
_GELU_B = _GELU_A * 0.044715


def _gelu(x):
    return x / (1.0 + jnp.exp2(x * (_GELU_A + _GELU_B * (x * x))))


def _iota_f(shape, dim):
    return lax.broadcasted_iota(I32, shape, dim).astype(F32)


def _mod_kernel(ct_ref, w_ref, b_ref, o_ref, *, n_rows):
    c = ct_ref[...]
    a = c * jax.nn.sigmoid(c)
    w = w_ref[...]
    rows = [jnp.sum(w * a[:, m:m + 1], axis=0, keepdims=True) + b_ref[...] for m in range(n_rows)]
    rows += [jnp.zeros_like(rows[0])] * (MOD_ROWS - n_rows)
    o_ref[...] = jnp.concatenate(rows, axis=0)


def _mod_call(cvec_t, w_ada, b_ada, n_rows):
    d, n = w_ada.shape
    tn = MOD_COLS
    return pl.pallas_call(
        functools.partial(_mod_kernel, n_rows=n_rows),
        out_shape=jax.ShapeDtypeStruct((MOD_ROWS, n), F32),
        grid=(n // tn,),
        in_specs=[
            pl.BlockSpec((d, MOD_ROWS), lambda j: (0, 0)),
            pl.BlockSpec((d, tn), lambda j: (0, j)),
            pl.BlockSpec((1, tn), lambda j: (0, j)),
        ],
        out_specs=pl.BlockSpec((MOD_ROWS, tn), lambda j: (0, j)),
        compiler_params=_cparams(("parallel",)),
        name="mod",
    )(cvec_t, w_ada, b_ada)


def _ctx_kv_kernel(x_ref, mod_ref, wk_ref, wv_ref, bk_ref, bv_ref, k_ref, v_ref):
    d = D_MODEL
    sh = mod_ref[0, :, 0:d]
    sc = mod_ref[0, :, d:2 * d]
    h = (_ln(x_ref[0]) * (1.0 + sc) + sh).astype(BF16)
    k = jnp.dot(h, wk_ref[...], preferred_element_type=F32) + bk_ref[...]
    k_ref[0] = (k * (RET_QK_DIM ** -0.5)).astype(BF16)
    v_ref[0] = (jnp.dot(h, wv_ref[...], preferred_element_type=F32) + bv_ref[...]).astype(BF16)


def _ctx_kv_call(ctx, mod3, w_in, b_in, ctx_row):
    bsz, n_ctx, d = ctx.shape
    kblk = Z_K // RET_QK_W
    vblk = Z_V // RET_V_W
    return pl.pallas_call(
        _ctx_kv_kernel,
        out_shape=(jax.ShapeDtypeStruct((bsz, n_ctx, RET_QK_W), BF16),
                   jax.ShapeDtypeStruct((bsz, n_ctx, RET_V_W), BF16)),
        grid=(bsz,),
        in_specs=[
            pl.BlockSpec((1, n_ctx, d), lambda b: (b, 0, 0)),
            pl.BlockSpec((1, 1, 6 * d), lambda b: (ctx_row, 0, 0)),
            pl.BlockSpec((d, RET_QK_W), lambda b: (0, kblk)),
            pl.BlockSpec((d, RET_V_W), lambda b: (0, vblk)),
            pl.BlockSpec((1, RET_QK_W), lambda b: (0, kblk)),
            pl.BlockSpec((1, RET_V_W), lambda b: (0, vblk)),
        ],
        out_specs=(pl.BlockSpec((1, n_ctx, RET_QK_W), lambda b: (b, 0, 0)),
                   pl.BlockSpec((1, n_ctx, RET_V_W), lambda b: (b, 0, 0))),
        compiler_params=_cparams(("parallel",)),
        name="ctx_kv",
    )(ctx, mod3, w_in, w_in, b_in, b_in)


def _inproj_kernel(x_ref, mod_ref, w_ref, b_ref, rope_ref, lnw_ref, lnb_ref, z_ref):
    d = D_MODEL
    sh = mod_ref[0, :, 0:d]
    sc = mod_ref[0, :, d:2 * d]
    h = (_ln(x_ref[0]) * (1.0 + sc) + sh).astype(BF16)

    def proj(c0, width):
        return jnp.dot(h, w_ref[:, c0:c0 + width], preferred_element_type=F32) + b_ref[:, c0:c0 + width]

    zq = proj(0, 2 * RET_QK_W)
    cos = rope_ref[:, 0:LANES]
    sin = rope_ref[:, LANES:2 * LANES]
    tm = zq.shape[0]
    upper = (lax.broadcasted_iota(I32, (tm, LANES), 1) & 32) != 0
    for hb in range(2 * RET_HEADS):
        zs = zq[:, hb * LANES:(hb + 1) * LANES]
        if hb >= RET_HEADS:
            zs = zs * (RET_QK_DIM ** -0.5)
        sw = jnp.where(upper, pltpu.roll(zs, 32, 1), pltpu.roll(zs, LANES - 32, 1))
        z_ref[0, :, hb * LANES:(hb + 1) * LANES] = (zs * cos + sw * sin).astype(BF16)

    def emit(c0, width, fn):
        for cc in range(c0, c0 + width, INPROJ_CHUNK):
            z_ref[0, :, cc:cc + INPROJ_CHUNK] = fn(proj(cc, INPROJ_CHUNK)).astype(BF16)

    vs = _ln(_gelu(proj(Z_VS, SGU_WIDTH))) * lnw_ref[...] + lnb_ref[...]
    z_ref[0, :, Z_VS:Z_VS + SGU_WIDTH] = vs.astype(BF16)
    emit(Z_U, SGU_WIDTH, _gelu)
    emit(Z_SG, RET_V_W, jax.nn.silu)
    emit(Z_GA, d, jax.nn.sigmoid)
    emit(Z_GB, d, jax.nn.sigmoid)
    emit(Z_V, RET_V_W, lambda z: z)


def _inproj_call(x, mod3, w_in, b_in, rope_t, sgu_ln_w, sgu_ln_b):
    bsz, t, d = x.shape
    tm = TM_INPROJ
    row = lambda b, i: (b, i, 0)
    const2 = lambda b, i: (0, 0)
    return pl.pallas_call(
        _inproj_kernel,
        out_shape=jax.ShapeDtypeStruct((bsz, t, D_IN), BF16),
        grid=(bsz, t // tm),
        in_specs=[
            pl.BlockSpec((1, tm, d), row),
            pl.BlockSpec((1, 1, 6 * d), lambda b, i: (b, 0, 0)),
            pl.BlockSpec((d, D_IN), const2, pipeline_mode=pl.Buffered(1)),
            pl.BlockSpec((1, D_IN), const2),
            pl.BlockSpec((tm, 2 * LANES), lambda b, i: (i, 0)),
            pl.BlockSpec((1, SGU_WIDTH), const2),
            pl.BlockSpec((1, SGU_WIDTH), const2),
        ],
        out_specs=pl.BlockSpec((1, tm, D_IN), row),
        compiler_params=_cparams(("parallel", "parallel")),
        name="inproj",
    )(x, mod3, w_in, b_in, rope_t, sgu_ln_w, sgu_ln_b)


def _dot_t0(a, b):
    return lax.dot_general(a, b, (((0,), (0,)), ((), ())), preferred_element_type=F32)


def _state_kernel(lg_ref, k_ref, v_ref, kc_ref, vc_ref, s_ref):
    hd = pl.program_id(1)
    lgf = lg_ref[0, hd]
    lgb = lg_ref[1, hd]
    n_chunks = s_ref.shape[2]
    n_ctx = kc_ref.shape[1]
    c = RET_CHUNK
    li = _iota_f((c, RET_QK_DIM), 0)
    zeta_f = jnp.exp((c - 1.0 - li) * lgf)
    zeta_b = jnp.exp(li * lgb)
    one = jnp.ones((1, 1), F32)
    cd_f = jnp.exp(one * (c * lgf))
    cd_b = jnp.exp(one * (c * lgb))

    tc = _iota_f((n_ctx, RET_QK_DIM), 0)
    kc = kc_ref[0].astype(F32)
    vc = vc_ref[0]
    s0f = _dot_t0((kc * jnp.exp((n_ctx - 1.0 - tc) * lgf)).astype(BF16), vc)
    s0b = _dot_t0((kc * jnp.exp(tc * lgb)).astype(BF16), vc)

    def chunk_kv(n, zeta):
        off = pl.multiple_of(n * c, c)
        kk = k_ref[0, pl.ds(off, c), :].astype(F32)
        return _dot_t0((kk * zeta).astype(BF16), v_ref[0, pl.ds(off, c), :])

    def step(i, carry):
        sf, sb = carry
        nb = n_chunks - 1 - i
        s_ref[0, 0, i, 0:RET_QK_DIM, :] = sf.astype(BF16)
        s_ref[0, 0, nb, RET_QK_DIM:2 * RET_QK_DIM, :] = sb.astype(BF16)
        return cd_f * sf + chunk_kv(i, zeta_f), cd_b * sb + chunk_kv(nb, zeta_b)

    lax.fori_loop(0, n_chunks, step, (s0f, s0b), unroll=STATE_UNROLL)


def _state_call(lg, z, kctx, vctx):
    bsz, t, _ = z.shape
    n_ctx = kctx.shape[1]
    n_chunks = t // RET_CHUNK
    return pl.pallas_call(
        _state_kernel,
        out_shape=jax.ShapeDtypeStruct((bsz, RET_HEADS, n_chunks, 2 * RET_QK_DIM, RET_V_DIM), BF16),
        grid=(bsz, RET_HEADS),
        in_specs=[
            pl.BlockSpec(memory_space=pltpu.SMEM),
            pl.BlockSpec((1, t, RET_QK_DIM), lambda b, h: (b, 0, RET_QK_W // RET_QK_DIM + h)),
            pl.BlockSpec((1, t, RET_V_DIM), lambda b, h: (b, 0, Z_V // RET_V_DIM + h)),
            pl.BlockSpec((1, n_ctx, RET_QK_DIM), lambda b, h: (b, 0, h)),
            pl.BlockSpec((1, n_ctx, RET_V_DIM), lambda b, h: (b, 0, h)),
        ],
        out_specs=pl.BlockSpec((1, 1, n_chunks, 2 * RET_QK_DIM, RET_V_DIM), lambda b, h: (b, h, 0, 0, 0)),
        compiler_params=_cparams(("parallel", "parallel")),
        name="states",
    )(lg, z, z, kctx, vctx)


def _mixer_kernel(lg_ref, za_ref, zb_ref, gb_ref, s_ref, x_ref, mod_ref,
                  wpa_ref, wpb_ref, wo_ref, sguw_ref, sgub_ref, gnw_ref, gnb_ref, bo_ref,
                  ln1w_ref, ln1b_ref, wr_ref, br_ref,
                  x1_ref, hm_ref, lgt_ref, ret_scr, sgu_scr):
    d = D_MODEL
    c = RET_CHUNK

    n_sub = za_ref.shape[1] // c
    row = lax.broadcasted_iota(I32, (c, c), 0)
    col = lax.broadcasted_iota(I32, (c, c), 1)
    diff = (row - col).astype(F32)
    rowq = _iota_f((c, RET_QK_DIM), 0)
    for hd in range(RET_HEADS):
        lgf = lg_ref[0, hd]
        lgb = lg_ref[1, hd]
        vcols = slice(hd * RET_V_DIM, (hd + 1) * RET_V_DIM)
        mask = (jnp.where(diff >= 0, jnp.exp(jnp.maximum(diff, 0.0) * lgf), 0.0)
                + jnp.where(diff <= 0, jnp.exp(jnp.maximum(-diff, 0.0) * lgb), 0.0))
        xi_f = jnp.exp((rowq + 1.0) * lgf)
        xi_b = jnp.exp((c - rowq) * lgb)
        for ci in range(n_sub):
            r0 = ci * c
            q = za_ref[0, r0:r0 + c, hd * RET_QK_DIM:(hd + 1) * RET_QK_DIM]
            k = za_ref[0, r0:r0 + c, RET_QK_W + hd * RET_QK_DIM:RET_QK_W + (hd + 1) * RET_QK_DIM]
            s = lax.dot_general(q, k, (((1,), (1,)), ((), ())), preferred_element_type=F32)
            vv = za_ref[0, r0:r0 + c, Z_V + hd * RET_V_DIM:Z_V + (hd + 1) * RET_V_DIM]
            intra = jnp.dot((s * mask).astype(BF16), vv, preferred_element_type=F32)
            qf = q.astype(F32)
            qx = jnp.concatenate([(qf * xi_f).astype(BF16), (qf * xi_b).astype(BF16)], axis=1)
            cross = jnp.dot(qx, s_ref[0, hd, ci], preferred_element_type=F32)
            o = _ln(intra + cross) * gnw_ref[:, vcols] + gnb_ref[:, vcols]
            gate = za_ref[0, r0:r0 + c, Z_SG + hd * RET_V_DIM:Z_SG + (hd + 1) * RET_V_DIM]
            ret_scr[r0:r0 + c, vcols] = (o * gate.astype(F32)).astype(BF16)

    for g in range(SGU_GROUPS):
        gcols = slice(g * SGU_GW, (g + 1) * SGU_GW)
        wg = sguw_ref[g]
        bg = sgub_ref[:, g:g + 1]
        for ci in range(n_sub):
            r0 = ci * c
            vsb = zb_ref[0, r0:r0 + c, Z_VS - Z_BLOCK + g * SGU_GW:Z_VS - Z_BLOCK + (g + 1) * SGU_GW]
            ub = zb_ref[0, r0:r0 + c, Z_U - Z_BLOCK + g * SGU_GW:Z_U - Z_BLOCK + (g + 1) * SGU_GW]
            sp = jnp.dot(wg, vsb, preferred_element_type=F32) + bg
            sgu_scr[r0:r0 + c, gcols] = (ub.astype(F32) * sp).astype(BF16)

    g1 = mod_ref[0, :, 2 * d:3 * d]
    sh2 = mod_ref[0, :, 3 * d:4 * d]
    sc2 = mod_ref[0, :, 4 * d:5 * d]
    for r0 in range(0, za_ref.shape[1], MERGE_ROWS):
        rows = slice(r0, r0 + MERGE_ROWS)
        pa = jnp.dot(ret_scr[rows, :], wpa_ref[...], preferred_element_type=F32)
        pb = jnp.dot(sgu_scr[rows, :], wpb_ref[...], preferred_element_type=F32)
        ga = zb_ref[0, rows, Z_GA - Z_BLOCK:Z_GA - Z_BLOCK + d]
        y = (ga.astype(F32) * pa + gb_ref[0, rows, :].astype(F32) * pb).astype(BF16)
        mix = jnp.dot(y, wo_ref[...], preferred_element_type=F32) + bo_ref[...]
        x1 = _ln(DEEPNORM_ALPHA * x_ref[0, rows, :] + g1 * mix) * ln1w_ref[...] + ln1b_ref[...]
        x1_ref[0, rows, :] = x1
        hm = (_ln(x1) * (1.0 + sc2) + sh2).astype(BF16)
        hm_ref[0, rows, :] = hm
        lgt_ref[0, :, rows] = lax.dot_general(wr_ref[...], hm, (((1,), (1,)), ((), ())),
                                              preferred_element_type=F32) + br_ref[...]


def _mixer_call(lg, z, states, x, mod3, wpa, wpb, wo, sguw, sgub_t, gnw, gnb, bo,
                ln1w, ln1b, wr_t, br):
    bsz, t, d = x.shape
    tt = TT_MIXER
    n_sub = tt // RET_CHUNK
    row = lambda b, i: (b, i, 0)
    c2 = lambda b, i: (0, 0)
    c3 = lambda b, i: (0, 0, 0)
    return pl.pallas_call(
        _mixer_kernel,
        out_shape=(jax.ShapeDtypeStruct((bsz, t, d), F32),
                   jax.ShapeDtypeStruct((bsz, t, d), BF16),
                   jax.ShapeDtypeStruct((bsz, N_EXPERTS, t), F32)),
        grid=(bsz, t // tt),
        in_specs=[
            pl.BlockSpec(memory_space=pltpu.SMEM),
            pl.BlockSpec((1, tt, Z_BLOCK), lambda b, i: (b, i, 0)),
            pl.BlockSpec((1, tt, Z_BLOCK), lambda b, i: (b, i, 1)),
            pl.BlockSpec((1, tt, d), lambda b, i: (b, i, Z_GB // d)),
            pl.BlockSpec((1, RET_HEADS, n_sub, 2 * RET_QK_DIM, RET_V_DIM), lambda b, i: (b, 0, i, 0, 0)),
            pl.BlockSpec((1, tt, d), row),
            pl.BlockSpec((1, 1, 6 * d), lambda b, i: (b, 0, 0)),
            pl.BlockSpec((RET_V_W, d), c2, pipeline_mode=pl.Buffered(1)),
            pl.BlockSpec((SGU_WIDTH, d), c2, pipeline_mode=pl.Buffered(1)),
            pl.BlockSpec((d, d), c2, pipeline_mode=pl.Buffered(1)),
            pl.BlockSpec((SGU_GROUPS, SGU_CHUNK, SGU_CHUNK), c3),
            pl.BlockSpec((SGU_CHUNK, SGU_GROUPS), c2),
            pl.BlockSpec((1, RET_V_W), c2),
            pl.BlockSpec((1, RET_V_W), c2),
            pl.BlockSpec((1, d), c2),
            pl.BlockSpec((1, d), c2),
            pl.BlockSpec((1, d), c2),
            pl.BlockSpec((N_EXPERTS, d), c2),
            pl.BlockSpec((N_EXPERTS, 1), c2),
        ],
        out_specs=(pl.BlockSpec((1, tt, d), row),
                   pl.BlockSpec((1, tt, d), row),
                   pl.BlockSpec((1, N_EXPERTS, tt), lambda b, i: (b, 0, i))),
        scratch_shapes=[pltpu.VMEM((tt, RET_V_W), BF16), pltpu.VMEM((tt, SGU_WIDTH), BF16)],
        compiler_params=_cparams(("parallel", "parallel")),
        name="mixer",
    )(lg, z, z, z, states, x, mod3, wpa, wpb, wo, sguw, sgub_t, gnw, gnb, bo,
      ln1w, ln1b, wr_t, br)


def _route_kernel(lgt_ref, pos_ref, gate_ref, cnt_ref, *, cap):
    lg = lgt_ref[0]
    n_e, t = lg.shape
    tb = TB_ROUTE
    m = jnp.max(lg, axis=0, keepdims=True)
    ex = jnp.exp(lg - m)
    aff = ex / jnp.sum(ex, axis=0, keepdims=True)

    def search(i, thr_bits):
        cand = thr_bits | lax.shift_left(jnp.int32(1), 30 - i)
        cnt = jnp.sum((aff >= lax.bitcast_convert_type(cand, F32)).astype(I32), axis=1, keepdims=True)
        return jnp.where(cnt >= cap, cand, thr_bits)

    thr_bits = lax.fori_loop(0, 31, search, jnp.zeros((n_e, 1), I32))
    floor_f = lax.bitcast_convert_type(thr_bits, F32)
    thr = jnp.min(jnp.where(aff >= floor_f, aff, jnp.inf), axis=1, keepdims=True)
    need = (cap - jnp.sum((aff > thr).astype(I32), axis=1, keepdims=True)).astype(F32)

    r = lax.broadcasted_iota(I32, (tb, tb), 0)
    cc = lax.broadcasted_iota(I32, (tb, tb), 1)
    tri = (r <= cc).astype(BF16)
    carry_eq = jnp.zeros((n_e, 1), F32)
    carry_sel = jnp.zeros((n_e, 1), F32)
    for blk in range(t // tb):
        sl = slice(blk * tb, (blk + 1) * tb)
        aff_b = aff[:, sl]
        eq = aff_b == thr
        eq_b = eq.astype(BF16)
        inc_eq = jnp.dot(eq_b, tri, preferred_element_type=F32)
        before = carry_eq + inc_eq - eq_b.astype(F32)
        sel = (aff_b > thr) | (eq & (before < need))
        sel_b = sel.astype(BF16)
        inc_sel = jnp.dot(sel_b, tri, preferred_element_type=F32)
        pos = carry_sel + inc_sel - 1.0
        pos_ref[0, blk] = jnp.where(sel, pos.astype(I32), -1)
        gate_ref[0, blk] = aff_b
        cnt_ref[0, blk] = jnp.broadcast_to(carry_sel, (n_e, LANES)).astype(I32)
        carry_eq = carry_eq + inc_eq[:, tb - 1:tb]
        carry_sel = carry_sel + inc_sel[:, tb - 1:tb]


def _route_call(logits_t, cap):
    bsz, n_e, t = logits_t.shape
    nblk = t // TB_ROUTE
    return pl.pallas_call(
        functools.partial(_route_kernel, cap=cap),
        out_shape=(jax.ShapeDtypeStruct((bsz, nblk, n_e, TB_ROUTE), I32),
                   jax.ShapeDtypeStruct((bsz, nblk, n_e, TB_ROUTE), F32),
                   jax.ShapeDtypeStruct((bsz, nblk, n_e, LANES), I32)),
        grid=(bsz,),
        in_specs=[pl.BlockSpec((1, n_e, t), lambda b: (b, 0, 0))],
        out_specs=(pl.BlockSpec((1, nblk, n_e, TB_ROUTE), lambda b: (b, 0, 0, 0)),
                   pl.BlockSpec((1, nblk, n_e, TB_ROUTE), lambda b: (b, 0, 0, 0)),
                   pl.BlockSpec((1, nblk, n_e, LANES), lambda b: (b, 0, 0, 0))),
        compiler_params=_cparams(("parallel",)),
        name="route",
    )(logits_t)


def _window_rows(ws, win, n_tok):
    return lax.broadcasted_iota(I32, (win, n_tok), 0) + ws


def _sweep_window(w, cap, win, prow, rest):
    off = pl.multiple_of(jnp.minimum(w * win, cap - win), WIN_ALIGN)
    rows = _window_rows(off, win, prow.shape[-1])
    return off, (rows == prow) & rest & (rows >= w * win)


def _gather_kernel(ws_ref, nx_ref, any_ref, hm_ref, pos_ref, gate_ref, xs_ref, gc_ref):
    b = pl.program_id(0)
    g = pl.program_id(1)
    step = pl.program_id(2)
    merge = DISPATCH_MERGE
    eg, tb = pos_ref.shape[2:]
    n_sub = pos_ref.shape[1] // merge
    n_e = eg * pl.num_programs(1)
    n_win = n_sub * pl.num_programs(2)
    cap = xs_ref.shape[2]
    win = WIN_DISPATCH

    @pl.when(step == 0)
    def _():
        xs_ref[...] = jnp.zeros_like(xs_ref)
        gc_ref[...] = jnp.zeros_like(gc_ref)

    def base(sub):
        return (b * n_win + step * n_sub + sub) * n_e + g * eg

    def tokens(sub):
        return hm_ref[0, sub * merge * tb:(sub + 1) * merge * tb, :]

    def lanes(ref, sub, el):
        return jnp.concatenate([ref[0, sub * merge + m, el:el + 1, :] for m in range(merge)], axis=-1)

    for sub in range(n_sub):
        for el in range(eg):
            ws = pl.multiple_of(ws_ref[base(sub) + el], WIN_ALIGN)
            match = _window_rows(ws, win, merge * tb) == lanes(pos_ref, sub, el)
            sl = pl.ds(ws, win)
            xs_ref[0, el, sl, :] += jnp.dot(match.astype(BF16), tokens(sub),
                                            preferred_element_type=F32).astype(BF16)
            gc_ref[0, el, sl, :] += jnp.sum(jnp.where(match, lanes(gate_ref, sub, el), 0.0),
                                             axis=1, keepdims=True)

    for sub in range(n_sub):
        @pl.when(any_ref[(b * n_win + step * n_sub + sub) * pl.num_programs(1) + g] > 0)
        def _(sub=sub):
            for el in range(eg):
                @pl.when(nx_ref[base(sub) + el] > 0)
                def _(el=el):
                    prow = lanes(pos_ref, sub, el)
                    grow = lanes(gate_ref, sub, el)
                    rest = prow >= ws_ref[base(sub) + el] + win

                    def body(w, carry):
                        off, m = _sweep_window(w, cap, win, prow, rest)
                        xs_ref[0, el, pl.ds(off, win), :] += jnp.dot(m.astype(BF16), tokens(sub),
                                                                      preferred_element_type=F32).astype(BF16)
                        gc_ref[0, el, pl.ds(off, win), :] += jnp.sum(jnp.where(m, grow, 0.0), axis=1,
                                                                     keepdims=True)
                        return carry

                    lax.fori_loop(0, pl.cdiv(cap, win), body, 0)


def _gather_call(ws, nx, nx_any, hm, pos_b, gate_b, cap):
    bsz, t, d = hm.shape
    nblk, n_e, tb = pos_b.shape[1:]
    eg = n_e // DISPATCH_GROUPS
    blocks = DISPATCH_STEP_WINDOWS * DISPATCH_MERGE
    grid_spec = pltpu.PrefetchScalarGridSpec(
        num_scalar_prefetch=3,
        grid=(bsz, DISPATCH_GROUPS, nblk // blocks),
        in_specs=[
            pl.BlockSpec((1, blocks * tb, d), lambda b, g, k, ws, nx, fl: (b, k, 0)),
            pl.BlockSpec((1, blocks, eg, tb), lambda b, g, k, ws, nx, fl: (b, k, g, 0)),
            pl.BlockSpec((1, blocks, eg, tb), lambda b, g, k, ws, nx, fl: (b, k, g, 0)),
        ],
        out_specs=(pl.BlockSpec((1, eg, cap, d), lambda b, g, k, ws, nx, fl: (b, g, 0, 0)),
                   pl.BlockSpec((1, eg, cap, 1), lambda b, g, k, ws, nx, fl: (b, g, 0, 0))),
    )
    return pl.pallas_call(
        _gather_kernel,
        out_shape=(jax.ShapeDtypeStruct((bsz, n_e, cap, d), BF16),
                   jax.ShapeDtypeStruct((bsz, n_e, cap, 1), F32)),
        grid_spec=grid_spec,
        compiler_params=_cparams(("parallel", "parallel", "arbitrary")),
        name="gather",
    )(ws, nx, nx_any, hm, pos_b, gate_b)


def _ffn_kernel(xs_ref, gc_ref, wg_ref, wu_ref, wd_ref, ye_ref, wgu_s, wd_s):
    f = wd_s.shape[0]
    wgu_s[:, 0:f] = wg_ref[0].astype(BF16)
    wgu_s[:, f:2 * f] = wu_ref[0].astype(BF16)
    wd_s[...] = wd_ref[0].astype(BF16)

    rt = FFN_ROWS
    for b in range(xs_ref.shape[0]):
        for j in range(xs_ref.shape[2] // rt):
            sl = slice(j * rt, (j + 1) * rt)
            xgu = jnp.dot(xs_ref[b, 0, sl, :], wgu_s[...], preferred_element_type=F32)
            hid = (jax.nn.silu(xgu[:, 0:f]) * xgu[:, f:2 * f]).astype(BF16)
            ye = jnp.dot(hid, wd_s[...], preferred_element_type=F32) * gc_ref[b, 0, sl, :]
            ye_ref[b, 0, sl, :] = ye.astype(BF16)


def _ffn_call(xs, gc, w_gate, w_up, w_down):
    bsz, n_e, cap, d = xs.shape
    f = w_gate.shape[2]
    return pl.pallas_call(
        _ffn_kernel,
        out_shape=jax.ShapeDtypeStruct((bsz, n_e, cap, d), BF16),
        grid=(n_e,),
        in_specs=[
            pl.BlockSpec((bsz, 1, cap, d), lambda e: (0, e, 0, 0)),
            pl.BlockSpec((bsz, 1, cap, 1), lambda e: (0, e, 0, 0)),
            pl.BlockSpec((1, d, f), lambda e: (e, 0, 0)),
            pl.BlockSpec((1, d, f), lambda e: (e, 0, 0)),
            pl.BlockSpec((1, f, d), lambda e: (e, 0, 0)),
        ],
        out_specs=pl.BlockSpec((bsz, 1, cap, d), lambda e: (0, e, 0, 0)),
        scratch_shapes=[pltpu.VMEM((d, 2 * f), BF16), pltpu.VMEM((f, d), BF16)],
        compiler_params=_cparams(("parallel",)),
        name="ffn",
    )(xs, gc, w_gate, w_up, w_down)


def _combine_kernel(ws_ref, nx_ref, any_ref, ye_ref, pos_ref, x1_ref, mod_ref, lnw_ref, lnb_ref, out_ref, acc_scr):
    b = pl.program_id(0)
    step = pl.program_id(1)
    n_sub, n_e, tb = pos_ref.shape[1:]
    nblk = n_sub * pl.num_programs(1)
    cap = ye_ref.shape[2]
    d = D_MODEL
    win = WIN_COMBINE
    g2 = mod_ref[0, :, 5 * d:6 * d]

    for sub in range(n_sub):
        base = (b * nblk + step * n_sub + sub) * n_e
        starts = [pl.multiple_of(ws_ref[base + e], WIN_ALIGN) for e in range(n_e)]
        onehot = jnp.concatenate(
            [(_window_rows(starts[e], win, tb) == pos_ref[0, sub, e:e + 1, :]).astype(BF16) for e in range(n_e)],
            axis=0)
        rows = jnp.concatenate([ye_ref[0, e, pl.ds(starts[e], win), :] for e in range(n_e)], axis=0)
        acc_scr[...] = _dot_t0(onehot, rows)

        @pl.when(any_ref[b * nblk + step * n_sub + sub] > 0)
        def _(sub=sub, base=base, starts=starts):
            for e in range(n_e):
                @pl.when(nx_ref[base + e] > 0)
                def _(e=e):
                    prow = pos_ref[0, sub, e:e + 1, :]
                    rest = prow >= starts[e] + win

                    def body(w, carry):
                        off, m = _sweep_window(w, cap, win, prow, rest)
                        acc_scr[...] += _dot_t0(m.astype(BF16), ye_ref[0, e, pl.ds(off, win), :])
                        return carry

                    lax.fori_loop(0, pl.cdiv(cap, win), body, 0)

        tok = slice(sub * tb, (sub + 1) * tb)
        out_ref[0, tok, :] = (_ln(DEEPNORM_ALPHA * x1_ref[0, tok, :] + g2 * acc_scr[...]) * lnw_ref[...]
                              + lnb_ref[...])


def _combine_call(ws, nx, nx_any, ye, pos_b, x1, mod3, ln2w, ln2b):
    bsz, n_e, cap, d = ye.shape
    nblk, _, tb = pos_b.shape[1:]
    n_sub = COMBINE_STEP_BLOCKS
    grid_spec = pltpu.PrefetchScalarGridSpec(
        num_scalar_prefetch=3,
        grid=(bsz, nblk // n_sub),
        in_specs=[
            pl.BlockSpec((1, n_e, cap, d), lambda b, k, ws, nx, fl: (b, 0, 0, 0), pipeline_mode=pl.Buffered(1)),
            pl.BlockSpec((1, n_sub, n_e, tb), lambda b, k, ws, nx, fl: (b, k, 0, 0)),
            pl.BlockSpec((1, n_sub * tb, d), lambda b, k, ws, nx, fl: (b, k, 0)),
            pl.BlockSpec((1, 1, 6 * d), lambda b, k, ws, nx, fl: (b, 0, 0)),
            pl.BlockSpec((1, d), lambda b, k, ws, nx, fl: (0, 0)),
            pl.BlockSpec((1, d), lambda b, k, ws, nx, fl: (0, 0)),
        ],
        out_specs=pl.BlockSpec((1, n_sub * tb, d), lambda b, k, ws, nx, fl: (b, k, 0)),
        scratch_shapes=[pltpu.VMEM((tb, d), F32)],
    )
    return pl.pallas_call(
        _combine_kernel,
        out_shape=jax.ShapeDtypeStruct((bsz, nblk * tb, d), F32),
        grid_spec=grid_spec,
        compiler_params=_cparams(("parallel", "arbitrary")),
        name="combine",
    )(ws, nx, nx_any, ye, pos_b, x1, mod3, ln2w, ln2b)


def _rope_tables(n_tokens):
    rows = n_tokens // GRID_W
    n_freq = RET_QK_DIM // 4
    inv = ROPE_THETA ** (-jnp.arange(n_freq, dtype=F32) / n_freq)
    ang_r = jnp.arange(rows, dtype=F32)[:, None] * inv
    ang_c = jnp.arange(GRID_W, dtype=F32)[:, None] * inv
    zr = jnp.zeros((rows, 2 * n_freq), F32)
    zc = jnp.zeros((GRID_W, 2 * n_freq), F32)
    cr, sr, cc, sc = jnp.cos(ang_r), jnp.sin(ang_r), jnp.cos(ang_c), jnp.sin(ang_c)
    cos_t = (jnp.concatenate([cr, cr, zr], axis=-1)[:, None, :]
             + jnp.concatenate([zc, cc, cc], axis=-1)[None, :, :]).reshape(n_tokens, RET_QK_DIM)
    sin_t = (jnp.concatenate([-sr, sr, zr], axis=-1)[:, None, :]
             + jnp.concatenate([zc, -sc, sc], axis=-1)[None, :, :]).reshape(n_tokens, RET_QK_DIM)
    return jnp.concatenate([cos_t, sin_t], axis=-1)


def kernel(x, c, ctx, c_ctx, w_ada, b_ada, w_in, b_in, ret_decay_f, ret_decay_b, ret_gn_w, ret_gn_b,
           sgu_ln_w, sgu_ln_b, sgu_w, sgu_b, w_pa, w_pb, w_o, b_o, ln1_w, ln1_b,
           w_router, b_router, w_gate, w_up, w_down, ln2_w, ln2_b):
    bsz, t, d = x.shape
    assert d == D_MODEL and w_ada.shape[0] == DEPTH == 1 and bsz < MOD_ROWS
    cap = EC_CAPACITY_FACTOR * t // N_EXPERTS
    for win in (WIN_DISPATCH, WIN_COMBINE):
        assert cap >= win and (cap - win) % WIN_ALIGN == 0 and win % WIN_ALIGN == 0
    assert t % (TB_ROUTE * DISPATCH_MERGE * DISPATCH_STEP_WINDOWS) == 0 and t % (TB_ROUTE * COMBINE_STEP_BLOCKS) == 0
    assert N_EXPERTS % DISPATCH_GROUPS == 0
    l = 0

    cvec = jnp.concatenate([c, c_ctx[None], jnp.zeros((MOD_ROWS - bsz - 1, d), F32)], axis=0)
    mod = _mod_call(cvec.T, w_ada[l], b_ada[l][None], bsz + 1)
    mod3 = mod[:, None, :]

    w_in_b = w_in[l].astype(BF16)
    b_in2 = b_in[l][None]
    kctx, vctx = _ctx_kv_call(ctx, mod3, w_in_b, b_in2, bsz)

    z = _inproj_call(x, mod3, w_in_b, b_in2, _rope_tables(t), sgu_ln_w[l][None], sgu_ln_b[l][None])

    lg = jnp.stack([jax.nn.log_sigmoid(ret_decay_f[l].astype(F32)),
                    jax.nn.log_sigmoid(ret_decay_b[l].astype(F32))])
    states = _state_call(lg, z, kctx, vctx)

    x1, hm, logits_t = _mixer_call(
        lg, z, states, x, mod3,
        w_pa[l].astype(BF16), w_pb[l].astype(BF16), w_o[l].astype(BF16),
        sgu_w[l].astype(BF16), sgu_b[l].T, ret_gn_w[l][None], ret_gn_b[l][None], b_o[l][None],
        ln1_w[l][None], ln1_b[l][None], w_router[l].T.astype(BF16), b_router[l][:, None])

    pos_b, gate_b, cnt_b = _route_call(logits_t, cap)

    c0 = cnt_b[..., 0]
    c1 = jnp.concatenate([c0[:, 1:], jnp.full_like(c0[:, :1], cap)], axis=1)

    def windows(start, end, win, groups):
        ws = jnp.minimum((start // WIN_ALIGN) * WIN_ALIGN, cap - win)
        nx = (end > ws + win).astype(I32)
        nx_any = nx.reshape(nx.shape[0], nx.shape[1], groups, -1).max(axis=-1)
        return ws.reshape(-1), nx.reshape(-1), nx_any.reshape(-1)

    ws_d, nx_d, any_d = windows(c0[:, ::DISPATCH_MERGE], c1[:, DISPATCH_MERGE - 1::DISPATCH_MERGE],
                                WIN_DISPATCH, DISPATCH_GROUPS)
    ws_c, nx_c, any_c = windows(c0, c1, WIN_COMBINE, 1)

    xs, gc = _gather_call(ws_d, nx_d, any_d, hm, pos_b, gate_b, cap)
    ye = _ffn_call(xs, gc, w_gate[l], w_up[l], w_down[l])
    return _combine_call(ws_c, nx_c, any_c, ye, pos_b, x1, mod3, ln2_w[l][None], ln2_b[l][None])
```

```python
import functools
import math

import jax
import jax.numpy as jnp
from jax import lax
from jax.experimental import pallas as pl
from jax.experimental.pallas import tpu as pltpu

F32 = jnp.float32
BF16 = jnp.bfloat16
I32 = jnp.int32

D_MODEL = 1024
DEPTH = 1
GRID_W = 64
RET_HEADS = 4
RET_QK_DIM = 128
RET_V_DIM = 256
RET_CHUNK = 128
ROPE_THETA = 10000.0
RET_QK_W = RET_HEADS * RET_QK_DIM
RET_V_W = RET_HEADS * RET_V_DIM
SGU_GROUPS = 4
SGU_CHUNK = 128
SGU_WIDTH = 1024
SGU_GW = SGU_WIDTH // SGU_GROUPS
N_EXPERTS = 16
EC_CAPACITY_FACTOR = 2
LN_EPS = 1e-6
DEEPNORM_ALPHA = (2.0 * DEPTH) ** 0.25
D_IN = 2 * RET_QK_W + 2 * RET_V_W + 2 * SGU_WIDTH + 2 * D_MODEL
Z_K = RET_QK_W
Z_V = 2 * RET_QK_W
Z_SG = Z_V + RET_V_W
Z_U = Z_SG + RET_V_W
Z_VS = Z_U + SGU_WIDTH
Z_GA = Z_VS + SGU_WIDTH
Z_GB = Z_GA + D_MODEL
Z_BLOCK = 3 * 1024

V7X_VMEM_BYTES = 64 * 1024 * 1024
VMEM_LIMIT = V7X_VMEM_BYTES - 4 * 1024 * 1024
LANES = 128

MOD_ROWS = 8
MOD_COLS = 2048
TM_INPROJ = 1024
INPROJ_CHUNK = 256
TT_MIXER = 512
MERGE_ROWS = 512
TB_ROUTE = 256
DISPATCH_MERGE = 2
WIN_DISPATCH = 128
WIN_COMBINE = 64
WIN_ALIGN = 16
DISPATCH_GROUPS = 2
DISPATCH_STEP_WINDOWS = 4
COMBINE_STEP_BLOCKS = 4
STATE_UNROLL = 16
FFN_ROWS = 256


def _cparams(sem):
    return pltpu.CompilerParams(dimension_semantics=sem, vmem_limit_bytes=VMEM_LIMIT)


def _ln(x):
    mu = jnp.mean(x, axis=-1, keepdims=True)
    xc = x - mu
    var = jnp.mean(xc * xc, axis=-1, keepdims=True)
    return xc * lax.rsqrt(var + LN_EPS)


_GELU_A = -2.0 * math.sqrt(2.0 / math.pi) * math.log2(math.e)
_GELU_B = _GELU_A * 0.044715


def _gelu(x):
    return x / (1.0 + jnp.exp2(x * (_GELU_A + _GELU_B * (x * x))))


def _iota_f(shape, dim):
    return lax.broadcasted_iota(I32, shape, dim).astype(F32)


def _mod_kernel(ct_ref, w_ref, b_ref, o_ref, *, n_rows):
    c = ct_ref[...]
    a = c * jax.nn.sigmoid(c)
    w = w_ref[...]
    rows = [jnp.sum(w * a[:, m:m + 1], axis=0, keepdims=True) + b_ref[...] for m in range(n_rows)]
    rows += [jnp.zeros_like(rows[0])] * (MOD_ROWS - n_rows)
    o_ref[...] = jnp.concatenate(rows, axis=0)


def _mod_call(cvec_t, w_ada, b_ada, n_rows):
    d, n = w_ada.shape
    tn = MOD_COLS
    return pl.pallas_call(
        functools.partial(_mod_kernel, n_rows=n_rows),
        out_shape=jax.ShapeDtypeStruct((MOD_ROWS, n), F32),
        grid=(n // tn,),
        in_specs=[
            pl.BlockSpec((d, MOD_ROWS), lambda j: (0, 0)),
            pl.BlockSpec((d, tn), lambda j: (0, j)),
            pl.BlockSpec((1, tn), lambda j: (0, j)),
        ],
        out_specs=pl.BlockSpec((MOD_ROWS, tn), lambda j: (0, j)),
        compiler_params=_cparams(("parallel",)),
        name="mod",
    )(cvec_t, w_ada, b_ada)


def _ctx_kv_kernel(x_ref, mod_ref, wk_ref, wv_ref, bk_ref, bv_ref, k_ref, v_ref):
    d = D_MODEL
    sh = mod_ref[0, :, 0:d]
    sc = mod_ref[0, :, d:2 * d]
    h = (_ln(x_ref[0]) * (1.0 + sc) + sh).astype(BF16)
    k = jnp.dot(h, wk_ref[...], preferred_element_type=F32) + bk_ref[...]
    k_ref[0] = (k * (RET_QK_DIM ** -0.5)).astype(BF16)
    v_ref[0] = (jnp.dot(h, wv_ref[...], preferred_element_type=F32) + bv_ref[...]).astype(BF16)


def _ctx_kv_call(ctx, mod3, w_in, b_in, ctx_row):
    bsz, n_ctx, d = ctx.shape
    kblk = Z_K // RET_QK_W
    vblk = Z_V // RET_V_W
    return pl.pallas_call(
        _ctx_kv_kernel,
        out_shape=(jax.ShapeDtypeStruct((bsz, n_ctx, RET_QK_W), BF16),
                   jax.ShapeDtypeStruct((bsz, n_ctx, RET_V_W), BF16)),
        grid=(bsz,),
        in_specs=[
            pl.BlockSpec((1, n_ctx, d), lambda b: (b, 0, 0)),
            pl.BlockSpec((1, 1, 6 * d), lambda b: (ctx_row, 0, 0)),
            pl.BlockSpec((d, RET_QK_W), lambda b: (0, kblk)),
            pl.BlockSpec((d, RET_V_W), lambda b: (0, vblk)),
            pl.BlockSpec((1, RET_QK_W), lambda b: (0, kblk)),
            pl.BlockSpec((1, RET_V_W), lambda b: (0, vblk)),
        ],
        out_specs=(pl.BlockSpec((1, n_ctx, RET_QK_W), lambda b: (b, 0, 0)),
                   pl.BlockSpec((1, n_ctx, RET_V_W), lambda b: (b, 0, 0))),
        compiler_params=_cparams(("parallel",)),
        name="ctx_kv",
    )(ctx, mod3, w_in, w_in, b_in, b_in)


def _inproj_kernel(x_ref, mod_ref, w_ref, b_ref, rope_ref, lnw_ref, lnb_ref, z_ref):
    d = D_MODEL
    sh = mod_ref[0, :, 0:d]
    sc = mod_ref[0, :, d:2 * d]
    h = (_ln(x_ref[0]) * (1.0 + sc) + sh).astype(BF16)

    def proj(c0, width):
        return jnp.dot(h, w_ref[:, c0:c0 + width], preferred_element_type=F32) + b_ref[:, c0:c0 + width]

    zq = proj(0, 2 * RET_QK_W)
    cos = rope_ref[:, 0:LANES]
    sin = rope_ref[:, LANES:2 * LANES]
    tm = zq.shape[0]
    upper = (lax.broadcasted_iota(I32, (tm, LANES), 1) & 32) != 0
    for hb in range(2 * RET_HEADS):
        zs = zq[:, hb * LANES:(hb + 1) * LANES]
        if hb >= RET_HEADS:
            zs = zs * (RET_QK_DIM ** -0.5)
        sw = jnp.where(upper, pltpu.roll(zs, 32, 1), pltpu.roll(zs, LANES - 32, 1))
        z_ref[0, :, hb * LANES:(hb + 1) * LANES] = (zs * cos + sw * sin).astype(BF16)

    def emit(c0, width, fn):
        for cc in range(c0, c0 + width, INPROJ_CHUNK):
            z_ref[0, :, cc:cc + INPROJ_CHUNK] = fn(proj(cc, INPROJ_CHUNK)).astype(BF16)

    vs = _ln(_gelu(proj(Z_VS, SGU_WIDTH))) * lnw_ref[...] + lnb_ref[...]
    z_ref[0, :, Z_VS:Z_VS + SGU_WIDTH] = vs.astype(BF16)
    emit(Z_U, SGU_WIDTH, _gelu)
    emit(Z_SG, RET_V_W, jax.nn.silu)
    emit(Z_GA, d, jax.nn.sigmoid)
    emit(Z_GB, d, jax.nn.sigmoid)
    emit(Z_V, RET_V_W, lambda z: z)


def _inproj_call(x, mod3, w_in, b_in, rope_t, sgu_ln_w, sgu_ln_b):
    bsz, t, d = x.shape
    tm = TM_INPROJ
    row = lambda b, i: (b, i, 0)
    const2 = lambda b, i: (0, 0)
    return pl.pallas_call(
        _inproj_kernel,
        out_shape=jax.ShapeDtypeStruct((bsz, t, D_IN), BF16),
        grid=(bsz, t // tm),
        in_specs=[
            pl.BlockSpec((1, tm, d), row),
            pl.BlockSpec((1, 1, 6 * d), lambda b, i: (b, 0, 0)),
            pl.BlockSpec((d, D_IN), const2, pipeline_mode=pl.Buffered(1)),
            pl.BlockSpec((1, D_IN), const2),
            pl.BlockSpec((tm, 2 * LANES), lambda b, i: (i, 0)),
            pl.BlockSpec((1, SGU_WIDTH), const2),
            pl.BlockSpec((1, SGU_WIDTH), const2),
        ],
        out_specs=pl.BlockSpec((1, tm, D_IN), row),
        compiler_params=_cparams(("parallel", "parallel")),
        name="inproj",
    )(x, mod3, w_in, b_in, rope_t, sgu_ln_w, sgu_ln_b)


def _dot_t0(a, b):
    return lax.dot_general(a, b, (((0,), (0,)), ((), ())), preferred_element_type=F32)


def _state_kernel(lg_ref, k_ref, v_ref, kc_ref, vc_ref, s_ref):
    hd = pl.program_id(1)
    lgf = lg_ref[0, hd]
    lgb = lg_ref[1, hd]
    n_chunks = s_ref.shape[2]
    n_ctx = kc_ref.shape[1]
    c = RET_CHUNK
    li = _iota_f((c, RET_QK_DIM), 0)
    zeta_f = jnp.exp((c - 1.0 - li) * lgf)
    zeta_b = jnp.exp(li * lgb)
    one = jnp.ones((1, 1), F32)
    cd_f = jnp.exp(one * (c * lgf))
    cd_b = jnp.exp(one * (c * lgb))

    tc = _iota_f((n_ctx, RET_QK_DIM), 0)
    kc = kc_ref[0].astype(F32)
    vc = vc_ref[0]
    s0f = _dot_t0((kc * jnp.exp((n_ctx - 1.0 - tc) * lgf)).astype(BF16), vc)
    s0b = _dot_t0((kc * jnp.exp(tc * lgb)).astype(BF16), vc)

    def chunk_kv(n, zeta):
        off = pl.multiple_of(n * c, c)
        kk = k_ref[0, pl.ds(off, c), :].astype(F32)
        return _dot_t0((kk * zeta).astype(BF16), v_ref[0, pl.ds(off, c), :])

    def step(i, carry):
        sf, sb = carry
        nb = n_chunks - 1 - i
        s_ref[0, 0, i, 0:RET_QK_DIM, :] = sf.astype(BF16)
        s_ref[0, 0, nb, RET_QK_DIM:2 * RET_QK_DIM, :] = sb.astype(BF16)
        return cd_f * sf + chunk_kv(i, zeta_f), cd_b * sb + chunk_kv(nb, zeta_b)

    lax.fori_loop(0, n_chunks, step, (s0f, s0b), unroll=STATE_UNROLL)


def _state_call(lg, z, kctx, vctx):
    bsz, t, _ = z.shape
    n_ctx = kctx.shape[1]
    n_chunks = t // RET_CHUNK
    return pl.pallas_call(
        _state_kernel,
        out_shape=jax.ShapeDtypeStruct((bsz, RET_HEADS, n_chunks, 2 * RET_QK_DIM, RET_V_DIM), BF16),
        grid=(bsz, RET_HEADS),
        in_specs=[
            pl.BlockSpec(memory_space=pltpu.SMEM),
            pl.BlockSpec((1, t, RET_QK_DIM), lambda b, h: (b, 0, RET_QK_W // RET_QK_DIM + h)),
            pl.BlockSpec((1, t, RET_V_DIM), lambda b, h: (b, 0, Z_V // RET_V_DIM + h)),
            pl.BlockSpec((1, n_ctx, RET_QK_DIM), lambda b, h: (b, 0, h)),
            pl.BlockSpec((1, n_ctx, RET_V_DIM), lambda b, h: (b, 0, h)),
        ],
        out_specs=pl.BlockSpec((1, 1, n_chunks, 2 * RET_QK_DIM, RET_V_DIM), lambda b, h: (b, h, 0, 0, 0)),
        compiler_params=_cparams(("parallel", "parallel")),
        name="states",
    )(lg, z, z, kctx, vctx)


def _mixer_kernel(lg_ref, za_ref, zb_ref, gb_ref, s_ref, x_ref, mod_ref,
                  wpa_ref, wpb_ref, wo_ref, sguw_ref, sgub_ref, gnw_ref, gnb_ref, bo_ref,
                  ln1w_ref, ln1b_ref, wr_ref, br_ref,
                  x1_ref, hm_ref, lgt_ref, ret_scr, sgu_scr):
    d = D_MODEL
    c = RET_CHUNK

    n_sub = za_ref.shape[1] // c
    row = lax.broadcasted_iota(I32, (c, c), 0)
    col = lax.broadcasted_iota(I32, (c, c), 1)
    diff = (row - col).astype(F32)
    rowq = _iota_f((c, RET_QK_DIM), 0)
    for hd in range(RET_HEADS):
        lgf = lg_ref[0, hd]
        lgb = lg_ref[1, hd]
        vcols = slice(hd * RET_V_DIM, (hd + 1) * RET_V_DIM)
        mask = (jnp.where(diff >= 0, jnp.exp(jnp.maximum(diff, 0.0) * lgf), 0.0)
                + jnp.where(diff <= 0, jnp.exp(jnp.maximum(-diff, 0.0) * lgb), 0.0))
        xi_f = jnp.exp((rowq + 1.0) * lgf)
        xi_b = jnp.exp((c - rowq) * lgb)
        for ci in range(n_sub):
            r0 = ci * c
            q = za_ref[0, r0:r0 + c, hd * RET_QK_DIM:(hd + 1) * RET_QK_DIM]
            k = za_ref[0, r0:r0 + c, RET_QK_W + hd * RET_QK_DIM:RET_QK_W + (hd + 1) * RET_QK_DIM]
            s = lax.dot_general(q, k, (((1,), (1,)), ((), ())), preferred_element_type=F32)
            vv = za_ref[0, r0:r0 + c, Z_V + hd * RET_V_DIM:Z_V + (hd + 1) * RET_V_DIM]
            intra = jnp.dot((s * mask).astype(BF16), vv, preferred_element_type=F32)
            qf = q.astype(F32)
            qx = jnp.concatenate([(qf * xi_f).astype(BF16), (qf * xi_b).astype(BF16)], axis=1)
            cross = jnp.dot(qx, s_ref[0, hd, ci], preferred_element_type=F32)
            o = _ln(intra + cross) * gnw_ref[:, vcols] + gnb_ref[:, vcols]
            gate = za_ref[0, r0:r0 + c, Z_SG + hd * RET_V_DIM:Z_SG + (hd + 1) * RET_V_DIM]
            ret_scr[r0:r0 + c, vcols] = (o * gate.astype(F32)).astype(BF16)

    for g in range(SGU_GROUPS):
        gcols = slice(g * SGU_GW, (g + 1) * SGU_GW)
        wg = sguw_ref[g]
        bg = sgub_ref[:, g:g + 1]
        for ci in range(n_sub):
            r0 = ci * c
            vsb = zb_ref[0, r0:r0 + c, Z_VS - Z_BLOCK + g * SGU_GW:Z_VS - Z_BLOCK + (g + 1) * SGU_GW]
            ub = zb_ref[0, r0:r0 + c, Z_U - Z_BLOCK + g * SGU_GW:Z_U - Z_BLOCK + (g + 1) * SGU_GW]
            sp = jnp.dot(wg, vsb, preferred_element_type=F32) + bg
            sgu_scr[r0:r0 + c, gcols] = (ub.astype(F32) * sp).astype(BF16)

    g1 = mod_ref[0, :, 2 * d:3 * d]
    sh2 = mod_ref[0, :, 3 * d:4 * d]
    sc2 = mod_ref[0, :, 4 * d:5 * d]
    for r0 in range(0, za_ref.shape[1], MERGE_ROWS):
        rows = slice(r0, r0 + MERGE_ROWS)
        pa = jnp.dot(ret_scr[rows, :], wpa_ref[...], preferred_element_type=F32)
        pb = jnp.dot(sgu_scr[rows, :], wpb_ref[...], preferred_element_type=F32)
        ga = zb_ref[0, rows, Z_GA - Z_BLOCK:Z_GA - Z_BLOCK + d]
        y = (ga.astype(F32) * pa + gb_ref[0, rows, :].astype(F32) * pb).astype(BF16)
        mix = jnp.dot(y, wo_ref[...], preferred_element_type=F32) + bo_ref[...]
        x1 = _ln(DEEPNORM_ALPHA * x_ref[0, rows, :] + g1 * mix) * ln1w_ref[...] + ln1b_ref[...]
        x1_ref[0, rows, :] = x1
        hm = (_ln(x1) * (1.0 + sc2) + sh2).astype(BF16)
        hm_ref[0, rows, :] = hm
        lgt_ref[0, :, rows] = lax.dot_general(wr_ref[...], hm, (((1,), (1,)), ((), ())),
                                              preferred_element_type=F32) + br_ref[...]


def _mixer_call(lg, z, states, x, mod3, wpa, wpb, wo, sguw, sgub_t, gnw, gnb, bo,
                ln1w, ln1b, wr_t, br):
    bsz, t, d = x.shape
    tt = TT_MIXER
    n_sub = tt // RET_CHUNK
    row = lambda b, i: (b, i, 0)
    c2 = lambda b, i: (0, 0)
    c3 = lambda b, i: (0, 0, 0)
    return pl.pallas_call(
        _mixer_kernel,
        out_shape=(jax.ShapeDtypeStruct((bsz, t, d), F32),
                   jax.ShapeDtypeStruct((bsz, t, d), BF16),
                   jax.ShapeDtypeStruct((bsz, N_EXPERTS, t), F32)),
        grid=(bsz, t // tt),
        in_specs=[
            pl.BlockSpec(memory_space=pltpu.SMEM),
            pl.BlockSpec((1, tt, Z_BLOCK), lambda b, i: (b, i, 0)),
            pl.BlockSpec((1, tt, Z_BLOCK), lambda b, i: (b, i, 1)),
            pl.BlockSpec((1, tt, d), lambda b, i: (b, i, Z_GB // d)),
            pl.BlockSpec((1, RET_HEADS, n_sub, 2 * RET_QK_DIM, RET_V_DIM), lambda b, i: (b, 0, i, 0, 0)),
            pl.BlockSpec((1, tt, d), row),
            pl.BlockSpec((1, 1, 6 * d), lambda b, i: (b, 0, 0)),
            pl.BlockSpec((RET_V_W, d), c2, pipeline_mode=pl.Buffered(1)),
            pl.BlockSpec((SGU_WIDTH, d), c2, pipeline_mode=pl.Buffered(1)),
            pl.BlockSpec((d, d), c2, pipeline_mode=pl.Buffered(1)),
            pl.BlockSpec((SGU_GROUPS, SGU_CHUNK, SGU_CHUNK), c3),
            pl.BlockSpec((SGU_CHUNK, SGU_GROUPS), c2),
            pl.BlockSpec((1, RET_V_W), c2),
            pl.BlockSpec((1, RET_V_W), c2),
            pl.BlockSpec((1, d), c2),
            pl.BlockSpec((1, d), c2),
            pl.BlockSpec((1, d), c2),
            pl.BlockSpec((N_EXPERTS, d), c2),
            pl.BlockSpec((N_EXPERTS, 1), c2),
        ],
        out_specs=(pl.BlockSpec((1, tt, d), row),
                   pl.BlockSpec((1, tt, d), row),
                   pl.BlockSpec((1, N_EXPERTS, tt), lambda b, i: (b, 0, i))),
        scratch_shapes=[pltpu.VMEM((tt, RET_V_W), BF16), pltpu.VMEM((tt, SGU_WIDTH), BF16)],
        compiler_params=_cparams(("parallel", "parallel")),
        name="mixer",
    )(lg, z, z, z, states, x, mod3, wpa, wpb, wo, sguw, sgub_t, gnw, gnb, bo,
      ln1w, ln1b, wr_t, br)


def _route_kernel(lgt_ref, pos_ref, gate_ref, wsd_ref, nxd_ref, anyd_ref, wsc_ref, nxc_ref, anyc_ref, *, cap):
    bsz, n_e, t = lgt_ref.shape
    tb = TB_ROUTE

    def softmax(lg):
        ex = jnp.exp(lg - jnp.max(lg, axis=0, keepdims=True))
        return ex / jnp.sum(ex, axis=0, keepdims=True)

    aff = jnp.concatenate([softmax(lgt_ref[bb]) for bb in range(bsz)], axis=0)
    n_rows = bsz * n_e

    def search(i, thr_bits):
        cand = thr_bits | lax.shift_left(jnp.int32(1), 30 - i)
        cnt = jnp.sum((aff >= lax.bitcast_convert_type(cand, F32)).astype(I32), axis=1, keepdims=True)
        return jnp.where(cnt >= cap, cand, thr_bits)

    thr_bits = lax.fori_loop(0, 31, search, jnp.zeros((n_rows, 1), I32))
    floor_f = lax.bitcast_convert_type(thr_bits, F32)
    thr = jnp.min(jnp.where(aff >= floor_f, aff, jnp.inf), axis=1, keepdims=True)
    need = (cap - jnp.sum((aff > thr).astype(I32), axis=1, keepdims=True)).astype(F32)

    r = lax.broadcasted_iota(I32, (tb, tb), 0)
    cc = lax.broadcasted_iota(I32, (tb, tb), 1)
    tri = (r <= cc).astype(BF16)
    carry_eq = jnp.zeros((n_rows, 1), F32)
    carry_sel = jnp.zeros((n_rows, 1), F32)
    nblk = t // tb
    merge = DISPATCH_MERGE
    blk_lane = lax.broadcasted_iota(I32, (n_rows, nblk), 1)
    zero = jnp.zeros((n_rows, nblk), F32)
    used_c, end_c, used_d, end_d = zero, zero, zero, zero
    for blk in range(nblk):
        sl = slice(blk * tb, (blk + 1) * tb)
        aff_b = aff[:, sl]
        eq = aff_b == thr
        eq_b = eq.astype(BF16)
        inc_eq = jnp.dot(eq_b, tri, preferred_element_type=F32)
        before = carry_eq + inc_eq - eq_b.astype(F32)
        sel = (aff_b > thr) | (eq & (before < need))
        sel_b = sel.astype(BF16)
        inc_sel = jnp.dot(sel_b, tri, preferred_element_type=F32)
        pos = jnp.where(sel, (carry_sel + inc_sel - 1.0).astype(I32), -1)
        for bb in range(bsz):
            rows = slice(bb * n_e, (bb + 1) * n_e)
            pos_ref[bb, blk] = pos[rows]
            gate_ref[bb, blk] = aff_b[rows]
        used_c = jnp.where(blk_lane == blk, carry_sel, used_c)
        if blk % merge == 0:
            used_d = jnp.where(blk_lane == blk // merge, carry_sel, used_d)
        carry_eq = carry_eq + inc_eq[:, tb - 1:tb]
        carry_sel = carry_sel + inc_sel[:, tb - 1:tb]
        end_c = jnp.where(blk_lane == blk, carry_sel, end_c)
        if blk % merge == merge - 1:
            end_d = jnp.where(blk_lane == blk // merge, carry_sel, end_d)

    def windows(used, end, win, groups, ws_ref, nx_ref, any_ref):
        n = ws_ref.shape[1]
        ws = jnp.minimum(jnp.floor(used[:, 0:n] * (1.0 / WIN_ALIGN)) * WIN_ALIGN, float(cap - win))
        nx = (end[:, 0:n] > ws + win).astype(I32)
        ws_ref[...] = ws.astype(I32)
        nx_ref[...] = nx
        rows_g = n_e // groups
        for r in range(bsz * groups):
            any_ref[r:r + 1, :] = jnp.max(nx[r * rows_g:(r + 1) * rows_g], axis=0, keepdims=True)

    windows(used_d, end_d, WIN_DISPATCH, DISPATCH_GROUPS, wsd_ref, nxd_ref, anyd_ref)
    windows(used_c, end_c, WIN_COMBINE, 1, wsc_ref, nxc_ref, anyc_ref)


def _route_call(logits_t, cap):
    bsz, n_e, t = logits_t.shape
    nblk = t // TB_ROUTE
    nwin = nblk // DISPATCH_MERGE
    small = [(bsz * n_e, nwin), (bsz * n_e, nwin), (bsz * DISPATCH_GROUPS, nwin),
             (bsz * n_e, nblk), (bsz * n_e, nblk), (bsz, nblk)]
    return pl.pallas_call(
        functools.partial(_route_kernel, cap=cap),
        out_shape=(jax.ShapeDtypeStruct((bsz, nblk, n_e, TB_ROUTE), I32),
                   jax.ShapeDtypeStruct((bsz, nblk, n_e, TB_ROUTE), F32))
        + tuple(jax.ShapeDtypeStruct(sh, I32) for sh in small),
        grid=(1,),
        in_specs=[pl.BlockSpec((bsz, n_e, t), lambda i: (0, 0, 0))],
        out_specs=(pl.BlockSpec((bsz, nblk, n_e, TB_ROUTE), lambda i: (0, 0, 0, 0)),
                   pl.BlockSpec((bsz, nblk, n_e, TB_ROUTE), lambda i: (0, 0, 0, 0)))
        + tuple(pl.BlockSpec(sh, lambda i: (0, 0)) for sh in small),
        compiler_params=_cparams(("arbitrary",)),
        name="route",
    )(logits_t)


def _window_rows(ws, win, n_tok):
    return lax.broadcasted_iota(I32, (win, n_tok), 0) + ws


def _sweep_window(w, cap, win, prow, rest):
    off = pl.multiple_of(jnp.minimum(w * win, cap - win), WIN_ALIGN)
    rows = _window_rows(off, win, prow.shape[-1])
    return off, (rows == prow) & rest & (rows >= w * win)


def _gather_kernel(ws_ref, nx_ref, any_ref, hm_ref, pos_ref, gate_ref, xs_ref, gc_ref):
    b = pl.program_id(0)
    g = pl.program_id(1)
    step = pl.program_id(2)
    merge = DISPATCH_MERGE
    eg, tb = pos_ref.shape[2:]
    n_sub = pos_ref.shape[1] // merge
    cap = xs_ref.shape[2]
    win = WIN_DISPATCH
    row0 = (b * pl.num_programs(1) + g) * eg

    @pl.when(step == 0)
    def _():
        xs_ref[...] = jnp.zeros_like(xs_ref)
        gc_ref[...] = jnp.zeros_like(gc_ref)

    def tokens(sub):
        return hm_ref[0, sub * merge * tb:(sub + 1) * merge * tb, :]

    def lanes(ref, sub, el):
        return jnp.concatenate([ref[0, sub * merge + m, el:el + 1, :] for m in range(merge)], axis=-1)

    for sub in range(n_sub):
        for el in range(eg):
            ws = pl.multiple_of(ws_ref[row0 + el, step * n_sub + sub], WIN_ALIGN)
            match = _window_rows(ws, win, merge * tb) == lanes(pos_ref, sub, el)
            sl = pl.ds(ws, win)
            xs_ref[0, el, sl, :] += jnp.dot(match.astype(BF16), tokens(sub),
                                            preferred_element_type=F32).astype(BF16)
            gc_ref[0, el, sl, :] += jnp.sum(jnp.where(match, lanes(gate_ref, sub, el), 0.0),
                                             axis=1, keepdims=True)

    for sub in range(n_sub):
        @pl.when(any_ref[b * pl.num_programs(1) + g, step * n_sub + sub] > 0)
        def _(sub=sub):
            for el in range(eg):
                @pl.when(nx_ref[row0 + el, step * n_sub + sub] > 0)
                def _(el=el):
                    prow = lanes(pos_ref, sub, el)
                    grow = lanes(gate_ref, sub, el)
                    rest = prow >= ws_ref[row0 + el, step * n_sub + sub] + win

                    def body(w, carry):
                        off, m = _sweep_window(w, cap, win, prow, rest)
                        xs_ref[0, el, pl.ds(off, win), :] += jnp.dot(m.astype(BF16), tokens(sub),
                                                                      preferred_element_type=F32).astype(BF16)
                        gc_ref[0, el, pl.ds(off, win), :] += jnp.sum(jnp.where(m, grow, 0.0), axis=1,
                                                                     keepdims=True)
                        return carry

                    lax.fori_loop(0, pl.cdiv(cap, win), body, 0)


def _gather_call(ws, nx, nx_any, hm, pos_b, gate_b, cap):
    bsz, t, d = hm.shape
    nblk, n_e, tb = pos_b.shape[1:]
    eg = n_e // DISPATCH_GROUPS
    blocks = DISPATCH_STEP_WINDOWS * DISPATCH_MERGE
    grid_spec = pltpu.PrefetchScalarGridSpec(
        num_scalar_prefetch=3,
        grid=(bsz, DISPATCH_GROUPS, nblk // blocks),
        in_specs=[
            pl.BlockSpec((1, blocks * tb, d), lambda b, g, k, ws, nx, fl: (b, k, 0)),
            pl.BlockSpec((1, blocks, eg, tb), lambda b, g, k, ws, nx, fl: (b, k, g, 0)),
            pl.BlockSpec((1, blocks, eg, tb), lambda b, g, k, ws, nx, fl: (b, k, g, 0)),
        ],
        out_specs=(pl.BlockSpec((1, eg, cap, d), lambda b, g, k, ws, nx, fl: (b, g, 0, 0)),
                   pl.BlockSpec((1, eg, cap, 1), lambda b, g, k, ws, nx, fl: (b, g, 0, 0))),
    )
    return pl.pallas_call(
        _gather_kernel,
        out_shape=(jax.ShapeDtypeStruct((bsz, n_e, cap, d), BF16),
                   jax.ShapeDtypeStruct((bsz, n_e, cap, 1), F32)),
        grid_spec=grid_spec,
        compiler_params=_cparams(("parallel", "parallel", "arbitrary")),
        name="gather",
    )(ws, nx, nx_any, hm, pos_b, gate_b)


def _ffn_kernel(xs_ref, gc_ref, wg_ref, wu_ref, wd_ref, ye_ref, wgu_s, wd_s):
    f = wd_s.shape[0]
    wgu_s[:, 0:f] = wg_ref[0].astype(BF16)
    wgu_s[:, f:2 * f] = wu_ref[0].astype(BF16)
    wd_s[...] = wd_ref[0].astype(BF16)

    rt = FFN_ROWS
    for b in range(xs_ref.shape[0]):
        for j in range(xs_ref.shape[2] // rt):
            sl = slice(j * rt, (j + 1) * rt)
            xgu = jnp.dot(xs_ref[b, 0, sl, :], wgu_s[...], preferred_element_type=F32)
            hid = (jax.nn.silu(xgu[:, 0:f]) * xgu[:, f:2 * f]).astype(BF16)
            ye = jnp.dot(hid, wd_s[...], preferred_element_type=F32) * gc_ref[b, 0, sl, :]
            ye_ref[b, 0, sl, :] = ye.astype(BF16)


def _ffn_call(xs, gc, w_gate, w_up, w_down):
    bsz, n_e, cap, d = xs.shape
    f = w_gate.shape[2]
    return pl.pallas_call(
        _ffn_kernel,
        out_shape=jax.ShapeDtypeStruct((bsz, n_e, cap, d), BF16),
        grid=(n_e,),
        in_specs=[
            pl.BlockSpec((bsz, 1, cap, d), lambda e: (0, e, 0, 0)),
            pl.BlockSpec((bsz, 1, cap, 1), lambda e: (0, e, 0, 0)),
            pl.BlockSpec((1, d, f), lambda e: (e, 0, 0)),
            pl.BlockSpec((1, d, f), lambda e: (e, 0, 0)),
            pl.BlockSpec((1, f, d), lambda e: (e, 0, 0)),
        ],
        out_specs=pl.BlockSpec((bsz, 1, cap, d), lambda e: (0, e, 0, 0)),
        scratch_shapes=[pltpu.VMEM((d, 2 * f), BF16), pltpu.VMEM((f, d), BF16)],
        compiler_params=_cparams(("parallel",)),
        name="ffn",
    )(xs, gc, w_gate, w_up, w_down)


def _combine_kernel(ws_ref, nx_ref, any_ref, ye_ref, pos_ref, x1_ref, mod_ref, lnw_ref, lnb_ref, out_ref, acc_scr):
    b = pl.program_id(0)
    step = pl.program_id(1)
    n_sub, n_e, tb = pos_ref.shape[1:]
    cap = ye_ref.shape[2]
    d = D_MODEL
    win = WIN_COMBINE
    g2 = mod_ref[0, :, 5 * d:6 * d]

    for sub in range(n_sub):
        blk = step * n_sub + sub
        starts = [pl.multiple_of(ws_ref[b * n_e + e, blk], WIN_ALIGN) for e in range(n_e)]
        onehot = jnp.concatenate(
            [(_window_rows(starts[e], win, tb) == pos_ref[0, sub, e:e + 1, :]).astype(BF16) for e in range(n_e)],
            axis=0)
        rows = jnp.concatenate([ye_ref[0, e, pl.ds(starts[e], win), :] for e in range(n_e)], axis=0)
        acc_scr[...] = _dot_t0(onehot, rows)

        @pl.when(any_ref[b, blk] > 0)
        def _(sub=sub, blk=blk, starts=starts):
            for e in range(n_e):
                @pl.when(nx_ref[b * n_e + e, blk] > 0)
                def _(e=e):
                    prow = pos_ref[0, sub, e:e + 1, :]
                    rest = prow >= starts[e] + win

                    def body(w, carry):
                        off, m = _sweep_window(w, cap, win, prow, rest)
                        acc_scr[...] += _dot_t0(m.astype(BF16), ye_ref[0, e, pl.ds(off, win), :])
                        return carry

                    lax.fori_loop(0, pl.cdiv(cap, win), body, 0)

        tok = slice(sub * tb, (sub + 1) * tb)
        out_ref[0, tok, :] = (_ln(DEEPNORM_ALPHA * x1_ref[0, tok, :] + g2 * acc_scr[...]) * lnw_ref[...]
                              + lnb_ref[...])


def _combine_call(ws, nx, nx_any, ye, pos_b, x1, mod3, ln2w, ln2b):
    bsz, n_e, cap, d = ye.shape
    nblk, _, tb = pos_b.shape[1:]
    n_sub = COMBINE_STEP_BLOCKS
    grid_spec = pltpu.PrefetchScalarGridSpec(
        num_scalar_prefetch=3,
        grid=(bsz, nblk // n_sub),
        in_specs=[
            pl.BlockSpec((1, n_e, cap, d), lambda b, k, ws, nx, fl: (b, 0, 0, 0), pipeline_mode=pl.Buffered(1)),
            pl.BlockSpec((1, n_sub, n_e, tb), lambda b, k, ws, nx, fl: (b, k, 0, 0)),
            pl.BlockSpec((1, n_sub * tb, d), lambda b, k, ws, nx, fl: (b, k, 0)),
            pl.BlockSpec((1, 1, 6 * d), lambda b, k, ws, nx, fl: (b, 0, 0)),
            pl.BlockSpec((1, d), lambda b, k, ws, nx, fl: (0, 0)),
            pl.BlockSpec((1, d), lambda b, k, ws, nx, fl: (0, 0)),
        ],
        out_specs=pl.BlockSpec((1, n_sub * tb, d), lambda b, k, ws, nx, fl: (b, k, 0)),
        scratch_shapes=[pltpu.VMEM((tb, d), F32)],
    )
    return pl.pallas_call(
        _combine_kernel,
        out_shape=jax.ShapeDtypeStruct((bsz, nblk * tb, d), F32),
        grid_spec=grid_spec,
        compiler_params=_cparams(("parallel", "arbitrary")),
        name="combine",
    )(ws, nx, nx_any, ye, pos_b, x1, mod3, ln2w, ln2b)


def _rope_tables(n_tokens):
    rows = n_tokens // GRID_W
    n_freq = RET_QK_DIM // 4
    inv = ROPE_THETA ** (-jnp.arange(n_freq, dtype=F32) / n_freq)
    ang_r = jnp.arange(rows, dtype=F32)[:, None] * inv
    ang_c = jnp.arange(GRID_W, dtype=F32)[:, None] * inv
    zr = jnp.zeros((rows, 2 * n_freq), F32)
    zc = jnp.zeros((GRID_W, 2 * n_freq), F32)
    cr, sr, cc, sc = jnp.cos(ang_r), jnp.sin(ang_r), jnp.cos(ang_c), jnp.sin(ang_c)
    cos_t = (jnp.concatenate([cr, cr, zr], axis=-1)[:, None, :]
             + jnp.concatenate([zc, cc, cc], axis=-1)[None, :, :]).reshape(n_tokens, RET_QK_DIM)
    sin_t = (jnp.concatenate([-sr, sr, zr], axis=-1)[:, None, :]
             + jnp.concatenate([zc, -sc, sc], axis=-1)[None, :, :]).reshape(n_tokens, RET_QK_DIM)
    return jnp.concatenate([cos_t, sin_t], axis=-1)


def kernel(x, c, ctx, c_ctx, w_ada, b_ada, w_in, b_in, ret_decay_f, ret_decay_b, ret_gn_w, ret_gn_b,
           sgu_ln_w, sgu_ln_b, sgu_w, sgu_b, w_pa, w_pb, w_o, b_o, ln1_w, ln1_b,
           w_router, b_router, w_gate, w_up, w_down, ln2_w, ln2_b):
    bsz, t, d = x.shape
    assert d == D_MODEL and w_ada.shape[0] == DEPTH == 1 and bsz < MOD_ROWS
    cap = EC_CAPACITY_FACTOR * t // N_EXPERTS
    for win in (WIN_DISPATCH, WIN_COMBINE):
        assert cap >= win and (cap - win) % WIN_ALIGN == 0 and win % WIN_ALIGN == 0
    assert t % (TB_ROUTE * DISPATCH_MERGE * DISPATCH_STEP_WINDOWS) == 0 and t % (TB_ROUTE * COMBINE_STEP_BLOCKS) == 0
    assert N_EXPERTS % DISPATCH_GROUPS == 0
    l = 0

    cvec = jnp.concatenate([c, c_ctx[None], jnp.zeros((MOD_ROWS - bsz - 1, d), F32)], axis=0)
    mod = _mod_call(cvec.T, w_ada[l], b_ada[l][None], bsz + 1)
    mod3 = mod[:, None, :]

    w_in_b = w_in[l].astype(BF16)
    b_in2 = b_in[l][None]
    kctx, vctx = _ctx_kv_call(ctx, mod3, w_in_b, b_in2, bsz)

    z = _inproj_call(x, mod3, w_in_b, b_in2, _rope_tables(t), sgu_ln_w[l][None], sgu_ln_b[l][None])

    lg = jnp.stack([jax.nn.log_sigmoid(ret_decay_f[l].astype(F32)),
                    jax.nn.log_sigmoid(ret_decay_b[l].astype(F32))])
    states = _state_call(lg, z, kctx, vctx)

    x1, hm, logits_t = _mixer_call(
        lg, z, states, x, mod3,
        w_pa[l].astype(BF16), w_pb[l].astype(BF16), w_o[l].astype(BF16),
        sgu_w[l].astype(BF16), sgu_b[l].T, ret_gn_w[l][None], ret_gn_b[l][None], b_o[l][None],
        ln1_w[l][None], ln1_b[l][None], w_router[l].T.astype(BF16), b_router[l][:, None])

    pos_b, gate_b, ws_d, nx_d, any_d, ws_c, nx_c, any_c = _route_call(logits_t, cap)
    xs, gc = _gather_call(ws_d, nx_d, any_d, hm, pos_b, gate_b, cap)
    ye = _ffn_call(xs, gc, w_gate[l], w_up[l], w_down[l])
    return _combine_call(ws_c, nx_c, any_c, ye, pos_b, x1, mod3, ln2_w[l][None], ln2_b[l][None])
```

```python
import functools
import math

import jax
import jax.numpy as jnp
from jax import lax
from jax.experimental import pallas as pl
from jax.experimental.pallas import tpu as pltpu

F32 = jnp.float32
BF16 = jnp.bfloat16
I32 = jnp.int32

D_MODEL = 1024
DEPTH = 1
GRID_W = 64
RET_HEADS = 4
RET_QK_DIM = 128
RET_V_DIM = 256
RET_CHUNK = 128
ROPE_THETA = 10000.0
RET_QK_W = RET_HEADS * RET_QK_DIM
RET_V_W = RET_HEADS * RET_V_DIM
SGU_GROUPS = 4
SGU_CHUNK = 128
SGU_WIDTH = 1024
SGU_GW = SGU_WIDTH // SGU_GROUPS
N_EXPERTS = 16
EC_CAPACITY_FACTOR = 2
LN_EPS = 1e-6
DEEPNORM_ALPHA = (2.0 * DEPTH) ** 0.25
D_IN = 2 * RET_QK_W + 2 * RET_V_W + 2 * SGU_WIDTH + 2 * D_MODEL
Z_K = RET_QK_W
Z_V = 2 * RET_QK_W
Z_SG = Z_V + RET_V_W
Z_U = Z_SG + RET_V_W
Z_VS = Z_U + SGU_WIDTH
Z_GA = Z_VS + SGU_WIDTH
Z_GB = Z_GA + D_MODEL
Z_BLOCK = 3 * 1024

V7X_VMEM_BYTES = 64 * 1024 * 1024
VMEM_LIMIT = V7X_VMEM_BYTES - 4 * 1024 * 1024
LANES = 128

MOD_ROWS = 8
MOD_COLS = 2048
TM_INPROJ = 1024
INPROJ_CHUNK = 256
TT_MIXER = 512
MERGE_ROWS = 512
TB_ROUTE = 256
DISPATCH_MERGE = 2
WIN_DISPATCH = 128
WIN_COMBINE = 64
WIN_ALIGN = 16
DISPATCH_GROUPS = 2
DISPATCH_STEP_WINDOWS = 4
COMBINE_STEP_BLOCKS = 4
STATE_UNROLL = 16
FFN_ROWS = 256


def _cparams(sem):
    return pltpu.CompilerParams(dimension_semantics=sem, vmem_limit_bytes=VMEM_LIMIT)


def _ln(x):
    mu = jnp.mean(x, axis=-1, keepdims=True)
    xc = x - mu
    var = jnp.mean(xc * xc, axis=-1, keepdims=True)
    return xc * lax.rsqrt(var + LN_EPS)


_GELU_A = -2.0 * math.sqrt(2.0 / math.pi) * math.log2(math.e)
_GELU_B = _GELU_A * 0.044715


def _gelu(x):
    return x / (1.0 + jnp.exp2(x * (_GELU_A + _GELU_B * (x * x))))


def _iota_f(shape, dim):
    return lax.broadcasted_iota(I32, shape, dim).astype(F32)


def _mod_kernel(ct_ref, w_ref, b_ref, o_ref, *, n_rows):
    c = ct_ref[...]
    a = c * jax.nn.sigmoid(c)
    w = w_ref[...]
    rows = [jnp.sum(w * a[:, m:m + 1], axis=0, keepdims=True) + b_ref[...] for m in range(n_rows)]
    rows += [jnp.zeros_like(rows[0])] * (MOD_ROWS - n_rows)
    o_ref[...] = jnp.concatenate(rows, axis=0)


def _mod_call(cvec_t, w_ada, b_ada, n_rows):
    d, n = w_ada.shape
    tn = MOD_COLS
    return pl.pallas_call(
        functools.partial(_mod_kernel, n_rows=n_rows),
        out_shape=jax.ShapeDtypeStruct((MOD_ROWS, n), F32),
        grid=(n // tn,),
        in_specs=[
            pl.BlockSpec((d, MOD_ROWS), lambda j: (0, 0)),
            pl.BlockSpec((d, tn), lambda j: (0, j)),
            pl.BlockSpec((1, tn), lambda j: (0, j)),
        ],
        out_specs=pl.BlockSpec((MOD_ROWS, tn), lambda j: (0, j)),
        compiler_params=_cparams(("parallel",)),
        name="mod",
    )(cvec_t, w_ada, b_ada)


def _ctx_kv_kernel(x_ref, mod_ref, wk_ref, wv_ref, bk_ref, bv_ref, k_ref, v_ref):
    d = D_MODEL
    sh = mod_ref[0, :, 0:d]
    sc = mod_ref[0, :, d:2 * d]
    h = (_ln(x_ref[0]) * (1.0 + sc) + sh).astype(BF16)
    k = jnp.dot(h, wk_ref[...], preferred_element_type=F32) + bk_ref[...]
    k_ref[0] = (k * (RET_QK_DIM ** -0.5)).astype(BF16)
    v_ref[0] = (jnp.dot(h, wv_ref[...], preferred_element_type=F32) + bv_ref[...]).astype(BF16)


def _ctx_kv_call(ctx, mod3, w_in, b_in, ctx_row):
    bsz, n_ctx, d = ctx.shape
    kblk = Z_K // RET_QK_W
    vblk = Z_V // RET_V_W
    return pl.pallas_call(
        _ctx_kv_kernel,
        out_shape=(jax.ShapeDtypeStruct((bsz, n_ctx, RET_QK_W), BF16),
                   jax.ShapeDtypeStruct((bsz, n_ctx, RET_V_W), BF16)),
        grid=(bsz,),
        in_specs=[
            pl.BlockSpec((1, n_ctx, d), lambda b: (b, 0, 0)),
            pl.BlockSpec((1, 1, 6 * d), lambda b: (ctx_row, 0, 0)),
            pl.BlockSpec((d, RET_QK_W), lambda b: (0, kblk)),
            pl.BlockSpec((d, RET_V_W), lambda b: (0, vblk)),
            pl.BlockSpec((1, RET_QK_W), lambda b: (0, kblk)),
            pl.BlockSpec((1, RET_V_W), lambda b: (0, vblk)),
        ],
        out_specs=(pl.BlockSpec((1, n_ctx, RET_QK_W), lambda b: (b, 0, 0)),
                   pl.BlockSpec((1, n_ctx, RET_V_W), lambda b: (b, 0, 0))),
        compiler_params=_cparams(("parallel",)),
        name="ctx_kv",
    )(ctx, mod3, w_in, w_in, b_in, b_in)


def _inproj_kernel(x_ref, mod_ref, w_ref, b_ref, rope_r_ref, rope_c_ref, lnw_ref, lnb_ref, z_ref, rope_scr):
    d = D_MODEL
    sh = mod_ref[0, :, 0:d]
    sc = mod_ref[0, :, d:2 * d]
    h = (_ln(x_ref[0]) * (1.0 + sc) + sh).astype(BF16)

    def proj(c0, width):
        return jnp.dot(h, w_ref[:, c0:c0 + width], preferred_element_type=F32) + b_ref[:, c0:c0 + width]

    zq = proj(0, 2 * RET_QK_W)
    for r in range(rope_r_ref.shape[0]):
        rope_scr[r * GRID_W:(r + 1) * GRID_W, :] = rope_r_ref[r:r + 1, :] + rope_c_ref[...]
    cos = rope_scr[:, 0:LANES]
    sin = rope_scr[:, LANES:2 * LANES]
    tm = zq.shape[0]
    upper = (lax.broadcasted_iota(I32, (tm, LANES), 1) & 32) != 0
    for hb in range(2 * RET_HEADS):
        zs = zq[:, hb * LANES:(hb + 1) * LANES]
        if hb >= RET_HEADS:
            zs = zs * (RET_QK_DIM ** -0.5)
        sw = jnp.where(upper, pltpu.roll(zs, 32, 1), pltpu.roll(zs, LANES - 32, 1))
        z_ref[0, :, hb * LANES:(hb + 1) * LANES] = (zs * cos + sw * sin).astype(BF16)

    def emit(c0, width, fn):
        for cc in range(c0, c0 + width, INPROJ_CHUNK):
            z_ref[0, :, cc:cc + INPROJ_CHUNK] = fn(proj(cc, INPROJ_CHUNK)).astype(BF16)

    vs = _ln(_gelu(proj(Z_VS, SGU_WIDTH))) * lnw_ref[...] + lnb_ref[...]
    z_ref[0, :, Z_VS:Z_VS + SGU_WIDTH] = vs.astype(BF16)
    emit(Z_U, SGU_WIDTH, _gelu)
    emit(Z_SG, RET_V_W, jax.nn.silu)
    emit(Z_GA, d, jax.nn.sigmoid)
    emit(Z_GB, d, jax.nn.sigmoid)
    emit(Z_V, RET_V_W, lambda z: z)


def _inproj_call(x, mod3, w_in, b_in, rope_r, rope_c, sgu_ln_w, sgu_ln_b):
    bsz, t, d = x.shape
    tm = TM_INPROJ
    assert tm % GRID_W == 0
    row = lambda b, i: (b, i, 0)
    const2 = lambda b, i: (0, 0)
    return pl.pallas_call(
        _inproj_kernel,
        out_shape=jax.ShapeDtypeStruct((bsz, t, D_IN), BF16),
        grid=(bsz, t // tm),
        in_specs=[
            pl.BlockSpec((1, tm, d), row),
            pl.BlockSpec((1, 1, 6 * d), lambda b, i: (b, 0, 0)),
            pl.BlockSpec((d, D_IN), const2, pipeline_mode=pl.Buffered(1)),
            pl.BlockSpec((1, D_IN), const2),
            pl.BlockSpec((tm // GRID_W, 2 * LANES), lambda b, i: (i, 0)),
            pl.BlockSpec((GRID_W, 2 * LANES), const2),
            pl.BlockSpec((1, SGU_WIDTH), const2),
            pl.BlockSpec((1, SGU_WIDTH), const2),
        ],
        out_specs=pl.BlockSpec((1, tm, D_IN), row),
        scratch_shapes=[pltpu.VMEM((tm, 2 * LANES), F32)],
        compiler_params=_cparams(("parallel", "parallel")),
        name="inproj",
    )(x, mod3, w_in, b_in, rope_r, rope_c, sgu_ln_w, sgu_ln_b)


def _dot_t0(a, b):
    return lax.dot_general(a, b, (((0,), (0,)), ((), ())), preferred_element_type=F32)


def _state_kernel(lg_ref, k_ref, v_ref, kc_ref, vc_ref, s_ref):
    hd = pl.program_id(1)
    lgf = lg_ref[0, hd]
    lgb = lg_ref[1, hd]
    n_chunks = s_ref.shape[2]
    n_ctx = kc_ref.shape[1]
    c = RET_CHUNK
    li = _iota_f((c, RET_QK_DIM), 0)
    zeta_f = jnp.exp((c - 1.0 - li) * lgf)
    zeta_b = jnp.exp(li * lgb)
    one = jnp.ones((1, 1), F32)
    cd_f = jnp.exp(one * (c * lgf))
    cd_b = jnp.exp(one * (c * lgb))

    tc = _iota_f((n_ctx, RET_QK_DIM), 0)
    kc = kc_ref[0].astype(F32)
    vc = vc_ref[0]
    s0f = _dot_t0((kc * jnp.exp((n_ctx - 1.0 - tc) * lgf)).astype(BF16), vc)
    s0b = _dot_t0((kc * jnp.exp(tc * lgb)).astype(BF16), vc)

    def chunk_kv(n, zeta):
        off = pl.multiple_of(n * c, c)
        kk = k_ref[0, pl.ds(off, c), :].astype(F32)
        return _dot_t0((kk * zeta).astype(BF16), v_ref[0, pl.ds(off, c), :])

    def step(i, carry):
        sf, sb = carry
        nb = n_chunks - 1 - i
        s_ref[0, 0, i, 0:RET_QK_DIM, :] = sf.astype(BF16)
        s_ref[0, 0, nb, RET_QK_DIM:2 * RET_QK_DIM, :] = sb.astype(BF16)
        return cd_f * sf + chunk_kv(i, zeta_f), cd_b * sb + chunk_kv(nb, zeta_b)

    lax.fori_loop(0, n_chunks, step, (s0f, s0b), unroll=STATE_UNROLL)


def _state_call(lg, z, kctx, vctx):
    bsz, t, _ = z.shape
    n_ctx = kctx.shape[1]
    n_chunks = t // RET_CHUNK
    return pl.pallas_call(
        _state_kernel,
        out_shape=jax.ShapeDtypeStruct((bsz, RET_HEADS, n_chunks, 2 * RET_QK_DIM, RET_V_DIM), BF16),
        grid=(bsz, RET_HEADS),
        in_specs=[
            pl.BlockSpec(memory_space=pltpu.SMEM),
            pl.BlockSpec((1, t, RET_QK_DIM), lambda b, h: (b, 0, RET_QK_W // RET_QK_DIM + h)),
            pl.BlockSpec((1, t, RET_V_DIM), lambda b, h: (b, 0, Z_V // RET_V_DIM + h)),
            pl.BlockSpec((1, n_ctx, RET_QK_DIM), lambda b, h: (b, 0, h)),
            pl.BlockSpec((1, n_ctx, RET_V_DIM), lambda b, h: (b, 0, h)),
        ],
        out_specs=pl.BlockSpec((1, 1, n_chunks, 2 * RET_QK_DIM, RET_V_DIM), lambda b, h: (b, h, 0, 0, 0)),
        compiler_params=_cparams(("parallel", "parallel")),
        name="states",
    )(lg, z, z, kctx, vctx)


def _mixer_kernel(lg_ref, za_ref, zb_ref, gb_ref, s_ref, x_ref, mod_ref,
                  wpa_ref, wpb_ref, wo_ref, sguw_ref, sgub_ref, gnw_ref, gnb_ref, bo_ref,
                  ln1w_ref, ln1b_ref, wr_ref, br_ref,
                  x1_ref, hm_ref, lgt_ref, ret_scr, sgu_scr):
    d = D_MODEL
    c = RET_CHUNK

    n_sub = za_ref.shape[1] // c
    row = lax.broadcasted_iota(I32, (c, c), 0)
    col = lax.broadcasted_iota(I32, (c, c), 1)
    diff = (row - col).astype(F32)
    rowq = _iota_f((c, RET_QK_DIM), 0)
    for hd in range(RET_HEADS):
        lgf = lg_ref[0, hd]
        lgb = lg_ref[1, hd]
        vcols = slice(hd * RET_V_DIM, (hd + 1) * RET_V_DIM)
        mask = (jnp.where(diff >= 0, jnp.exp(jnp.maximum(diff, 0.0) * lgf), 0.0)
                + jnp.where(diff <= 0, jnp.exp(jnp.maximum(-diff, 0.0) * lgb), 0.0))
        xi_f = jnp.exp((rowq + 1.0) * lgf)
        xi_b = jnp.exp((c - rowq) * lgb)
        for ci in range(n_sub):
            r0 = ci * c
            q = za_ref[0, r0:r0 + c, hd * RET_QK_DIM:(hd + 1) * RET_QK_DIM]
            k = za_ref[0, r0:r0 + c, RET_QK_W + hd * RET_QK_DIM:RET_QK_W + (hd + 1) * RET_QK_DIM]
            s = lax.dot_general(q, k, (((1,), (1,)), ((), ())), preferred_element_type=F32)
            vv = za_ref[0, r0:r0 + c, Z_V + hd * RET_V_DIM:Z_V + (hd + 1) * RET_V_DIM]
            intra = jnp.dot((s * mask).astype(BF16), vv, preferred_element_type=F32)
            qf = q.astype(F32)
            qx = jnp.concatenate([(qf * xi_f).astype(BF16), (qf * xi_b).astype(BF16)], axis=1)
            cross = jnp.dot(qx, s_ref[0, hd, ci], preferred_element_type=F32)
            o = _ln(intra + cross) * gnw_ref[:, vcols] + gnb_ref[:, vcols]
            gate = za_ref[0, r0:r0 + c, Z_SG + hd * RET_V_DIM:Z_SG + (hd + 1) * RET_V_DIM]
            ret_scr[r0:r0 + c, vcols] = (o * gate.astype(F32)).astype(BF16)

    for g in range(SGU_GROUPS):
        gcols = slice(g * SGU_GW, (g + 1) * SGU_GW)
        wg = sguw_ref[g]
        bg = sgub_ref[:, g:g + 1]
        for ci in range(n_sub):
            r0 = ci * c
            vsb = zb_ref[0, r0:r0 + c, Z_VS - Z_BLOCK + g * SGU_GW:Z_VS - Z_BLOCK + (g + 1) * SGU_GW]
            ub = zb_ref[0, r0:r0 + c, Z_U - Z_BLOCK + g * SGU_GW:Z_U - Z_BLOCK + (g + 1) * SGU_GW]
            sp = jnp.dot(wg, vsb, preferred_element_type=F32) + bg
            sgu_scr[r0:r0 + c, gcols] = (ub.astype(F32) * sp).astype(BF16)

    g1 = mod_ref[0, :, 2 * d:3 * d]
    sh2 = mod_ref[0, :, 3 * d:4 * d]
    sc2 = mod_ref[0, :, 4 * d:5 * d]
    for r0 in range(0, za_ref.shape[1], MERGE_ROWS):
        rows = slice(r0, r0 + MERGE_ROWS)
        pa = jnp.dot(ret_scr[rows, :], wpa_ref[...], preferred_element_type=F32)
        pb = jnp.dot(sgu_scr[rows, :], wpb_ref[...], preferred_element_type=F32)
        ga = zb_ref[0, rows, Z_GA - Z_BLOCK:Z_GA - Z_BLOCK + d]
        y = (ga.astype(F32) * pa + gb_ref[0, rows, :].astype(F32) * pb).astype(BF16)
        mix = jnp.dot(y, wo_ref[...], preferred_element_type=F32) + bo_ref[...]
        x1 = _ln(DEEPNORM_ALPHA * x_ref[0, rows, :] + g1 * mix) * ln1w_ref[...] + ln1b_ref[...]
        x1_ref[0, rows, :] = x1
        hm = (_ln(x1) * (1.0 + sc2) + sh2).astype(BF16)
        hm_ref[0, rows, :] = hm
        lgt_ref[0, :, rows] = lax.dot_general(wr_ref[...], hm, (((1,), (1,)), ((), ())),
                                              preferred_element_type=F32) + br_ref[...]


def _mixer_call(lg, z, states, x, mod3, wpa, wpb, wo, sguw, sgub_t, gnw, gnb, bo,
                ln1w, ln1b, wr_t, br):
    bsz, t, d = x.shape
    tt = TT_MIXER
    n_sub = tt // RET_CHUNK
    row = lambda b, i: (b, i, 0)
    c2 = lambda b, i: (0, 0)
    c3 = lambda b, i: (0, 0, 0)
    return pl.pallas_call(
        _mixer_kernel,
        out_shape=(jax.ShapeDtypeStruct((bsz, t, d), F32),
                   jax.ShapeDtypeStruct((bsz, t, d), BF16),
                   jax.ShapeDtypeStruct((bsz, N_EXPERTS, t), F32)),
        grid=(bsz, t // tt),
        in_specs=[
            pl.BlockSpec(memory_space=pltpu.SMEM),
            pl.BlockSpec((1, tt, Z_BLOCK), lambda b, i: (b, i, 0)),
            pl.BlockSpec((1, tt, Z_BLOCK), lambda b, i: (b, i, 1)),
            pl.BlockSpec((1, tt, d), lambda b, i: (b, i, Z_GB // d)),
            pl.BlockSpec((1, RET_HEADS, n_sub, 2 * RET_QK_DIM, RET_V_DIM), lambda b, i: (b, 0, i, 0, 0)),
            pl.BlockSpec((1, tt, d), row),
            pl.BlockSpec((1, 1, 6 * d), lambda b, i: (b, 0, 0)),
            pl.BlockSpec((RET_V_W, d), c2, pipeline_mode=pl.Buffered(1)),
            pl.BlockSpec((SGU_WIDTH, d), c2, pipeline_mode=pl.Buffered(1)),
            pl.BlockSpec((d, d), c2, pipeline_mode=pl.Buffered(1)),
            pl.BlockSpec((SGU_GROUPS, SGU_CHUNK, SGU_CHUNK), c3),
            pl.BlockSpec((SGU_CHUNK, SGU_GROUPS), c2),
            pl.BlockSpec((1, RET_V_W), c2),
            pl.BlockSpec((1, RET_V_W), c2),
            pl.BlockSpec((1, d), c2),
            pl.BlockSpec((1, d), c2),
            pl.BlockSpec((1, d), c2),
            pl.BlockSpec((N_EXPERTS, d), c2),
            pl.BlockSpec((N_EXPERTS, 1), c2),
        ],
        out_specs=(pl.BlockSpec((1, tt, d), row),
                   pl.BlockSpec((1, tt, d), row),
                   pl.BlockSpec((1, N_EXPERTS, tt), lambda b, i: (b, 0, i))),
        scratch_shapes=[pltpu.VMEM((tt, RET_V_W), BF16), pltpu.VMEM((tt, SGU_WIDTH), BF16)],
        compiler_params=_cparams(("parallel", "parallel")),
        name="mixer",
    )(lg, z, z, z, states, x, mod3, wpa, wpb, wo, sguw, sgub_t, gnw, gnb, bo,
      ln1w, ln1b, wr_t, br)


def _route_kernel(lgt_ref, pos_ref, gate_ref, wsd_ref, nxd_ref, anyd_ref, wsc_ref, nxc_ref, anyc_ref, *, cap):
    bsz, n_e, t = lgt_ref.shape
    tb = TB_ROUTE

    def softmax(lg):
        ex = jnp.exp(lg - jnp.max(lg, axis=0, keepdims=True))
        return ex / jnp.sum(ex, axis=0, keepdims=True)

    aff = jnp.concatenate([softmax(lgt_ref[bb]) for bb in range(bsz)], axis=0)
    n_rows = bsz * n_e

    def search(i, thr_bits):
        cand = thr_bits | lax.shift_left(jnp.int32(1), 30 - i)
        cnt = jnp.sum((aff >= lax.bitcast_convert_type(cand, F32)).astype(I32), axis=1, keepdims=True)
        return jnp.where(cnt >= cap, cand, thr_bits)

    thr_bits = lax.fori_loop(0, 31, search, jnp.zeros((n_rows, 1), I32))
    floor_f = lax.bitcast_convert_type(thr_bits, F32)
    thr = jnp.min(jnp.where(aff >= floor_f, aff, jnp.inf), axis=1, keepdims=True)
    need = (cap - jnp.sum((aff > thr).astype(I32), axis=1, keepdims=True)).astype(F32)

    r = lax.broadcasted_iota(I32, (tb, tb), 0)
    cc = lax.broadcasted_iota(I32, (tb, tb), 1)
    tri = (r <= cc).astype(BF16)
    carry_eq = jnp.zeros((n_rows, 1), F32)
    carry_sel = jnp.zeros((n_rows, 1), F32)
    nblk = t // tb
    merge = DISPATCH_MERGE
    blk_lane = lax.broadcasted_iota(I32, (n_rows, nblk), 1)
    zero = jnp.zeros((n_rows, nblk), F32)
    used_c, end_c, used_d, end_d = zero, zero, zero, zero
    for blk in range(nblk):
        sl = slice(blk * tb, (blk + 1) * tb)
        aff_b = aff[:, sl]
        eq = aff_b == thr
        eq_b = eq.astype(BF16)
        inc_eq = jnp.dot(eq_b, tri, preferred_element_type=F32)
        before = carry_eq + inc_eq - eq_b.astype(F32)
        sel = (aff_b > thr) | (eq & (before < need))
        sel_b = sel.astype(BF16)
        inc_sel = jnp.dot(sel_b, tri, preferred_element_type=F32)
        pos = jnp.where(sel, (carry_sel + inc_sel - 1.0).astype(I32), -1)
        for bb in range(bsz):
            rows = slice(bb * n_e, (bb + 1) * n_e)
            pos_ref[bb, blk] = pos[rows]
            gate_ref[bb, blk] = aff_b[rows]
        used_c = jnp.where(blk_lane == blk, carry_sel, used_c)
        if blk % merge == 0:
            used_d = jnp.where(blk_lane == blk // merge, carry_sel, used_d)
        carry_eq = carry_eq + inc_eq[:, tb - 1:tb]
        carry_sel = carry_sel + inc_sel[:, tb - 1:tb]
        end_c = jnp.where(blk_lane == blk, carry_sel, end_c)
        if blk % merge == merge - 1:
            end_d = jnp.where(blk_lane == blk // merge, carry_sel, end_d)

    def windows(used, end, win, groups, ws_ref, nx_ref, any_ref):
        n = ws_ref.shape[1]
        ws = jnp.minimum(jnp.floor(used[:, 0:n] * (1.0 / WIN_ALIGN)) * WIN_ALIGN, float(cap - win))
        nx = (end[:, 0:n] > ws + win).astype(I32)
        ws_ref[...] = ws.astype(I32)
        nx_ref[...] = nx
        rows_g = n_e // groups
        for r in range(bsz * groups):
            any_ref[r:r + 1, :] = jnp.max(nx[r * rows_g:(r + 1) * rows_g], axis=0, keepdims=True)

    windows(used_d, end_d, WIN_DISPATCH, DISPATCH_GROUPS, wsd_ref, nxd_ref, anyd_ref)
    windows(used_c, end_c, WIN_COMBINE, 1, wsc_ref, nxc_ref, anyc_ref)


def _route_call(logits_t, cap):
    bsz, n_e, t = logits_t.shape
    nblk = t // TB_ROUTE
    nwin = nblk // DISPATCH_MERGE
    small = [(bsz * n_e, nwin), (bsz * n_e, nwin), (bsz * DISPATCH_GROUPS, nwin),
             (bsz * n_e, nblk), (bsz * n_e, nblk), (bsz, nblk)]
    return pl.pallas_call(
        functools.partial(_route_kernel, cap=cap),
        out_shape=(jax.ShapeDtypeStruct((bsz, nblk, n_e, TB_ROUTE), I32),
                   jax.ShapeDtypeStruct((bsz, nblk, n_e, TB_ROUTE), F32))
        + tuple(jax.ShapeDtypeStruct(sh, I32) for sh in small),
        grid=(1,),
        in_specs=[pl.BlockSpec((bsz, n_e, t), lambda i: (0, 0, 0))],
        out_specs=(pl.BlockSpec((bsz, nblk, n_e, TB_ROUTE), lambda i: (0, 0, 0, 0)),
                   pl.BlockSpec((bsz, nblk, n_e, TB_ROUTE), lambda i: (0, 0, 0, 0)))
        + tuple(pl.BlockSpec(sh, lambda i: (0, 0)) for sh in small),
        compiler_params=_cparams(("arbitrary",)),
        name="route",
    )(logits_t)


def _window_rows(ws, win, n_tok):
    return lax.broadcasted_iota(I32, (win, n_tok), 0) + ws


def _sweep_window(w, cap, win, prow, rest):
    off = pl.multiple_of(jnp.minimum(w * win, cap - win), WIN_ALIGN)
    rows = _window_rows(off, win, prow.shape[-1])
    return off, (rows == prow) & rest & (rows >= w * win)


def _gather_kernel(ws_ref, nx_ref, any_ref, hm_ref, pos_ref, gate_ref, xs_ref, gc_ref):
    b = pl.program_id(0)
    g = pl.program_id(1)
    step = pl.program_id(2)
    merge = DISPATCH_MERGE
    eg, tb = pos_ref.shape[2:]
    n_sub = pos_ref.shape[1] // merge
    cap = xs_ref.shape[2]
    win = WIN_DISPATCH
    row0 = (b * pl.num_programs(1) + g) * eg

    @pl.when(step == 0)
    def _():
        xs_ref[...] = jnp.zeros_like(xs_ref)
        gc_ref[...] = jnp.zeros_like(gc_ref)

    def tokens(sub):
        return hm_ref[0, sub * merge * tb:(sub + 1) * merge * tb, :]

    def lanes(ref, sub, el):
        return jnp.concatenate([ref[0, sub * merge + m, el:el + 1, :] for m in range(merge)], axis=-1)

    for sub in range(n_sub):
        for el in range(eg):
            ws = pl.multiple_of(ws_ref[row0 + el, step * n_sub + sub], WIN_ALIGN)
            match = _window_rows(ws, win, merge * tb) == lanes(pos_ref, sub, el)
            sl = pl.ds(ws, win)
            xs_ref[0, el, sl, :] += jnp.dot(match.astype(BF16), tokens(sub),
                                            preferred_element_type=F32).astype(BF16)
            gc_ref[0, el, sl, :] += jnp.sum(jnp.where(match, lanes(gate_ref, sub, el), 0.0),
                                             axis=1, keepdims=True)

    for sub in range(n_sub):
        @pl.when(any_ref[b * pl.num_programs(1) + g, step * n_sub + sub] > 0)
        def _(sub=sub):
            for el in range(eg):
                @pl.when(nx_ref[row0 + el, step * n_sub + sub] > 0)
                def _(el=el):
                    prow = lanes(pos_ref, sub, el)
                    grow = lanes(gate_ref, sub, el)
                    rest = prow >= ws_ref[row0 + el, step * n_sub + sub] + win

                    def body(w, carry):
                        off, m = _sweep_window(w, cap, win, prow, rest)
                        xs_ref[0, el, pl.ds(off, win), :] += jnp.dot(m.astype(BF16), tokens(sub),
                                                                      preferred_element_type=F32).astype(BF16)
                        gc_ref[0, el, pl.ds(off, win), :] += jnp.sum(jnp.where(m, grow, 0.0), axis=1,
                                                                     keepdims=True)
                        return carry

                    lax.fori_loop(0, pl.cdiv(cap, win), body, 0)


def _gather_call(ws, nx, nx_any, hm, pos_b, gate_b, cap):
    bsz, t, d = hm.shape
    nblk, n_e, tb = pos_b.shape[1:]
    eg = n_e // DISPATCH_GROUPS
    blocks = DISPATCH_STEP_WINDOWS * DISPATCH_MERGE
    grid_spec = pltpu.PrefetchScalarGridSpec(
        num_scalar_prefetch=3,
        grid=(bsz, DISPATCH_GROUPS, nblk // blocks),
        in_specs=[
            pl.BlockSpec((1, blocks * tb, d), lambda b, g, k, ws, nx, fl: (b, k, 0)),
            pl.BlockSpec((1, blocks, eg, tb), lambda b, g, k, ws, nx, fl: (b, k, g, 0)),
            pl.BlockSpec((1, blocks, eg, tb), lambda b, g, k, ws, nx, fl: (b, k, g, 0)),
        ],
        out_specs=(pl.BlockSpec((1, eg, cap, d), lambda b, g, k, ws, nx, fl: (b, g, 0, 0)),
                   pl.BlockSpec((1, eg, cap, 1), lambda b, g, k, ws, nx, fl: (b, g, 0, 0))),
    )
    return pl.pallas_call(
        _gather_kernel,
        out_shape=(jax.ShapeDtypeStruct((bsz, n_e, cap, d), BF16),
                   jax.ShapeDtypeStruct((bsz, n_e, cap, 1), F32)),
        grid_spec=grid_spec,
        compiler_params=_cparams(("parallel", "parallel", "arbitrary")),
        name="gather",
    )(ws, nx, nx_any, hm, pos_b, gate_b)


def _ffn_kernel(xs_ref, gc_ref, wg_ref, wu_ref, wd_ref, ye_ref, wgu_s, wd_s):
    f = wd_s.shape[0]
    wgu_s[:, 0:f] = wg_ref[0].astype(BF16)
    wgu_s[:, f:2 * f] = wu_ref[0].astype(BF16)
    wd_s[...] = wd_ref[0].astype(BF16)

    rt = FFN_ROWS
    for b in range(xs_ref.shape[0]):
        for j in range(xs_ref.shape[2] // rt):
            sl = slice(j * rt, (j + 1) * rt)
            xgu = jnp.dot(xs_ref[b, 0, sl, :], wgu_s[...], preferred_element_type=F32)
            hid = (jax.nn.silu(xgu[:, 0:f]) * xgu[:, f:2 * f]).astype(BF16)
            ye = jnp.dot(hid, wd_s[...], preferred_element_type=F32) * gc_ref[b, 0, sl, :]
            ye_ref[b, 0, sl, :] = ye.astype(BF16)


def _ffn_call(xs, gc, w_gate, w_up, w_down):
    bsz, n_e, cap, d = xs.shape
    f = w_gate.shape[2]
    return pl.pallas_call(
        _ffn_kernel,
        out_shape=jax.ShapeDtypeStruct((bsz, n_e, cap, d), BF16),
        grid=(n_e,),
        in_specs=[
            pl.BlockSpec((bsz, 1, cap, d), lambda e: (0, e, 0, 0)),
            pl.BlockSpec((bsz, 1, cap, 1), lambda e: (0, e, 0, 0)),
            pl.BlockSpec((1, d, f), lambda e: (e, 0, 0)),
            pl.BlockSpec((1, d, f), lambda e: (e, 0, 0)),
            pl.BlockSpec((1, f, d), lambda e: (e, 0, 0)),
        ],
        out_specs=pl.BlockSpec((bsz, 1, cap, d), lambda e: (0, e, 0, 0)),
        scratch_shapes=[pltpu.VMEM((d, 2 * f), BF16), pltpu.VMEM((f, d), BF16)],
        compiler_params=_cparams(("parallel",)),
        name="ffn",
    )(xs, gc, w_gate, w_up, w_down)


def _combine_kernel(ws_ref, nx_ref, any_ref, ye_ref, pos_ref, x1_ref, mod_ref, lnw_ref, lnb_ref, out_ref, acc_scr):
    b = pl.program_id(0)
    step = pl.program_id(1)
    n_sub, n_e, tb = pos_ref.shape[1:]
    cap = ye_ref.shape[2]
    d = D_MODEL
    win = WIN_COMBINE
    g2 = mod_ref[0, :, 5 * d:6 * d]

    for sub in range(n_sub):
        blk = step * n_sub + sub
        starts = [pl.multiple_of(ws_ref[b * n_e + e, blk], WIN_ALIGN) for e in range(n_e)]
        onehot = jnp.concatenate(
            [(_window_rows(starts[e], win, tb) == pos_ref[0, sub, e:e + 1, :]).astype(BF16) for e in range(n_e)],
            axis=0)
        rows = jnp.concatenate([ye_ref[0, e, pl.ds(starts[e], win), :] for e in range(n_e)], axis=0)
        acc_scr[...] = _dot_t0(onehot, rows)

        @pl.when(any_ref[b, blk] > 0)
        def _(sub=sub, blk=blk, starts=starts):
            for e in range(n_e):
                @pl.when(nx_ref[b * n_e + e, blk] > 0)
                def _(e=e):
                    prow = pos_ref[0, sub, e:e + 1, :]
                    rest = prow >= starts[e] + win

                    def body(w, carry):
                        off, m = _sweep_window(w, cap, win, prow, rest)
                        acc_scr[...] += _dot_t0(m.astype(BF16), ye_ref[0, e, pl.ds(off, win), :])
                        return carry

                    lax.fori_loop(0, pl.cdiv(cap, win), body, 0)

        tok = slice(sub * tb, (sub + 1) * tb)
        out_ref[0, tok, :] = (_ln(DEEPNORM_ALPHA * x1_ref[0, tok, :] + g2 * acc_scr[...]) * lnw_ref[...]
                              + lnb_ref[...])


def _combine_call(ws, nx, nx_any, ye, pos_b, x1, mod3, ln2w, ln2b):
    bsz, n_e, cap, d = ye.shape
    nblk, _, tb = pos_b.shape[1:]
    n_sub = COMBINE_STEP_BLOCKS
    grid_spec = pltpu.PrefetchScalarGridSpec(
        num_scalar_prefetch=3,
        grid=(bsz, nblk // n_sub),
        in_specs=[
            pl.BlockSpec((1, n_e, cap, d), lambda b, k, ws, nx, fl: (b, 0, 0, 0), pipeline_mode=pl.Buffered(1)),
            pl.BlockSpec((1, n_sub, n_e, tb), lambda b, k, ws, nx, fl: (b, k, 0, 0)),
            pl.BlockSpec((1, n_sub * tb, d), lambda b, k, ws, nx, fl: (b, k, 0)),
            pl.BlockSpec((1, 1, 6 * d), lambda b, k, ws, nx, fl: (b, 0, 0)),
            pl.BlockSpec((1, d), lambda b, k, ws, nx, fl: (0, 0)),
            pl.BlockSpec((1, d), lambda b, k, ws, nx, fl: (0, 0)),
        ],
        out_specs=pl.BlockSpec((1, n_sub * tb, d), lambda b, k, ws, nx, fl: (b, k, 0)),
        scratch_shapes=[pltpu.VMEM((tb, d), F32)],
    )
    return pl.pallas_call(
        _combine_kernel,
        out_shape=jax.ShapeDtypeStruct((bsz, nblk * tb, d), F32),
        grid_spec=grid_spec,
        compiler_params=_cparams(("parallel", "arbitrary")),
        name="combine",
    )(ws, nx, nx_any, ye, pos_b, x1, mod3, ln2w, ln2b)


def _rope_tables(n_tokens):
    rows = n_tokens // GRID_W
    n_freq = RET_QK_DIM // 4
    inv = ROPE_THETA ** (-jnp.arange(n_freq, dtype=F32) / n_freq)
    ang_r = jnp.arange(rows, dtype=F32)[:, None] * inv
    ang_c = jnp.arange(GRID_W, dtype=F32)[:, None] * inv
    zr = jnp.zeros((rows, 2 * n_freq), F32)
    zc = jnp.zeros((GRID_W, 2 * n_freq), F32)
    cr, sr, cc, sc = jnp.cos(ang_r), jnp.sin(ang_r), jnp.cos(ang_c), jnp.sin(ang_c)
    return (jnp.concatenate([cr, cr, zr, -sr, sr, zr], axis=-1),
            jnp.concatenate([zc, cc, cc, zc, -sc, sc], axis=-1))


def kernel(x, c, ctx, c_ctx, w_ada, b_ada, w_in, b_in, ret_decay_f, ret_decay_b, ret_gn_w, ret_gn_b,
           sgu_ln_w, sgu_ln_b, sgu_w, sgu_b, w_pa, w_pb, w_o, b_o, ln1_w, ln1_b,
           w_router, b_router, w_gate, w_up, w_down, ln2_w, ln2_b):
    bsz, t, d = x.shape
    assert d == D_MODEL and w_ada.shape[0] == DEPTH == 1 and bsz < MOD_ROWS
    cap = EC_CAPACITY_FACTOR * t // N_EXPERTS
    for win in (WIN_DISPATCH, WIN_COMBINE):
        assert cap >= win and (cap - win) % WIN_ALIGN == 0 and win % WIN_ALIGN == 0
    assert t % (TB_ROUTE * DISPATCH_MERGE * DISPATCH_STEP_WINDOWS) == 0 and t % (TB_ROUTE * COMBINE_STEP_BLOCKS) == 0
    assert N_EXPERTS % DISPATCH_GROUPS == 0
    l = 0

    cvec = jnp.concatenate([c, c_ctx[None], jnp.zeros((MOD_ROWS - bsz - 1, d), F32)], axis=0)
    mod = _mod_call(cvec.T, w_ada[l], b_ada[l][None], bsz + 1)
    mod3 = mod[:, None, :]

    w_in_b = w_in[l].astype(BF16)
    b_in2 = b_in[l][None]
    kctx, vctx = _ctx_kv_call(ctx, mod3, w_in_b, b_in2, bsz)

    z = _inproj_call(x, mod3, w_in_b, b_in2, *_rope_tables(t), sgu_ln_w[l][None], sgu_ln_b[l][None])

    lg = jax.nn.log_sigmoid(jnp.stack([ret_decay_f[l], ret_decay_b[l]]).astype(F32))
    states = _state_call(lg, z, kctx, vctx)

    x1, hm, logits_t = _mixer_call(
        lg, z, states, x, mod3,
        w_pa[l].astype(BF16), w_pb[l].astype(BF16), w_o[l].astype(BF16),
        sgu_w[l].astype(BF16), sgu_b[l].T, ret_gn_w[l][None], ret_gn_b[l][None], b_o[l][None],
        ln1_w[l][None], ln1_b[l][None], w_router[l].T.astype(BF16), b_router[l][:, None])

    pos_b, gate_b, ws_d, nx_d, any_d, ws_c, nx_c, any_c = _route_call(logits_t, cap)
    xs, gc = _gather_call(ws_d, nx_d, any_d, hm, pos_b, gate_b, cap)
    ye = _ffn_call(xs, gc, w_gate[l], w_up[l], w_down[l])
    return _combine_call(ws_c, nx_c, any_c, ye, pos_b, x1, mod3, ln2_w[l][None], ln2_b[l][None])
```

```python
import functools
import math

import jax
import jax.numpy as jnp
from jax import lax
from jax.experimental import pallas as pl
from jax.experimental.pallas import tpu as pltpu

F32 = jnp.float32
BF16 = jnp.bfloat16
I32 = jnp.int32

D_MODEL = 1024
DEPTH = 1
GRID_W = 64
RET_HEADS = 4
RET_QK_DIM = 128
RET_V_DIM = 256
RET_CHUNK = 128
ROPE_THETA = 10000.0
RET_QK_W = RET_HEADS * RET_QK_DIM
RET_V_W = RET_HEADS * RET_V_DIM
SGU_GROUPS = 4
SGU_CHUNK = 128
SGU_WIDTH = 1024
SGU_GW = SGU_WIDTH // SGU_GROUPS
N_EXPERTS = 16
EC_CAPACITY_FACTOR = 2
LN_EPS = 1e-6
DEEPNORM_ALPHA = (2.0 * DEPTH) ** 0.25
D_IN = 2 * RET_QK_W + 2 * RET_V_W + 2 * SGU_WIDTH + 2 * D_MODEL
Z_K = RET_QK_W
Z_V = 2 * RET_QK_W
Z_SG = Z_V + RET_V_W
Z_U = Z_SG + RET_V_W
Z_VS = Z_U + SGU_WIDTH
Z_GA = Z_VS + SGU_WIDTH
Z_GB = Z_GA + D_MODEL
Z_BLOCK = 3 * 1024

V7X_VMEM_BYTES = 64 * 1024 * 1024
VMEM_LIMIT = V7X_VMEM_BYTES - 4 * 1024 * 1024
LANES = 128

MOD_ROWS = 8
MOD_COLS = 2048
TM_INPROJ = 1024
INPROJ_CHUNK = 256
TT_MIXER = 512
MERGE_ROWS = 512
TB_ROUTE = 256
DISPATCH_MERGE = 2
WIN_DISPATCH = 112
WIN_COMBINE = 64
WIN_ALIGN = 16
DISPATCH_GROUPS = 2
DISPATCH_STEP_WINDOWS = 4
COMBINE_STEP_BLOCKS = 4
STATE_UNROLL = 16
FFN_ROWS = 256


def _cparams(sem):
    return pltpu.CompilerParams(dimension_semantics=sem, vmem_limit_bytes=VMEM_LIMIT)


def _ln(x):
    mu = jnp.mean(x, axis=-1, keepdims=True)
    xc = x - mu
    var = jnp.mean(xc * xc, axis=-1, keepdims=True)
    return xc * lax.rsqrt(var + LN_EPS)


_GELU_A = -2.0 * math.sqrt(2.0 / math.pi) * math.log2(math.e)
_GELU_B = _GELU_A * 0.044715


def _gelu(x):
    return x / (1.0 + jnp.exp2(x * (_GELU_A + _GELU_B * (x * x))))


def _iota_f(shape, dim):
    return lax.broadcasted_iota(I32, shape, dim).astype(F32)


def _mod_kernel(ct_ref, w_ref, b_ref, o_ref, *, n_rows):
    c = ct_ref[...]
    a = c * jax.nn.sigmoid(c)
    w = w_ref[...]
    rows = [jnp.sum(w * a[:, m:m + 1], axis=0, keepdims=True) + b_ref[...] for m in range(n_rows)]
    rows += [jnp.zeros_like(rows[0])] * (MOD_ROWS - n_rows)
    o_ref[...] = jnp.concatenate(rows, axis=0)


def _mod_call(cvec_t, w_ada, b_ada, n_rows):
    d, n = w_ada.shape
    tn = MOD_COLS
    return pl.pallas_call(
        functools.partial(_mod_kernel, n_rows=n_rows),
        out_shape=jax.ShapeDtypeStruct((MOD_ROWS, n), F32),
        grid=(n // tn,),
        in_specs=[
            pl.BlockSpec((d, MOD_ROWS), lambda j: (0, 0)),
            pl.BlockSpec((d, tn), lambda j: (0, j)),
            pl.BlockSpec((1, tn), lambda j: (0, j)),
        ],
        out_specs=pl.BlockSpec((MOD_ROWS, tn), lambda j: (0, j)),
        compiler_params=_cparams(("parallel",)),
        name="mod",
    )(cvec_t, w_ada, b_ada)


def _ctx_kv_kernel(x_ref, mod_ref, wk_ref, wv_ref, bk_ref, bv_ref, k_ref, v_ref):
    d = D_MODEL
    sh = mod_ref[0, :, 0:d]
    sc = mod_ref[0, :, d:2 * d]
    h = (_ln(x_ref[0]) * (1.0 + sc) + sh).astype(BF16)
    k = jnp.dot(h, wk_ref[...], preferred_element_type=F32) + bk_ref[...]
    k_ref[0] = (k * (RET_QK_DIM ** -0.5)).astype(BF16)
    v_ref[0] = (jnp.dot(h, wv_ref[...], preferred_element_type=F32) + bv_ref[...]).astype(BF16)


def _ctx_kv_call(ctx, mod3, w_in, b_in, ctx_row):
    bsz, n_ctx, d = ctx.shape
    kblk = Z_K // RET_QK_W
    vblk = Z_V // RET_V_W
    return pl.pallas_call(
        _ctx_kv_kernel,
        out_shape=(jax.ShapeDtypeStruct((bsz, n_ctx, RET_QK_W), BF16),
                   jax.ShapeDtypeStruct((bsz, n_ctx, RET_V_W), BF16)),
        grid=(bsz,),
        in_specs=[
            pl.BlockSpec((1, n_ctx, d), lambda b: (b, 0, 0)),
            pl.BlockSpec((1, 1, 6 * d), lambda b: (ctx_row, 0, 0)),
            pl.BlockSpec((d, RET_QK_W), lambda b: (0, kblk)),
            pl.BlockSpec((d, RET_V_W), lambda b: (0, vblk)),
            pl.BlockSpec((1, RET_QK_W), lambda b: (0, kblk)),
            pl.BlockSpec((1, RET_V_W), lambda b: (0, vblk)),
        ],
        out_specs=(pl.BlockSpec((1, n_ctx, RET_QK_W), lambda b: (b, 0, 0)),
                   pl.BlockSpec((1, n_ctx, RET_V_W), lambda b: (b, 0, 0))),
        compiler_params=_cparams(("parallel",)),
        name="ctx_kv",
    )(ctx, mod3, w_in, w_in, b_in, b_in)


def _inproj_kernel(x_ref, mod_ref, w_ref, b_ref, rope_ref, lnw_ref, lnb_ref, z_ref):
    d = D_MODEL
    sh = mod_ref[0, :, 0:d]
    sc = mod_ref[0, :, d:2 * d]
    h = (_ln(x_ref[0]) * (1.0 + sc) + sh).astype(BF16)

    def proj(c0, width):
        return jnp.dot(h, w_ref[:, c0:c0 + width], preferred_element_type=F32) + b_ref[:, c0:c0 + width]

    zq = proj(0, 2 * RET_QK_W)
    cos = rope_ref[:, 0:LANES]
    sin = rope_ref[:, LANES:2 * LANES]
    tm = zq.shape[0]
    upper = (lax.broadcasted_iota(I32, (tm, LANES), 1) & 32) != 0
    for hb in range(2 * RET_HEADS):
        zs = zq[:, hb * LANES:(hb + 1) * LANES]
        if hb >= RET_HEADS:
            zs = zs * (RET_QK_DIM ** -0.5)
        sw = jnp.where(upper, pltpu.roll(zs, 32, 1), pltpu.roll(zs, LANES - 32, 1))
        z_ref[0, :, hb * LANES:(hb + 1) * LANES] = (zs * cos + sw * sin).astype(BF16)

    def emit(c0, width, fn):
        for cc in range(c0, c0 + width, INPROJ_CHUNK):
            z_ref[0, :, cc:cc + INPROJ_CHUNK] = fn(proj(cc, INPROJ_CHUNK)).astype(BF16)

    vs = _ln(_gelu(proj(Z_VS, SGU_WIDTH))) * lnw_ref[...] + lnb_ref[...]
    z_ref[0, :, Z_VS:Z_VS + SGU_WIDTH] = vs.astype(BF16)
    emit(Z_U, SGU_WIDTH, _gelu)
    emit(Z_SG, RET_V_W, jax.nn.silu)
    emit(Z_GA, d, jax.nn.sigmoid)
    emit(Z_GB, d, jax.nn.sigmoid)
    emit(Z_V, RET_V_W, lambda z: z)


def _inproj_call(x, mod3, w_in, b_in, rope_t, sgu_ln_w, sgu_ln_b):
    bsz, t, d = x.shape
    tm = TM_INPROJ
    row = lambda b, i: (b, i, 0)
    const2 = lambda b, i: (0, 0)
    return pl.pallas_call(
        _inproj_kernel,
        out_shape=jax.ShapeDtypeStruct((bsz, t, D_IN), BF16),
        grid=(bsz, t // tm),
        in_specs=[
            pl.BlockSpec((1, tm, d), row),
            pl.BlockSpec((1, 1, 6 * d), lambda b, i: (b, 0, 0)),
            pl.BlockSpec((d, D_IN), const2, pipeline_mode=pl.Buffered(1)),
            pl.BlockSpec((1, D_IN), const2),
            pl.BlockSpec((tm, 2 * LANES), lambda b, i: (i, 0)),
            pl.BlockSpec((1, SGU_WIDTH), const2),
            pl.BlockSpec((1, SGU_WIDTH), const2),
        ],
        out_specs=pl.BlockSpec((1, tm, D_IN), row),
        compiler_params=_cparams(("parallel", "parallel")),
        name="inproj",
    )(x, mod3, w_in, b_in, rope_t, sgu_ln_w, sgu_ln_b)


def _dot_t0(a, b):
    return lax.dot_general(a, b, (((0,), (0,)), ((), ())), preferred_element_type=F32)


def _state_kernel(lg_ref, k_ref, v_ref, kc_ref, vc_ref, s_ref):
    hd = pl.program_id(1)
    lgf = lg_ref[0, hd]
    lgb = lg_ref[1, hd]
    n_chunks = s_ref.shape[2]
    n_ctx = kc_ref.shape[1]
    c = RET_CHUNK
    li = _iota_f((c, RET_QK_DIM), 0)
    zeta_f = jnp.exp((c - 1.0 - li) * lgf)
    zeta_b = jnp.exp(li * lgb)
    one = jnp.ones((1, 1), F32)
    cd_f = jnp.exp(one * (c * lgf))
    cd_b = jnp.exp(one * (c * lgb))

    tc = _iota_f((n_ctx, RET_QK_DIM), 0)
    kc = kc_ref[0].astype(F32)
    vc = vc_ref[0]
    s0f = _dot_t0((kc * jnp.exp((n_ctx - 1.0 - tc) * lgf)).astype(BF16), vc)
    s0b = _dot_t0((kc * jnp.exp(tc * lgb)).astype(BF16), vc)

    def chunk_kv(n, zeta):
        off = pl.multiple_of(n * c, c)
        kk = k_ref[0, pl.ds(off, c), :].astype(F32)
        return _dot_t0((kk * zeta).astype(BF16), v_ref[0, pl.ds(off, c), :])

    def step(i, carry):
        sf, sb = carry
        nb = n_chunks - 1 - i
        s_ref[0, 0, i, 0:RET_QK_DIM, :] = sf.astype(BF16)
        s_ref[0, 0, nb, RET_QK_DIM:2 * RET_QK_DIM, :] = sb.astype(BF16)
        return cd_f * sf + chunk_kv(i, zeta_f), cd_b * sb + chunk_kv(nb, zeta_b)

    lax.fori_loop(0, n_chunks, step, (s0f, s0b), unroll=STATE_UNROLL)


def _state_call(lg, z, kctx, vctx):
    bsz, t, _ = z.shape
    n_ctx = kctx.shape[1]
    n_chunks = t // RET_CHUNK
    return pl.pallas_call(
        _state_kernel,
        out_shape=jax.ShapeDtypeStruct((bsz, RET_HEADS, n_chunks, 2 * RET_QK_DIM, RET_V_DIM), BF16),
        grid=(bsz, RET_HEADS),
        in_specs=[
            pl.BlockSpec(memory_space=pltpu.SMEM),
            pl.BlockSpec((1, t, RET_QK_DIM), lambda b, h: (b, 0, RET_QK_W // RET_QK_DIM + h)),
            pl.BlockSpec((1, t, RET_V_DIM), lambda b, h: (b, 0, Z_V // RET_V_DIM + h)),
            pl.BlockSpec((1, n_ctx, RET_QK_DIM), lambda b, h: (b, 0, h)),
            pl.BlockSpec((1, n_ctx, RET_V_DIM), lambda b, h: (b, 0, h)),
        ],
        out_specs=pl.BlockSpec((1, 1, n_chunks, 2 * RET_QK_DIM, RET_V_DIM), lambda b, h: (b, h, 0, 0, 0)),
        compiler_params=_cparams(("parallel", "parallel")),
        name="states",
    )(lg, z, z, kctx, vctx)


def _mixer_kernel(lg_ref, za_ref, zb_ref, gb_ref, s_ref, x_ref, mod_ref,
                  wpa_ref, wpb_ref, wo_ref, sguw_ref, sgub_ref, gnw_ref, gnb_ref, bo_ref,
                  ln1w_ref, ln1b_ref, wr_ref, br_ref,
                  x1_ref, hm_ref, lgt_ref, ret_scr, sgu_scr):
    d = D_MODEL
    c = RET_CHUNK

    n_sub = za_ref.shape[1] // c
    row = lax.broadcasted_iota(I32, (c, c), 0)
    col = lax.broadcasted_iota(I32, (c, c), 1)
    diff = (row - col).astype(F32)
    rowq = _iota_f((c, RET_QK_DIM), 0)
    for hd in range(RET_HEADS):
        lgf = lg_ref[0, hd]
        lgb = lg_ref[1, hd]
        vcols = slice(hd * RET_V_DIM, (hd + 1) * RET_V_DIM)
        mask = (jnp.where(diff >= 0, jnp.exp(jnp.maximum(diff, 0.0) * lgf), 0.0)
                + jnp.where(diff <= 0, jnp.exp(jnp.maximum(-diff, 0.0) * lgb), 0.0))
        xi_f = jnp.exp((rowq + 1.0) * lgf)
        xi_b = jnp.exp((c - rowq) * lgb)
        for ci in range(n_sub):
            r0 = ci * c
            q = za_ref[0, r0:r0 + c, hd * RET_QK_DIM:(hd + 1) * RET_QK_DIM]
            k = za_ref[0, r0:r0 + c, RET_QK_W + hd * RET_QK_DIM:RET_QK_W + (hd + 1) * RET_QK_DIM]
            s = lax.dot_general(q, k, (((1,), (1,)), ((), ())), preferred_element_type=F32)
            vv = za_ref[0, r0:r0 + c, Z_V + hd * RET_V_DIM:Z_V + (hd + 1) * RET_V_DIM]
            intra = jnp.dot((s * mask).astype(BF16), vv, preferred_element_type=F32)
            qf = q.astype(F32)
            qx = jnp.concatenate([(qf * xi_f).astype(BF16), (qf * xi_b).astype(BF16)], axis=1)
            cross = jnp.dot(qx, s_ref[0, hd, ci], preferred_element_type=F32)
            o = _ln(intra + cross) * gnw_ref[:, vcols] + gnb_ref[:, vcols]
            gate = za_ref[0, r0:r0 + c, Z_SG + hd * RET_V_DIM:Z_SG + (hd + 1) * RET_V_DIM]
            ret_scr[r0:r0 + c, vcols] = (o * gate.astype(F32)).astype(BF16)

    for g in range(SGU_GROUPS):
        gcols = slice(g * SGU_GW, (g + 1) * SGU_GW)
        wg = sguw_ref[g]
        bg = sgub_ref[:, g:g + 1]
        for ci in range(n_sub):
            r0 = ci * c
            vsb = zb_ref[0, r0:r0 + c, Z_VS - Z_BLOCK + g * SGU_GW:Z_VS - Z_BLOCK + (g + 1) * SGU_GW]
            ub = zb_ref[0, r0:r0 + c, Z_U - Z_BLOCK + g * SGU_GW:Z_U - Z_BLOCK + (g + 1) * SGU_GW]
            sp = jnp.dot(wg, vsb, preferred_element_type=F32) + bg
            sgu_scr[r0:r0 + c, gcols] = (ub.astype(F32) * sp).astype(BF16)

    g1 = mod_ref[0, :, 2 * d:3 * d]
    sh2 = mod_ref[0, :, 3 * d:4 * d]
    sc2 = mod_ref[0, :, 4 * d:5 * d]
    for r0 in range(0, za_ref.shape[1], MERGE_ROWS):
        rows = slice(r0, r0 + MERGE_ROWS)
        pa = jnp.dot(ret_scr[rows, :], wpa_ref[...], preferred_element_type=F32)
        pb = jnp.dot(sgu_scr[rows, :], wpb_ref[...], preferred_element_type=F32)
        ga = zb_ref[0, rows, Z_GA - Z_BLOCK:Z_GA - Z_BLOCK + d]
        y = (ga.astype(F32) * pa + gb_ref[0, rows, :].astype(F32) * pb).astype(BF16)
        mix = jnp.dot(y, wo_ref[...], preferred_element_type=F32) + bo_ref[...]
        x1 = _ln(DEEPNORM_ALPHA * x_ref[0, rows, :] + g1 * mix) * ln1w_ref[...] + ln1b_ref[...]
        x1_ref[0, rows, :] = x1
        hm = (_ln(x1) * (1.0 + sc2) + sh2).astype(BF16)
        hm_ref[0, rows, :] = hm
        lgt_ref[0, :, rows] = lax.dot_general(wr_ref[...], hm, (((1,), (1,)), ((), ())),
                                              preferred_element_type=F32) + br_ref[...]


def _mixer_call(lg, z, states, x, mod3, wpa, wpb, wo, sguw, sgub_t, gnw, gnb, bo,
                ln1w, ln1b, wr_t, br):
    bsz, t, d = x.shape
    tt = TT_MIXER
    n_sub = tt // RET_CHUNK
    row = lambda b, i: (b, i, 0)
    c2 = lambda b, i: (0, 0)
    c3 = lambda b, i: (0, 0, 0)
    return pl.pallas_call(
        _mixer_kernel,
        out_shape=(jax.ShapeDtypeStruct((bsz, t, d), F32),
                   jax.ShapeDtypeStruct((bsz, t, d), BF16),
                   jax.ShapeDtypeStruct((bsz, N_EXPERTS, t), F32)),
        grid=(bsz, t // tt),
        in_specs=[
            pl.BlockSpec(memory_space=pltpu.SMEM),
            pl.BlockSpec((1, tt, Z_BLOCK), lambda b, i: (b, i, 0)),
            pl.BlockSpec((1, tt, Z_BLOCK), lambda b, i: (b, i, 1)),
            pl.BlockSpec((1, tt, d), lambda b, i: (b, i, Z_GB // d)),
            pl.BlockSpec((1, RET_HEADS, n_sub, 2 * RET_QK_DIM, RET_V_DIM), lambda b, i: (b, 0, i, 0, 0)),
            pl.BlockSpec((1, tt, d), row),
            pl.BlockSpec((1, 1, 6 * d), lambda b, i: (b, 0, 0)),
            pl.BlockSpec((RET_V_W, d), c2, pipeline_mode=pl.Buffered(1)),
            pl.BlockSpec((SGU_WIDTH, d), c2, pipeline_mode=pl.Buffered(1)),
            pl.BlockSpec((d, d), c2, pipeline_mode=pl.Buffered(1)),
            pl.BlockSpec((SGU_GROUPS, SGU_CHUNK, SGU_CHUNK), c3),
            pl.BlockSpec((SGU_CHUNK, SGU_GROUPS), c2),
            pl.BlockSpec((1, RET_V_W), c2),
            pl.BlockSpec((1, RET_V_W), c2),
            pl.BlockSpec((1, d), c2),
            pl.BlockSpec((1, d), c2),
            pl.BlockSpec((1, d), c2),
            pl.BlockSpec((N_EXPERTS, d), c2),
            pl.BlockSpec((N_EXPERTS, 1), c2),
        ],
        out_specs=(pl.BlockSpec((1, tt, d), row),
                   pl.BlockSpec((1, tt, d), row),
                   pl.BlockSpec((1, N_EXPERTS, tt), lambda b, i: (b, 0, i))),
        scratch_shapes=[pltpu.VMEM((tt, RET_V_W), BF16), pltpu.VMEM((tt, SGU_WIDTH), BF16)],
        compiler_params=_cparams(("parallel", "parallel")),
        name="mixer",
    )(lg, z, z, z, states, x, mod3, wpa, wpb, wo, sguw, sgub_t, gnw, gnb, bo,
      ln1w, ln1b, wr_t, br)


def _route_kernel(lgt_ref, pos_ref, gate_ref, wsd_ref, nxd_ref, anyd_ref, wsc_ref, nxc_ref, anyc_ref, *, cap):
    bsz, n_e, t = lgt_ref.shape
    tb = TB_ROUTE

    def softmax(lg):
        ex = jnp.exp(lg - jnp.max(lg, axis=0, keepdims=True))
        return ex / jnp.sum(ex, axis=0, keepdims=True)

    aff = jnp.concatenate([softmax(lgt_ref[bb]) for bb in range(bsz)], axis=0)
    n_rows = bsz * n_e

    def search(i, thr_bits):
        cand = thr_bits | lax.shift_left(jnp.int32(1), 30 - i)
        cnt = jnp.sum((aff >= lax.bitcast_convert_type(cand, F32)).astype(I32), axis=1, keepdims=True)
        return jnp.where(cnt >= cap, cand, thr_bits)

    thr_bits = lax.fori_loop(0, 31, search, jnp.zeros((n_rows, 1), I32))
    floor_f = lax.bitcast_convert_type(thr_bits, F32)
    thr = jnp.min(jnp.where(aff >= floor_f, aff, jnp.inf), axis=1, keepdims=True)
    need = (cap - jnp.sum((aff > thr).astype(I32), axis=1, keepdims=True)).astype(F32)

    r = lax.broadcasted_iota(I32, (tb, tb), 0)
    cc = lax.broadcasted_iota(I32, (tb, tb), 1)
    tri = (r <= cc).astype(BF16)
    carry_eq = jnp.zeros((n_rows, 1), F32)
    carry_sel = jnp.zeros((n_rows, 1), F32)
    nblk = t // tb
    merge = DISPATCH_MERGE
    blk_lane = lax.broadcasted_iota(I32, (n_rows, nblk), 1)
    zero = jnp.zeros((n_rows, nblk), F32)
    used_c, end_c, used_d, end_d = zero, zero, zero, zero
    for blk in range(nblk):
        sl = slice(blk * tb, (blk + 1) * tb)
        aff_b = aff[:, sl]
        eq = aff_b == thr
        eq_b = eq.astype(BF16)
        inc_eq = jnp.dot(eq_b, tri, preferred_element_type=F32)
        before = carry_eq + inc_eq - eq_b.astype(F32)
        sel = (aff_b > thr) | (eq & (before < need))
        sel_b = sel.astype(BF16)
        inc_sel = jnp.dot(sel_b, tri, preferred_element_type=F32)
        pos = jnp.where(sel, (carry_sel + inc_sel - 1.0).astype(I32), -1)
        for bb in range(bsz):
            rows = slice(bb * n_e, (bb + 1) * n_e)
            pos_ref[bb, blk] = pos[rows]
            gate_ref[bb, blk] = aff_b[rows]
        used_c = jnp.where(blk_lane == blk, carry_sel, used_c)
        if blk % merge == 0:
            used_d = jnp.where(blk_lane == blk // merge, carry_sel, used_d)
        carry_eq = carry_eq + inc_eq[:, tb - 1:tb]
        carry_sel = carry_sel + inc_sel[:, tb - 1:tb]
        end_c = jnp.where(blk_lane == blk, carry_sel, end_c)
        if blk % merge == merge - 1:
            end_d = jnp.where(blk_lane == blk // merge, carry_sel, end_d)

    def windows(used, end, win, groups, ws_ref, nx_ref, any_ref):
        n = ws_ref.shape[1]
        ws = jnp.minimum(jnp.floor(used[:, 0:n] * (1.0 / WIN_ALIGN)) * WIN_ALIGN, float(cap - win))
        nx = (end[:, 0:n] > ws + win).astype(I32)
        ws_ref[...] = ws.astype(I32)
        nx_ref[...] = nx
        rows_g = n_e // groups
        for r in range(bsz * groups):
            any_ref[r:r + 1, :] = jnp.max(nx[r * rows_g:(r + 1) * rows_g], axis=0, keepdims=True)

    windows(used_d, end_d, WIN_DISPATCH, DISPATCH_GROUPS, wsd_ref, nxd_ref, anyd_ref)
    windows(used_c, end_c, WIN_COMBINE, 1, wsc_ref, nxc_ref, anyc_ref)


def _route_call(logits_t, cap):
    bsz, n_e, t = logits_t.shape
    nblk = t // TB_ROUTE
    nwin = nblk // DISPATCH_MERGE
    small = [(bsz * n_e, nwin), (bsz * n_e, nwin), (bsz * DISPATCH_GROUPS, nwin),
             (bsz * n_e, nblk), (bsz * n_e, nblk), (bsz, nblk)]
    return pl.pallas_call(
        functools.partial(_route_kernel, cap=cap),
        out_shape=(jax.ShapeDtypeStruct((bsz, nblk, n_e, TB_ROUTE), I32),
                   jax.ShapeDtypeStruct((bsz, nblk, n_e, TB_ROUTE), F32))
        + tuple(jax.ShapeDtypeStruct(sh, I32) for sh in small),
        grid=(1,),
        in_specs=[pl.BlockSpec((bsz, n_e, t), lambda i: (0, 0, 0))],
        out_specs=(pl.BlockSpec((bsz, nblk, n_e, TB_ROUTE), lambda i: (0, 0, 0, 0)),
                   pl.BlockSpec((bsz, nblk, n_e, TB_ROUTE), lambda i: (0, 0, 0, 0)))
        + tuple(pl.BlockSpec(sh, lambda i: (0, 0)) for sh in small),
        compiler_params=_cparams(("arbitrary",)),
        name="route",
    )(logits_t)


def _window_rows(ws, win, n_tok):
    return lax.broadcasted_iota(I32, (win, n_tok), 0) + ws


def _sweep_window(w, cap, win, prow, rest):
    off = pl.multiple_of(jnp.minimum(w * win, cap - win), WIN_ALIGN)
    rows = _window_rows(off, win, prow.shape[-1])
    return off, (rows == prow) & rest & (rows >= w * win)


def _gather_kernel(ws_ref, nx_ref, any_ref, hm_ref, pos_ref, gate_ref, xs_ref, gc_ref):
    b = pl.program_id(0)
    g = pl.program_id(1)
    step = pl.program_id(2)
    merge = DISPATCH_MERGE
    eg, tb = pos_ref.shape[2:]
    n_sub = pos_ref.shape[1] // merge
    cap = xs_ref.shape[2]
    win = WIN_DISPATCH
    row0 = (b * pl.num_programs(1) + g) * eg

    @pl.when(step == 0)
    def _():
        xs_ref[...] = jnp.zeros_like(xs_ref)
        gc_ref[...] = jnp.zeros_like(gc_ref)

    def tokens(sub):
        return hm_ref[0, sub * merge * tb:(sub + 1) * merge * tb, :]

    def lanes(ref, sub, el):
        return jnp.concatenate([ref[0, sub * merge + m, el:el + 1, :] for m in range(merge)], axis=-1)

    for sub in range(n_sub):
        for el in range(eg):
            ws = pl.multiple_of(ws_ref[row0 + el, step * n_sub + sub], WIN_ALIGN)
            match = _window_rows(ws, win, merge * tb) == lanes(pos_ref, sub, el)
            sl = pl.ds(ws, win)
            xs_ref[0, el, sl, :] += jnp.dot(match.astype(BF16), tokens(sub),
                                            preferred_element_type=F32).astype(BF16)
            gc_ref[0, el, sl, :] += jnp.sum(jnp.where(match, lanes(gate_ref, sub, el), 0.0),
                                             axis=1, keepdims=True)

    for sub in range(n_sub):
        @pl.when(any_ref[b * pl.num_programs(1) + g, step * n_sub + sub] > 0)
        def _(sub=sub):
            for el in range(eg):
                @pl.when(nx_ref[row0 + el, step * n_sub + sub] > 0)
                def _(el=el):
                    prow = lanes(pos_ref, sub, el)
                    grow = lanes(gate_ref, sub, el)
                    rest = prow >= ws_ref[row0 + el, step * n_sub + sub] + win

                    def body(w, carry):
                        off, m = _sweep_window(w, cap, win, prow, rest)
                        xs_ref[0, el, pl.ds(off, win), :] += jnp.dot(m.astype(BF16), tokens(sub),
                                                                      preferred_element_type=F32).astype(BF16)
                        gc_ref[0, el, pl.ds(off, win), :] += jnp.sum(jnp.where(m, grow, 0.0), axis=1,
                                                                     keepdims=True)
                        return carry

                    lax.fori_loop(0, pl.cdiv(cap, win), body, 0)


def _gather_call(ws, nx, nx_any, hm, pos_b, gate_b, cap):
    bsz, t, d = hm.shape
    nblk, n_e, tb = pos_b.shape[1:]
    eg = n_e // DISPATCH_GROUPS
    blocks = DISPATCH_STEP_WINDOWS * DISPATCH_MERGE
    grid_spec = pltpu.PrefetchScalarGridSpec(
        num_scalar_prefetch=3,
        grid=(bsz, DISPATCH_GROUPS, nblk // blocks),
        in_specs=[
            pl.BlockSpec((1, blocks * tb, d), lambda b, g, k, ws, nx, fl: (b, k, 0)),
            pl.BlockSpec((1, blocks, eg, tb), lambda b, g, k, ws, nx, fl: (b, k, g, 0)),
            pl.BlockSpec((1, blocks, eg, tb), lambda b, g, k, ws, nx, fl: (b, k, g, 0)),
        ],
        out_specs=(pl.BlockSpec((1, eg, cap, d), lambda b, g, k, ws, nx, fl: (b, g, 0, 0)),
                   pl.BlockSpec((1, eg, cap, 1), lambda b, g, k, ws, nx, fl: (b, g, 0, 0))),
    )
    return pl.pallas_call(
        _gather_kernel,
        out_shape=(jax.ShapeDtypeStruct((bsz, n_e, cap, d), BF16),
                   jax.ShapeDtypeStruct((bsz, n_e, cap, 1), F32)),
        grid_spec=grid_spec,
        compiler_params=_cparams(("parallel", "parallel", "arbitrary")),
        name="gather",
    )(ws, nx, nx_any, hm, pos_b, gate_b)


def _ffn_kernel(xs_ref, gc_ref, wg_ref, wu_ref, wd_ref, ye_ref, wgu_s, wd_s):
    f = wd_s.shape[0]
    wgu_s[:, 0:f] = wg_ref[0].astype(BF16)
    wgu_s[:, f:2 * f] = wu_ref[0].astype(BF16)
    wd_s[...] = wd_ref[0].astype(BF16)

    rt = FFN_ROWS
    for b in range(xs_ref.shape[0]):
        for j in range(xs_ref.shape[2] // rt):
            sl = slice(j * rt, (j + 1) * rt)
            xgu = jnp.dot(xs_ref[b, 0, sl, :], wgu_s[...], preferred_element_type=F32)
            hid = (jax.nn.silu(xgu[:, 0:f]) * xgu[:, f:2 * f]).astype(BF16)
            ye = jnp.dot(hid, wd_s[...], preferred_element_type=F32) * gc_ref[b, 0, sl, :]
            ye_ref[b, 0, sl, :] = ye.astype(BF16)


def _ffn_call(xs, gc, w_gate, w_up, w_down):
    bsz, n_e, cap, d = xs.shape
    f = w_gate.shape[2]
    return pl.pallas_call(
        _ffn_kernel,
        out_shape=jax.ShapeDtypeStruct((bsz, n_e, cap, d), BF16),
        grid=(n_e,),
        in_specs=[
            pl.BlockSpec((bsz, 1, cap, d), lambda e: (0, e, 0, 0)),
            pl.BlockSpec((bsz, 1, cap, 1), lambda e: (0, e, 0, 0)),
            pl.BlockSpec((1, d, f), lambda e: (e, 0, 0)),
            pl.BlockSpec((1, d, f), lambda e: (e, 0, 0)),
            pl.BlockSpec((1, f, d), lambda e: (e, 0, 0)),
        ],
        out_specs=pl.BlockSpec((bsz, 1, cap, d), lambda e: (0, e, 0, 0)),
        scratch_shapes=[pltpu.VMEM((d, 2 * f), BF16), pltpu.VMEM((f, d), BF16)],
        compiler_params=_cparams(("parallel",)),
        name="ffn",
    )(xs, gc, w_gate, w_up, w_down)


def _combine_kernel(ws_ref, nx_ref, any_ref, ye_ref, pos_ref, x1_ref, mod_ref, lnw_ref, lnb_ref, out_ref, acc_scr):
    b = pl.program_id(0)
    step = pl.program_id(1)
    n_sub, n_e, tb = pos_ref.shape[1:]
    cap = ye_ref.shape[2]
    d = D_MODEL
    win = WIN_COMBINE
    g2 = mod_ref[0, :, 5 * d:6 * d]

    for sub in range(n_sub):
        blk = step * n_sub + sub
        starts = [pl.multiple_of(ws_ref[b * n_e + e, blk], WIN_ALIGN) for e in range(n_e)]
        onehot = jnp.concatenate(
            [(_window_rows(starts[e], win, tb) == pos_ref[0, sub, e:e + 1, :]).astype(BF16) for e in range(n_e)],
            axis=0)
        rows = jnp.concatenate([ye_ref[0, e, pl.ds(starts[e], win), :] for e in range(n_e)], axis=0)
        acc_scr[...] = _dot_t0(onehot, rows)

        @pl.when(any_ref[b, blk] > 0)
        def _(sub=sub, blk=blk, starts=starts):
            for e in range(n_e):
                @pl.when(nx_ref[b * n_e + e, blk] > 0)
                def _(e=e):
                    prow = pos_ref[0, sub, e:e + 1, :]
                    rest = prow >= starts[e] + win

                    def body(w, carry):
                        off, m = _sweep_window(w, cap, win, prow, rest)
                        acc_scr[...] += _dot_t0(m.astype(BF16), ye_ref[0, e, pl.ds(off, win), :])
                        return carry

                    lax.fori_loop(0, pl.cdiv(cap, win), body, 0)

        tok = slice(sub * tb, (sub + 1) * tb)
        out_ref[0, tok, :] = (_ln(DEEPNORM_ALPHA * x1_ref[0, tok, :] + g2 * acc_scr[...]) * lnw_ref[...]
                              + lnb_ref[...])


def _combine_call(ws, nx, nx_any, ye, pos_b, x1, mod3, ln2w, ln2b):
    bsz, n_e, cap, d = ye.shape
    nblk, _, tb = pos_b.shape[1:]
    n_sub = COMBINE_STEP_BLOCKS
    grid_spec = pltpu.PrefetchScalarGridSpec(
        num_scalar_prefetch=3,
        grid=(bsz, nblk // n_sub),
        in_specs=[
            pl.BlockSpec((1, n_e, cap, d), lambda b, k, ws, nx, fl: (b, 0, 0, 0), pipeline_mode=pl.Buffered(1)),
            pl.BlockSpec((1, n_sub, n_e, tb), lambda b, k, ws, nx, fl: (b, k, 0, 0)),
            pl.BlockSpec((1, n_sub * tb, d), lambda b, k, ws, nx, fl: (b, k, 0)),
            pl.BlockSpec((1, 1, 6 * d), lambda b, k, ws, nx, fl: (b, 0, 0)),
            pl.BlockSpec((1, d), lambda b, k, ws, nx, fl: (0, 0)),
            pl.BlockSpec((1, d), lambda b, k, ws, nx, fl: (0, 0)),
        ],
        out_specs=pl.BlockSpec((1, n_sub * tb, d), lambda b, k, ws, nx, fl: (b, k, 0)),
        scratch_shapes=[pltpu.VMEM((tb, d), F32)],
    )
    return pl.pallas_call(
        _combine_kernel,
        out_shape=jax.ShapeDtypeStruct((bsz, nblk * tb, d), F32),
        grid_spec=grid_spec,
        compiler_params=_cparams(("parallel", "arbitrary")),
        name="combine",
    )(ws, nx, nx_any, ye, pos_b, x1, mod3, ln2w, ln2b)


def _rope_tables(n_tokens):
    rows = n_tokens // GRID_W
    n_freq = RET_QK_DIM // 4
    inv = ROPE_THETA ** (-jnp.arange(n_freq, dtype=F32) / n_freq)
    ang_r = jnp.arange(rows, dtype=F32)[:, None] * inv
    ang_c = jnp.arange(GRID_W, dtype=F32)[:, None] * inv
    zr = jnp.zeros((rows, 2 * n_freq), F32)
    zc = jnp.zeros((GRID_W, 2 * n_freq), F32)
    cr, sr, cc, sc = jnp.cos(ang_r), jnp.sin(ang_r), jnp.cos(ang_c), jnp.sin(ang_c)
    cos_t = (jnp.concatenate([cr, cr, zr], axis=-1)[:, None, :]
             + jnp.concatenate([zc, cc, cc], axis=-1)[None, :, :]).reshape(n_tokens, RET_QK_DIM)
    sin_t = (jnp.concatenate([-sr, sr, zr], axis=-1)[:, None, :]
             + jnp.concatenate([zc, -sc, sc], axis=-1)[None, :, :]).reshape(n_tokens, RET_QK_DIM)
    return jnp.concatenate([cos_t, sin_t], axis=-1)


def kernel(x, c, ctx, c_ctx, w_ada, b_ada, w_in, b_in, ret_decay_f, ret_decay_b, ret_gn_w, ret_gn_b,
           sgu_ln_w, sgu_ln_b, sgu_w, sgu_b, w_pa, w_pb, w_o, b_o, ln1_w, ln1_b,
           w_router, b_router, w_gate, w_up, w_down, ln2_w, ln2_b):
    bsz, t, d = x.shape
    assert d == D_MODEL and w_ada.shape[0] == DEPTH == 1 and bsz < MOD_ROWS
    cap = EC_CAPACITY_FACTOR * t // N_EXPERTS
    for win in (WIN_DISPATCH, WIN_COMBINE):
        assert cap >= win and (cap - win) % WIN_ALIGN == 0 and win % WIN_ALIGN == 0
    assert t % (TB_ROUTE * DISPATCH_MERGE * DISPATCH_STEP_WINDOWS) == 0 and t % (TB_ROUTE * COMBINE_STEP_BLOCKS) == 0
    assert N_EXPERTS % DISPATCH_GROUPS == 0
    l = 0

    cvec = jnp.concatenate([c, c_ctx[None], jnp.zeros((MOD_ROWS - bsz - 1, d), F32)], axis=0)
    mod = _mod_call(cvec.T, w_ada[l], b_ada[l][None], bsz + 1)
    mod3 = mod[:, None, :]

    w_in_b = w_in[l].astype(BF16)
    b_in2 = b_in[l][None]
    kctx, vctx = _ctx_kv_call(ctx, mod3, w_in_b, b_in2, bsz)

    z = _inproj_call(x, mod3, w_in_b, b_in2, _rope_tables(t), sgu_ln_w[l][None], sgu_ln_b[l][None])

    lg = jnp.stack([jax.nn.log_sigmoid(ret_decay_f[l].astype(F32)),
                    jax.nn.log_sigmoid(ret_decay_b[l].astype(F32))])
    states = _state_call(lg, z, kctx, vctx)

    x1, hm, logits_t = _mixer_call(
        lg, z, states, x, mod3,
        w_pa[l].astype(BF16), w_pb[l].astype(BF16), w_o[l].astype(BF16),
        sgu_w[l].astype(BF16), sgu_b[l].T, ret_gn_w[l][None], ret_gn_b[l][None], b_o[l][None],
        ln1_w[l][None], ln1_b[l][None], w_router[l].T.astype(BF16), b_router[l][:, None])

    pos_b, gate_b, ws_d, nx_d, any_d, ws_c, nx_c, any_c = _route_call(logits_t, cap)
    xs, gc = _gather_call(ws_d, nx_d, any_d, hm, pos_b, gate_b, cap)
    ye = _ffn_call(xs, gc, w_gate[l], w_up[l], w_down[l])
    return _combine_call(ws_c, nx_c, any_c, ye, pos_b, x1, mod3, ln2_w[l][None], ln2_b[l][None])
```

```python
import functools
import math

import jax
import jax.numpy as jnp
from jax import lax
from jax.experimental import pallas as pl
from jax.experimental.pallas import tpu as pltpu

F32 = jnp.float32
BF16 = jnp.bfloat16
I32 = jnp.int32

D_MODEL = 1024
DEPTH = 1
GRID_W = 64
RET_HEADS = 4
RET_QK_DIM = 128
RET_V_DIM = 256
RET_CHUNK = 128
ROPE_THETA = 10000.0
RET_QK_W = RET_HEADS * RET_QK_DIM
RET_V_W = RET_HEADS * RET_V_DIM
SGU_GROUPS = 4
SGU_CHUNK = 128
SGU_WIDTH = 1024
SGU_GW = SGU_WIDTH // SGU_GROUPS
N_EXPERTS = 16
EC_CAPACITY_FACTOR = 2
LN_EPS = 1e-6
DEEPNORM_ALPHA = (2.0 * DEPTH) ** 0.25
D_IN = 2 * RET_QK_W + 2 * RET_V_W + 2 * SGU_WIDTH + 2 * D_MODEL
Z_K = RET_QK_W
Z_V = 2 * RET_QK_W
Z_SG = Z_V + RET_V_W
Z_U = Z_SG + RET_V_W
Z_VS = Z_U + SGU_WIDTH
Z_GA = Z_VS + SGU_WIDTH
Z_GB = Z_GA + D_MODEL
Z_BLOCK = 3 * 1024

V7X_VMEM_BYTES = 64 * 1024 * 1024
VMEM_LIMIT = V7X_VMEM_BYTES - 4 * 1024 * 1024
LANES = 128

MOD_ROWS = 8
MOD_COLS = 2048
TM_INPROJ = 1024
INPROJ_CHUNK = 256
TT_MIXER = 512
MERGE_ROWS = 512
TB_ROUTE = 256
DISPATCH_MERGE = 2
STATE_RING = 3
WIN_DISPATCH = 128
WIN_COMBINE = 64
WIN_ALIGN = 16
DISPATCH_GROUPS = 2
DISPATCH_STEP_WINDOWS = 4
COMBINE_STEP_BLOCKS = 4
STATE_UNROLL = 16
FFN_ROWS = 256


def _cparams(sem):
    return pltpu.CompilerParams(dimension_semantics=sem, vmem_limit_bytes=VMEM_LIMIT)


def _ln(x):
    mu = jnp.mean(x, axis=-1, keepdims=True)
    xc = x - mu
    var = jnp.mean(xc * xc, axis=-1, keepdims=True)
    return xc * lax.rsqrt(var + LN_EPS)


_GELU_A = -2.0 * math.sqrt(2.0 / math.pi) * math.log2(math.e)
_GELU_B = _GELU_A * 0.044715


def _gelu(x):
    return x / (1.0 + jnp.exp2(x * (_GELU_A + _GELU_B * (x * x))))


def _iota_f(shape, dim):
    return lax.broadcasted_iota(I32, shape, dim).astype(F32)


def _mod_kernel(ct_ref, w_ref, b_ref, o_ref, *, n_rows):
    c = ct_ref[...]
    a = c * jax.nn.sigmoid(c)
    w = w_ref[...]
    rows = [jnp.sum(w * a[:, m:m + 1], axis=0, keepdims=True) + b_ref[...] for m in range(n_rows)]
    rows += [jnp.zeros_like(rows[0])] * (MOD_ROWS - n_rows)
    o_ref[...] = jnp.concatenate(rows, axis=0)


def _mod_call(cvec_t, w_ada, b_ada, n_rows):
    d, n = w_ada.shape
    tn = MOD_COLS
    return pl.pallas_call(
        functools.partial(_mod_kernel, n_rows=n_rows),
        out_shape=jax.ShapeDtypeStruct((MOD_ROWS, n), F32),
        grid=(n // tn,),
        in_specs=[
            pl.BlockSpec((d, MOD_ROWS), lambda j: (0, 0)),
            pl.BlockSpec((d, tn), lambda j: (0, j)),
            pl.BlockSpec((1, tn), lambda j: (0, j)),
        ],
        out_specs=pl.BlockSpec((MOD_ROWS, tn), lambda j: (0, j)),
        compiler_params=_cparams(("parallel",)),
        name="mod",
    )(cvec_t, w_ada, b_ada)


def _ctx_kv_kernel(x_ref, mod_ref, wk_ref, wv_ref, bk_ref, bv_ref, k_ref, v_ref):
    d = D_MODEL
    sh = mod_ref[0, :, 0:d]
    sc = mod_ref[0, :, d:2 * d]
    h = (_ln(x_ref[0]) * (1.0 + sc) + sh).astype(BF16)
    k = jnp.dot(h, wk_ref[...], preferred_element_type=F32) + bk_ref[...]
    k_ref[0] = (k * (RET_QK_DIM ** -0.5)).astype(BF16)
    v_ref[0] = (jnp.dot(h, wv_ref[...], preferred_element_type=F32) + bv_ref[...]).astype(BF16)


def _ctx_kv_call(ctx, mod3, w_in, b_in, ctx_row):
    bsz, n_ctx, d = ctx.shape
    kblk = Z_K // RET_QK_W
    vblk = Z_V // RET_V_W
    return pl.pallas_call(
        _ctx_kv_kernel,
        out_shape=(jax.ShapeDtypeStruct((bsz, n_ctx, RET_QK_W), BF16),
                   jax.ShapeDtypeStruct((bsz, n_ctx, RET_V_W), BF16)),
        grid=(bsz,),
        in_specs=[
            pl.BlockSpec((1, n_ctx, d), lambda b: (b, 0, 0)),
            pl.BlockSpec((1, 1, 6 * d), lambda b: (ctx_row, 0, 0)),
            pl.BlockSpec((d, RET_QK_W), lambda b: (0, kblk)),
            pl.BlockSpec((d, RET_V_W), lambda b: (0, vblk)),
            pl.BlockSpec((1, RET_QK_W), lambda b: (0, kblk)),
            pl.BlockSpec((1, RET_V_W), lambda b: (0, vblk)),
        ],
        out_specs=(pl.BlockSpec((1, n_ctx, RET_QK_W), lambda b: (b, 0, 0)),
                   pl.BlockSpec((1, n_ctx, RET_V_W), lambda b: (b, 0, 0))),
        compiler_params=_cparams(("parallel",)),
        name="ctx_kv",
    )(ctx, mod3, w_in, w_in, b_in, b_in)


def _inproj_kernel(x_ref, mod_ref, w_ref, b_ref, rope_ref, lnw_ref, lnb_ref, z_ref):
    d = D_MODEL
    sh = mod_ref[0, :, 0:d]
    sc = mod_ref[0, :, d:2 * d]
    h = (_ln(x_ref[0]) * (1.0 + sc) + sh).astype(BF16)

    def proj(c0, width):
        return jnp.dot(h, w_ref[:, c0:c0 + width], preferred_element_type=F32) + b_ref[:, c0:c0 + width]

    zq = proj(0, 2 * RET_QK_W)
    cos = rope_ref[:, 0:LANES]
    sin = rope_ref[:, LANES:2 * LANES]
    tm = zq.shape[0]
    upper = (lax.broadcasted_iota(I32, (tm, LANES), 1) & 32) != 0
    for hb in range(2 * RET_HEADS):
        zs = zq[:, hb * LANES:(hb + 1) * LANES]
        if hb >= RET_HEADS:
            zs = zs * (RET_QK_DIM ** -0.5)
        sw = jnp.where(upper, pltpu.roll(zs, 32, 1), pltpu.roll(zs, LANES - 32, 1))
        z_ref[0, :, hb * LANES:(hb + 1) * LANES] = (zs * cos + sw * sin).astype(BF16)

    def emit(c0, width, fn):
        for cc in range(c0, c0 + width, INPROJ_CHUNK):
            z_ref[0, :, cc:cc + INPROJ_CHUNK] = fn(proj(cc, INPROJ_CHUNK)).astype(BF16)

    vs = _ln(_gelu(proj(Z_VS, SGU_WIDTH))) * lnw_ref[...] + lnb_ref[...]
    z_ref[0, :, Z_VS:Z_VS + SGU_WIDTH] = vs.astype(BF16)
    emit(Z_U, SGU_WIDTH, _gelu)
    emit(Z_SG, RET_V_W, jax.nn.silu)
    emit(Z_GA, d, jax.nn.sigmoid)
    emit(Z_GB, d, jax.nn.sigmoid)
    emit(Z_V, RET_V_W, lambda z: z)


def _inproj_call(x, mod3, w_in, b_in, rope_t, sgu_ln_w, sgu_ln_b):
    bsz, t, d = x.shape
    tm = TM_INPROJ
    row = lambda b, i: (b, i, 0)
    const2 = lambda b, i: (0, 0)
    return pl.pallas_call(
        _inproj_kernel,
        out_shape=jax.ShapeDtypeStruct((bsz, t, D_IN), BF16),
        grid=(bsz, t // tm),
        in_specs=[
            pl.BlockSpec((1, tm, d), row),
            pl.BlockSpec((1, 1, 6 * d), lambda b, i: (b, 0, 0)),
            pl.BlockSpec((d, D_IN), const2, pipeline_mode=pl.Buffered(1)),
            pl.BlockSpec((1, D_IN), const2),
            pl.BlockSpec((tm, 2 * LANES), lambda b, i: (i, 0)),
            pl.BlockSpec((1, SGU_WIDTH), const2),
            pl.BlockSpec((1, SGU_WIDTH), const2),
        ],
        out_specs=pl.BlockSpec((1, tm, D_IN), row),
        compiler_params=_cparams(("parallel", "parallel")),
        name="inproj",
    )(x, mod3, w_in, b_in, rope_t, sgu_ln_w, sgu_ln_b)


def _dot_t0(a, b):
    return lax.dot_general(a, b, (((0,), (0,)), ((), ())), preferred_element_type=F32)


def _state_kernel(lg_ref, k_ref, v_ref, kc_ref, vc_ref, s_ref):
    hd = pl.program_id(1)
    lgf = lg_ref[0, hd]
    lgb = lg_ref[1, hd]
    n_chunks = s_ref.shape[2]
    n_ctx = kc_ref.shape[1]
    c = RET_CHUNK
    li = _iota_f((c, RET_QK_DIM), 0)
    zeta_f = jnp.exp((c - 1.0 - li) * lgf)
    zeta_b = jnp.exp(li * lgb)
    one = jnp.ones((1, 1), F32)
    cd_f = jnp.exp(one * (c * lgf))
    cd_b = jnp.exp(one * (c * lgb))

    tc = _iota_f((n_ctx, RET_QK_DIM), 0)
    kc = kc_ref[0].astype(F32)
    vc = vc_ref[0]
    s0f = _dot_t0((kc * jnp.exp((n_ctx - 1.0 - tc) * lgf)).astype(BF16), vc)
    s0b = _dot_t0((kc * jnp.exp(tc * lgb)).astype(BF16), vc)

    def chunk_kv(n, zeta):
        off = pl.multiple_of(n * c, c)
        kk = k_ref[0, pl.ds(off, c), :].astype(F32)
        return _dot_t0((kk * zeta).astype(BF16), v_ref[0, pl.ds(off, c), :])

    def step(i, carry):
        sf, sb = carry
        nb = n_chunks - 1 - i
        s_ref[0, 0, i, 0:RET_QK_DIM, :] = sf.astype(BF16)
        s_ref[0, 0, nb, RET_QK_DIM:2 * RET_QK_DIM, :] = sb.astype(BF16)
        return cd_f * sf + chunk_kv(i, zeta_f), cd_b * sb + chunk_kv(nb, zeta_b)

    lax.fori_loop(0, n_chunks, step, (s0f, s0b), unroll=STATE_UNROLL)


def _state_call(lg, z, kctx, vctx):
    bsz, t, _ = z.shape
    n_ctx = kctx.shape[1]
    n_chunks = t // RET_CHUNK
    return pl.pallas_call(
        _state_kernel,
        out_shape=jax.ShapeDtypeStruct((bsz, RET_HEADS, n_chunks, 2 * RET_QK_DIM, RET_V_DIM), BF16),
        grid=(bsz, RET_HEADS),
        in_specs=[
            pl.BlockSpec(memory_space=pltpu.SMEM),
            pl.BlockSpec((1, t, RET_QK_DIM), lambda b, h: (b, 0, RET_QK_W // RET_QK_DIM + h)),
            pl.BlockSpec((1, t, RET_V_DIM), lambda b, h: (b, 0, Z_V // RET_V_DIM + h)),
            pl.BlockSpec((1, n_ctx, RET_QK_DIM), lambda b, h: (b, 0, h)),
            pl.BlockSpec((1, n_ctx, RET_V_DIM), lambda b, h: (b, 0, h)),
        ],
        out_specs=pl.BlockSpec((1, 1, n_chunks, 2 * RET_QK_DIM, RET_V_DIM), lambda b, h: (b, h, 0, 0, 0)),
        compiler_params=_cparams(("parallel", "parallel")),
        name="states",
    )(lg, z, z, kctx, vctx)


def _mixer_kernel(lg_ref, za_ref, zb_ref, gb_ref, s_ref, x_ref, mod_ref,
                  wpa_ref, wpb_ref, wo_ref, sguw_ref, sgub_ref, gnw_ref, gnb_ref, bo_ref,
                  ln1w_ref, ln1b_ref, wr_ref, br_ref,
                  x1_ref, hm_ref, lgt_ref, ret_scr, sgu_scr, s_buf, s_sem):
    d = D_MODEL
    c = RET_CHUNK
    n_sub = za_ref.shape[1] // c

    n_i = pl.num_programs(1)
    step = pl.program_id(0) * n_i + pl.program_id(1)
    n_steps = pl.num_programs(0) * n_i

    def state_copy(s):
        return pltpu.make_async_copy(s_ref.at[s // n_i, :, pl.ds((s % n_i) * n_sub, n_sub)],
                                     s_buf.at[s % STATE_RING], s_sem.at[s % STATE_RING])

    @pl.when(step == 0)
    def _():
        for s in range(STATE_RING - 1):
            state_copy(s).start()

    @pl.when(step + (STATE_RING - 1) < n_steps)
    def _():
        state_copy(step + (STATE_RING - 1)).start()

    state_copy(step).wait()
    slot = step % STATE_RING

    row = lax.broadcasted_iota(I32, (c, c), 0)
    col = lax.broadcasted_iota(I32, (c, c), 1)
    diff = (row - col).astype(F32)
    rowq = _iota_f((c, RET_QK_DIM), 0)
    for hd in range(RET_HEADS):
        lgf = lg_ref[0, hd]
        lgb = lg_ref[1, hd]
        vcols = slice(hd * RET_V_DIM, (hd + 1) * RET_V_DIM)
        mask = (jnp.where(diff >= 0, jnp.exp(jnp.maximum(diff, 0.0) * lgf), 0.0)
                + jnp.where(diff <= 0, jnp.exp(jnp.maximum(-diff, 0.0) * lgb), 0.0))
        xi_f = jnp.exp((rowq + 1.0) * lgf)
        xi_b = jnp.exp((c - rowq) * lgb)
        for ci in range(n_sub):
            r0 = ci * c
            q = za_ref[0, r0:r0 + c, hd * RET_QK_DIM:(hd + 1) * RET_QK_DIM]
            k = za_ref[0, r0:r0 + c, RET_QK_W + hd * RET_QK_DIM:RET_QK_W + (hd + 1) * RET_QK_DIM]
            s = lax.dot_general(q, k, (((1,), (1,)), ((), ())), preferred_element_type=F32)
            vv = za_ref[0, r0:r0 + c, Z_V + hd * RET_V_DIM:Z_V + (hd + 1) * RET_V_DIM]
            intra = jnp.dot((s * mask).astype(BF16), vv, preferred_element_type=F32)
            qf = q.astype(F32)
            qx = jnp.concatenate([(qf * xi_f).astype(BF16), (qf * xi_b).astype(BF16)], axis=1)
            cross = jnp.dot(qx, s_buf[slot, hd, ci], preferred_element_type=F32)
            o = _ln(intra + cross) * gnw_ref[:, vcols] + gnb_ref[:, vcols]
            gate = za_ref[0, r0:r0 + c, Z_SG + hd * RET_V_DIM:Z_SG + (hd + 1) * RET_V_DIM]
            ret_scr[r0:r0 + c, vcols] = (o * gate.astype(F32)).astype(BF16)

    for g in range(SGU_GROUPS):
        gcols = slice(g * SGU_GW, (g + 1) * SGU_GW)
        wg = sguw_ref[g]
        bg = sgub_ref[:, g:g + 1]
        for ci in range(n_sub):
            r0 = ci * c
            vsb = zb_ref[0, r0:r0 + c, Z_VS - Z_BLOCK + g * SGU_GW:Z_VS - Z_BLOCK + (g + 1) * SGU_GW]
            ub = zb_ref[0, r0:r0 + c, Z_U - Z_BLOCK + g * SGU_GW:Z_U - Z_BLOCK + (g + 1) * SGU_GW]
            sp = jnp.dot(wg, vsb, preferred_element_type=F32) + bg
            sgu_scr[r0:r0 + c, gcols] = (ub.astype(F32) * sp).astype(BF16)

    g1 = mod_ref[0, :, 2 * d:3 * d]
    sh2 = mod_ref[0, :, 3 * d:4 * d]
    sc2 = mod_ref[0, :, 4 * d:5 * d]
    for r0 in range(0, za_ref.shape[1], MERGE_ROWS):
        rows = slice(r0, r0 + MERGE_ROWS)
        pa = jnp.dot(ret_scr[rows, :], wpa_ref[...], preferred_element_type=F32)
        pb = jnp.dot(sgu_scr[rows, :], wpb_ref[...], preferred_element_type=F32)
        ga = zb_ref[0, rows, Z_GA - Z_BLOCK:Z_GA - Z_BLOCK + d]
        y = (ga.astype(F32) * pa + gb_ref[0, rows, :].astype(F32) * pb).astype(BF16)
        mix = jnp.dot(y, wo_ref[...], preferred_element_type=F32) + bo_ref[...]
        x1 = _ln(DEEPNORM_ALPHA * x_ref[0, rows, :] + g1 * mix) * ln1w_ref[...] + ln1b_ref[...]
        x1_ref[0, rows, :] = x1
        hm = (_ln(x1) * (1.0 + sc2) + sh2).astype(BF16)
        hm_ref[0, rows, :] = hm
        lgt_ref[0, :, rows] = lax.dot_general(wr_ref[...], hm, (((1,), (1,)), ((), ())),
                                              preferred_element_type=F32) + br_ref[...]


def _mixer_call(lg, z, states, x, mod3, wpa, wpb, wo, sguw, sgub_t, gnw, gnb, bo,
                ln1w, ln1b, wr_t, br):
    bsz, t, d = x.shape
    tt = TT_MIXER
    n_sub = tt // RET_CHUNK
    row = lambda b, i: (b, i, 0)
    c2 = lambda b, i: (0, 0)
    c3 = lambda b, i: (0, 0, 0)
    return pl.pallas_call(
        _mixer_kernel,
        out_shape=(jax.ShapeDtypeStruct((bsz, t, d), F32),
                   jax.ShapeDtypeStruct((bsz, t, d), BF16),
                   jax.ShapeDtypeStruct((bsz, N_EXPERTS, t), F32)),
        grid=(bsz, t // tt),
        in_specs=[
            pl.BlockSpec(memory_space=pltpu.SMEM),
            pl.BlockSpec((1, tt, Z_BLOCK), lambda b, i: (b, i, 0)),
            pl.BlockSpec((1, tt, Z_BLOCK), lambda b, i: (b, i, 1)),
            pl.BlockSpec((1, tt, d), lambda b, i: (b, i, Z_GB // d)),
            pl.BlockSpec(memory_space=pl.ANY),
            pl.BlockSpec((1, tt, d), row),
            pl.BlockSpec((1, 1, 6 * d), lambda b, i: (b, 0, 0)),
            pl.BlockSpec((RET_V_W, d), c2, pipeline_mode=pl.Buffered(1)),
            pl.BlockSpec((SGU_WIDTH, d), c2, pipeline_mode=pl.Buffered(1)),
            pl.BlockSpec((d, d), c2, pipeline_mode=pl.Buffered(1)),
            pl.BlockSpec((SGU_GROUPS, SGU_CHUNK, SGU_CHUNK), c3),
            pl.BlockSpec((SGU_CHUNK, SGU_GROUPS), c2),
            pl.BlockSpec((1, RET_V_W), c2),
            pl.BlockSpec((1, RET_V_W), c2),
            pl.BlockSpec((1, d), c2),
            pl.BlockSpec((1, d), c2),
            pl.BlockSpec((1, d), c2),
            pl.BlockSpec((N_EXPERTS, d), c2),
            pl.BlockSpec((N_EXPERTS, 1), c2),
        ],
        out_specs=(pl.BlockSpec((1, tt, d), row),
                   pl.BlockSpec((1, tt, d), row),
                   pl.BlockSpec((1, N_EXPERTS, tt), lambda b, i: (b, 0, i))),
        scratch_shapes=[pltpu.VMEM((tt, RET_V_W), BF16), pltpu.VMEM((tt, SGU_WIDTH), BF16),
                        pltpu.VMEM((STATE_RING, RET_HEADS, n_sub, 2 * RET_QK_DIM, RET_V_DIM), BF16),
                        pltpu.SemaphoreType.DMA((STATE_RING,))],
        compiler_params=_cparams(("arbitrary", "arbitrary")),
        name="mixer",
    )(lg, z, z, z, states, x, mod3, wpa, wpb, wo, sguw, sgub_t, gnw, gnb, bo,
      ln1w, ln1b, wr_t, br)


def _route_kernel(lgt_ref, pos_ref, gate_ref, wsd_ref, nxd_ref, anyd_ref, wsc_ref, nxc_ref, anyc_ref, *, cap):
    bsz, n_e, t = lgt_ref.shape
    tb = TB_ROUTE

    def softmax(lg):
        ex = jnp.exp(lg - jnp.max(lg, axis=0, keepdims=True))
        return ex / jnp.sum(ex, axis=0, keepdims=True)

    aff = jnp.concatenate([softmax(lgt_ref[bb]) for bb in range(bsz)], axis=0)
    n_rows = bsz * n_e

    def search(i, thr_bits):
        cand = thr_bits | lax.shift_left(jnp.int32(1), 30 - i)
        cnt = jnp.sum((aff >= lax.bitcast_convert_type(cand, F32)).astype(I32), axis=1, keepdims=True)
        return jnp.where(cnt >= cap, cand, thr_bits)

    thr_bits = lax.fori_loop(0, 31, search, jnp.zeros((n_rows, 1), I32))
    floor_f = lax.bitcast_convert_type(thr_bits, F32)
    thr = jnp.min(jnp.where(aff >= floor_f, aff, jnp.inf), axis=1, keepdims=True)
    need = (cap - jnp.sum((aff > thr).astype(I32), axis=1, keepdims=True)).astype(F32)

    r = lax.broadcasted_iota(I32, (tb, tb), 0)
    cc = lax.broadcasted_iota(I32, (tb, tb), 1)
    tri = (r <= cc).astype(BF16)
    carry_eq = jnp.zeros((n_rows, 1), F32)
    carry_sel = jnp.zeros((n_rows, 1), F32)
    nblk = t // tb
    merge = DISPATCH_MERGE
    blk_lane = lax.broadcasted_iota(I32, (n_rows, nblk), 1)
    zero = jnp.zeros((n_rows, nblk), F32)
    used_c, end_c, used_d, end_d = zero, zero, zero, zero
    for blk in range(nblk):
        sl = slice(blk * tb, (blk + 1) * tb)
        aff_b = aff[:, sl]
        eq = aff_b == thr
        eq_b = eq.astype(BF16)
        inc_eq = jnp.dot(eq_b, tri, preferred_element_type=F32)
        before = carry_eq + inc_eq - eq_b.astype(F32)
        sel = (aff_b > thr) | (eq & (before < need))
        sel_b = sel.astype(BF16)
        inc_sel = jnp.dot(sel_b, tri, preferred_element_type=F32)
        pos = jnp.where(sel, (carry_sel + inc_sel - 1.0).astype(I32), -1)
        for bb in range(bsz):
            rows = slice(bb * n_e, (bb + 1) * n_e)
            pos_ref[bb, blk] = pos[rows]
            gate_ref[bb, blk] = aff_b[rows]
        used_c = jnp.where(blk_lane == blk, carry_sel, used_c)
        if blk % merge == 0:
            used_d = jnp.where(blk_lane == blk // merge, carry_sel, used_d)
        carry_eq = carry_eq + inc_eq[:, tb - 1:tb]
        carry_sel = carry_sel + inc_sel[:, tb - 1:tb]
        end_c = jnp.where(blk_lane == blk, carry_sel, end_c)
        if blk % merge == merge - 1:
            end_d = jnp.where(blk_lane == blk // merge, carry_sel, end_d)

    def windows(used, end, win, groups, ws_ref, nx_ref, any_ref):
        n = ws_ref.shape[1]
        ws = jnp.minimum(jnp.floor(used[:, 0:n] * (1.0 / WIN_ALIGN)) * WIN_ALIGN, float(cap - win))
        nx = (end[:, 0:n] > ws + win).astype(I32)
        ws_ref[...] = ws.astype(I32)
        nx_ref[...] = nx
        rows_g = n_e // groups
        for r in range(bsz * groups):
            any_ref[r:r + 1, :] = jnp.max(nx[r * rows_g:(r + 1) * rows_g], axis=0, keepdims=True)

    windows(used_d, end_d, WIN_DISPATCH, DISPATCH_GROUPS, wsd_ref, nxd_ref, anyd_ref)
    windows(used_c, end_c, WIN_COMBINE, 1, wsc_ref, nxc_ref, anyc_ref)


def _route_call(logits_t, cap):
    bsz, n_e, t = logits_t.shape
    nblk = t // TB_ROUTE
    nwin = nblk // DISPATCH_MERGE
    small = [(bsz * n_e, nwin), (bsz * n_e, nwin), (bsz * DISPATCH_GROUPS, nwin),
             (bsz * n_e, nblk), (bsz * n_e, nblk), (bsz, nblk)]
    return pl.pallas_call(
        functools.partial(_route_kernel, cap=cap),
        out_shape=(jax.ShapeDtypeStruct((bsz, nblk, n_e, TB_ROUTE), I32),
                   jax.ShapeDtypeStruct((bsz, nblk, n_e, TB_ROUTE), F32))
        + tuple(jax.ShapeDtypeStruct(sh, I32) for sh in small),
        grid=(1,),
        in_specs=[pl.BlockSpec((bsz, n_e, t), lambda i: (0, 0, 0))],
        out_specs=(pl.BlockSpec((bsz, nblk, n_e, TB_ROUTE), lambda i: (0, 0, 0, 0)),
                   pl.BlockSpec((bsz, nblk, n_e, TB_ROUTE), lambda i: (0, 0, 0, 0)))
        + tuple(pl.BlockSpec(sh, lambda i: (0, 0)) for sh in small),
        compiler_params=_cparams(("arbitrary",)),
        name="route",
    )(logits_t)


def _window_rows(ws, win, n_tok):
    return lax.broadcasted_iota(I32, (win, n_tok), 0) + ws


def _sweep_window(w, cap, win, prow, rest):
    off = pl.multiple_of(jnp.minimum(w * win, cap - win), WIN_ALIGN)
    rows = _window_rows(off, win, prow.shape[-1])
    return off, (rows == prow) & rest & (rows >= w * win)


def _gather_kernel(ws_ref, nx_ref, any_ref, hm_ref, pos_ref, gate_ref, xs_ref, gc_ref):
    b = pl.program_id(0)
    g = pl.program_id(1)
    step = pl.program_id(2)
    merge = DISPATCH_MERGE
    eg, tb = pos_ref.shape[2:]
    n_sub = pos_ref.shape[1] // merge
    cap = xs_ref.shape[2]
    win = WIN_DISPATCH
    row0 = (b * pl.num_programs(1) + g) * eg

    @pl.when(step == 0)
    def _():
        xs_ref[...] = jnp.zeros_like(xs_ref)
        gc_ref[...] = jnp.zeros_like(gc_ref)

    def tokens(sub):
        return hm_ref[0, sub * merge * tb:(sub + 1) * merge * tb, :]

    def lanes(ref, sub, el):
        return jnp.concatenate([ref[0, sub * merge + m, el:el + 1, :] for m in range(merge)], axis=-1)

    for sub in range(n_sub):
        for el in range(eg):
            ws = pl.multiple_of(ws_ref[row0 + el, step * n_sub + sub], WIN_ALIGN)
            match = _window_rows(ws, win, merge * tb) == lanes(pos_ref, sub, el)
            sl = pl.ds(ws, win)
            xs_ref[0, el, sl, :] += jnp.dot(match.astype(BF16), tokens(sub),
                                            preferred_element_type=F32).astype(BF16)
            gc_ref[0, el, sl, :] += jnp.sum(jnp.where(match, lanes(gate_ref, sub, el), 0.0),
                                             axis=1, keepdims=True)

    for sub in range(n_sub):
        @pl.when(any_ref[b * pl.num_programs(1) + g, step * n_sub + sub] > 0)
        def _(sub=sub):
            for el in range(eg):
                @pl.when(nx_ref[row0 + el, step * n_sub + sub] > 0)
                def _(el=el):
                    prow = lanes(pos_ref, sub, el)
                    grow = lanes(gate_ref, sub, el)
                    rest = prow >= ws_ref[row0 + el, step * n_sub + sub] + win

                    def body(w, carry):
                        off, m = _sweep_window(w, cap, win, prow, rest)
                        xs_ref[0, el, pl.ds(off, win), :] += jnp.dot(m.astype(BF16), tokens(sub),
                                                                      preferred_element_type=F32).astype(BF16)
                        gc_ref[0, el, pl.ds(off, win), :] += jnp.sum(jnp.where(m, grow, 0.0), axis=1,
                                                                     keepdims=True)
                        return carry

                    lax.fori_loop(0, pl.cdiv(cap, win), body, 0)


def _gather_call(ws, nx, nx_any, hm, pos_b, gate_b, cap):
    bsz, t, d = hm.shape
    nblk, n_e, tb = pos_b.shape[1:]
    eg = n_e // DISPATCH_GROUPS
    blocks = DISPATCH_STEP_WINDOWS * DISPATCH_MERGE
    grid_spec = pltpu.PrefetchScalarGridSpec(
        num_scalar_prefetch=3,
        grid=(bsz, DISPATCH_GROUPS, nblk // blocks),
        in_specs=[
            pl.BlockSpec((1, blocks * tb, d), lambda b, g, k, ws, nx, fl: (b, k, 0)),
            pl.BlockSpec((1, blocks, eg, tb), lambda b, g, k, ws, nx, fl: (b, k, g, 0)),
            pl.BlockSpec((1, blocks, eg, tb), lambda b, g, k, ws, nx, fl: (b, k, g, 0)),
        ],
        out_specs=(pl.BlockSpec((1, eg, cap, d), lambda b, g, k, ws, nx, fl: (b, g, 0, 0)),
                   pl.BlockSpec((1, eg, cap, 1), lambda b, g, k, ws, nx, fl: (b, g, 0, 0))),
    )
    return pl.pallas_call(
        _gather_kernel,
        out_shape=(jax.ShapeDtypeStruct((bsz, n_e, cap, d), BF16),
                   jax.ShapeDtypeStruct((bsz, n_e, cap, 1), F32)),
        grid_spec=grid_spec,
        compiler_params=_cparams(("parallel", "parallel", "arbitrary")),
        name="gather",
    )(ws, nx, nx_any, hm, pos_b, gate_b)


def _ffn_kernel(xs_ref, gc_ref, wg_ref, wu_ref, wd_ref, ye_ref, wgu_s, wd_s):
    f = wd_s.shape[0]
    wgu_s[:, 0:f] = wg_ref[0].astype(BF16)
    wgu_s[:, f:2 * f] = wu_ref[0].astype(BF16)
    wd_s[...] = wd_ref[0].astype(BF16)

    rt = FFN_ROWS
    for b in range(xs_ref.shape[0]):
        for j in range(xs_ref.shape[2] // rt):
            sl = slice(j * rt, (j + 1) * rt)
            xgu = jnp.dot(xs_ref[b, 0, sl, :], wgu_s[...], preferred_element_type=F32)
            hid = (jax.nn.silu(xgu[:, 0:f]) * xgu[:, f:2 * f]).astype(BF16)
            ye = jnp.dot(hid, wd_s[...], preferred_element_type=F32) * gc_ref[b, 0, sl, :]
            ye_ref[b, 0, sl, :] = ye.astype(BF16)


def _ffn_call(xs, gc, w_gate, w_up, w_down):
    bsz, n_e, cap, d = xs.shape
    f = w_gate.shape[2]
    return pl.pallas_call(
        _ffn_kernel,
        out_shape=jax.ShapeDtypeStruct((bsz, n_e, cap, d), BF16),
        grid=(n_e,),
        in_specs=[
            pl.BlockSpec((bsz, 1, cap, d), lambda e: (0, e, 0, 0)),
            pl.BlockSpec((bsz, 1, cap, 1), lambda e: (0, e, 0, 0)),
            pl.BlockSpec((1, d, f), lambda e: (e, 0, 0)),
            pl.BlockSpec((1, d, f), lambda e: (e, 0, 0)),
            pl.BlockSpec((1, f, d), lambda e: (e, 0, 0)),
        ],
        out_specs=pl.BlockSpec((bsz, 1, cap, d), lambda e: (0, e, 0, 0)),
        scratch_shapes=[pltpu.VMEM((d, 2 * f), BF16), pltpu.VMEM((f, d), BF16)],
        compiler_params=_cparams(("parallel",)),
        name="ffn",
    )(xs, gc, w_gate, w_up, w_down)


def _combine_kernel(ws_ref, nx_ref, any_ref, ye_ref, pos_ref, x1_ref, mod_ref, lnw_ref, lnb_ref, out_ref, acc_scr):
    b = pl.program_id(0)
    step = pl.program_id(1)
    n_sub, n_e, tb = pos_ref.shape[1:]
    cap = ye_ref.shape[2]
    d = D_MODEL
    win = WIN_COMBINE
    g2 = mod_ref[0, :, 5 * d:6 * d]

    for sub in range(n_sub):
        blk = step * n_sub + sub
        starts = [pl.multiple_of(ws_ref[b * n_e + e, blk], WIN_ALIGN) for e in range(n_e)]
        onehot = jnp.concatenate(
            [(_window_rows(starts[e], win, tb) == pos_ref[0, sub, e:e + 1, :]).astype(BF16) for e in range(n_e)],
            axis=0)
        rows = jnp.concatenate([ye_ref[0, e, pl.ds(starts[e], win), :] for e in range(n_e)], axis=0)
        acc_scr[...] = _dot_t0(onehot, rows)

        @pl.when(any_ref[b, blk] > 0)
        def _(sub=sub, blk=blk, starts=starts):
            for e in range(n_e):
                @pl.when(nx_ref[b * n_e + e, blk] > 0)
                def _(e=e):
                    prow = pos_ref[0, sub, e:e + 1, :]
                    rest = prow >= starts[e] + win

                    def body(w, carry):
                        off, m = _sweep_window(w, cap, win, prow, rest)
                        acc_scr[...] += _dot_t0(m.astype(BF16), ye_ref[0, e, pl.ds(off, win), :])
                        return carry

                    lax.fori_loop(0, pl.cdiv(cap, win), body, 0)

        tok = slice(sub * tb, (sub + 1) * tb)
        out_ref[0, tok, :] = (_ln(DEEPNORM_ALPHA * x1_ref[0, tok, :] + g2 * acc_scr[...]) * lnw_ref[...]
                              + lnb_ref[...])


def _combine_call(ws, nx, nx_any, ye, pos_b, x1, mod3, ln2w, ln2b):
    bsz, n_e, cap, d = ye.shape
    nblk, _, tb = pos_b.shape[1:]
    n_sub = COMBINE_STEP_BLOCKS
    grid_spec = pltpu.PrefetchScalarGridSpec(
        num_scalar_prefetch=3,
        grid=(bsz, nblk // n_sub),
        in_specs=[
            pl.BlockSpec((1, n_e, cap, d), lambda b, k, ws, nx, fl: (b, 0, 0, 0), pipeline_mode=pl.Buffered(1)),
            pl.BlockSpec((1, n_sub, n_e, tb), lambda b, k, ws, nx, fl: (b, k, 0, 0)),
            pl.BlockSpec((1, n_sub * tb, d), lambda b, k, ws, nx, fl: (b, k, 0)),
            pl.BlockSpec((1, 1, 6 * d), lambda b, k, ws, nx, fl: (b, 0, 0)),
            pl.BlockSpec((1, d), lambda b, k, ws, nx, fl: (0, 0)),
            pl.BlockSpec((1, d), lambda b, k, ws, nx, fl: (0, 0)),
        ],
        out_specs=pl.BlockSpec((1, n_sub * tb, d), lambda b, k, ws, nx, fl: (b, k, 0)),
        scratch_shapes=[pltpu.VMEM((tb, d), F32)],
    )
    return pl.pallas_call(
        _combine_kernel,
        out_shape=jax.ShapeDtypeStruct((bsz, nblk * tb, d), F32),
        grid_spec=grid_spec,
        compiler_params=_cparams(("parallel", "arbitrary")),
        name="combine",
    )(ws, nx, nx_any, ye, pos_b, x1, mod3, ln2w, ln2b)


def _rope_tables(n_tokens):
    rows = n_tokens // GRID_W
    n_freq = RET_QK_DIM // 4
    inv = ROPE_THETA ** (-jnp.arange(n_freq, dtype=F32) / n_freq)
    ang_r = jnp.arange(rows, dtype=F32)[:, None] * inv
    ang_c = jnp.arange(GRID_W, dtype=F32)[:, None] * inv
    zr = jnp.zeros((rows, 2 * n_freq), F32)
    zc = jnp.zeros((GRID_W, 2 * n_freq), F32)
    cr, sr, cc, sc = jnp.cos(ang_r), jnp.sin(ang_r), jnp.cos(ang_c), jnp.sin(ang_c)
    cos_t = (jnp.concatenate([cr, cr, zr], axis=-1)[:, None, :]
             + jnp.concatenate([zc, cc, cc], axis=-1)[None, :, :]).reshape(n_tokens, RET_QK_DIM)
    sin_t = (jnp.concatenate([-sr, sr, zr], axis=-1)[:, None, :]
             + jnp.concatenate([zc, -sc, sc], axis=-1)[None, :, :]).reshape(n_tokens, RET_QK_DIM)
    return jnp.concatenate([cos_t, sin_t], axis=-1)


def kernel(x, c, ctx, c_ctx, w_ada, b_ada, w_in, b_in, ret_decay_f, ret_decay_b, ret_gn_w, ret_gn_b,
           sgu_ln_w, sgu_ln_b, sgu_w, sgu_b, w_pa, w_pb, w_o, b_o, ln1_w, ln1_b,
           w_router, b_router, w_gate, w_up, w_down, ln2_w, ln2_b):
    bsz, t, d = x.shape
    assert d == D_MODEL and w_ada.shape[0] == DEPTH == 1 and bsz < MOD_ROWS
    cap = EC_CAPACITY_FACTOR * t // N_EXPERTS
    for win in (WIN_DISPATCH, WIN_COMBINE):
        assert cap >= win and (cap - win) % WIN_ALIGN == 0 and win % WIN_ALIGN == 0
    assert t % (TB_ROUTE * DISPATCH_MERGE * DISPATCH_STEP_WINDOWS) == 0 and t % (TB_ROUTE * COMBINE_STEP_BLOCKS) == 0
    assert N_EXPERTS % DISPATCH_GROUPS == 0
    l = 0

    cvec = jnp.concatenate([c, c_ctx[None], jnp.zeros((MOD_ROWS - bsz - 1, d), F32)], axis=0)
    mod = _mod_call(cvec.T, w_ada[l], b_ada[l][None], bsz + 1)
    mod3 = mod[:, None, :]

    w_in_b = w_in[l].astype(BF16)
    b_in2 = b_in[l][None]
    kctx, vctx = _ctx_kv_call(ctx, mod3, w_in_b, b_in2, bsz)

    z = _inproj_call(x, mod3, w_in_b, b_in2, _rope_tables(t), sgu_ln_w[l][None], sgu_ln_b[l][None])

    lg = jnp.stack([jax.nn.log_sigmoid(ret_decay_f[l].astype(F32)),
                    jax.nn.log_sigmoid(ret_decay_b[l].astype(F32))])
    states = _state_call(lg, z, kctx, vctx)

    x1, hm, logits_t = _mixer_call(
        lg, z, states, x, mod3,
        w_pa[l].astype(BF16), w_pb[l].astype(BF16), w_o[l].astype(BF16),
        sgu_w[l].astype(BF16), sgu_b[l].T, ret_gn_w[l][None], ret_gn_b[l][None], b_o[l][None],
        ln1_w[l][None], ln1_b[l][None], w_router[l].T.astype(BF16), b_router[l][:, None])

    pos_b, gate_b, ws_d, nx_d, any_d, ws_c, nx_c, any_c = _route_call(logits_t, cap)
    xs, gc = _gather_call(ws_d, nx_d, any_d, hm, pos_b, gate_b, cap)
    ye = _ffn_call(xs, gc, w_gate[l], w_up[l], w_down[l])
    return _combine_call(ws_c, nx_c, any_c, ye, pos_b, x1, mod3, ln2_w[l][None], ln2_b[l][None])
```
